```python
import jax, jax.numpy as jnp
from jax import lax
import numpy as np

D_MODEL = 1024
BATCH = 16
SEQ = 256
DEPTH = 4
DEC_BATCH = 8
DEC_SEQ = 1024
PAST_LEN = 512

GRID_W = 64
N_REC = (DEPTH + 1) // 2
N_ATT = DEPTH // 2
POOL_WIDTH = D_MODEL // 2
POOL_WINDOWS = (2, 4, 8, 16)
POOL_GROUPS = len(POOL_WINDOWS)
POOL_GROUP_DIM = POOL_WIDTH // POOL_GROUPS
REC_WIDTH = D_MODEL // 2
REC_HEAD_DIM = 128
REC_HEADS = REC_WIDTH // REC_HEAD_DIM
CHUNK = 16
REC_IN_WIDTH = 2 * POOL_WIDTH + 5 * REC_WIDTH
REC_OUT_IN = POOL_WIDTH + REC_WIDTH
ATT_HEAD_DIM = 128
ATT_HEADS = D_MODEL // ATT_HEAD_DIM
ATT_KV_HEADS = 2
ATT_WIDTH = ATT_HEADS * ATT_HEAD_DIM
KV_WIDTH = ATT_KV_HEADS * ATT_HEAD_DIM
ATT_IN_WIDTH = 2 * ATT_WIDTH + 2 * KV_WIDTH
AXIS_DIM = ATT_HEAD_DIM // 2
Q_BLOCK = 128
ROPE_THETA = 10000.0
EPS = 1e-6
F_MIN = 1e-6

kernel_name = "hybrid_pool_hgrn2_gqa_prefix_diffusion_step"


def rmsnorm(x, g):
    xf = x.astype(jnp.float32)
    y = xf * lax.rsqrt(jnp.mean(xf * xf, axis=-1, keepdims=True) + EPS)
    return (y * g.astype(jnp.float32)).astype(x.dtype)


def modulation(cvec, w, b):
    m = jax.nn.silu(cvec) @ w + b
    shift, scale, gate = jnp.split(m, 3, axis=-1)
    return shift[:, None, :], scale[:, None, :], gate[:, None, :]


def axial_rope(n_tokens):
    rows = n_tokens // GRID_W
    t = jnp.arange(rows * GRID_W)
    row = (t // GRID_W).astype(jnp.float32)
    col = (t % GRID_W).astype(jnp.float32)
    inv = ROPE_THETA ** (-jnp.arange(0, AXIS_DIM, 2, dtype=jnp.float32) / AXIS_DIM)
    ang = jnp.concatenate([row[:, None] * inv[None, :], col[:, None] * inv[None, :]], axis=-1)
    return jnp.cos(ang)[None, :, None, :], jnp.sin(ang)[None, :, None, :]


def apply_rope(x, cos, sin):
    xf = x.astype(jnp.float32)
    x1, x2 = xf[..., 0::2], xf[..., 1::2]
    y = jnp.stack([x1 * cos - x2 * sin, x1 * sin + x2 * cos], axis=-1).reshape(x.shape)
    return y.astype(x.dtype)


def pool_mixer(u, w, scale):
    B, L, _ = u.shape
    uf = u.astype(jnp.float32)
    cs = jnp.concatenate([jnp.zeros((B, 1, POOL_WIDTH), jnp.float32), jnp.cumsum(uf, axis=1)], axis=1)
    t = jnp.arange(L)
    outs = []
    for g, win in enumerate(POOL_WINDOWS):
        sl = slice(g * POOL_GROUP_DIM, (g + 1) * POOL_GROUP_DIM)
        lo = jnp.clip(t - win // 2, 0, L)
        hi = jnp.clip(t + win // 2, 0, L)
        cnt = (hi - lo).astype(jnp.float32)
        mean = (cs[:, hi, sl] - cs[:, lo, sl]) / cnt[None, :, None]
        outs.append(mean - uf[..., sl])
    d = jnp.stack(outs, axis=2)
    y = jnp.einsum('blgc,gcd->blgd', d, w.astype(jnp.float32)).reshape(B, L, POOL_WIDTH)
    return (y * scale.astype(jnp.float32)).astype(u.dtype)


def hgrn2_scan(q, k, v, log_f, s0):
    B, L, H, DK = q.shape
    DV = v.shape[-1]
    n = L // CHUNK

    def chunks(a):
        return a.reshape(B, n, CHUNK, H, a.shape[-1]).transpose(1, 0, 3, 2, 4)

    qc, kc, vc, gc = chunks(q), chunks(k), chunks(v), chunks(log_f)
    b = jnp.cumsum(gc, axis=3)
    mask = jnp.tril(jnp.ones((CHUNK, CHUNK), dtype=bool))[:, :, None]
    diff = b[..., :, None, :] - b[..., None, :, :]
    decay = jnp.where(mask, jnp.exp(jnp.where(mask, diff, 0.0)), 0.0)
    a_intra = jnp.einsum('nbhtd,nbhsd,nbhtsd->nbhts', qc, kc, decay)
    o_intra = jnp.einsum('nbhts,nbhsv->nbhtv', a_intra, vc)
    q_in = qc * jnp.exp(b)
    k_out = kc * jnp.exp(b[..., -1:, :] - b)
    a_last = jnp.exp(b[..., -1, :])

    def step(S, inp):
        qi, ko, vi, al = inp
        o = jnp.einsum('bhtd,bhdv->bhtv', qi, S)
        S = al[..., None] * S + jnp.einsum('bhsd,bhsv->bhdv', ko, vi)
        return S, o

    s_final, o_inter = lax.scan(step, s0.astype(jnp.float32), (q_in, k_out, vc, a_last))
    o = (o_intra + o_inter).transpose(1, 0, 3, 2, 4).reshape(B, L, H, DV)
    return o, s_final


def rec_mixer(h, s0, w_in, lb, head_norm, p_w, p_scale, w_out):
    B, L, _ = h.shape
    P, R = POOL_WIDTH, REC_WIDTH
    z = h @ w_in
    u_pool, g_pool, q, f_fw, f_bw, i_v, g_rec = jnp.split(
        z, [P, 2 * P, 2 * P + R, 2 * P + 2 * R, 2 * P + 3 * R, 2 * P + 4 * R], axis=-1)
    y_pool = pool_mixer(u_pool, p_w, p_scale) * jax.nn.silu(g_pool)

    qf = jax.nn.silu(q.astype(jnp.float32)).reshape(B, L, REC_HEADS, REC_HEAD_DIM)
    vf = i_v.astype(jnp.float32).reshape(B, L, REC_HEADS, REC_HEAD_DIM)
    lbf = lb.astype(jnp.float32)

    def forget(zf, lower):
        f = lower + (1.0 - lower) * jax.nn.sigmoid(zf.astype(jnp.float32))
        f = jnp.clip(f, F_MIN, 1.0).reshape(B, L, REC_HEADS, REC_HEAD_DIM)
        return jnp.log(f), 1.0 - f

    lf_f, k_f = forget(f_fw, lbf[0])
    lf_b, k_b = forget(f_bw, lbf[1])
    o_f, s_f = hgrn2_scan(qf, k_f, vf, lf_f, s0[:, 0])
    flip = lambda a: jnp.flip(a, axis=1)
    o_b, s_b = hgrn2_scan(flip(qf), flip(k_b), flip(vf), flip(lf_b), s0[:, 1])
    o = o_f + flip(o_b)
    o = rmsnorm(o, head_norm).reshape(B, L, REC_WIDTH).astype(h.dtype) * jax.nn.silu(g_rec)
    y = jnp.concatenate([y_pool, o], axis=-1) @ w_out
    return y, jnp.stack([s_f, s_b], axis=1)


def att_project(h, w_in, q_norm, k_norm):
    B, L, _ = h.shape
    z = h @ w_in
    q, k, v, g = jnp.split(z, [ATT_WIDTH, ATT_WIDTH + KV_WIDTH, ATT_WIDTH + 2 * KV_WIDTH], axis=-1)
    q = rmsnorm(q.reshape(B, L, ATT_HEADS, ATT_HEAD_DIM), q_norm)
    k = rmsnorm(k.reshape(B, L, ATT_KV_HEADS, ATT_HEAD_DIM), k_norm)
    v = v.reshape(B, L, ATT_KV_HEADS, ATT_HEAD_DIM)
    return q, k, v, g


def attention(q, k, v):
    B, Lq, H, D = q.shape
    rep = H // ATT_KV_HEADS
    scale = D ** -0.5
    qb = q.reshape(B, Lq // Q_BLOCK, Q_BLOCK, ATT_KV_HEADS, rep, D).transpose(1, 0, 2, 3, 4, 5)
    kf = k.astype(jnp.float32)
    vf = v.astype(jnp.float32)

    def one_block(qblk):
        s = jnp.einsum('bqgrd,bkgd->bgrqk', qblk.astype(jnp.float32), kf) * scale
        p = jax.nn.softmax(s, axis=-1)
        return jnp.einsum('bgrqk,bkgd->bqgrd', p, vf)

    o = lax.map(one_block, qb)
    return o.transpose(1, 0, 2, 3, 4, 5).reshape(B, Lq, H * D).astype(q.dtype)


def setup_inputs(seed: int = 0) -> dict:
    key = jax.random.key(seed)
    ks = jax.random.split(key, 21)
    f32 = jnp.float32

    def nrm(k, shape, scale):
        return jax.random.normal(k, shape, f32) * scale

    D = D_MODEL
    return {
        "x_prompt": nrm(ks[0], (BATCH, SEQ, D), 1.0),
        "x_sample": nrm(ks[1], (DEC_BATCH, DEC_SEQ, D), 1.0),
        "c": nrm(ks[2], (DEC_BATCH, D), 1.0),
        "state_hgrn": nrm(ks[3], (DEC_BATCH, N_REC, 2, REC_HEADS, REC_HEAD_DIM, REC_HEAD_DIM), 0.5),
        "cache_k": nrm(ks[4], (DEC_BATCH, N_ATT, PAST_LEN, ATT_KV_HEADS, ATT_HEAD_DIM), 1.0),
        "cache_v": nrm(ks[5], (DEC_BATCH, N_ATT, PAST_LEN, ATT_KV_HEADS, ATT_HEAD_DIM), 1.0),
        "c_ctx": nrm(ks[6], (D,), 1.0),
        "ada_w": nrm(ks[7], (DEPTH, D, 3 * D), 0.5 * D ** -0.5),
        "ada_b": nrm(ks[8], (DEPTH, 3 * D), 0.02),
        "norm_pre": 1.0 + nrm(ks[9], (DEPTH, D), 0.05),
        "norm_post": 1.0 + nrm(ks[10], (DEPTH, D), 0.05),
        "rec_w_in": nrm(ks[11], (N_REC, D, REC_IN_WIDTH), D ** -0.5),
        "rec_lb_logits": nrm(ks[12], (N_REC, 2, REC_WIDTH), 0.1),
        "rec_head_norm": 1.0 + nrm(ks[13], (N_REC, REC_HEAD_DIM), 0.05),
        "pool_w": nrm(ks[14], (N_REC, POOL_GROUPS, POOL_GROUP_DIM, POOL_GROUP_DIM), POOL_GROUP_DIM ** -0.5),
        "pool_scale": 1.0 + nrm(ks[15], (N_REC, POOL_WIDTH), 0.05),
        "rec_w_out": nrm(ks[16], (N_REC, REC_OUT_IN, D), REC_OUT_IN ** -0.5),
        "att_w_in": nrm(ks[17], (N_ATT, D, ATT_IN_WIDTH), D ** -0.5),
        "att_q_norm": 1.0 + nrm(ks[18], (N_ATT, ATT_HEAD_DIM), 0.05),
        "att_k_norm": 1.0 + nrm(ks[19], (N_ATT, ATT_HEAD_DIM), 0.05),
        "att_w_out": nrm(ks[20], (N_ATT, ATT_WIDTH, D), ATT_WIDTH ** -0.5),
    }


def reference(x_prompt, x_sample, c, state_hgrn, cache_k, cache_v, c_ctx, ada_w, ada_b,
              norm_pre, norm_post, rec_w_in, rec_lb_logits, rec_head_norm, pool_w, pool_scale,
              rec_w_out, att_w_in, att_q_norm, att_k_norm, att_w_out):
    lb_p = jax.nn.softmax(rec_lb_logits.astype(jnp.float32), axis=0)
    lower_bounds = jnp.clip(jnp.cumsum(lb_p, axis=0) - lb_p[0], 0.0, 1.0)
    cos, sin = axial_rope(x_sample.shape[1])

    xc, xl = x_prompt, x_sample
    new_states, new_k, new_v = [], [], []
    for i in range(DEPTH):
        j = i // 2
        sh_c, sc_c, gt_c = modulation(c_ctx[None, :], ada_w[i], ada_b[i])
        sh_l, sc_l, gt_l = modulation(c, ada_w[i], ada_b[i])
        hc = rmsnorm(xc, norm_pre[i]) * (1.0 + sc_c) + sh_c
        hl = rmsnorm(xl, norm_pre[i]) * (1.0 + sc_l) + sh_l
        if i % 2 == 0:
            s_zero = jnp.zeros((hc.shape[0], 2, REC_HEADS, REC_HEAD_DIM, REC_HEAD_DIM), jnp.float32)
            oc, s_ctx = rec_mixer(hc, s_zero, rec_w_in[j], lower_bounds[j], rec_head_norm[j],
                                  pool_w[j], pool_scale[j], rec_w_out[j])
            new_states.append(s_ctx.astype(x_prompt.dtype))
            ol, _ = rec_mixer(hl, state_hgrn[:, j], rec_w_in[j], lower_bounds[j], rec_head_norm[j],
                              pool_w[j], pool_scale[j], rec_w_out[j])
        else:
            qc_, kc_, vc_, gc_ = att_project(hc, att_w_in[j], att_q_norm[j], att_k_norm[j])
            oc = (attention(qc_, kc_, vc_) * jax.nn.silu(gc_)) @ att_w_out[j]
            new_k.append(kc_)
            new_v.append(vc_)
            ql, kl, vl, gl = att_project(hl, att_w_in[j], att_q_norm[j], att_k_norm[j])
            ql = apply_rope(ql, cos, sin)
            kl = apply_rope(kl, cos, sin)
            keys = jnp.concatenate([cache_k[:, j].astype(kl.dtype), kl], axis=1)
            vals = jnp.concatenate([cache_v[:, j].astype(vl.dtype), vl], axis=1)
            ol = (attention(ql, keys, vals) * jax.nn.silu(gl)) @ att_w_out[j]
        xc = xc + gt_c * rmsnorm(oc, norm_post[i])
        xl = xl + gt_l * rmsnorm(ol, norm_post[i])

    return (xc, xl, jnp.stack(new_states, axis=1), jnp.stack(new_k, axis=1), jnp.stack(new_v, axis=1))
```

```python
import functools

import jax
import jax.numpy as jnp
from jax import lax
from jax.experimental import pallas as pl
from jax.experimental.pallas import tpu as pltpu

D_MODEL = 1024
DEPTH = 4
GRID_W = 64
N_REC = (DEPTH + 1) // 2
N_ATT = DEPTH // 2
POOL_WIDTH = D_MODEL // 2
POOL_WINDOWS = (2, 4, 8, 16)
POOL_GROUP_DIM = POOL_WIDTH // len(POOL_WINDOWS)
REC_WIDTH = D_MODEL // 2
REC_HEAD_DIM = 128
REC_HEADS = REC_WIDTH // REC_HEAD_DIM
REC_IN_WIDTH = 2 * POOL_WIDTH + 5 * REC_WIDTH
ATT_HEAD_DIM = 128
ATT_HEADS = D_MODEL // ATT_HEAD_DIM
ATT_KV_HEADS = 2
ATT_REP = ATT_HEADS // ATT_KV_HEADS
ATT_WIDTH = ATT_HEADS * ATT_HEAD_DIM
KV_WIDTH = ATT_KV_HEADS * ATT_HEAD_DIM
ATT_IN_WIDTH = 2 * ATT_WIDTH + 2 * KV_WIDTH
AXIS_DIM = ATT_HEAD_DIM // 2
ROPE_THETA = 10000.0
EPS = 1e-6
F_MIN = 1e-6

MOD_ROWS = 16
TOKEN_TILE = 256
SCAN_CHUNK = 128
POOL_HALO = 8
VMEM_LIMIT = 56 * 1024 * 1024

F32 = jnp.float32
BF16 = jnp.bfloat16


def _cparams(n_axes):
    return pltpu.CompilerParams(
        dimension_semantics=("arbitrary",) * n_axes, vmem_limit_bytes=VMEM_LIMIT)


def _rms(x, g):
    return x * lax.rsqrt(jnp.mean(x * x, axis=-1, keepdims=True) + EPS) * g


def _dot(a, b):
    return jnp.dot(a, b, preferred_element_type=F32)


def _dot_nt(a, b):
    return lax.dot_general(a, b, (((1,), (1,)), ((), ())), preferred_element_type=F32)


def _mod_kernel(cv_ref, w_ref, b_ref, o_ref):
    cv = cv_ref[...]
    a = (cv * jax.nn.sigmoid(cv)).astype(BF16)
    o_ref[...] = _dot(a, w_ref[...].astype(BF16)) + b_ref[...]


def _modulation(cvec, ada_w, ada_b):
    tn = 512
    out = pl.pallas_call(
        _mod_kernel,
        grid=(DEPTH, 3 * D_MODEL // tn),
        in_specs=[
            pl.BlockSpec((MOD_ROWS, D_MODEL), lambda l, n: (0, 0)),
            pl.BlockSpec((None, D_MODEL, tn), lambda l, n: (l, 0, n)),
            pl.BlockSpec((None, 1, tn), lambda l, n: (l, 0, n)),
        ],
        out_specs=pl.BlockSpec((None, MOD_ROWS, tn), lambda l, n: (l, 0, n)),
        out_shape=jax.ShapeDtypeStruct((DEPTH, MOD_ROWS, 3 * D_MODEL), F32),
        compiler_params=_cparams(2),
        name="modulation",
    )(cvec, ada_w, ada_b.reshape(DEPTH, 1, 3 * D_MODEL))
    return out.reshape(DEPTH, MOD_ROWS, 3, 1, D_MODEL)


def _mod_spec(layer, seq_len, row0):
    tiles_per_seq = seq_len // TOKEN_TILE
    if row0 == 0:
        index = lambda i: (layer, 0, 0, 0, 0)
    else:
        index = lambda i: (layer, row0 + i // tiles_per_seq, 0, 0, 0)
    return pl.BlockSpec((None, None, 3, 1, D_MODEL), index)


def _pre_kernel(x_ref, mod_ref, g_ref, w_ref, z_ref, *, n_out):
    h = _rms(x_ref[...], g_ref[...]) * (1.0 + mod_ref[1]) + mod_ref[0]
    h = h.astype(BF16)
    step = 512
    for n in range(n_out // step):
        z_ref[:, n * step:(n + 1) * step] = _dot(h, w_ref[:, n * step:(n + 1) * step])


def _in_proj(x, mod, layer, seq_len, row0, gain, w):
    t, n_out = x.shape[0], w.shape[1]
    return pl.pallas_call(
        functools.partial(_pre_kernel, n_out=n_out),
        grid=(t // TOKEN_TILE,),
        in_specs=[
            pl.BlockSpec((TOKEN_TILE, D_MODEL), lambda i: (i, 0)),
            _mod_spec(layer, seq_len, row0),
            pl.BlockSpec((1, D_MODEL), lambda i: (0, 0)),
            pl.BlockSpec((D_MODEL, n_out), lambda i: (0, 0)),
        ],
        out_specs=pl.BlockSpec((TOKEN_TILE, n_out), lambda i: (i, 0)),
        out_shape=jax.ShapeDtypeStruct((t, n_out), F32),
        compiler_params=_cparams(1),
        name="in_proj",
    )(x, mod, gain, w)


def _post_kernel(ya_ref, yb_ref, w_ref, x_ref, mod_ref, g_ref, o_ref):
    half = D_MODEL // 2
    p = _dot(ya_ref[...].astype(BF16), w_ref[0:half, :])
    p = p + _dot(yb_ref[...].astype(BF16), w_ref[half:D_MODEL, :])
    o_ref[...] = x_ref[...] + mod_ref[2] * _rms(p, g_ref[...])


def _out_proj(ya, yb, ia, ib, w, x, mod, layer, seq_len, row0, gain):
    t = x.shape[0]
    half = D_MODEL // 2
    return pl.pallas_call(
        _post_kernel,
        grid=(t // TOKEN_TILE,),
        in_specs=[
            pl.BlockSpec((TOKEN_TILE, half), lambda i: (i, ia)),
            pl.BlockSpec((TOKEN_TILE, half), lambda i: (i, ib)),
            pl.BlockSpec((D_MODEL, D_MODEL), lambda i: (0, 0)),
            pl.BlockSpec((TOKEN_TILE, D_MODEL), lambda i: (i, 0)),
            _mod_spec(layer, seq_len, row0),
            pl.BlockSpec((1, D_MODEL), lambda i: (0, 0)),
        ],
        out_specs=pl.BlockSpec((TOKEN_TILE, D_MODEL), lambda i: (i, 0)),
        out_shape=jax.ShapeDtypeStruct((t, D_MODEL), F32),
        compiler_params=_cparams(1),
        name="out_proj",
    )(ya, yb, w, x, mod, gain)


def _pool_kernel(u_ref, gp_ref, pw_ref, ps_ref, o_ref, pad_ref, *, seq_len):
    gd = POOL_GROUP_DIM
    zeros = jnp.zeros((POOL_HALO, POOL_WIDTH), F32)
    pad_ref[0:POOL_HALO, :] = zeros
    pad_ref[POOL_HALO + seq_len:2 * POOL_HALO + seq_len, :] = zeros
    pad_ref[POOL_HALO:POOL_HALO + seq_len, :] = u_ref[...]
    rows = TOKEN_TILE
    for g, win in enumerate(POOL_WINDOWS):
        cols = slice(g * gd, (g + 1) * gd)
        for r in range(seq_len // rows):
            base = r * rows
            acc = None
            for j in range(-(win // 2), win // 2):
                start = POOL_HALO + base + j
                part = pad_ref[start:start + rows, cols]
                acc = part if acc is None else acc + part
            t = base + lax.broadcasted_iota(jnp.int32, (rows, 1), 0)
            lo = jnp.clip(t - win // 2, 0, seq_len)
            hi = jnp.clip(t + win // 2, 0, seq_len)
            mean = acc / (hi - lo).astype(F32)
            d = mean - u_ref[base:base + rows, cols]
            y = _dot(d.astype(BF16), pw_ref[g]) * ps_ref[:, cols]
            gate = gp_ref[base:base + rows, cols]
            o_ref[base:base + rows, cols] = y * (gate * jax.nn.sigmoid(gate))


def _pool_mixer(z, batch, seq_len, blk0, pool_w, pool_scale):
    return pl.pallas_call(
        functools.partial(_pool_kernel, seq_len=seq_len),
        grid=(batch,),
        in_specs=[
            pl.BlockSpec((seq_len, POOL_WIDTH), lambda b: (blk0 + b, 0)),
            pl.BlockSpec((seq_len, POOL_WIDTH), lambda b: (blk0 + b, 1)),
            pl.BlockSpec((len(POOL_WINDOWS), POOL_GROUP_DIM, POOL_GROUP_DIM), lambda b: (0, 0, 0)),
            pl.BlockSpec((1, POOL_WIDTH), lambda b: (0, 0)),
        ],
        out_specs=pl.BlockSpec((seq_len, POOL_WIDTH), lambda b: (b, 0)),
        out_shape=jax.ShapeDtypeStruct((batch * seq_len, POOL_WIDTH), F32),
        scratch_shapes=[pltpu.VMEM((seq_len + 2 * POOL_HALO, POOL_WIDTH), F32)],
        compiler_params=_cparams(1),
        name="pool_mixer",
    )(z, z, pool_w, pool_scale)


def _block_row(p, block, row):
    c, w = p.shape
    p3 = p.reshape(c // block, block, w)
    return jnp.broadcast_to(p3[:, row:row + 1, :], p3.shape).reshape(c, w)


def _scan_chunk(q, zf, v_b, v_t, lower, tri, level, state, rev):
    c = SCAN_CHUNK
    f = jnp.clip(lower + (1.0 - lower) * jax.nn.sigmoid(zf), F_MIN, 1.0)
    g = jnp.log(f)
    k = 1.0 - f
    p = jnp.dot(tri, g, precision=lax.Precision.HIGHEST, preferred_element_type=F32)

    e0 = p - _block_row(p, 8, 4 if rev else 3)
    a = _dot_nt((q * jnp.exp(e0)).astype(BF16), (k * jnp.exp(-e0)).astype(BF16))
    a = jnp.where(level == 0, a, 0.0)
    for lv, block in enumerate((16, 32, 64, 128), 1):
        beta = _block_row(p, block, block // 2 if rev else block // 2 - 1)
        eq = jnp.minimum(p - beta, 0.0)
        ek = jnp.minimum(beta - p, 0.0)
        m = _dot_nt((q * jnp.exp(eq)).astype(BF16), (k * jnp.exp(ek)).astype(BF16))
        a = jnp.where(level == lv, m, a)

    edge = p[0:1, :] if rev else p[c - 1:c, :]
    o = _dot(a.astype(BF16), v_b)
    o = o + _dot_nt((q * jnp.exp(p)).astype(BF16), state.astype(BF16))
    k_out = (k * jnp.exp(edge - p)).astype(BF16)
    new_state = jnp.exp(edge) * state + _dot(v_t, k_out)
    return o, new_state


def _rec_kernel(*refs, seq_len, has_s0, out_state):
    q_ref, ff_ref, fb_ref, v_ref, gr_ref, lb_ref, hn_ref = refs[:7]
    rest = list(refs[7:])
    s0_ref = rest.pop(0) if has_s0 else None
    o_ref = rest.pop(0)
    st_ref = rest.pop(0) if out_state else None
    acc_ref, sf_ref, sb_ref = rest

    c = SCAN_CHUNK
    n_chunks = seq_len // c
    ti = lax.broadcasted_iota(jnp.int32, (c, c), 0)
    si = lax.broadcasted_iota(jnp.int32, (c, c), 1)
    tri_f = (si <= ti).astype(F32)
    tri_b = (si >= ti).astype(F32)
    x = ti ^ si
    lvl = jnp.where(x < 8, 0, jnp.where(x < 16, 1, jnp.where(x < 32, 2, jnp.where(x < 64, 3, 4))))
    none = 5
    lvl_f = jnp.where(si <= ti, lvl, none)
    lvl_b = jnp.where(si >= ti, lvl, none)

    acc_ref[...] = jnp.zeros_like(acc_ref)
    if has_s0:
        sf_ref[...] = s0_ref[0].T
        sb_ref[...] = s0_ref[1].T
    else:
        sf_ref[...] = jnp.zeros_like(sf_ref)
        sb_ref[...] = jnp.zeros_like(sb_ref)
    lb_f = lb_ref[0]
    lb_b = lb_ref[1]

    def load(n):
        rows = pl.ds(pl.multiple_of(n * c, c), c)
        qz = q_ref[rows, :]
        v = v_ref[rows, :]
        return rows, qz * jax.nn.sigmoid(qz), v.astype(BF16), v.T.astype(BF16)

    def body(i, carry):
        rows, q, v_b, v_t = load(i)
        o, s = _scan_chunk(q, ff_ref[rows, :], v_b, v_t, lb_f, tri_f, lvl_f, sf_ref[...], False)
        acc_ref[rows, :] += o
        sf_ref[...] = s
        rows, q, v_b, v_t = load(n_chunks - 1 - i)
        o, s = _scan_chunk(q, fb_ref[rows, :], v_b, v_t, lb_b, tri_b, lvl_b, sb_ref[...], True)
        acc_ref[rows, :] += o
        sb_ref[...] = s
        return carry

    lax.fori_loop(0, n_chunks, body, 0)

    def finish(i, carry):
        rows = pl.ds(pl.multiple_of(i * c, c), c)
        gate = gr_ref[rows, :]
        o_ref[rows, :] = _rms(acc_ref[rows, :], hn_ref[...]) * (gate * jax.nn.sigmoid(gate))
        return carry

    lax.fori_loop(0, n_chunks, finish, 0)
    if out_state:
        st_ref[0] = sf_ref[...].T
        st_ref[1] = sb_ref[...].T


def _rec_mixer(z, batch, seq_len, blk0, lower, head_norm, s0):
    hd = REC_HEAD_DIM
    col0 = 2 * POOL_WIDTH // hd
    nh = REC_HEADS

    def zspec(part):
        return pl.BlockSpec((seq_len, hd), lambda b, h: (blk0 + b, col0 + part * nh + h))

    in_specs = [zspec(0), zspec(1), zspec(2), zspec(3), zspec(4),
                pl.BlockSpec((2, None, 1, hd), lambda b, h: (0, h, 0, 0)),
                pl.BlockSpec((1, hd), lambda b, h: (0, 0))]
    args = [z, z, z, z, z, lower, head_norm]
    has_s0 = s0 is not None
    if has_s0:
        s0_arr, j = s0
        in_specs.append(pl.BlockSpec((None, None, 2, None, hd, hd), lambda b, h: (b, j, 0, h, 0, 0)))
        args.append(s0_arr)
    out_state = not has_s0
    out_specs = [pl.BlockSpec((seq_len, hd), lambda b, h: (b, h))]
    out_shape = [jax.ShapeDtypeStruct((batch * seq_len, REC_WIDTH), F32)]
    if out_state:
        out_specs.append(pl.BlockSpec((None, 2, None, hd, hd), lambda b, h: (b, 0, h, 0, 0)))
        out_shape.append(jax.ShapeDtypeStruct((batch, 2, nh, hd, hd), F32))
    outs = pl.pallas_call(
        functools.partial(_rec_kernel, seq_len=seq_len, has_s0=has_s0, out_state=out_state),
        grid=(batch, nh),
        in_specs=in_specs,
        out_specs=out_specs,
        out_shape=out_shape,
        scratch_shapes=[pltpu.VMEM((seq_len, hd), F32), pltpu.VMEM((hd, hd), F32),
                        pltpu.VMEM((hd, hd), F32)],
        compiler_params=_cparams(2),
        name="rec_mixer",
    )(*args)
    return outs if out_state else (outs[0], None)


def _rope(x, cos, sin_signed):
    lane = lax.broadcasted_iota(jnp.int32, x.shape, 1)
    partner = jnp.where(lane % 2 == 0, pltpu.roll(x, ATT_HEAD_DIM - 1, 1), pltpu.roll(x, 1, 1))
    return x * cos + partner * sin_signed


def _att_kernel(*refs, seq_len, latent):
    q_ref, k_ref, v_ref, g_ref, qn_ref, kn_ref = refs[:6]
    if latent:
        ck_ref, cv_ref, cos_ref, sin_ref, y_ref = refs[6:]
    else:
        y_ref, ko_ref, vo_ref = refs[6:]
    hd = ATT_HEAD_DIM
    scale = hd ** -0.5
    tq = TOKEN_TILE

    k = _rms(k_ref[...], kn_ref[...])
    v = v_ref[...]
    if latent:
        k = _rope(k, cos_ref[...], sin_ref[...])
        ck = ck_ref[...].astype(BF16)
        cv = cv_ref[...].astype(BF16)
    else:
        ko_ref[...] = k
        vo_ref[...] = v
    kb = k.astype(BF16)
    vb = v.astype(BF16)

    for h in range(ATT_REP):
        cols = slice(h * hd, (h + 1) * hd)

        def body(i, carry, cols=cols):
            rows = pl.ds(pl.multiple_of(i * tq, tq), tq)
            q = _rms(q_ref[rows, cols], qn_ref[...])
            if latent:
                q = _rope(q, cos_ref[rows, :], sin_ref[rows, :])
            qb = q.astype(BF16)
            s_new = _dot_nt(qb, kb) * scale
            m = jnp.max(s_new, axis=-1, keepdims=True)
            if latent:
                s_old = _dot_nt(qb, ck) * scale
                m = jnp.maximum(m, jnp.max(s_old, axis=-1, keepdims=True))
                p_old = jnp.exp(s_old - m)
            p_new = jnp.exp(s_new - m)
            den = jnp.sum(p_new, axis=-1, keepdims=True)
            o = _dot(p_new.astype(BF16), vb)
            if latent:
                den = den + jnp.sum(p_old, axis=-1, keepdims=True)
                o = o + _dot(p_old.astype(BF16), cv)
            gate = g_ref[rows, cols]
            y_ref[rows, cols] = (o / den) * (gate * jax.nn.sigmoid(gate))
            return carry

        lax.fori_loop(0, seq_len // tq, body, 0)


def _attention(z, batch, seq_len, q_norm, k_norm, cache=None):
    hd = ATT_HEAD_DIM
    qw = ATT_REP * hd
    latent = cache is not None
    in_specs = [
        pl.BlockSpec((seq_len, qw), lambda b, g: (b, g)),
        pl.BlockSpec((seq_len, hd), lambda b, g: (b, ATT_WIDTH // hd + g)),
        pl.BlockSpec((seq_len, hd), lambda b, g: (b, (ATT_WIDTH + KV_WIDTH) // hd + g)),
        pl.BlockSpec((seq_len, qw), lambda b, g: (b, (ATT_WIDTH + 2 * KV_WIDTH) // qw + g)),
        pl.BlockSpec((1, hd), lambda b, g: (0, 0)),
        pl.BlockSpec((1, hd), lambda b, g: (0, 0)),
    ]
    args = [z, z, z, z, q_norm, k_norm]
    out_specs = [pl.BlockSpec((seq_len, qw), lambda b, g: (b, g))]
    out_shape = [jax.ShapeDtypeStruct((batch * seq_len, ATT_WIDTH), F32)]
    if latent:
        cache_k, cache_v, j, cos, sin_signed = cache
        past = cache_k.shape[2]
        ck = cache_k.reshape(batch, N_ATT, past, KV_WIDTH)
        cv = cache_v.reshape(batch, N_ATT, past, KV_WIDTH)
        cspec = pl.BlockSpec((None, None, past, hd), lambda b, g: (b, j, 0, g))
        tspec = pl.BlockSpec((seq_len, hd), lambda b, g: (0, 0))
        in_specs += [cspec, cspec, tspec, tspec]
        args += [ck, cv, cos, sin_signed]
    else:
        kv_spec = pl.BlockSpec((None, seq_len, hd), lambda b, g: (b, 0, g))
        out_specs += [kv_spec, kv_spec]
        out_shape += [jax.ShapeDtypeStruct((batch, seq_len, KV_WIDTH), F32)] * 2
    return pl.pallas_call(
        functools.partial(_att_kernel, seq_len=seq_len, latent=latent),
        grid=(batch, ATT_KV_HEADS),
        in_specs=in_specs,
        out_specs=out_specs,
        out_shape=out_shape,
        compiler_params=_cparams(2),
        name="attention",
    )(*args)


def _rope_tables(n_tokens):
    t = jnp.arange(n_tokens)
    row = (t // GRID_W).astype(F32)
    col = (t % GRID_W).astype(F32)
    inv = ROPE_THETA ** (-jnp.arange(0, AXIS_DIM, 2, dtype=F32) / AXIS_DIM)
    ang = jnp.concatenate([row[:, None] * inv[None, :], col[:, None] * inv[None, :]], axis=-1)
    cos = jnp.repeat(jnp.cos(ang), 2, axis=-1)
    sin = jnp.repeat(jnp.sin(ang), 2, axis=-1)
    sign = jnp.where(jnp.arange(ATT_HEAD_DIM) % 2 == 0, -1.0, 1.0).astype(F32)
    return cos, sin * sign


def kernel(x_prompt, x_sample, c, state_hgrn, cache_k, cache_v, c_ctx, ada_w, ada_b, norm_pre, norm_post, rec_w_in, rec_lb_logits, rec_head_norm, pool_w, pool_scale, rec_w_out, att_w_in, att_q_norm, att_k_norm, att_w_out):
    nb_c, len_c, _ = x_prompt.shape
    nb_l, len_l, _ = x_sample.shape

    lb_p = jax.nn.softmax(rec_lb_logits.astype(F32), axis=0)
    lower_bounds = jnp.clip(jnp.cumsum(lb_p, axis=0) - lb_p[0], 0.0, 1.0)
    lower_bounds = lower_bounds.reshape(N_REC, 2, REC_HEADS, 1, REC_HEAD_DIM)
    cos, sin_signed = _rope_tables(len_l)

    cvec = jnp.zeros((MOD_ROWS, D_MODEL), F32).at[0].set(c_ctx).at[1:1 + nb_l].set(c)
    mod = _modulation(cvec, ada_w, ada_b)

    xc = x_prompt.reshape(nb_c * len_c, D_MODEL)
    xl = x_sample.reshape(nb_l * len_l, D_MODEL)
    streams = ((nb_c, len_c, 0), (nb_l, len_l, 1))
    new_states, new_k, new_v = [], [], []
    for i in range(DEPTH):
        j = i // 2
        gain_pre = norm_pre[i].reshape(1, D_MODEL)
        gain_post = norm_post[i].reshape(1, D_MODEL)
        xs = [xc, xl]
        if i % 2 == 0:
            w_in = rec_w_in[j].astype(BF16)
            w_out = rec_w_out[j].astype(BF16)
            pw = pool_w[j].astype(BF16)
            ps = pool_scale[j].reshape(1, POOL_WIDTH)
            hn = rec_head_norm[j].reshape(1, REC_HEAD_DIM)
            for s, (nb, sl, row0) in enumerate(streams):
                z = _in_proj(xs[s], mod, i, sl, row0, gain_pre, w_in)
                y_pool = _pool_mixer(z, nb, sl, 0, pw, ps)
                s0 = (state_hgrn, j) if s == 1 else None
                y_rec, st = _rec_mixer(z, nb, sl, 0, lower_bounds[j], hn, s0)
                if st is not None:
                    new_states.append(st)
                xs[s] = _out_proj(y_pool, y_rec, 0, 0, w_out, xs[s], mod, i, sl, row0, gain_post)
        else:
            w_in = att_w_in[j].astype(BF16)
            w_out = att_w_out[j].astype(BF16)
            qn = att_q_norm[j].reshape(1, ATT_HEAD_DIM)
            kn = att_k_norm[j].reshape(1, ATT_HEAD_DIM)
            for s, (nb, sl, row0) in enumerate(streams):
                z = _in_proj(xs[s], mod, i, sl, row0, gain_pre, w_in)
                if s == 0:
                    y, k_new, v_new = _attention(z, nb, sl, qn, kn)
                    new_k.append(k_new.reshape(nb, sl, ATT_KV_HEADS, ATT_HEAD_DIM))
                    new_v.append(v_new.reshape(nb, sl, ATT_KV_HEADS, ATT_HEAD_DIM))
                else:
                    (y,) = _attention(z, nb, sl, qn, kn, (cache_k, cache_v, j, cos, sin_signed))
                xs[s] = _out_proj(y, y, 0, 1, w_out, xs[s], mod, i, sl, row0, gain_post)
        xc, xl = xs

    return (xc.reshape(nb_c, len_c, D_MODEL), xl.reshape(nb_l, len_l, D_MODEL),
            jnp.stack(new_states, axis=1), jnp.stack(new_k, axis=1), jnp.stack(new_v, axis=1))
```

```python
import functools

import jax
import jax.numpy as jnp
from jax import lax
from jax.experimental import pallas as pl
from jax.experimental.pallas import tpu as pltpu

D_MODEL = 1024
DEPTH = 4
GRID_W = 64
N_REC = (DEPTH + 1) // 2
N_ATT = DEPTH // 2
POOL_WIDTH = D_MODEL // 2
POOL_WINDOWS = (2, 4, 8, 16)
POOL_GROUP_DIM = POOL_WIDTH // len(POOL_WINDOWS)
REC_WIDTH = D_MODEL // 2
REC_HEAD_DIM = 128
REC_HEADS = REC_WIDTH // REC_HEAD_DIM
REC_IN_WIDTH = 2 * POOL_WIDTH + 5 * REC_WIDTH
ATT_HEAD_DIM = 128
ATT_HEADS = D_MODEL // ATT_HEAD_DIM
ATT_KV_HEADS = 2
ATT_REP = ATT_HEADS // ATT_KV_HEADS
ATT_WIDTH = ATT_HEADS * ATT_HEAD_DIM
KV_WIDTH = ATT_KV_HEADS * ATT_HEAD_DIM
ATT_IN_WIDTH = 2 * ATT_WIDTH + 2 * KV_WIDTH
AXIS_DIM = ATT_HEAD_DIM // 2
ROPE_THETA = 10000.0
EPS = 1e-6
F_MIN = 1e-6

MOD_ROWS = 16
TOKEN_TILE = 256
SCAN_CHUNK = 128
POOL_HALO = 8
ATT_Q_TILE = 128
ATT_KEY_TILE = 512
LOG2_E = 1.4426950408889634
VMEM_LIMIT = 56 * 1024 * 1024

F32 = jnp.float32
BF16 = jnp.bfloat16


def _cparams(n_axes):
    return pltpu.CompilerParams(
        dimension_semantics=("arbitrary",) * n_axes, vmem_limit_bytes=VMEM_LIMIT)


def _rms(x, g):
    return x * lax.rsqrt(jnp.mean(x * x, axis=-1, keepdims=True) + EPS) * g


def _dot(a, b):
    return jnp.dot(a, b, preferred_element_type=F32)


def _dot_nt(a, b):
    return lax.dot_general(a, b, (((1,), (1,)), ((), ())), preferred_element_type=F32)


def _mod_kernel(cv_ref, w_ref, b_ref, o_ref):
    cv = cv_ref[...]
    a = (cv * jax.nn.sigmoid(cv)).astype(BF16)
    o_ref[...] = _dot(a, w_ref[...].astype(BF16)) + b_ref[...]


def _modulation(cvec, ada_w, ada_b):
    tn = 512
    out = pl.pallas_call(
        _mod_kernel,
        grid=(DEPTH, 3 * D_MODEL // tn),
        in_specs=[
            pl.BlockSpec((MOD_ROWS, D_MODEL), lambda l, n: (0, 0)),
            pl.BlockSpec((None, D_MODEL, tn), lambda l, n: (l, 0, n)),
            pl.BlockSpec((None, 1, tn), lambda l, n: (l, 0, n)),
        ],
        out_specs=pl.BlockSpec((None, MOD_ROWS, tn), lambda l, n: (l, 0, n)),
        out_shape=jax.ShapeDtypeStruct((DEPTH, MOD_ROWS, 3 * D_MODEL), F32),
        compiler_params=_cparams(2),
        name="modulation",
    )(cvec, ada_w, ada_b.reshape(DEPTH, 1, 3 * D_MODEL))
    return out.reshape(DEPTH, MOD_ROWS, 3, 1, D_MODEL)


def _mod_spec(layer, seq_len, row0):
    tiles_per_seq = seq_len // TOKEN_TILE
    if row0 == 0:
        index = lambda i: (layer, 0, 0, 0, 0)
    else:
        index = lambda i: (layer, row0 + i // tiles_per_seq, 0, 0, 0)
    return pl.BlockSpec((None, None, 3, 1, D_MODEL), index)


def _pre_kernel(x_ref, mod_ref, g_ref, w_ref, z_ref, *, n_out):
    h = _rms(x_ref[...], g_ref[...]) * (1.0 + mod_ref[1]) + mod_ref[0]
    h = h.astype(BF16)
    step = 512
    for n in range(n_out // step):
        z_ref[:, n * step:(n + 1) * step] = _dot(h, w_ref[:, n * step:(n + 1) * step])


def _in_proj(x, mod, layer, seq_len, row0, gain, w):
    t, n_out = x.shape[0], w.shape[1]
    return pl.pallas_call(
        functools.partial(_pre_kernel, n_out=n_out),
        grid=(t // TOKEN_TILE,),
        in_specs=[
            pl.BlockSpec((TOKEN_TILE, D_MODEL), lambda i: (i, 0)),
            _mod_spec(layer, seq_len, row0),
            pl.BlockSpec((1, D_MODEL), lambda i: (0, 0)),
            pl.BlockSpec((D_MODEL, n_out), lambda i: (0, 0)),
        ],
        out_specs=pl.BlockSpec((TOKEN_TILE, n_out), lambda i: (i, 0)),
        out_shape=jax.ShapeDtypeStruct((t, n_out), F32),
        compiler_params=_cparams(1),
        name="in_proj",
    )(x, mod, gain, w)


def _post_kernel(ya_ref, yb_ref, w_ref, x_ref, mod_ref, g_ref, o_ref):
    half = D_MODEL // 2
    p = _dot(ya_ref[...].astype(BF16), w_ref[0:half, :])
    p = p + _dot(yb_ref[...].astype(BF16), w_ref[half:D_MODEL, :])
    o_ref[...] = x_ref[...] + mod_ref[2] * _rms(p, g_ref[...])


def _out_proj(ya, yb, ia, ib, w, x, mod, layer, seq_len, row0, gain):
    t = x.shape[0]
    half = D_MODEL // 2
    return pl.pallas_call(
        _post_kernel,
        grid=(t // TOKEN_TILE,),
        in_specs=[
            pl.BlockSpec((TOKEN_TILE, half), lambda i: (i, ia)),
            pl.BlockSpec((TOKEN_TILE, half), lambda i: (i, ib)),
            pl.BlockSpec((D_MODEL, D_MODEL), lambda i: (0, 0)),
            pl.BlockSpec((TOKEN_TILE, D_MODEL), lambda i: (i, 0)),
            _mod_spec(layer, seq_len, row0),
            pl.BlockSpec((1, D_MODEL), lambda i: (0, 0)),
        ],
        out_specs=pl.BlockSpec((TOKEN_TILE, D_MODEL), lambda i: (i, 0)),
        out_shape=jax.ShapeDtypeStruct((t, D_MODEL), F32),
        compiler_params=_cparams(1),
        name="out_proj",
    )(ya, yb, w, x, mod, gain)


def _pool_kernel(u_ref, gp_ref, pw_ref, ps_ref, o_ref, pad_ref, *, seq_len):
    gd = POOL_GROUP_DIM
    zeros = jnp.zeros((POOL_HALO, POOL_WIDTH), F32)
    pad_ref[0:POOL_HALO, :] = zeros
    pad_ref[POOL_HALO + seq_len:2 * POOL_HALO + seq_len, :] = zeros
    pad_ref[POOL_HALO:POOL_HALO + seq_len, :] = u_ref[...]
    rows = TOKEN_TILE
    for g, win in enumerate(POOL_WINDOWS):
        cols = slice(g * gd, (g + 1) * gd)
        for r in range(seq_len // rows):
            base = r * rows
            acc = None
            for j in range(-(win // 2), win // 2):
                start = POOL_HALO + base + j
                part = pad_ref[start:start + rows, cols]
                acc = part if acc is None else acc + part
            t = base + lax.broadcasted_iota(jnp.int32, (rows, 1), 0)
            lo = jnp.clip(t - win // 2, 0, seq_len)
            hi = jnp.clip(t + win // 2, 0, seq_len)
            mean = acc / (hi - lo).astype(F32)
            d = mean - u_ref[base:base + rows, cols]
            y = _dot(d.astype(BF16), pw_ref[g]) * ps_ref[:, cols]
            gate = gp_ref[base:base + rows, cols]
            o_ref[base:base + rows, cols] = y * (gate * jax.nn.sigmoid(gate))


def _pool_mixer(z, batch, seq_len, blk0, pool_w, pool_scale):
    return pl.pallas_call(
        functools.partial(_pool_kernel, seq_len=seq_len),
        grid=(batch,),
        in_specs=[
            pl.BlockSpec((seq_len, POOL_WIDTH), lambda b: (blk0 + b, 0)),
            pl.BlockSpec((seq_len, POOL_WIDTH), lambda b: (blk0 + b, 1)),
            pl.BlockSpec((len(POOL_WINDOWS), POOL_GROUP_DIM, POOL_GROUP_DIM), lambda b: (0, 0, 0)),
            pl.BlockSpec((1, POOL_WIDTH), lambda b: (0, 0)),
        ],
        out_specs=pl.BlockSpec((seq_len, POOL_WIDTH), lambda b: (b, 0)),
        out_shape=jax.ShapeDtypeStruct((batch * seq_len, POOL_WIDTH), F32),
        scratch_shapes=[pltpu.VMEM((seq_len + 2 * POOL_HALO, POOL_WIDTH), F32)],
        compiler_params=_cparams(1),
        name="pool_mixer",
    )(z, z, pool_w, pool_scale)


def _block_row(p, block, row):
    c, w = p.shape
    p3 = p.reshape(c // block, block, w)
    return jnp.broadcast_to(p3[:, row:row + 1, :], p3.shape).reshape(c, w)


def _interleave(lo, hi, block):
    half = block // 2
    parts = []
    for b in range(SCAN_CHUNK // block):
        parts.append(lo[b * block:b * block + half])
        parts.append(hi[b * block + half:(b + 1) * block])
    return jnp.concatenate(parts, axis=0)


def _rec_kernel(*refs, seq_len, has_s0, out_state):
    q_ref, ff_ref, fb_ref, v_ref, gr_ref, lb_ref, hn_ref = refs[:7]
    rest = list(refs[7:])
    s0_ref = rest.pop(0) if has_s0 else None
    o_ref = rest.pop(0)
    st_ref = rest.pop(0) if out_state else None
    acc_ref, qs_ref, k_ref, p_ref, a_ref, qin_ref, u_ref, dec_ref, tri_ref, lvl_ref = rest

    c = SCAN_CHUNK
    hd = REC_HEAD_DIM
    n_chunks = seq_len // c
    pairs = n_chunks // 2
    zf_refs = (ff_ref, fb_ref)

    ti = lax.broadcasted_iota(jnp.int32, (c, c), 0)
    si = lax.broadcasted_iota(jnp.int32, (c, c), 1)
    x = ti ^ si
    lvl = jnp.where(x < 8, 0, jnp.where(x < 16, 1, jnp.where(x < 32, 2, jnp.where(x < 64, 3, 4))))
    for d, causal in enumerate((si <= ti, si >= ti)):
        tri_ref[d] = jnp.where(causal, LOG2_E, 0.0)
        lvl_ref[d] = jnp.where(causal, lvl, 5)

    def chunk_rows(n):
        return pl.ds(pl.multiple_of(n * c, c), c)

    def both(stage):
        def body(i, carry):
            stage(2 * i)
            stage(2 * i + 1)
            return carry
        lax.fori_loop(0, pairs, body, 0)

    def gates(n):
        rows = chunk_rows(n)
        qz = q_ref[rows, :]
        qs_ref[rows, :] = qz * jax.nn.sigmoid(qz)
        for d in range(2):
            lower = lb_ref[d]
            f = jnp.clip(lower + (1.0 - lower) * jax.nn.sigmoid(zf_refs[d][rows, :]), F_MIN, 1.0)
            k_ref[d, rows, :] = 1.0 - f
            p_ref[d, rows, :] = jnp.dot(tri_ref[d], jnp.log(f), precision=lax.Precision.HIGHEST,
                                        preferred_element_type=F32)

    both(gates)

    def scores(n):
        rows = chunk_rows(n)
        q = qs_ref[rows, :]
        v_t = v_ref[rows, :].T.astype(BF16)
        for d in range(2):
            rev = d == 1
            level = lvl_ref[d]
            k = k_ref[d, rows, :]
            p = p_ref[d, rows, :]
            e0 = p - _block_row(p, 8, 4 if rev else 3)
            a = _dot_nt((q * jnp.exp2(e0)).astype(BF16), (k * jnp.exp2(-e0)).astype(BF16))
            a = jnp.where(level == 0, a, 0.0)
            for lv, block in enumerate((16, 32, 64, 128), 1):
                beta = _block_row(p, block, block // 2 if rev else block // 2 - 1)
                if rev:
                    e = _interleave(p, beta, block) - _interleave(beta, p, block)
                    src = _interleave(q, k, block)
                else:
                    e = _interleave(beta, p, block) - _interleave(p, beta, block)
                    src = _interleave(k, q, block)
                m = (src * jnp.exp2(e)).astype(BF16)
                a = jnp.where(level == lv, _dot_nt(m, m), a)
            a_ref[rows, d * c:(d + 1) * c] = a.astype(BF16)

            edge = p[0:1, :] if rev else p[c - 1:c, :]
            qin_ref[d, rows, :] = (q * jnp.exp2(p)).astype(BF16)
            u_ref[d, n] = _dot(v_t, (k * jnp.exp2(edge - p)).astype(BF16))
            dec_ref[d, n] = jnp.broadcast_to(jnp.exp2(edge), (8, hd))

    both(scores)

    def intra(n):
        rows = chunk_rows(n)
        v_b = v_ref[rows, :].astype(BF16)
        acc_ref[rows, :] = _dot(a_ref[rows, :], jnp.concatenate([v_b, v_b], axis=0))

    both(intra)

    if has_s0:
        states = [s0_ref[0].T, s0_ref[1].T]
    else:
        states = [jnp.zeros((hd, hd), F32)] * 2
    inter = [[None] * n_chunks, [None] * n_chunks]
    for i in range(n_chunks):
        for d, n in ((0, i), (1, n_chunks - 1 - i)):
            rows = slice(n * c, (n + 1) * c)
            inter[d][n] = _dot_nt(qin_ref[d, rows, :], states[d].astype(BF16))
            states[d] = dec_ref[d, n][0:1, :] * states[d] + u_ref[d, n]
    for n in range(n_chunks):
        rows = slice(n * c, (n + 1) * c)
        gate = gr_ref[rows, :]
        o = acc_ref[rows, :] + inter[0][n] + inter[1][n]
        o_ref[rows, :] = _rms(o, hn_ref[...]) * (gate * jax.nn.sigmoid(gate))
    if out_state:
        st_ref[0] = states[0].T
        st_ref[1] = states[1].T


def _rec_mixer(z, batch, seq_len, blk0, lower, head_norm, s0):
    hd = REC_HEAD_DIM
    col0 = 2 * POOL_WIDTH // hd
    nh = REC_HEADS

    def zspec(part):
        return pl.BlockSpec((seq_len, hd), lambda b, h: (blk0 + b, col0 + part * nh + h))

    in_specs = [zspec(0), zspec(1), zspec(2), zspec(3), zspec(4),
                pl.BlockSpec((2, None, 1, hd), lambda b, h: (0, h, 0, 0)),
                pl.BlockSpec((1, hd), lambda b, h: (0, 0))]
    args = [z, z, z, z, z, lower, head_norm]
    has_s0 = s0 is not None
    if has_s0:
        s0_arr, j = s0
        in_specs.append(pl.BlockSpec((None, None, 2, None, hd, hd), lambda b, h: (b, j, 0, h, 0, 0)))
        args.append(s0_arr)
    out_state = not has_s0
    out_specs = [pl.BlockSpec((seq_len, hd), lambda b, h: (b, h))]
    out_shape = [jax.ShapeDtypeStruct((batch * seq_len, REC_WIDTH), F32)]
    if out_state:
        out_specs.append(pl.BlockSpec((None, 2, None, hd, hd), lambda b, h: (b, 0, h, 0, 0)))
        out_shape.append(jax.ShapeDtypeStruct((batch, 2, nh, hd, hd), F32))
    outs = pl.pallas_call(
        functools.partial(_rec_kernel, seq_len=seq_len, has_s0=has_s0, out_state=out_state),
        grid=(batch, nh),
        in_specs=in_specs,
        out_specs=out_specs,
        out_shape=out_shape,
        scratch_shapes=[pltpu.VMEM((seq_len, hd), F32),
                        pltpu.VMEM((seq_len, hd), F32),
                        pltpu.VMEM((2, seq_len, hd), F32),
                        pltpu.VMEM((2, seq_len, hd), F32),
                        pltpu.VMEM((seq_len, 2 * SCAN_CHUNK), BF16),
                        pltpu.VMEM((2, seq_len, hd), BF16),
                        pltpu.VMEM((2, seq_len // SCAN_CHUNK, hd, hd), F32),
                        pltpu.VMEM((2, seq_len // SCAN_CHUNK, 8, hd), F32),
                        pltpu.VMEM((2, SCAN_CHUNK, SCAN_CHUNK), F32),
                        pltpu.VMEM((2, SCAN_CHUNK, SCAN_CHUNK), jnp.int32)],
        compiler_params=_cparams(2),
        name="rec_mixer",
    )(*args)
    return outs if out_state else (outs[0], None)


def _rope(x, cos, sin_signed):
    lane = lax.broadcasted_iota(jnp.int32, x.shape, 1)
    partner = jnp.where((lane & 1) == 0, pltpu.roll(x, ATT_HEAD_DIM - 1, 1), pltpu.roll(x, 1, 1))
    return x * cos + partner * sin_signed


def _att_kernel(*refs, seq_len, n_keys, latent):
    q_ref, k_ref, v_ref, g_ref, qn_ref, kn_ref = refs[:6]
    if latent:
        ck_ref, cv_ref, cos_ref, sin_ref, y_ref, kall, vall = refs[6:-4]
    else:
        y_ref, ko_ref, vo_ref, kall, vall = refs[6:-4]
    s_bufs, m_bufs = refs[-4:-2], refs[-2:]
    hd = ATT_HEAD_DIM
    tq = ATT_Q_TILE
    tk = min(ATT_KEY_TILE, n_keys)
    n_kb = n_keys // tk
    n_past = n_keys - seq_len
    exp2_scale = hd ** -0.5 * LOG2_E

    k = _rms(k_ref[...], kn_ref[...])
    v = v_ref[...]
    if latent:
        k = _rope(k, cos_ref[...], sin_ref[...])
        kall[0:n_past, :] = ck_ref[...].astype(BF16)
        vall[0:n_past, 0:hd] = cv_ref[...].astype(BF16)
    else:
        ko_ref[...] = k
        vo_ref[...] = v
    kall[n_past:n_keys, :] = k.astype(BF16)
    vall[n_past:n_keys, 0:hd] = v.astype(BF16)
    lane = lax.broadcasted_iota(jnp.int32, (n_keys, hd), 1)
    vall[:, hd:2 * hd] = (lane == 0).astype(BF16)

    def pass_a(i, slot):
        s_scr, m_scr = s_bufs[slot], m_bufs[slot]
        rows = pl.ds(pl.multiple_of(i * tq, tq), tq)
        heads = []
        for h in range(ATT_REP):
            q = _rms(q_ref[rows, h * hd:(h + 1) * hd], qn_ref[...])
            if latent:
                q = _rope(q, cos_ref[rows, :], sin_ref[rows, :])
            heads.append(q.astype(BF16))
        qs = jnp.concatenate(heads, axis=0)
        mx = None
        for kb in range(n_kb):
            s = _dot_nt(qs, kall[kb * tk:(kb + 1) * tk, :])
            s_scr[:, kb * tk:(kb + 1) * tk] = s
            for c in range(tk // hd):
                part = s[:, c * hd:(c + 1) * hd]
                mx = part if mx is None else jnp.maximum(mx, part)
        m = jnp.max(mx, axis=-1, keepdims=True) * exp2_scale
        m_scr[...] = jnp.broadcast_to(m, (ATT_REP * tq, hd))

    def pass_b(i, slot):
        s_scr, m_scr = s_bufs[slot], m_bufs[slot]
        rows = pl.ds(pl.multiple_of(i * tq, tq), tq)
        m = m_scr[...]
        ps = []
        for c in range(n_keys // hd):
            s = s_scr[:, c * hd:(c + 1) * hd]
            ps.append(jnp.exp2(s * exp2_scale - m).astype(BF16))
        o = _dot(jnp.concatenate(ps, axis=1), vall[...])
        o = o[:, 0:hd] / o[:, hd:hd + 1]
        for h in range(ATT_REP):
            gate = g_ref[rows, h * hd:(h + 1) * hd]
            y_ref[rows, h * hd:(h + 1) * hd] = o[h * tq:(h + 1) * tq, :] * (gate * jax.nn.sigmoid(gate))

    n_tiles = seq_len // tq
    pass_a(0, 0)

    def body(j, carry):
        pass_a(2 * j + 1, 1)
        pass_b(2 * j, 0)
        pass_a(2 * j + 2, 0)
        pass_b(2 * j + 1, 1)
        return carry

    lax.fori_loop(0, n_tiles // 2 - 1, body, 0)
    pass_a(n_tiles - 1, 1)
    pass_b(n_tiles - 2, 0)
    pass_b(n_tiles - 1, 1)


def _attention(z, batch, seq_len, q_norm, k_norm, cache=None):
    hd = ATT_HEAD_DIM
    qw = ATT_REP * hd
    latent = cache is not None
    in_specs = [
        pl.BlockSpec((seq_len, qw), lambda b, g: (b, g)),
        pl.BlockSpec((seq_len, hd), lambda b, g: (b, ATT_WIDTH // hd + g)),
        pl.BlockSpec((seq_len, hd), lambda b, g: (b, (ATT_WIDTH + KV_WIDTH) // hd + g)),
        pl.BlockSpec((seq_len, qw), lambda b, g: (b, (ATT_WIDTH + 2 * KV_WIDTH) // qw + g)),
        pl.BlockSpec((1, hd), lambda b, g: (0, 0)),
        pl.BlockSpec((1, hd), lambda b, g: (0, 0)),
    ]
    args = [z, z, z, z, q_norm, k_norm]
    out_specs = [pl.BlockSpec((seq_len, qw), lambda b, g: (b, g))]
    out_shape = [jax.ShapeDtypeStruct((batch * seq_len, ATT_WIDTH), F32)]
    n_keys = seq_len
    if latent:
        cache_k, cache_v, j, cos, sin_signed = cache
        past = cache_k.shape[2]
        n_keys += past
        ck = cache_k.reshape(batch, N_ATT, past, KV_WIDTH)
        cv = cache_v.reshape(batch, N_ATT, past, KV_WIDTH)
        cspec = pl.BlockSpec((None, None, past, hd), lambda b, g: (b, j, 0, g))
        tspec = pl.BlockSpec((seq_len, hd), lambda b, g: (0, 0))
        in_specs += [cspec, cspec, tspec, tspec]
        args += [ck, cv, cos, sin_signed]
    else:
        kv_spec = pl.BlockSpec((None, seq_len, hd), lambda b, g: (b, 0, g))
        out_specs += [kv_spec, kv_spec]
        out_shape += [jax.ShapeDtypeStruct((batch, seq_len, KV_WIDTH), F32)] * 2
    return pl.pallas_call(
        functools.partial(_att_kernel, seq_len=seq_len, n_keys=n_keys, latent=latent),
        grid=(batch, ATT_KV_HEADS),
        in_specs=in_specs,
        out_specs=out_specs,
        out_shape=out_shape,
        scratch_shapes=[pltpu.VMEM((n_keys, hd), BF16), pltpu.VMEM((n_keys, 2 * hd), BF16),
                        pltpu.VMEM((ATT_REP * ATT_Q_TILE, n_keys), F32),
                        pltpu.VMEM((ATT_REP * ATT_Q_TILE, n_keys), F32),
                        pltpu.VMEM((ATT_REP * ATT_Q_TILE, hd), F32),
                        pltpu.VMEM((ATT_REP * ATT_Q_TILE, hd), F32)],
        compiler_params=_cparams(2),
        name="attention",
    )(*args)


def _rope_tables(n_tokens):
    t = jnp.arange(n_tokens)
    row = (t // GRID_W).astype(F32)
    col = (t % GRID_W).astype(F32)
    inv = ROPE_THETA ** (-jnp.arange(0, AXIS_DIM, 2, dtype=F32) / AXIS_DIM)
    ang = jnp.concatenate([row[:, None] * inv[None, :], col[:, None] * inv[None, :]], axis=-1)
    cos = jnp.repeat(jnp.cos(ang), 2, axis=-1)
    sin = jnp.repeat(jnp.sin(ang), 2, axis=-1)
    sign = jnp.where(jnp.arange(ATT_HEAD_DIM) % 2 == 0, -1.0, 1.0).astype(F32)
    return cos, sin * sign


def kernel(x_prompt, x_sample, c, state_hgrn, cache_k, cache_v, c_ctx, ada_w, ada_b, norm_pre, norm_post, rec_w_in, rec_lb_logits, rec_head_norm, pool_w, pool_scale, rec_w_out, att_w_in, att_q_norm, att_k_norm, att_w_out):
    nb_c, len_c, _ = x_prompt.shape
    nb_l, len_l, _ = x_sample.shape

    lb_p = jax.nn.softmax(rec_lb_logits.astype(F32), axis=0)
    lower_bounds = jnp.clip(jnp.cumsum(lb_p, axis=0) - lb_p[0], 0.0, 1.0)
    lower_bounds = lower_bounds.reshape(N_REC, 2, REC_HEADS, 1, REC_HEAD_DIM)
    cos, sin_signed = _rope_tables(len_l)

    cvec = jnp.zeros((MOD_ROWS, D_MODEL), F32).at[0].set(c_ctx).at[1:1 + nb_l].set(c)
    mod = _modulation(cvec, ada_w, ada_b)

    xc = x_prompt.reshape(nb_c * len_c, D_MODEL)
    xl = x_sample.reshape(nb_l * len_l, D_MODEL)
    streams = ((nb_c, len_c, 0), (nb_l, len_l, 1))
    new_states, new_k, new_v = [], [], []
    for i in range(DEPTH):
        j = i // 2
        gain_pre = norm_pre[i].reshape(1, D_MODEL)
        gain_post = norm_post[i].reshape(1, D_MODEL)
        xs = [xc, xl]
        if i % 2 == 0:
            w_in = rec_w_in[j].astype(BF16)
            w_out = rec_w_out[j].astype(BF16)
            pw = pool_w[j].astype(BF16)
            ps = pool_scale[j].reshape(1, POOL_WIDTH)
            hn = rec_head_norm[j].reshape(1, REC_HEAD_DIM)
            for s, (nb, sl, row0) in enumerate(streams):
                z = _in_proj(xs[s], mod, i, sl, row0, gain_pre, w_in)
                y_pool = _pool_mixer(z, nb, sl, 0, pw, ps)
                s0 = (state_hgrn, j) if s == 1 else None
                y_rec, st = _rec_mixer(z, nb, sl, 0, lower_bounds[j], hn, s0)
                if st is not None:
                    new_states.append(st)
                xs[s] = _out_proj(y_pool, y_rec, 0, 0, w_out, xs[s], mod, i, sl, row0, gain_post)
        else:
            w_in = att_w_in[j].astype(BF16)
            w_out = att_w_out[j].astype(BF16)
            qn = att_q_norm[j].reshape(1, ATT_HEAD_DIM)
            kn = att_k_norm[j].reshape(1, ATT_HEAD_DIM)
            for s, (nb, sl, row0) in enumerate(streams):
                z = _in_proj(xs[s], mod, i, sl, row0, gain_pre, w_in)
                if s == 0:
                    y, k_new, v_new = _attention(z, nb, sl, qn, kn)
                    new_k.append(k_new.reshape(nb, sl, ATT_KV_HEADS, ATT_HEAD_DIM))
                    new_v.append(v_new.reshape(nb, sl, ATT_KV_HEADS, ATT_HEAD_DIM))
                else:
                    (y,) = _attention(z, nb, sl, qn, kn, (cache_k, cache_v, j, cos, sin_signed))
                xs[s] = _out_proj(y, y, 0, 1, w_out, xs[s], mod, i, sl, row0, gain_post)
        xc, xl = xs

    return (xc.reshape(nb_c, len_c, D_MODEL), xl.reshape(nb_l, len_l, D_MODEL),
            jnp.stack(new_states, axis=1), jnp.stack(new_k, axis=1), jnp.stack(new_v, axis=1))
```

```python
import functools

import jax
import jax.numpy as jnp
from jax import lax
from jax.experimental import pallas as pl
from jax.experimental.pallas import tpu as pltpu

D_MODEL = 1024
DEPTH = 4
GRID_W = 64
N_REC = (DEPTH + 1) // 2
N_ATT = DEPTH // 2
POOL_WIDTH = D_MODEL // 2
POOL_WINDOWS = (2, 4, 8, 16)
POOL_GROUP_DIM = POOL_WIDTH // len(POOL_WINDOWS)
REC_WIDTH = D_MODEL // 2
REC_HEAD_DIM = 128
REC_HEADS = REC_WIDTH // REC_HEAD_DIM
REC_IN_WIDTH = 2 * POOL_WIDTH + 5 * REC_WIDTH
ATT_HEAD_DIM = 128
ATT_HEADS = D_MODEL // ATT_HEAD_DIM
ATT_KV_HEADS = 2
ATT_REP = ATT_HEADS // ATT_KV_HEADS
ATT_WIDTH = ATT_HEADS * ATT_HEAD_DIM
KV_WIDTH = ATT_KV_HEADS * ATT_HEAD_DIM
ATT_IN_WIDTH = 2 * ATT_WIDTH + 2 * KV_WIDTH
AXIS_DIM = ATT_HEAD_DIM // 2
ROPE_THETA = 10000.0
EPS = 1e-6
F_MIN = 1e-6

MOD_ROWS = 16
TOKEN_TILE = 256
IN_PROJ_STEP = 512
SCAN_CHUNK = 128
POOL_HALO = 8
ATT_Q_TILE = 128
ATT_KEY_TILE = 512
LOG2_E = 1.4426950408889634
VMEM_LIMIT = 56 * 1024 * 1024

F32 = jnp.float32
BF16 = jnp.bfloat16


def _cparams(n_axes):
    return pltpu.CompilerParams(
        dimension_semantics=("arbitrary",) * n_axes, vmem_limit_bytes=VMEM_LIMIT)


def _rms(x, g):
    return x * lax.rsqrt(jnp.mean(x * x, axis=-1, keepdims=True) + EPS) * g


def _dot(a, b):
    return jnp.dot(a, b, preferred_element_type=F32)


def _dot_nt(a, b):
    return lax.dot_general(a, b, (((1,), (1,)), ((), ())), preferred_element_type=F32)


def _mod_kernel(cv_ref, w_ref, b_ref, o_ref):
    cv = cv_ref[...]
    a = (cv * jax.nn.sigmoid(cv)).astype(BF16)
    o_ref[...] = _dot(a, w_ref[...].astype(BF16)) + b_ref[...]


def _modulation(cvec, ada_w, ada_b):
    tn = 512
    out = pl.pallas_call(
        _mod_kernel,
        grid=(DEPTH, 3 * D_MODEL // tn),
        in_specs=[
            pl.BlockSpec((MOD_ROWS, D_MODEL), lambda l, n: (0, 0)),
            pl.BlockSpec((None, D_MODEL, tn), lambda l, n: (l, 0, n)),
            pl.BlockSpec((None, 1, tn), lambda l, n: (l, 0, n)),
        ],
        out_specs=pl.BlockSpec((None, MOD_ROWS, tn), lambda l, n: (l, 0, n)),
        out_shape=jax.ShapeDtypeStruct((DEPTH, MOD_ROWS, 3 * D_MODEL), F32),
        compiler_params=_cparams(2),
        name="modulation",
    )(cvec, ada_w, ada_b.reshape(DEPTH, 1, 3 * D_MODEL))
    return out.reshape(DEPTH, MOD_ROWS, 3, 1, D_MODEL)


def _mod_spec(layer, seq_len, row0):
    tiles_per_seq = seq_len // TOKEN_TILE
    if row0 == 0:
        index = lambda i: (layer, 0, 0, 0, 0)
    else:
        index = lambda i: (layer, row0 + i // tiles_per_seq, 0, 0, 0)
    return pl.BlockSpec((None, None, 3, 1, D_MODEL), index)


def _pre_kernel(x_ref, mod_ref, g_ref, w_ref, z_ref, *zf_ref, n_out, f32_cols):
    h = _rms(x_ref[...], g_ref[...]) * (1.0 + mod_ref[1]) + mod_ref[0]
    h = h.astype(BF16)
    step = IN_PROJ_STEP
    lo, hi = f32_cols
    for n in range(n_out // step):
        c0 = n * step
        y = _dot(h, w_ref[:, c0:c0 + step])
        if lo <= c0 < hi:
            zf_ref[0][:, c0 - lo:c0 - lo + step] = y
        else:
            c1 = c0 if c0 < lo else c0 - (hi - lo)
            z_ref[:, c1:c1 + step] = y.astype(BF16)


def _in_proj(x, mod, layer, seq_len, row0, gain, w, f32_cols=(0, 0)):
    t, n_out = x.shape[0], w.shape[1]
    lo, hi = f32_cols
    assert lo % IN_PROJ_STEP == 0 and hi % IN_PROJ_STEP == 0 and n_out % IN_PROJ_STEP == 0
    out_specs = [pl.BlockSpec((TOKEN_TILE, n_out - (hi - lo)), lambda i: (i, 0))]
    out_shape = [jax.ShapeDtypeStruct((t, n_out - (hi - lo)), BF16)]
    if hi > lo:
        out_specs.append(pl.BlockSpec((TOKEN_TILE, hi - lo), lambda i: (i, 0)))
        out_shape.append(jax.ShapeDtypeStruct((t, hi - lo), F32))
    return pl.pallas_call(
        functools.partial(_pre_kernel, n_out=n_out, f32_cols=f32_cols),
        grid=(t // TOKEN_TILE,),
        in_specs=[
            pl.BlockSpec((TOKEN_TILE, D_MODEL), lambda i: (i, 0)),
            _mod_spec(layer, seq_len, row0),
            pl.BlockSpec((1, D_MODEL), lambda i: (0, 0)),
            pl.BlockSpec((D_MODEL, n_out), lambda i: (0, 0)),
        ],
        out_specs=out_specs,
        out_shape=out_shape,
        compiler_params=_cparams(1),
        name="in_proj",
    )(x, mod, gain, w)


def _post_kernel(ya_ref, yb_ref, w_ref, x_ref, mod_ref, g_ref, o_ref):
    half = D_MODEL // 2
    p = _dot(ya_ref[...], w_ref[0:half, :])
    p = p + _dot(yb_ref[...], w_ref[half:D_MODEL, :])
    o_ref[...] = x_ref[...] + mod_ref[2] * _rms(p, g_ref[...])


def _out_proj(ya, yb, ia, ib, w, x, mod, layer, seq_len, row0, gain):
    t = x.shape[0]
    half = D_MODEL // 2
    return pl.pallas_call(
        _post_kernel,
        grid=(t // TOKEN_TILE,),
        in_specs=[
            pl.BlockSpec((TOKEN_TILE, half), lambda i: (i, ia)),
            pl.BlockSpec((TOKEN_TILE, half), lambda i: (i, ib)),
            pl.BlockSpec((D_MODEL, D_MODEL), lambda i: (0, 0)),
            pl.BlockSpec((TOKEN_TILE, D_MODEL), lambda i: (i, 0)),
            _mod_spec(layer, seq_len, row0),
            pl.BlockSpec((1, D_MODEL), lambda i: (0, 0)),
        ],
        out_specs=pl.BlockSpec((TOKEN_TILE, D_MODEL), lambda i: (i, 0)),
        out_shape=jax.ShapeDtypeStruct((t, D_MODEL), F32),
        compiler_params=_cparams(1),
        name="out_proj",
    )(ya, yb, w, x, mod, gain)


def _pool_kernel(u_ref, gp_ref, pw_ref, ps_ref, o_ref, pad_ref, *, seq_len):
    gd = POOL_GROUP_DIM
    zeros = jnp.zeros((POOL_HALO, POOL_WIDTH), F32)
    pad_ref[0:POOL_HALO, :] = zeros
    pad_ref[POOL_HALO + seq_len:2 * POOL_HALO + seq_len, :] = zeros
    pad_ref[POOL_HALO:POOL_HALO + seq_len, :] = u_ref[...].astype(F32)
    rows = TOKEN_TILE
    for g, win in enumerate(POOL_WINDOWS):
        cols = slice(g * gd, (g + 1) * gd)
        for r in range(seq_len // rows):
            base = r * rows
            acc = None
            for j in range(-(win // 2), win // 2):
                start = POOL_HALO + base + j
                part = pad_ref[start:start + rows, cols]
                acc = part if acc is None else acc + part
            t = base + lax.broadcasted_iota(jnp.int32, (rows, 1), 0)
            lo = jnp.clip(t - win // 2, 0, seq_len)
            hi = jnp.clip(t + win // 2, 0, seq_len)
            mean = acc / (hi - lo).astype(F32)
            d = mean - pad_ref[POOL_HALO + base:POOL_HALO + base + rows, cols]
            y = _dot(d.astype(BF16), pw_ref[g]) * ps_ref[:, cols]
            gate = gp_ref[base:base + rows, cols].astype(F32)
            o_ref[base:base + rows, cols] = (y * (gate * jax.nn.sigmoid(gate))).astype(BF16)


def _pool_mixer(z, batch, seq_len, pool_w, pool_scale):
    return pl.pallas_call(
        functools.partial(_pool_kernel, seq_len=seq_len),
        grid=(batch,),
        in_specs=[
            pl.BlockSpec((seq_len, POOL_WIDTH), lambda b: (b, 0)),
            pl.BlockSpec((seq_len, POOL_WIDTH), lambda b: (b, 1)),
            pl.BlockSpec((len(POOL_WINDOWS), POOL_GROUP_DIM, POOL_GROUP_DIM), lambda b: (0, 0, 0)),
            pl.BlockSpec((1, POOL_WIDTH), lambda b: (0, 0)),
        ],
        out_specs=pl.BlockSpec((seq_len, POOL_WIDTH), lambda b: (b, 0)),
        out_shape=jax.ShapeDtypeStruct((batch * seq_len, POOL_WIDTH), BF16),
        scratch_shapes=[pltpu.VMEM((seq_len + 2 * POOL_HALO, POOL_WIDTH), F32)],
        compiler_params=_cparams(1),
        name="pool_mixer",
    )(z, z, pool_w, pool_scale)


def _block_row(p, block, row):
    c, w = p.shape
    p3 = p.reshape(c // block, block, w)
    return jnp.broadcast_to(p3[:, row:row + 1, :], p3.shape).reshape(c, w)


def _interleave(lo, hi, block):
    half = block // 2
    parts = []
    for b in range(SCAN_CHUNK // block):
        parts.append(lo[b * block:b * block + half])
        parts.append(hi[b * block + half:(b + 1) * block])
    return jnp.concatenate(parts, axis=0)


def _rec_kernel(*refs, seq_len, has_s0, out_state):
    q_ref, ff_ref, fb_ref, v_ref, gr_ref, lb_ref, hn_ref = refs[:7]
    rest = list(refs[7:])
    s0_ref = rest.pop(0) if has_s0 else None
    o_ref = rest.pop(0)
    st_ref = rest.pop(0) if out_state else None
    acc_ref, qs_ref, k_ref, p_ref, a_ref, qin_ref, u_ref, dec_ref, tri_ref, lvl_ref = rest

    c = SCAN_CHUNK
    hd = REC_HEAD_DIM
    n_chunks = seq_len // c
    pairs = n_chunks // 2
    zf_refs = (ff_ref, fb_ref)

    ti = lax.broadcasted_iota(jnp.int32, (c, c), 0)
    si = lax.broadcasted_iota(jnp.int32, (c, c), 1)
    x = ti ^ si
    lvl = jnp.where(x < 8, 0, jnp.where(x < 16, 1, jnp.where(x < 32, 2, jnp.where(x < 64, 3, 4))))
    for d, causal in enumerate((si <= ti, si >= ti)):
        tri_ref[d] = jnp.where(causal, 1.0, 0.0).astype(BF16)
        lvl_ref[d] = jnp.where(causal, lvl, 5)

    def chunk_rows(n):
        return pl.ds(pl.multiple_of(n * c, c), c)

    def both(stage):
        def body(i, carry):
            stage(2 * i)
            stage(2 * i + 1)
            return carry
        lax.fori_loop(0, pairs, body, 0)

    def gates(n):
        rows = chunk_rows(n)
        qz = q_ref[rows, :].astype(F32)
        qs_ref[rows, :] = qz * jax.nn.sigmoid(qz)
        for d in range(2):
            lower = lb_ref[d]
            f = jnp.clip(lower + (1.0 - lower) * jax.nn.sigmoid(zf_refs[d][rows, :]), F_MIN, 1.0)
            k_ref[d, rows, :] = 1.0 - f
            g = jnp.log(f) * LOG2_E
            g_hi = g.astype(BF16)
            rest = g - g_hi.astype(F32)
            g_mid = rest.astype(BF16)
            g_lo = (rest - g_mid.astype(F32)).astype(BF16)
            sums = _dot(tri_ref[d], jnp.concatenate([g_hi, g_mid, g_lo], axis=1))
            p_ref[d, rows, :] = (sums[:, 0:hd] + sums[:, hd:2 * hd]) + sums[:, 2 * hd:3 * hd]

    both(gates)

    def scores(n):
        rows = chunk_rows(n)
        q = qs_ref[rows, :]
        v_t = v_ref[rows, :].astype(F32).T.astype(BF16)
        for d in range(2):
            rev = d == 1
            level = lvl_ref[d]
            k = k_ref[d, rows, :]
            p = p_ref[d, rows, :]
            e0 = p - _block_row(p, 8, 4 if rev else 3)
            a = _dot_nt((q * jnp.exp2(e0)).astype(BF16), (k * jnp.exp2(-e0)).astype(BF16))
            a = jnp.where(level == 0, a, 0.0)
            for lv, block in enumerate((16, 32, 64, 128), 1):
                beta = _block_row(p, block, block // 2 if rev else block // 2 - 1)
                if rev:
                    e = _interleave(p, beta, block) - _interleave(beta, p, block)
                    src = _interleave(q, k, block)
                else:
                    e = _interleave(beta, p, block) - _interleave(p, beta, block)
                    src = _interleave(k, q, block)
                m = (src * jnp.exp2(e)).astype(BF16)
                a = jnp.where(level == lv, _dot_nt(m, m), a)
            a_ref[rows, d * c:(d + 1) * c] = a.astype(BF16)

            edge = p[0:1, :] if rev else p[c - 1:c, :]
            qin_ref[d, rows, :] = (q * jnp.exp2(p)).astype(BF16)
            u_ref[d, n] = _dot(v_t, (k * jnp.exp2(edge - p)).astype(BF16))
            dec_ref[d, n] = jnp.broadcast_to(jnp.exp2(edge), (8, hd))

    both(scores)

    def intra(n):
        rows = chunk_rows(n)
        v_b = v_ref[rows, :]
        acc_ref[rows, :] = _dot(a_ref[rows, :], jnp.concatenate([v_b, v_b], axis=0))

    both(intra)

    if has_s0:
        states = [s0_ref[0].T, s0_ref[1].T]
    else:
        states = [jnp.zeros((hd, hd), F32)] * 2
    inter = [[None] * n_chunks, [None] * n_chunks]
    for i in range(n_chunks):
        for d, n in ((0, i), (1, n_chunks - 1 - i)):
            rows = slice(n * c, (n + 1) * c)
            inter[d][n] = _dot_nt(qin_ref[d, rows, :], states[d].astype(BF16))
            states[d] = dec_ref[d, n][0:1, :] * states[d] + u_ref[d, n]
    for n in range(n_chunks):
        rows = slice(n * c, (n + 1) * c)
        gate = gr_ref[rows, :].astype(F32)
        o = acc_ref[rows, :] + inter[0][n] + inter[1][n]
        o_ref[rows, :] = (_rms(o, hn_ref[...]) * (gate * jax.nn.sigmoid(gate))).astype(BF16)
    if out_state:
        st_ref[0] = states[0].T
        st_ref[1] = states[1].T


def _rec_mixer(z, zf, batch, seq_len, lower, head_norm, s0):
    hd = REC_HEAD_DIM
    col0 = 2 * POOL_WIDTH // hd
    nh = REC_HEADS

    def zspec(part):
        return pl.BlockSpec((seq_len, hd), lambda b, h: (b, col0 + part * nh + h))

    def fspec(part):
        return pl.BlockSpec((seq_len, hd), lambda b, h: (b, part * nh + h))

    in_specs = [zspec(0), fspec(0), fspec(1), zspec(1), zspec(2),
                pl.BlockSpec((2, None, 1, hd), lambda b, h: (0, h, 0, 0)),
                pl.BlockSpec((1, hd), lambda b, h: (0, 0))]
    args = [z, zf, zf, z, z, lower, head_norm]
    has_s0 = s0 is not None
    if has_s0:
        s0_arr, j = s0
        in_specs.append(pl.BlockSpec((None, None, 2, None, hd, hd), lambda b, h: (b, j, 0, h, 0, 0)))
        args.append(s0_arr)
    out_state = not has_s0
    out_specs = [pl.BlockSpec((seq_len, hd), lambda b, h: (b, h))]
    out_shape = [jax.ShapeDtypeStruct((batch * seq_len, REC_WIDTH), BF16)]
    if out_state:
        out_specs.append(pl.BlockSpec((None, 2, None, hd, hd), lambda b, h: (b, 0, h, 0, 0)))
        out_shape.append(jax.ShapeDtypeStruct((batch, 2, nh, hd, hd), F32))
    outs = pl.pallas_call(
        functools.partial(_rec_kernel, seq_len=seq_len, has_s0=has_s0, out_state=out_state),
        grid=(batch, nh),
        in_specs=in_specs,
        out_specs=out_specs,
        out_shape=out_shape,
        scratch_shapes=[pltpu.VMEM((seq_len, hd), F32),
                        pltpu.VMEM((seq_len, hd), F32),
                        pltpu.VMEM((2, seq_len, hd), F32),
                        pltpu.VMEM((2, seq_len, hd), F32),
                        pltpu.VMEM((seq_len, 2 * SCAN_CHUNK), BF16),
                        pltpu.VMEM((2, seq_len, hd), BF16),
                        pltpu.VMEM((2, seq_len // SCAN_CHUNK, hd, hd), F32),
                        pltpu.VMEM((2, seq_len // SCAN_CHUNK, 8, hd), F32),
                        pltpu.VMEM((2, SCAN_CHUNK, SCAN_CHUNK), BF16),
                        pltpu.VMEM((2, SCAN_CHUNK, SCAN_CHUNK), jnp.int32)],
        compiler_params=_cparams(2),
        name="rec_mixer",
    )(*args)
    return outs if out_state else (outs[0], None)


def _rope(x, cos, sin_signed):
    lane = lax.broadcasted_iota(jnp.int32, x.shape, 1)
    partner = jnp.where((lane & 1) == 0, pltpu.roll(x, ATT_HEAD_DIM - 1, 1), pltpu.roll(x, 1, 1))
    return x * cos + partner * sin_signed


def _att_kernel(*refs, seq_len, n_keys, latent):
    q_ref, k_ref, v_ref, g_ref, qn_ref, kn_ref = refs[:6]
    if latent:
        ck_ref, cv_ref, cos_ref, sin_ref, y_ref, kall, vall = refs[6:-8]
    else:
        y_ref, ko_ref, vo_ref, kall, vall = refs[6:-8]
    qs_bufs, s_bufs, m_bufs, o_bufs = refs[-8:-6], refs[-6:-4], refs[-4:-2], refs[-2:]
    hd = ATT_HEAD_DIM
    tq = ATT_Q_TILE
    tk = min(ATT_KEY_TILE, n_keys)
    n_kb = n_keys // tk
    n_past = n_keys - seq_len
    n_tiles = seq_len // tq
    exp2_scale = hd ** -0.5 * LOG2_E

    k = _rms(k_ref[...].astype(F32), kn_ref[...])
    v = v_ref[...].astype(F32)
    if latent:
        k = _rope(k, cos_ref[...], sin_ref[...])
        kall[0:n_past, :] = ck_ref[...].astype(BF16)
        vall[0:n_past, 0:hd] = cv_ref[...].astype(BF16)
    else:
        ko_ref[...] = k
        vo_ref[...] = v
    kall[n_past:n_keys, :] = k.astype(BF16)
    vall[n_past:n_keys, 0:hd] = v.astype(BF16)
    lane = lax.broadcasted_iota(jnp.int32, (n_keys, hd), 1)
    vall[:, hd:2 * hd] = (lane == 0).astype(BF16)

    def tile_rows(t):
        return pl.ds(pl.multiple_of(t * tq, tq), tq)

    def prep(t, slot):
        rows = tile_rows(t)
        for h in range(ATT_REP):
            q = _rms(q_ref[rows, h * hd:(h + 1) * hd].astype(F32), qn_ref[...])
            if latent:
                q = _rope(q, cos_ref[rows, :], sin_ref[rows, :])
            qs_bufs[slot][h * tq:(h + 1) * tq, :] = q.astype(BF16)

    def scores(t, slot):
        qs = qs_bufs[slot][...]
        mx = None
        for kb in range(n_kb):
            s = _dot_nt(qs, kall[kb * tk:(kb + 1) * tk, :])
            s_bufs[slot][:, kb * tk:(kb + 1) * tk] = s
            for c in range(tk // hd):
                part = s[:, c * hd:(c + 1) * hd]
                mx = part if mx is None else jnp.maximum(mx, part)
        m = jnp.max(mx, axis=-1, keepdims=True) * exp2_scale
        m_bufs[slot][...] = jnp.broadcast_to(m, (ATT_REP * tq, hd))

    def mix(t, slot):
        m = m_bufs[slot][...]
        ps = []
        for c in range(n_keys // hd):
            s = s_bufs[slot][:, c * hd:(c + 1) * hd]
            ps.append(jnp.exp2(s * exp2_scale - m).astype(BF16))
        o_bufs[slot][...] = _dot(jnp.concatenate(ps, axis=1), vall[...])

    def finish(t, slot):
        rows = tile_rows(t)
        o = o_bufs[slot][:, 0:hd] / o_bufs[slot][:, hd:hd + 1]
        for h in range(ATT_REP):
            gate = g_ref[rows, h * hd:(h + 1) * hd].astype(F32)
            y = o[h * tq:(h + 1) * tq, :] * (gate * jax.nn.sigmoid(gate))
            y_ref[rows, h * hd:(h + 1) * hd] = y.astype(BF16)

    stages = (prep, scores, mix, finish)

    def step(k, parity):
        for a, stage in enumerate(stages):
            t = k - a
            if isinstance(t, int) and not 0 <= t < n_tiles:
                continue
            stage(t, (parity + a) % 2)

    depth = len(stages) - 1
    for k in range(min(depth, n_tiles + depth)):
        step(k, k % 2)
    n_steady = max(n_tiles - depth, 0)

    def steady_pair(j, carry):
        k = depth + 2 * j
        step(k, depth % 2)
        step(k + 1, (depth + 1) % 2)
        return carry

    lax.fori_loop(0, n_steady // 2, steady_pair, 0)
    for k in range(depth + 2 * (n_steady // 2), n_tiles + depth):
        step(k, k % 2)


def _attention(z, batch, seq_len, q_norm, k_norm, cache=None):
    hd = ATT_HEAD_DIM
    qw = ATT_REP * hd
    latent = cache is not None
    in_specs = [
        pl.BlockSpec((seq_len, qw), lambda b, g: (b, g)),
        pl.BlockSpec((seq_len, hd), lambda b, g: (b, ATT_WIDTH // hd + g)),
        pl.BlockSpec((seq_len, hd), lambda b, g: (b, (ATT_WIDTH + KV_WIDTH) // hd + g)),
        pl.BlockSpec((seq_len, qw), lambda b, g: (b, (ATT_WIDTH + 2 * KV_WIDTH) // qw + g)),
        pl.BlockSpec((1, hd), lambda b, g: (0, 0)),
        pl.BlockSpec((1, hd), lambda b, g: (0, 0)),
    ]
    args = [z, z, z, z, q_norm, k_norm]
    out_specs = [pl.BlockSpec((seq_len, qw), lambda b, g: (b, g))]
    out_shape = [jax.ShapeDtypeStruct((batch * seq_len, ATT_WIDTH), BF16)]
    n_keys = seq_len
    if latent:
        cache_k, cache_v, j, cos, sin_signed = cache
        past = cache_k.shape[2]
        n_keys += past
        ck = cache_k.reshape(batch, N_ATT, past, KV_WIDTH)
        cv = cache_v.reshape(batch, N_ATT, past, KV_WIDTH)
        cspec = pl.BlockSpec((None, None, past, hd), lambda b, g: (b, j, 0, g))
        tspec = pl.BlockSpec((seq_len, hd), lambda b, g: (0, 0))
        in_specs += [cspec, cspec, tspec, tspec]
        args += [ck, cv, cos, sin_signed]
    else:
        kv_spec = pl.BlockSpec((None, seq_len, hd), lambda b, g: (b, 0, g))
        out_specs += [kv_spec, kv_spec]
        out_shape += [jax.ShapeDtypeStruct((batch, seq_len, KV_WIDTH), F32)] * 2
    return pl.pallas_call(
        functools.partial(_att_kernel, seq_len=seq_len, n_keys=n_keys, latent=latent),
        grid=(batch, ATT_KV_HEADS),
        in_specs=in_specs,
        out_specs=out_specs,
        out_shape=out_shape,
        scratch_shapes=[pltpu.VMEM((n_keys, hd), BF16), pltpu.VMEM((n_keys, 2 * hd), BF16),
                        pltpu.VMEM((ATT_REP * ATT_Q_TILE, hd), BF16), pltpu.VMEM((ATT_REP * ATT_Q_TILE, hd), BF16),
                        pltpu.VMEM((ATT_REP * ATT_Q_TILE, n_keys), F32), pltpu.VMEM((ATT_REP * ATT_Q_TILE, n_keys), F32),
                        pltpu.VMEM((ATT_REP * ATT_Q_TILE, hd), F32), pltpu.VMEM((ATT_REP * ATT_Q_TILE, hd), F32),
                        pltpu.VMEM((ATT_REP * ATT_Q_TILE, 2 * hd), F32), pltpu.VMEM((ATT_REP * ATT_Q_TILE, 2 * hd), F32)],
        compiler_params=_cparams(2),
        name="attention",
    )(*args)


def _rope_tables(n_tokens):
    t = jnp.arange(n_tokens)
    row = (t // GRID_W).astype(F32)
    col = (t % GRID_W).astype(F32)
    inv = ROPE_THETA ** (-jnp.arange(0, AXIS_DIM, 2, dtype=F32) / AXIS_DIM)
    ang = jnp.concatenate([row[:, None] * inv[None, :], col[:, None] * inv[None, :]], axis=-1)
    cos = jnp.repeat(jnp.cos(ang), 2, axis=-1)
    sin = jnp.repeat(jnp.sin(ang), 2, axis=-1)
    sign = jnp.where(jnp.arange(ATT_HEAD_DIM) % 2 == 0, -1.0, 1.0).astype(F32)
    return cos, sin * sign


def kernel(x_prompt, x_sample, c, state_hgrn, cache_k, cache_v, c_ctx, ada_w, ada_b, norm_pre, norm_post, rec_w_in, rec_lb_logits, rec_head_norm, pool_w, pool_scale, rec_w_out, att_w_in, att_q_norm, att_k_norm, att_w_out):
    nb_c, len_c, _ = x_prompt.shape
    nb_l, len_l, _ = x_sample.shape

    lb_p = jax.nn.softmax(rec_lb_logits.astype(F32), axis=0)
    lower_bounds = jnp.clip(jnp.cumsum(lb_p, axis=0) - lb_p[0], 0.0, 1.0)
    lower_bounds = lower_bounds.reshape(N_REC, 2, REC_HEADS, 1, REC_HEAD_DIM)
    cos, sin_signed = _rope_tables(len_l)

    cvec = jnp.zeros((MOD_ROWS, D_MODEL), F32).at[0].set(c_ctx).at[1:1 + nb_l].set(c)
    mod = _modulation(cvec, ada_w, ada_b)

    xc = x_prompt.reshape(nb_c * len_c, D_MODEL)
    xl = x_sample.reshape(nb_l * len_l, D_MODEL)
    streams = ((nb_c, len_c, 0), (nb_l, len_l, 1))
    forget_cols = (2 * POOL_WIDTH + REC_WIDTH, 2 * POOL_WIDTH + 3 * REC_WIDTH)
    new_states, new_k, new_v = [], [], []
    for i in range(DEPTH):
        j = i // 2
        gain_pre = norm_pre[i].reshape(1, D_MODEL)
        gain_post = norm_post[i].reshape(1, D_MODEL)
        xs = [xc, xl]
        if i % 2 == 0:
            w_in = rec_w_in[j].astype(BF16)
            w_out = rec_w_out[j].astype(BF16)
            pw = pool_w[j].astype(BF16)
            ps = pool_scale[j].reshape(1, POOL_WIDTH)
            hn = rec_head_norm[j].reshape(1, REC_HEAD_DIM)
            for s, (nb, sl, row0) in enumerate(streams):
                z, zf = _in_proj(xs[s], mod, i, sl, row0, gain_pre, w_in, forget_cols)
                y_pool = _pool_mixer(z, nb, sl, pw, ps)
                s0 = (state_hgrn, j) if s == 1 else None
                y_rec, st = _rec_mixer(z, zf, nb, sl, lower_bounds[j], hn, s0)
                if st is not None:
                    new_states.append(st)
                xs[s] = _out_proj(y_pool, y_rec, 0, 0, w_out, xs[s], mod, i, sl, row0, gain_post)
        else:
            w_in = att_w_in[j].astype(BF16)
            w_out = att_w_out[j].astype(BF16)
            qn = att_q_norm[j].reshape(1, ATT_HEAD_DIM)
            kn = att_k_norm[j].reshape(1, ATT_HEAD_DIM)
            for s, (nb, sl, row0) in enumerate(streams):
                (z,) = _in_proj(xs[s], mod, i, sl, row0, gain_pre, w_in)
                if s == 0:
                    y, k_new, v_new = _attention(z, nb, sl, qn, kn)
                    new_k.append(k_new.reshape(nb, sl, ATT_KV_HEADS, ATT_HEAD_DIM))
                    new_v.append(v_new.reshape(nb, sl, ATT_KV_HEADS, ATT_HEAD_DIM))
                else:
                    (y,) = _attention(z, nb, sl, qn, kn, (cache_k, cache_v, j, cos, sin_signed))
                xs[s] = _out_proj(y, y, 0, 1, w_out, xs[s], mod, i, sl, row0, gain_post)
        xc, xl = xs

    return (xc.reshape(nb_c, len_c, D_MODEL), xl.reshape(nb_l, len_l, D_MODEL),
            jnp.stack(new_states, axis=1), jnp.stack(new_k, axis=1), jnp.stack(new_v, axis=1))
```

```python
import functools

import jax
import jax.numpy as jnp
from jax import lax
from jax.experimental import pallas as pl
from jax.experimental.pallas import tpu as pltpu

D_MODEL = 1024
DEPTH = 4
GRID_W = 64
N_REC = (DEPTH + 1) // 2
N_ATT = DEPTH // 2
POOL_WIDTH = D_MODEL // 2
POOL_WINDOWS = (2, 4, 8, 16)
POOL_GROUP_DIM = POOL_WIDTH // len(POOL_WINDOWS)
REC_WIDTH = D_MODEL // 2
REC_HEAD_DIM = 128
REC_HEADS = REC_WIDTH // REC_HEAD_DIM
REC_IN_WIDTH = 2 * POOL_WIDTH + 5 * REC_WIDTH
ATT_HEAD_DIM = 128
ATT_HEADS = D_MODEL // ATT_HEAD_DIM
ATT_KV_HEADS = 2
ATT_REP = ATT_HEADS // ATT_KV_HEADS
ATT_WIDTH = ATT_HEADS * ATT_HEAD_DIM
KV_WIDTH = ATT_KV_HEADS * ATT_HEAD_DIM
ATT_IN_WIDTH = 2 * ATT_WIDTH + 2 * KV_WIDTH
AXIS_DIM = ATT_HEAD_DIM // 2
ROPE_THETA = 10000.0
EPS = 1e-6
F_MIN = 1e-6

MOD_ROWS = 16
TOKEN_TILE = 512
POOL_ROWS = 256
IN_PROJ_STEP = 512
MOD_COLS = 1536
SCAN_CHUNK = 128
POOL_HALO = 8
ATT_Q_TILE = 128
ATT_KEY_TILE = 512
LOG2_E = 1.4426950408889634
VMEM_LIMIT = 56 * 1024 * 1024

F32 = jnp.float32
BF16 = jnp.bfloat16


def _cparams(n_axes):
    return pltpu.CompilerParams(
        dimension_semantics=("arbitrary",) * n_axes, vmem_limit_bytes=VMEM_LIMIT)


def _rms(x, g):
    return x * lax.rsqrt(jnp.mean(x * x, axis=-1, keepdims=True) + EPS) * g


def _dot(a, b):
    return jnp.dot(a, b, preferred_element_type=F32)


def _dot_nt(a, b):
    return lax.dot_general(a, b, (((1,), (1,)), ((), ())), preferred_element_type=F32)


def _mod_kernel(cv_ref, w_ref, b_ref, o_ref):
    cv = cv_ref[...]
    a = (cv * jax.nn.sigmoid(cv)).astype(BF16)
    o_ref[...] = _dot(a, w_ref[...].astype(BF16)) + b_ref[...]


def _modulation(cvec, ada_w, ada_b):
    tn = MOD_COLS
    out = pl.pallas_call(
        _mod_kernel,
        grid=(DEPTH, 3 * D_MODEL // tn),
        in_specs=[
            pl.BlockSpec((MOD_ROWS, D_MODEL), lambda l, n: (0, 0)),
            pl.BlockSpec((None, D_MODEL, tn), lambda l, n: (l, 0, n)),
            pl.BlockSpec((None, 1, tn), lambda l, n: (l, 0, n)),
        ],
        out_specs=pl.BlockSpec((None, MOD_ROWS, tn), lambda l, n: (l, 0, n)),
        out_shape=jax.ShapeDtypeStruct((DEPTH, MOD_ROWS, 3 * D_MODEL), F32),
        compiler_params=_cparams(2),
        name="modulation",
    )(cvec, ada_w, ada_b.reshape(DEPTH, 1, 3 * D_MODEL))
    return out.reshape(DEPTH, MOD_ROWS, 3, 1, D_MODEL)


def _mod_spec(layer, seq_len, row0):
    tiles_per_seq = seq_len // TOKEN_TILE
    if row0 == 0:
        index = lambda i: (layer, 0, 0, 0, 0)
    else:
        index = lambda i: (layer, row0 + i // tiles_per_seq, 0, 0, 0)
    return pl.BlockSpec((None, None, 3, 1, D_MODEL), index)


def _proj_kernel(*refs, post, pre, n_out, f32_cols):
    refs = list(refs)
    x = refs.pop(0)[...]
    if post:
        ya_ref, yb_ref, wo_ref, modp_ref, gpost_ref = (refs.pop(0) for _ in range(5))
    if pre:
        modn_ref, gpre_ref, wi_ref = (refs.pop(0) for _ in range(3))
    if post:
        half = D_MODEL // 2
        p = _dot(ya_ref[...], wo_ref[0:half, :]) + _dot(yb_ref[...], wo_ref[half:D_MODEL, :])
        x = x + modp_ref[2] * _rms(p, gpost_ref[...])
        refs.pop(0)[...] = x
    if pre:
        h = (_rms(x, gpre_ref[...]) * (1.0 + modn_ref[1]) + modn_ref[0]).astype(BF16)
        z_ref = refs.pop(0)
        lo, hi = f32_cols
        for c0 in range(0, n_out, IN_PROJ_STEP):
            y = _dot(h, wi_ref[:, c0:c0 + IN_PROJ_STEP])
            if lo <= c0 < hi:
                refs[0][:, c0 - lo:c0 - lo + IN_PROJ_STEP] = y
            else:
                c1 = c0 if c0 < lo else c0 - (hi - lo)
                z_ref[:, c1:c1 + IN_PROJ_STEP] = y.astype(BF16)


def _proj(x, mod, seq_len, row0, post=None, pre=None):
    t = x.shape[0]
    half = D_MODEL // 2
    row = lambda i: (i, 0)
    fixed = lambda i: (0, 0)
    in_specs = [pl.BlockSpec((TOKEN_TILE, D_MODEL), row)]
    args = [x]
    out_specs, out_shape = [], []
    n_out, f32_cols = 0, (0, 0)
    if post:
        ya, ia, yb, ib, w_out, layer, gain = post
        in_specs += [pl.BlockSpec((TOKEN_TILE, half), lambda i: (i, ia)),
                     pl.BlockSpec((TOKEN_TILE, half), lambda i: (i, ib)),
                     pl.BlockSpec((D_MODEL, D_MODEL), fixed),
                     _mod_spec(layer, seq_len, row0),
                     pl.BlockSpec((1, D_MODEL), fixed)]
        args += [ya, yb, w_out, mod, gain]
        out_specs.append(pl.BlockSpec((TOKEN_TILE, D_MODEL), row))
        out_shape.append(jax.ShapeDtypeStruct((t, D_MODEL), F32))
    if pre:
        w_in, layer, gain, f32_cols = pre
        n_out = w_in.shape[1]
        lo, hi = f32_cols
        assert lo % IN_PROJ_STEP == 0 and hi % IN_PROJ_STEP == 0 and n_out % IN_PROJ_STEP == 0
        in_specs += [_mod_spec(layer, seq_len, row0),
                     pl.BlockSpec((1, D_MODEL), fixed),
                     pl.BlockSpec((D_MODEL, n_out), fixed)]
        args += [mod, gain, w_in]
        out_specs.append(pl.BlockSpec((TOKEN_TILE, n_out - (hi - lo)), row))
        out_shape.append(jax.ShapeDtypeStruct((t, n_out - (hi - lo)), BF16))
        if hi > lo:
            out_specs.append(pl.BlockSpec((TOKEN_TILE, hi - lo), row))
            out_shape.append(jax.ShapeDtypeStruct((t, hi - lo), F32))
    return pl.pallas_call(
        functools.partial(_proj_kernel, post=bool(post), pre=bool(pre), n_out=n_out, f32_cols=f32_cols),
        grid=(t // TOKEN_TILE,),
        in_specs=in_specs,
        out_specs=out_specs,
        out_shape=out_shape,
        compiler_params=_cparams(1),
        name="proj",
    )(*args)


def _pool_kernel(u_ref, gp_ref, pw_ref, ps_ref, o_ref, pad_ref, *, seq_len):
    gd = POOL_GROUP_DIM
    zeros = jnp.zeros((POOL_HALO, POOL_WIDTH), F32)
    pad_ref[0:POOL_HALO, :] = zeros
    pad_ref[POOL_HALO + seq_len:2 * POOL_HALO + seq_len, :] = zeros
    pad_ref[POOL_HALO:POOL_HALO + seq_len, :] = u_ref[...].astype(F32)
    rows = POOL_ROWS
    for g, win in enumerate(POOL_WINDOWS):
        cols = slice(g * gd, (g + 1) * gd)
        for r in range(seq_len // rows):
            base = r * rows
            acc = None
            for j in range(-(win // 2), win // 2):
                start = POOL_HALO + base + j
                part = pad_ref[start:start + rows, cols]
                acc = part if acc is None else acc + part
            t = base + lax.broadcasted_iota(jnp.int32, (rows, 1), 0)
            lo = jnp.clip(t - win // 2, 0, seq_len)
            hi = jnp.clip(t + win // 2, 0, seq_len)
            mean = acc / (hi - lo).astype(F32)
            d = mean - pad_ref[POOL_HALO + base:POOL_HALO + base + rows, cols]
            y = _dot(d.astype(BF16), pw_ref[g]) * ps_ref[:, cols]
            gate = gp_ref[base:base + rows, cols].astype(F32)
            o_ref[base:base + rows, cols] = (y * (gate * jax.nn.sigmoid(gate))).astype(BF16)


def _pool_mixer(z, batch, seq_len, pool_w, pool_scale):
    return pl.pallas_call(
        functools.partial(_pool_kernel, seq_len=seq_len),
        grid=(batch,),
        in_specs=[
            pl.BlockSpec((seq_len, POOL_WIDTH), lambda b: (b, 0)),
            pl.BlockSpec((seq_len, POOL_WIDTH), lambda b: (b, 1)),
            pl.BlockSpec((len(POOL_WINDOWS), POOL_GROUP_DIM, POOL_GROUP_DIM), lambda b: (0, 0, 0)),
            pl.BlockSpec((1, POOL_WIDTH), lambda b: (0, 0)),
        ],
        out_specs=pl.BlockSpec((seq_len, POOL_WIDTH), lambda b: (b, 0)),
        out_shape=jax.ShapeDtypeStruct((batch * seq_len, POOL_WIDTH), BF16),
        scratch_shapes=[pltpu.VMEM((seq_len + 2 * POOL_HALO, POOL_WIDTH), F32)],
        compiler_params=_cparams(1),
        name="pool_mixer",
    )(z, z, pool_w, pool_scale)


def _block_row(p, block, row):
    c, w = p.shape
    p3 = p.reshape(c // block, block, w)
    return jnp.broadcast_to(p3[:, row:row + 1, :], p3.shape).reshape(c, w)


def _interleave(lo, hi, block):
    half = block // 2
    parts = []
    for b in range(SCAN_CHUNK // block):
        parts.append(lo[b * block:b * block + half])
        parts.append(hi[b * block + half:(b + 1) * block])
    return jnp.concatenate(parts, axis=0)


def _rec_kernel(*refs, seq_len, has_s0, out_state):
    q_ref, ff_ref, fb_ref, v_ref, gr_ref, lb_ref, hn_ref = refs[:7]
    rest = list(refs[7:])
    s0_ref = rest.pop(0) if has_s0 else None
    o_ref = rest.pop(0)
    st_ref = rest.pop(0) if out_state else None
    acc_ref, qs_ref, k_ref, p_ref, a_ref, qin_ref, u_ref, dec_ref, tri_ref, lvl_ref = rest

    c = SCAN_CHUNK
    hd = REC_HEAD_DIM
    n_chunks = seq_len // c
    pairs = n_chunks // 2
    zf_refs = (ff_ref, fb_ref)

    ti = lax.broadcasted_iota(jnp.int32, (c, c), 0)
    si = lax.broadcasted_iota(jnp.int32, (c, c), 1)
    x = ti ^ si
    lvl = jnp.where(x < 8, 0, jnp.where(x < 16, 1, jnp.where(x < 32, 2, jnp.where(x < 64, 3, 4))))
    for d, causal in enumerate((si <= ti, si >= ti)):
        tri_ref[d] = jnp.where(causal, 1.0, 0.0).astype(BF16)
        lvl_ref[d] = jnp.where(causal, lvl, 5)

    def chunk_rows(n):
        return pl.ds(pl.multiple_of(n * c, c), c)

    def both(stage):
        def body(i, carry):
            stage(2 * i)
            stage(2 * i + 1)
            return carry
        lax.fori_loop(0, pairs, body, 0)

    def gates(n):
        rows = chunk_rows(n)
        qz = q_ref[rows, :].astype(F32)
        qs_ref[rows, :] = qz * jax.nn.sigmoid(qz)
        for d in range(2):
            lower = lb_ref[d]
            f = jnp.clip(lower + (1.0 - lower) * jax.nn.sigmoid(zf_refs[d][rows, :]), F_MIN, 1.0)
            k_ref[d, rows, :] = 1.0 - f
            g = jnp.log(f) * LOG2_E
            g_hi = g.astype(BF16)
            rest = g - g_hi.astype(F32)
            g_mid = rest.astype(BF16)
            g_lo = (rest - g_mid.astype(F32)).astype(BF16)
            sums = _dot(tri_ref[d], jnp.concatenate([g_hi, g_mid, g_lo], axis=1))
            p_ref[d, rows, :] = (sums[:, 0:hd] + sums[:, hd:2 * hd]) + sums[:, 2 * hd:3 * hd]

    both(gates)

    def scores(n):
        rows = chunk_rows(n)
        q = qs_ref[rows, :]
        v_t = v_ref[rows, :].astype(F32).T.astype(BF16)
        for d in range(2):
            rev = d == 1
            level = lvl_ref[d]
            k = k_ref[d, rows, :]
            p = p_ref[d, rows, :]
            e0 = p - _block_row(p, 8, 4 if rev else 3)
            a = _dot_nt((q * jnp.exp2(e0)).astype(BF16), (k * jnp.exp2(-e0)).astype(BF16))
            a = jnp.where(level == 0, a, 0.0)
            for lv, block in enumerate((16, 32, 64, 128), 1):
                beta = _block_row(p, block, block // 2 if rev else block // 2 - 1)
                if rev:
                    e = _interleave(p, beta, block) - _interleave(beta, p, block)
                    src = _interleave(q, k, block)
                else:
                    e = _interleave(beta, p, block) - _interleave(p, beta, block)
                    src = _interleave(k, q, block)
                m = (src * jnp.exp2(e)).astype(BF16)
                a = jnp.where(level == lv, _dot_nt(m, m), a)
            a_ref[rows, d * c:(d + 1) * c] = a.astype(BF16)

            edge = p[0:1, :] if rev else p[c - 1:c, :]
            qin_ref[d, rows, :] = (q * jnp.exp2(p)).astype(BF16)
            u_ref[d, n] = _dot(v_t, (k * jnp.exp2(edge - p)).astype(BF16))
            dec_ref[d, n] = jnp.broadcast_to(jnp.exp2(edge), (8, hd))

    both(scores)

    def intra(n):
        rows = chunk_rows(n)
        v_b = v_ref[rows, :]
        acc_ref[rows, :] = _dot(a_ref[rows, :], jnp.concatenate([v_b, v_b], axis=0))

    both(intra)

    if has_s0:
        states = [s0_ref[0].T, s0_ref[1].T]
    else:
        states = [jnp.zeros((hd, hd), F32)] * 2
    inter = [[None] * n_chunks, [None] * n_chunks]
    for i in range(n_chunks):
        for d, n in ((0, i), (1, n_chunks - 1 - i)):
            rows = slice(n * c, (n + 1) * c)
            inter[d][n] = _dot_nt(qin_ref[d, rows, :], states[d].astype(BF16))
            states[d] = dec_ref[d, n][0:1, :] * states[d] + u_ref[d, n]
    for n in range(n_chunks):
        rows = slice(n * c, (n + 1) * c)
        gate = gr_ref[rows, :].astype(F32)
        o = acc_ref[rows, :] + inter[0][n] + inter[1][n]
        o_ref[rows, :] = (_rms(o, hn_ref[...]) * (gate * jax.nn.sigmoid(gate))).astype(BF16)
    if out_state:
        st_ref[0] = states[0].T
        st_ref[1] = states[1].T


def _rec_mixer(z, zf, batch, seq_len, lower, head_norm, s0):
    hd = REC_HEAD_DIM
    col0 = 2 * POOL_WIDTH // hd
    nh = REC_HEADS

    def zspec(part):
        return pl.BlockSpec((seq_len, hd), lambda b, h: (b, col0 + part * nh + h))

    def fspec(part):
        return pl.BlockSpec((seq_len, hd), lambda b, h: (b, part * nh + h))

    in_specs = [zspec(0), fspec(0), fspec(1), zspec(1), zspec(2),
                pl.BlockSpec((2, None, 1, hd), lambda b, h: (0, h, 0, 0)),
                pl.BlockSpec((1, hd), lambda b, h: (0, 0))]
    args = [z, zf, zf, z, z, lower, head_norm]
    has_s0 = s0 is not None
    if has_s0:
        s0_arr, j = s0
        in_specs.append(pl.BlockSpec((None, None, 2, None, hd, hd), lambda b, h: (b, j, 0, h, 0, 0)))
        args.append(s0_arr)
    out_state = not has_s0
    out_specs = [pl.BlockSpec((seq_len, hd), lambda b, h: (b, h))]
    out_shape = [jax.ShapeDtypeStruct((batch * seq_len, REC_WIDTH), BF16)]
    if out_state:
        out_specs.append(pl.BlockSpec((None, 2, None, hd, hd), lambda b, h: (b, 0, h, 0, 0)))
        out_shape.append(jax.ShapeDtypeStruct((batch, 2, nh, hd, hd), F32))
    outs = pl.pallas_call(
        functools.partial(_rec_kernel, seq_len=seq_len, has_s0=has_s0, out_state=out_state),
        grid=(batch, nh),
        in_specs=in_specs,
        out_specs=out_specs,
        out_shape=out_shape,
        scratch_shapes=[pltpu.VMEM((seq_len, hd), F32),
                        pltpu.VMEM((seq_len, hd), F32),
                        pltpu.VMEM((2, seq_len, hd), F32),
                        pltpu.VMEM((2, seq_len, hd), F32),
                        pltpu.VMEM((seq_len, 2 * SCAN_CHUNK), BF16),
                        pltpu.VMEM((2, seq_len, hd), BF16),
                        pltpu.VMEM((2, seq_len // SCAN_CHUNK, hd, hd), F32),
                        pltpu.VMEM((2, seq_len // SCAN_CHUNK, 8, hd), F32),
                        pltpu.VMEM((2, SCAN_CHUNK, SCAN_CHUNK), BF16),
                        pltpu.VMEM((2, SCAN_CHUNK, SCAN_CHUNK), jnp.int32)],
        compiler_params=_cparams(2),
        name="rec_mixer",
    )(*args)
    return outs if out_state else (outs[0], None)


def _rope(x, cos, sin_signed):
    lane = lax.broadcasted_iota(jnp.int32, x.shape, 1)
    partner = jnp.where((lane & 1) == 0, pltpu.roll(x, ATT_HEAD_DIM - 1, 1), pltpu.roll(x, 1, 1))
    return x * cos + partner * sin_signed


def _att_kernel(*refs, seq_len, n_keys, latent):
    q_ref, k_ref, v_ref, g_ref, qn_ref, kn_ref = refs[:6]
    if latent:
        ck_ref, cv_ref, cos_ref, sin_ref, y_ref, kall, vall = refs[6:-8]
    else:
        y_ref, ko_ref, vo_ref, kall, vall = refs[6:-8]
    qs_bufs, s_bufs, m_bufs, o_bufs = refs[-8:-6], refs[-6:-4], refs[-4:-2], refs[-2:]
    hd = ATT_HEAD_DIM
    tq = ATT_Q_TILE
    tk = min(ATT_KEY_TILE, n_keys)
    n_kb = n_keys // tk
    n_past = n_keys - seq_len
    n_tiles = seq_len // tq
    exp2_scale = hd ** -0.5 * LOG2_E

    k = _rms(k_ref[...].astype(F32), kn_ref[...])
    v = v_ref[...].astype(F32)
    if latent:
        k = _rope(k, cos_ref[...], sin_ref[...])
        kall[0:n_past, :] = ck_ref[...].astype(BF16)
        vall[0:n_past, 0:hd] = cv_ref[...].astype(BF16)
    else:
        ko_ref[...] = k
        vo_ref[...] = v
    kall[n_past:n_keys, :] = k.astype(BF16)
    vall[n_past:n_keys, 0:hd] = v.astype(BF16)
    lane = lax.broadcasted_iota(jnp.int32, (n_keys, hd), 1)
    vall[:, hd:2 * hd] = (lane == 0).astype(BF16)

    def tile_rows(t):
        return pl.ds(pl.multiple_of(t * tq, tq), tq)

    def prep(t, slot):
        rows = tile_rows(t)
        for h in range(ATT_REP):
            q = _rms(q_ref[rows, h * hd:(h + 1) * hd].astype(F32), qn_ref[...])
            if latent:
                q = _rope(q, cos_ref[rows, :], sin_ref[rows, :])
            qs_bufs[slot][h * tq:(h + 1) * tq, :] = q.astype(BF16)

    def scores(t, slot):
        qs = qs_bufs[slot][...]
        mx = None
        for kb in range(n_kb):
            s = _dot_nt(qs, kall[kb * tk:(kb + 1) * tk, :])
            s_bufs[slot][:, kb * tk:(kb + 1) * tk] = s
            for c in range(tk // hd):
                part = s[:, c * hd:(c + 1) * hd]
                mx = part if mx is None else jnp.maximum(mx, part)
        m = jnp.max(mx, axis=-1, keepdims=True) * exp2_scale
        m_bufs[slot][...] = jnp.broadcast_to(m, (ATT_REP * tq, hd))

    def mix(t, slot):
        m = m_bufs[slot][...]
        ps = []
        for c in range(n_keys // hd):
            s = s_bufs[slot][:, c * hd:(c + 1) * hd]
            ps.append(jnp.exp2(s * exp2_scale - m).astype(BF16))
        o_bufs[slot][...] = _dot(jnp.concatenate(ps, axis=1), vall[...])

    def finish(t, slot):
        rows = tile_rows(t)
        o = o_bufs[slot][:, 0:hd] / o_bufs[slot][:, hd:hd + 1]
        for h in range(ATT_REP):
            gate = g_ref[rows, h * hd:(h + 1) * hd].astype(F32)
            y = o[h * tq:(h + 1) * tq, :] * (gate * jax.nn.sigmoid(gate))
            y_ref[rows, h * hd:(h + 1) * hd] = y.astype(BF16)

    stages = (prep, scores, mix, finish)

    def step(k, parity):
        for a, stage in enumerate(stages):
            t = k - a
            if isinstance(t, int) and not 0 <= t < n_tiles:
                continue
            stage(t, (parity + a) % 2)

    depth = len(stages) - 1
    for k in range(min(depth, n_tiles + depth)):
        step(k, k % 2)
    n_steady = max(n_tiles - depth, 0)

    def steady_pair(j, carry):
        k = depth + 2 * j
        step(k, depth % 2)
        step(k + 1, (depth + 1) % 2)
        return carry

    lax.fori_loop(0, n_steady // 2, steady_pair, 0)
    for k in range(depth + 2 * (n_steady // 2), n_tiles + depth):
        step(k, k % 2)


def _attention(z, batch, seq_len, q_norm, k_norm, cache=None):
    hd = ATT_HEAD_DIM
    qw = ATT_REP * hd
    latent = cache is not None
    in_specs = [
        pl.BlockSpec((seq_len, qw), lambda b, g: (b, g)),
        pl.BlockSpec((seq_len, hd), lambda b, g: (b, ATT_WIDTH // hd + g)),
        pl.BlockSpec((seq_len, hd), lambda b, g: (b, (ATT_WIDTH + KV_WIDTH) // hd + g)),
        pl.BlockSpec((seq_len, qw), lambda b, g: (b, (ATT_WIDTH + 2 * KV_WIDTH) // qw + g)),
        pl.BlockSpec((1, hd), lambda b, g: (0, 0)),
        pl.BlockSpec((1, hd), lambda b, g: (0, 0)),
    ]
    args = [z, z, z, z, q_norm, k_norm]
    out_specs = [pl.BlockSpec((seq_len, qw), lambda b, g: (b, g))]
    out_shape = [jax.ShapeDtypeStruct((batch * seq_len, ATT_WIDTH), BF16)]
    n_keys = seq_len
    if latent:
        cache_k, cache_v, j, cos, sin_signed = cache
        past = cache_k.shape[2]
        n_keys += past
        ck = cache_k.reshape(batch, N_ATT, past, KV_WIDTH)
        cv = cache_v.reshape(batch, N_ATT, past, KV_WIDTH)
        cspec = pl.BlockSpec((None, None, past, hd), lambda b, g: (b, j, 0, g))
        tspec = pl.BlockSpec((seq_len, hd), lambda b, g: (0, 0))
        in_specs += [cspec, cspec, tspec, tspec]
        args += [ck, cv, cos, sin_signed]
    else:
        kv_spec = pl.BlockSpec((None, seq_len, hd), lambda b, g: (b, 0, g))
        out_specs += [kv_spec, kv_spec]
        out_shape += [jax.ShapeDtypeStruct((batch, seq_len, KV_WIDTH), F32)] * 2
    return pl.pallas_call(
        functools.partial(_att_kernel, seq_len=seq_len, n_keys=n_keys, latent=latent),
        grid=(batch, ATT_KV_HEADS),
        in_specs=in_specs,
        out_specs=out_specs,
        out_shape=out_shape,
        scratch_shapes=[pltpu.VMEM((n_keys, hd), BF16), pltpu.VMEM((n_keys, 2 * hd), BF16),
                        pltpu.VMEM((ATT_REP * ATT_Q_TILE, hd), BF16), pltpu.VMEM((ATT_REP * ATT_Q_TILE, hd), BF16),
                        pltpu.VMEM((ATT_REP * ATT_Q_TILE, n_keys), F32), pltpu.VMEM((ATT_REP * ATT_Q_TILE, n_keys), F32),
                        pltpu.VMEM((ATT_REP * ATT_Q_TILE, hd), F32), pltpu.VMEM((ATT_REP * ATT_Q_TILE, hd), F32),
                        pltpu.VMEM((ATT_REP * ATT_Q_TILE, 2 * hd), F32), pltpu.VMEM((ATT_REP * ATT_Q_TILE, 2 * hd), F32)],
        compiler_params=_cparams(2),
        name="attention",
    )(*args)


def _rope_tables(n_tokens):
    t = jnp.arange(n_tokens)
    row = (t // GRID_W).astype(F32)
    col = (t % GRID_W).astype(F32)
    inv = ROPE_THETA ** (-jnp.arange(0, AXIS_DIM, 2, dtype=F32) / AXIS_DIM)
    ang = jnp.concatenate([row[:, None] * inv[None, :], col[:, None] * inv[None, :]], axis=-1)
    cos = jnp.repeat(jnp.cos(ang), 2, axis=-1)
    sin = jnp.repeat(jnp.sin(ang), 2, axis=-1)
    sign = jnp.where(jnp.arange(ATT_HEAD_DIM) % 2 == 0, -1.0, 1.0).astype(F32)
    return cos, sin * sign


def kernel(x_prompt, x_sample, c, state_hgrn, cache_k, cache_v, c_ctx, ada_w, ada_b, norm_pre, norm_post, rec_w_in, rec_lb_logits, rec_head_norm, pool_w, pool_scale, rec_w_out, att_w_in, att_q_norm, att_k_norm, att_w_out):
    nb_c, len_c, _ = x_prompt.shape
    nb_l, len_l, _ = x_sample.shape

    lb_p = jax.nn.softmax(rec_lb_logits.astype(F32), axis=0)
    lower_bounds = jnp.clip(jnp.cumsum(lb_p, axis=0) - lb_p[0], 0.0, 1.0)
    lower_bounds = lower_bounds.reshape(N_REC, 2, REC_HEADS, 1, REC_HEAD_DIM)
    cos, sin_signed = _rope_tables(len_l)

    cvec = jnp.zeros((MOD_ROWS, D_MODEL), F32).at[0].set(c_ctx).at[1:1 + nb_l].set(c)
    mod = _modulation(cvec, ada_w, ada_b)

    forget_cols = (2 * POOL_WIDTH + REC_WIDTH, 2 * POOL_WIDTH + 3 * REC_WIDTH)
    layers = []
    for i in range(DEPTH):
        j = i // 2
        rec = i % 2 == 0
        w_in, w_out = (rec_w_in, rec_w_out) if rec else (att_w_in, att_w_out)
        layers.append(dict(
            w_in=w_in[j].astype(BF16), w_out=w_out[j].astype(BF16), f32_cols=forget_cols if rec else (0, 0),
            gain_pre=norm_pre[i].reshape(1, D_MODEL), gain_post=norm_post[i].reshape(1, D_MODEL)))

    def pre_args(i):
        return (layers[i]["w_in"], i, layers[i]["gain_pre"], layers[i]["f32_cols"])

    xs = [x_prompt.reshape(nb_c * len_c, D_MODEL), x_sample.reshape(nb_l * len_l, D_MODEL)]
    streams = ((nb_c, len_c, 0), (nb_l, len_l, 1))
    new_states, new_k, new_v = [], [], []
    for s, (nb, sl, row0) in enumerate(streams):
        x = xs[s]
        zs = _proj(x, mod, sl, row0, pre=pre_args(0))
        for i in range(DEPTH):
            j = i // 2
            if i % 2 == 0:
                z, zf = zs
                pw = pool_w[j].astype(BF16)
                ps = pool_scale[j].reshape(1, POOL_WIDTH)
                hn = rec_head_norm[j].reshape(1, REC_HEAD_DIM)
                y_pool = _pool_mixer(z, nb, sl, pw, ps)
                s0 = (state_hgrn, j) if s == 1 else None
                y_rec, st = _rec_mixer(z, zf, nb, sl, lower_bounds[j], hn, s0)
                if st is not None:
                    new_states.append(st)
                halves = (y_pool, 0, y_rec, 0)
            else:
                (z,) = zs
                qn = att_q_norm[j].reshape(1, ATT_HEAD_DIM)
                kn = att_k_norm[j].reshape(1, ATT_HEAD_DIM)
                if s == 0:
                    y, k_new, v_new = _attention(z, nb, sl, qn, kn)
                    new_k.append(k_new.reshape(nb, sl, ATT_KV_HEADS, ATT_HEAD_DIM))
                    new_v.append(v_new.reshape(nb, sl, ATT_KV_HEADS, ATT_HEAD_DIM))
                else:
                    (y,) = _attention(z, nb, sl, qn, kn, (cache_k, cache_v, j, cos, sin_signed))
                halves = (y, 0, y, 1)
            post = halves + (layers[i]["w_out"], i, layers[i]["gain_post"])
            x, *zs = _proj(x, mod, sl, row0, post=post, pre=pre_args(i + 1) if i + 1 < DEPTH else None)
        xs[s] = x
    xc, xl = xs

    return (xc.reshape(nb_c, len_c, D_MODEL), xl.reshape(nb_l, len_l, D_MODEL),
            jnp.stack(new_states, axis=1), jnp.stack(new_k, axis=1), jnp.stack(new_v, axis=1))
```

```python
import functools

import jax
import jax.numpy as jnp
from jax import lax
from jax.experimental import pallas as pl
from jax.experimental.pallas import tpu as pltpu

D_MODEL = 1024
DEPTH = 4
GRID_W = 64
N_REC = (DEPTH + 1) // 2
N_ATT = DEPTH // 2
POOL_WIDTH = D_MODEL // 2
POOL_WINDOWS = (2, 4, 8, 16)
POOL_GROUP_DIM = POOL_WIDTH // len(POOL_WINDOWS)
REC_WIDTH = D_MODEL // 2
REC_HEAD_DIM = 128
REC_HEADS = REC_WIDTH // REC_HEAD_DIM
REC_IN_WIDTH = 2 * POOL_WIDTH + 5 * REC_WIDTH
ATT_HEAD_DIM = 128
ATT_HEADS = D_MODEL // ATT_HEAD_DIM
ATT_KV_HEADS = 2
ATT_REP = ATT_HEADS // ATT_KV_HEADS
ATT_WIDTH = ATT_HEADS * ATT_HEAD_DIM
KV_WIDTH = ATT_KV_HEADS * ATT_HEAD_DIM
ATT_IN_WIDTH = 2 * ATT_WIDTH + 2 * KV_WIDTH
AXIS_DIM = ATT_HEAD_DIM // 2
ROPE_THETA = 10000.0
EPS = 1e-6
F_MIN = 1e-6

MOD_ROWS = 16
TOKEN_TILE = 1024
MIN_PROJ_STEPS = 8
PROJ_ROW_GROUPS = 2
POOL_ROWS = 256
IN_PROJ_STEP = 512
MOD_COLS = 1536
SCAN_CHUNK = 128
SCAN_UNITS = 8
POOL_HALO = 8
ATT_Q_TILE = 128
ATT_KEY_TILE = 512
LOG2_E = 1.4426950408889634
VMEM_LIMIT = 56 * 1024 * 1024

F32 = jnp.float32
BF16 = jnp.bfloat16


def _cparams(n_axes):
    return pltpu.CompilerParams(
        dimension_semantics=("arbitrary",) * n_axes, vmem_limit_bytes=VMEM_LIMIT)


def _rms(x, g):
    return x * lax.rsqrt(jnp.mean(x * x, axis=-1, keepdims=True) + EPS) * g


def _dot(a, b):
    return jnp.dot(a, b, preferred_element_type=F32)


def _dot_nt(a, b):
    return lax.dot_general(a, b, (((1,), (1,)), ((), ())), preferred_element_type=F32)


def _mod_kernel(cv_ref, w_ref, b_ref, o_ref):
    cv = cv_ref[...]
    a = (cv * jax.nn.sigmoid(cv)).astype(BF16)
    o_ref[...] = _dot(a, w_ref[...].astype(BF16)) + b_ref[...]


def _modulation(cvec, ada_w, ada_b):
    tn = MOD_COLS
    out = pl.pallas_call(
        _mod_kernel,
        grid=(DEPTH, 3 * D_MODEL // tn),
        in_specs=[
            pl.BlockSpec((MOD_ROWS, D_MODEL), lambda l, n: (0, 0)),
            pl.BlockSpec((None, D_MODEL, tn), lambda l, n: (l, 0, n)),
            pl.BlockSpec((None, 1, tn), lambda l, n: (l, 0, n)),
        ],
        out_specs=pl.BlockSpec((None, MOD_ROWS, tn), lambda l, n: (l, 0, n)),
        out_shape=jax.ShapeDtypeStruct((DEPTH, MOD_ROWS, 3 * D_MODEL), F32),
        compiler_params=_cparams(2),
        name="modulation",
    )(cvec, ada_w, ada_b.reshape(DEPTH, 1, 3 * D_MODEL))
    return out.reshape(DEPTH, MOD_ROWS, 3, 1, D_MODEL)


def _mod_spec(layer, seq_len, row0, tile):
    assert row0 == 0 or seq_len % tile == 0
    tiles_per_seq = seq_len // tile
    if row0 == 0:
        index = lambda i: (layer, 0, 0, 0, 0)
    else:
        index = lambda i: (layer, row0 + i // tiles_per_seq, 0, 0, 0)
    return pl.BlockSpec((None, None, 3, 1, D_MODEL), index)


def _proj_kernel(*refs, post, pre, n_out, f32_cols):
    refs = list(refs)
    x_ref = refs.pop(0)
    if post:
        ya_ref, yb_ref, wo_ref, modp_ref, gpost_ref = (refs.pop(0) for _ in range(5))
    if pre:
        modn_ref, gpre_ref, wi_ref = (refs.pop(0) for _ in range(3))
    xo_ref = refs.pop(0) if post else None
    rows_all = x_ref.shape[0]
    for r0 in range(0, rows_all, rows_all // PROJ_ROW_GROUPS):
        rows = slice(r0, r0 + rows_all // PROJ_ROW_GROUPS)
        x = x_ref[rows, :]
        if post:
            half = D_MODEL // 2
            p = _dot(ya_ref[rows, :], wo_ref[0:half, :]) + _dot(yb_ref[rows, :], wo_ref[half:D_MODEL, :])
            x = x + modp_ref[2] * _rms(p, gpost_ref[...])
            xo_ref[rows, :] = x
        if pre:
            h = (_rms(x, gpre_ref[...]) * (1.0 + modn_ref[1]) + modn_ref[0]).astype(BF16)
            lo, hi = f32_cols
            for c0 in range(0, n_out, IN_PROJ_STEP):
                y = _dot(h, wi_ref[:, c0:c0 + IN_PROJ_STEP])
                if lo <= c0 < hi:
                    refs[1][rows, c0 - lo:c0 - lo + IN_PROJ_STEP] = y
                else:
                    c1 = c0 if c0 < lo else c0 - (hi - lo)
                    refs[0][rows, c1:c1 + IN_PROJ_STEP] = y.astype(BF16)


def _proj(x, mod, seq_len, row0, post=None, pre=None):
    t = x.shape[0]
    tile = min(TOKEN_TILE, t // MIN_PROJ_STEPS)
    half = D_MODEL // 2
    row = lambda i: (i, 0)
    fixed = lambda i: (0, 0)
    resident = dict(index_map=fixed, pipeline_mode=pl.Buffered(1))
    in_specs = [pl.BlockSpec((tile, D_MODEL), row)]
    args = [x]
    out_specs, out_shape = [], []
    n_out, f32_cols = 0, (0, 0)
    if post:
        ya, ia, yb, ib, w_out, layer, gain = post
        in_specs += [pl.BlockSpec((tile, half), lambda i: (i, ia)),
                     pl.BlockSpec((tile, half), lambda i: (i, ib)),
                     pl.BlockSpec((D_MODEL, D_MODEL), **resident),
                     _mod_spec(layer, seq_len, row0, tile),
                     pl.BlockSpec((1, D_MODEL), fixed)]
        args += [ya, yb, w_out, mod, gain]
        out_specs.append(pl.BlockSpec((tile, D_MODEL), row))
        out_shape.append(jax.ShapeDtypeStruct((t, D_MODEL), F32))
    if pre:
        w_in, layer, gain, f32_cols = pre
        n_out = w_in.shape[1]
        lo, hi = f32_cols
        assert lo % IN_PROJ_STEP == 0 and hi % IN_PROJ_STEP == 0 and n_out % IN_PROJ_STEP == 0
        in_specs += [_mod_spec(layer, seq_len, row0, tile),
                     pl.BlockSpec((1, D_MODEL), fixed),
                     pl.BlockSpec((D_MODEL, n_out), **resident)]
        args += [mod, gain, w_in]
        out_specs.append(pl.BlockSpec((tile, n_out - (hi - lo)), row))
        out_shape.append(jax.ShapeDtypeStruct((t, n_out - (hi - lo)), BF16))
        if hi > lo:
            out_specs.append(pl.BlockSpec((tile, hi - lo), row))
            out_shape.append(jax.ShapeDtypeStruct((t, hi - lo), F32))
    return pl.pallas_call(
        functools.partial(_proj_kernel, post=bool(post), pre=bool(pre), n_out=n_out, f32_cols=f32_cols),
        grid=(t // tile,),
        in_specs=in_specs,
        out_specs=out_specs,
        out_shape=out_shape,
        compiler_params=_cparams(1),
        name="proj",
    )(*args)


def _pool_kernel(u_ref, gp_ref, pw_ref, ps_ref, o_ref, pad_ref, *, seq_len):
    gd = POOL_GROUP_DIM
    zeros = jnp.zeros((POOL_HALO, POOL_WIDTH), F32)
    pad_ref[0:POOL_HALO, :] = zeros
    pad_ref[POOL_HALO + seq_len:2 * POOL_HALO + seq_len, :] = zeros
    pad_ref[POOL_HALO:POOL_HALO + seq_len, :] = u_ref[...].astype(F32)
    rows = POOL_ROWS
    for g, win in enumerate(POOL_WINDOWS):
        cols = slice(g * gd, (g + 1) * gd)
        for r in range(seq_len // rows):
            base = r * rows
            acc = None
            for j in range(-(win // 2), win // 2):
                start = POOL_HALO + base + j
                part = pad_ref[start:start + rows, cols]
                acc = part if acc is None else acc + part
            t = base + lax.broadcasted_iota(jnp.int32, (rows, 1), 0)
            lo = jnp.clip(t - win // 2, 0, seq_len)
            hi = jnp.clip(t + win // 2, 0, seq_len)
            mean = acc / (hi - lo).astype(F32)
            d = mean - pad_ref[POOL_HALO + base:POOL_HALO + base + rows, cols]
            y = _dot(d.astype(BF16), pw_ref[g]) * ps_ref[:, cols]
            gate = gp_ref[base:base + rows, cols].astype(F32)
            o_ref[base:base + rows, cols] = (y * (gate * jax.nn.sigmoid(gate))).astype(BF16)


def _pool_mixer(z, batch, seq_len, pool_w, pool_scale):
    return pl.pallas_call(
        functools.partial(_pool_kernel, seq_len=seq_len),
        grid=(batch,),
        in_specs=[
            pl.BlockSpec((seq_len, POOL_WIDTH), lambda b: (b, 0)),
            pl.BlockSpec((seq_len, POOL_WIDTH), lambda b: (b, 1)),
            pl.BlockSpec((len(POOL_WINDOWS), POOL_GROUP_DIM, POOL_GROUP_DIM), lambda b: (0, 0, 0)),
            pl.BlockSpec((1, POOL_WIDTH), lambda b: (0, 0)),
        ],
        out_specs=pl.BlockSpec((seq_len, POOL_WIDTH), lambda b: (b, 0)),
        out_shape=jax.ShapeDtypeStruct((batch * seq_len, POOL_WIDTH), BF16),
        scratch_shapes=[pltpu.VMEM((seq_len + 2 * POOL_HALO, POOL_WIDTH), F32)],
        compiler_params=_cparams(1),
        name="pool_mixer",
    )(z, z, pool_w, pool_scale)


def _block_row(p, block, row):
    c, w = p.shape
    p3 = p.reshape(c // block, block, w)
    return jnp.broadcast_to(p3[:, row:row + 1, :], p3.shape).reshape(c, w)


def _interleave(lo, hi, block):
    half = block // 2
    parts = []
    for b in range(SCAN_CHUNK // block):
        parts.append(lo[b * block:b * block + half])
        parts.append(hi[b * block + half:(b + 1) * block])
    return jnp.concatenate(parts, axis=0)


def _rec_kernel(*refs, seq_len, heads, has_s0, out_state):
    q_ref, ff_ref, fb_ref, v_ref, gr_ref, lb_ref, hn_ref = refs[:7]
    rest = list(refs[7:])
    s0_ref = rest.pop(0) if has_s0 else None
    o_ref = rest.pop(0)
    st_ref = rest.pop(0) if out_state else None
    acc_ref, qs_ref, k_ref, p_ref, a_ref, qin_ref, u_ref, dec_ref, tri_ref, lvl_ref = rest

    c = SCAN_CHUNK
    hd = REC_HEAD_DIM
    n_chunks = seq_len // c
    zf_refs = (ff_ref, fb_ref)
    units = [(h, n) for h in range(heads) for n in range(n_chunks)]

    def rows_in(n):
        return slice(n * c, (n + 1) * c)

    def rows_sc(h, n):
        return slice((h * n_chunks + n) * c, (h * n_chunks + n + 1) * c)

    def cols(h):
        return slice(h * hd, (h + 1) * hd)

    ti = lax.broadcasted_iota(jnp.int32, (c, c), 0)
    si = lax.broadcasted_iota(jnp.int32, (c, c), 1)
    x = ti ^ si
    lvl = jnp.where(x < 8, 0, jnp.where(x < 16, 1, jnp.where(x < 32, 2, jnp.where(x < 64, 3, 4))))
    for d, causal in enumerate((si <= ti, si >= ti)):
        tri_ref[d] = jnp.where(causal, 1.0, 0.0).astype(BF16)
        lvl_ref[d] = jnp.where(causal, lvl, 5)

    for h, n in units:
        qz = q_ref[rows_in(n), cols(h)].astype(F32)
        qs_ref[rows_sc(h, n), :] = qz * jax.nn.sigmoid(qz)
        for d in range(2):
            lower = lb_ref[d, h]
            f = jnp.clip(lower + (1.0 - lower) * jax.nn.sigmoid(zf_refs[d][rows_in(n), cols(h)]), F_MIN, 1.0)
            k_ref[d, rows_sc(h, n), :] = 1.0 - f
            g = jnp.log(f) * LOG2_E
            g_hi = g.astype(BF16)
            rest = g - g_hi.astype(F32)
            g_mid = rest.astype(BF16)
            g_lo = (rest - g_mid.astype(F32)).astype(BF16)
            sums = _dot(tri_ref[d], jnp.concatenate([g_hi, g_mid, g_lo], axis=1))
            p_ref[d, rows_sc(h, n), :] = (sums[:, 0:hd] + sums[:, hd:2 * hd]) + sums[:, 2 * hd:3 * hd]

    for h, n in units:
        rows = rows_sc(h, n)
        q = qs_ref[rows, :]
        v_t = v_ref[rows_in(n), cols(h)].astype(F32).T.astype(BF16)
        for d in range(2):
            rev = d == 1
            level = lvl_ref[d]
            k = k_ref[d, rows, :]
            p = p_ref[d, rows, :]
            e0 = p - _block_row(p, 8, 4 if rev else 3)
            a = _dot_nt((q * jnp.exp2(e0)).astype(BF16), (k * jnp.exp2(-e0)).astype(BF16))
            a = jnp.where(level == 0, a, 0.0)
            for lv, block in enumerate((16, 32, 64, 128), 1):
                beta = _block_row(p, block, block // 2 if rev else block // 2 - 1)
                if rev:
                    e = _interleave(p, beta, block) - _interleave(beta, p, block)
                    src = _interleave(q, k, block)
                else:
                    e = _interleave(beta, p, block) - _interleave(p, beta, block)
                    src = _interleave(k, q, block)
                m = (src * jnp.exp2(e)).astype(BF16)
                a = jnp.where(level == lv, _dot_nt(m, m), a)
            a_ref[rows, d * c:(d + 1) * c] = a.astype(BF16)

            edge = p[0:1, :] if rev else p[c - 1:c, :]
            qin_ref[d, rows, :] = (q * jnp.exp2(p)).astype(BF16)
            u_ref[d, h * n_chunks + n] = _dot(v_t, (k * jnp.exp2(edge - p)).astype(BF16))
            dec_ref[d, h * n_chunks + n] = jnp.broadcast_to(jnp.exp2(edge), (8, hd))

    for h, n in units:
        v_b = v_ref[rows_in(n), cols(h)]
        acc_ref[rows_sc(h, n), :] = _dot(a_ref[rows_sc(h, n), :], jnp.concatenate([v_b, v_b], axis=0))

    for h in range(heads):
        if has_s0:
            states = [s0_ref[0, h].T, s0_ref[1, h].T]
        else:
            states = [jnp.zeros((hd, hd), F32)] * 2
        inter = [[None] * n_chunks, [None] * n_chunks]
        for i in range(n_chunks):
            for d, n in ((0, i), (1, n_chunks - 1 - i)):
                inter[d][n] = _dot_nt(qin_ref[d, rows_sc(h, n), :], states[d].astype(BF16))
                states[d] = dec_ref[d, h * n_chunks + n][0:1, :] * states[d] + u_ref[d, h * n_chunks + n]
        for n in range(n_chunks):
            gate = gr_ref[rows_in(n), cols(h)].astype(F32)
            o = acc_ref[rows_sc(h, n), :] + inter[0][n] + inter[1][n]
            o_ref[rows_in(n), cols(h)] = (_rms(o, hn_ref[...]) * (gate * jax.nn.sigmoid(gate))).astype(BF16)
        if out_state:
            st_ref[0, h] = states[0].T
            st_ref[1, h] = states[1].T


def _rec_mixer(z, zf, batch, seq_len, lower, head_norm, s0):
    hd = REC_HEAD_DIM
    nh = REC_HEADS
    n_chunks = seq_len // SCAN_CHUNK
    heads = min(nh, max(1, SCAN_UNITS // n_chunks))
    width = heads * hd
    col0 = 2 * POOL_WIDTH // width

    def zspec(part):
        return pl.BlockSpec((seq_len, width), lambda b, g: (b, col0 + part * (nh // heads) + g))

    def fspec(part):
        return pl.BlockSpec((seq_len, width), lambda b, g: (b, part * (nh // heads) + g))

    in_specs = [zspec(0), fspec(0), fspec(1), zspec(1), zspec(2),
                pl.BlockSpec((2, heads, 1, hd), lambda b, g: (0, g, 0, 0)),
                pl.BlockSpec((1, hd), lambda b, g: (0, 0))]
    args = [z, zf, zf, z, z, lower, head_norm]
    has_s0 = s0 is not None
    if has_s0:
        s0_arr, j = s0
        in_specs.append(pl.BlockSpec((None, None, 2, heads, hd, hd), lambda b, g: (b, j, 0, g, 0, 0)))
        args.append(s0_arr)
    out_state = not has_s0
    out_specs = [pl.BlockSpec((seq_len, width), lambda b, g: (b, g))]
    out_shape = [jax.ShapeDtypeStruct((batch * seq_len, REC_WIDTH), BF16)]
    if out_state:
        out_specs.append(pl.BlockSpec((None, 2, heads, hd, hd), lambda b, g: (b, 0, g, 0, 0)))
        out_shape.append(jax.ShapeDtypeStruct((batch, 2, nh, hd, hd), F32))
    tokens = heads * seq_len
    outs = pl.pallas_call(
        functools.partial(_rec_kernel, seq_len=seq_len, heads=heads, has_s0=has_s0, out_state=out_state),
        grid=(batch, nh // heads),
        in_specs=in_specs,
        out_specs=out_specs,
        out_shape=out_shape,
        scratch_shapes=[pltpu.VMEM((tokens, hd), F32),
                        pltpu.VMEM((tokens, hd), F32),
                        pltpu.VMEM((2, tokens, hd), F32),
                        pltpu.VMEM((2, tokens, hd), F32),
                        pltpu.VMEM((tokens, 2 * SCAN_CHUNK), BF16),
                        pltpu.VMEM((2, tokens, hd), BF16),
                        pltpu.VMEM((2, tokens // SCAN_CHUNK, hd, hd), F32),
                        pltpu.VMEM((2, tokens // SCAN_CHUNK, 8, hd), F32),
                        pltpu.VMEM((2, SCAN_CHUNK, SCAN_CHUNK), BF16),
                        pltpu.VMEM((2, SCAN_CHUNK, SCAN_CHUNK), jnp.int32)],
        compiler_params=_cparams(2),
        name="rec_mixer",
    )(*args)
    return outs if out_state else (outs[0], None)


def _rope(x, cos, sin_signed):
    lane = lax.broadcasted_iota(jnp.int32, x.shape, 1)
    partner = jnp.where((lane & 1) == 0, pltpu.roll(x, ATT_HEAD_DIM - 1, 1), pltpu.roll(x, 1, 1))
    return x * cos + partner * sin_signed


def _att_kernel(*refs, seq_len, n_keys, latent):
    q_ref, k_ref, v_ref, g_ref, qn_ref, kn_ref = refs[:6]
    if latent:
        ck_ref, cv_ref, cos_ref, sin_ref, y_ref, kall, vall = refs[6:-8]
    else:
        y_ref, ko_ref, vo_ref, kall, vall = refs[6:-8]
    qs_bufs, s_bufs, m_bufs, o_bufs = refs[-8:-6], refs[-6:-4], refs[-4:-2], refs[-2:]
    hd = ATT_HEAD_DIM
    tq = ATT_Q_TILE
    tk = min(ATT_KEY_TILE, n_keys)
    n_kb = n_keys // tk
    n_past = n_keys - seq_len
    n_tiles = seq_len // tq
    exp2_scale = hd ** -0.5 * LOG2_E

    k = _rms(k_ref[...].astype(F32), kn_ref[...])
    v = v_ref[...].astype(F32)
    if latent:
        k = _rope(k, cos_ref[...], sin_ref[...])
        kall[0:n_past, :] = ck_ref[...].astype(BF16)
        vall[0:n_past, 0:hd] = cv_ref[...].astype(BF16)
    else:
        ko_ref[...] = k
        vo_ref[...] = v
    kall[n_past:n_keys, :] = k.astype(BF16)
    vall[n_past:n_keys, 0:hd] = v.astype(BF16)
    lane = lax.broadcasted_iota(jnp.int32, (n_keys, hd), 1)
    vall[:, hd:2 * hd] = (lane == 0).astype(BF16)

    def tile_rows(t):
        return pl.ds(pl.multiple_of(t * tq, tq), tq)

    def prep(t, slot):
        rows = tile_rows(t)
        for h in range(ATT_REP):
            q = _rms(q_ref[rows, h * hd:(h + 1) * hd].astype(F32), qn_ref[...])
            if latent:
                q = _rope(q, cos_ref[rows, :], sin_ref[rows, :])
            qs_bufs[slot][h * tq:(h + 1) * tq, :] = q.astype(BF16)

    def scores(t, slot):
        qs = qs_bufs[slot][...]
        mx = None
        for kb in range(n_kb):
            s = _dot_nt(qs, kall[kb * tk:(kb + 1) * tk, :])
            s_bufs[slot][:, kb * tk:(kb + 1) * tk] = s
            for c in range(tk // hd):
                part = s[:, c * hd:(c + 1) * hd]
                mx = part if mx is None else jnp.maximum(mx, part)
        m = jnp.max(mx, axis=-1, keepdims=True) * exp2_scale
        m_bufs[slot][...] = jnp.broadcast_to(m, (ATT_REP * tq, hd))

    def mix(t, slot):
        m = m_bufs[slot][...]
        ps = []
        for c in range(n_keys // hd):
            s = s_bufs[slot][:, c * hd:(c + 1) * hd]
            ps.append(jnp.exp2(s * exp2_scale - m).astype(BF16))
        o_bufs[slot][...] = _dot(jnp.concatenate(ps, axis=1), vall[...])

    def finish(t, slot):
        rows = tile_rows(t)
        o = o_bufs[slot][:, 0:hd] / o_bufs[slot][:, hd:hd + 1]
        for h in range(ATT_REP):
            gate = g_ref[rows, h * hd:(h + 1) * hd].astype(F32)
            y = o[h * tq:(h + 1) * tq, :] * (gate * jax.nn.sigmoid(gate))
            y_ref[rows, h * hd:(h + 1) * hd] = y.astype(BF16)

    stages = (prep, scores, mix, finish)

    def step(k, parity):
        for a, stage in enumerate(stages):
            t = k - a
            if isinstance(t, int) and not 0 <= t < n_tiles:
                continue
            stage(t, (parity + a) % 2)

    depth = len(stages) - 1
    for k in range(min(depth, n_tiles + depth)):
        step(k, k % 2)
    n_steady = max(n_tiles - depth, 0)

    def steady_pair(j, carry):
        k = depth + 2 * j
        step(k, depth % 2)
        step(k + 1, (depth + 1) % 2)
        return carry

    lax.fori_loop(0, n_steady // 2, steady_pair, 0)
    for k in range(depth + 2 * (n_steady // 2), n_tiles + depth):
        step(k, k % 2)


def _attention(z, batch, seq_len, q_norm, k_norm, cache=None):
    hd = ATT_HEAD_DIM
    qw = ATT_REP * hd
    latent = cache is not None
    in_specs = [
        pl.BlockSpec((seq_len, qw), lambda b, g: (b, g)),
        pl.BlockSpec((seq_len, hd), lambda b, g: (b, ATT_WIDTH // hd + g)),
        pl.BlockSpec((seq_len, hd), lambda b, g: (b, (ATT_WIDTH + KV_WIDTH) // hd + g)),
        pl.BlockSpec((seq_len, qw), lambda b, g: (b, (ATT_WIDTH + 2 * KV_WIDTH) // qw + g)),
        pl.BlockSpec((1, hd), lambda b, g: (0, 0)),
        pl.BlockSpec((1, hd), lambda b, g: (0, 0)),
    ]
    args = [z, z, z, z, q_norm, k_norm]
    out_specs = [pl.BlockSpec((seq_len, qw), lambda b, g: (b, g))]
    out_shape = [jax.ShapeDtypeStruct((batch * seq_len, ATT_WIDTH), BF16)]
    n_keys = seq_len
    if latent:
        cache_k, cache_v, j, cos, sin_signed = cache
        past = cache_k.shape[2]
        n_keys += past
        ck = cache_k.reshape(batch, N_ATT, past, KV_WIDTH)
        cv = cache_v.reshape(batch, N_ATT, past, KV_WIDTH)
        cspec = pl.BlockSpec((None, None, past, hd), lambda b, g: (b, j, 0, g))
        tspec = pl.BlockSpec((seq_len, hd), lambda b, g: (0, 0))
        in_specs += [cspec, cspec, tspec, tspec]
        args += [ck, cv, cos, sin_signed]
    else:
        kv_spec = pl.BlockSpec((None, seq_len, hd), lambda b, g: (b, 0, g))
        out_specs += [kv_spec, kv_spec]
        out_shape += [jax.ShapeDtypeStruct((batch, seq_len, KV_WIDTH), F32)] * 2
    return pl.pallas_call(
        functools.partial(_att_kernel, seq_len=seq_len, n_keys=n_keys, latent=latent),
        grid=(batch, ATT_KV_HEADS),
        in_specs=in_specs,
        out_specs=out_specs,
        out_shape=out_shape,
        scratch_shapes=[pltpu.VMEM((n_keys, hd), BF16), pltpu.VMEM((n_keys, 2 * hd), BF16),
                        pltpu.VMEM((ATT_REP * ATT_Q_TILE, hd), BF16), pltpu.VMEM((ATT_REP * ATT_Q_TILE, hd), BF16),
                        pltpu.VMEM((ATT_REP * ATT_Q_TILE, n_keys), F32), pltpu.VMEM((ATT_REP * ATT_Q_TILE, n_keys), F32),
                        pltpu.VMEM((ATT_REP * ATT_Q_TILE, hd), F32), pltpu.VMEM((ATT_REP * ATT_Q_TILE, hd), F32),
                        pltpu.VMEM((ATT_REP * ATT_Q_TILE, 2 * hd), F32), pltpu.VMEM((ATT_REP * ATT_Q_TILE, 2 * hd), F32)],
        compiler_params=_cparams(2),
        name="attention",
    )(*args)


def _rope_tables(n_tokens):
    t = jnp.arange(n_tokens)
    row = (t // GRID_W).astype(F32)
    col = (t % GRID_W).astype(F32)
    inv = ROPE_THETA ** (-jnp.arange(0, AXIS_DIM, 2, dtype=F32) / AXIS_DIM)
    ang = jnp.concatenate([row[:, None] * inv[None, :], col[:, None] * inv[None, :]], axis=-1)
    cos = jnp.repeat(jnp.cos(ang), 2, axis=-1)
    sin = jnp.repeat(jnp.sin(ang), 2, axis=-1)
    sign = jnp.where(jnp.arange(ATT_HEAD_DIM) % 2 == 0, -1.0, 1.0).astype(F32)
    return cos, sin * sign


def kernel(x_prompt, x_sample, c, state_hgrn, cache_k, cache_v, c_ctx, ada_w, ada_b, norm_pre, norm_post, rec_w_in, rec_lb_logits, rec_head_norm, pool_w, pool_scale, rec_w_out, att_w_in, att_q_norm, att_k_norm, att_w_out):
    nb_c, len_c, _ = x_prompt.shape
    nb_l, len_l, _ = x_sample.shape

    lb_p = jax.nn.softmax(rec_lb_logits.astype(F32), axis=0)
    lower_bounds = jnp.clip(jnp.cumsum(lb_p, axis=0) - lb_p[0], 0.0, 1.0)
    lower_bounds = lower_bounds.reshape(N_REC, 2, REC_HEADS, 1, REC_HEAD_DIM)
    cos, sin_signed = _rope_tables(len_l)

    cvec = jnp.zeros((MOD_ROWS, D_MODEL), F32).at[0].set(c_ctx).at[1:1 + nb_l].set(c)
    mod = _modulation(cvec, ada_w, ada_b)

    forget_cols = (2 * POOL_WIDTH + REC_WIDTH, 2 * POOL_WIDTH + 3 * REC_WIDTH)
    layers = []
    for i in range(DEPTH):
        j = i // 2
        rec = i % 2 == 0
        w_in, w_out = (rec_w_in, rec_w_out) if rec else (att_w_in, att_w_out)
        layers.append(dict(
            w_in=w_in[j].astype(BF16), w_out=w_out[j].astype(BF16), f32_cols=forget_cols if rec else (0, 0),
            gain_pre=norm_pre[i].reshape(1, D_MODEL), gain_post=norm_post[i].reshape(1, D_MODEL)))

    def pre_args(i):
        return (layers[i]["w_in"], i, layers[i]["gain_pre"], layers[i]["f32_cols"])

    xs = [x_prompt.reshape(nb_c * len_c, D_MODEL), x_sample.reshape(nb_l * len_l, D_MODEL)]
    streams = ((nb_c, len_c, 0), (nb_l, len_l, 1))
    new_states, new_k, new_v = [], [], []
    for s, (nb, sl, row0) in enumerate(streams):
        x = xs[s]
        zs = _proj(x, mod, sl, row0, pre=pre_args(0))
        for i in range(DEPTH):
            j = i // 2
            if i % 2 == 0:
                z, zf = zs
                pw = pool_w[j].astype(BF16)
                ps = pool_scale[j].reshape(1, POOL_WIDTH)
                hn = rec_head_norm[j].reshape(1, REC_HEAD_DIM)
                y_pool = _pool_mixer(z, nb, sl, pw, ps)
                s0 = (state_hgrn, j) if s == 1 else None
                y_rec, st = _rec_mixer(z, zf, nb, sl, lower_bounds[j], hn, s0)
                if st is not None:
                    new_states.append(st)
                halves = (y_pool, 0, y_rec, 0)
            else:
                (z,) = zs
                qn = att_q_norm[j].reshape(1, ATT_HEAD_DIM)
                kn = att_k_norm[j].reshape(1, ATT_HEAD_DIM)
                if s == 0:
                    y, k_new, v_new = _attention(z, nb, sl, qn, kn)
                    new_k.append(k_new.reshape(nb, sl, ATT_KV_HEADS, ATT_HEAD_DIM))
                    new_v.append(v_new.reshape(nb, sl, ATT_KV_HEADS, ATT_HEAD_DIM))
                else:
                    (y,) = _attention(z, nb, sl, qn, kn, (cache_k, cache_v, j, cos, sin_signed))
                halves = (y, 0, y, 1)
            post = halves + (layers[i]["w_out"], i, layers[i]["gain_post"])
            x, *zs = _proj(x, mod, sl, row0, post=post, pre=pre_args(i + 1) if i + 1 < DEPTH else None)
        xs[s] = x
    xc, xl = xs

    return (xc.reshape(nb_c, len_c, D_MODEL), xl.reshape(nb_l, len_l, D_MODEL),
            jnp.stack(new_states, axis=1), jnp.stack(new_k, axis=1), jnp.stack(new_v, axis=1))
```

```python
import functools

import jax
import jax.numpy as jnp
from jax import lax
from jax.experimental import pallas as pl
from jax.experimental.pallas import tpu as pltpu

D_MODEL = 1024
DEPTH = 4
GRID_W = 64
N_REC = (DEPTH + 1) // 2
N_ATT = DEPTH // 2
POOL_WIDTH = D_MODEL // 2
POOL_WINDOWS = (2, 4, 8, 16)
POOL_GROUP_DIM = POOL_WIDTH // len(POOL_WINDOWS)
REC_WIDTH = D_MODEL // 2
REC_HEAD_DIM = 128
REC_HEADS = REC_WIDTH // REC_HEAD_DIM
REC_IN_WIDTH = 2 * POOL_WIDTH + 5 * REC_WIDTH
ATT_HEAD_DIM = 128
ATT_HEADS = D_MODEL // ATT_HEAD_DIM
ATT_KV_HEADS = 2
ATT_REP = ATT_HEADS // ATT_KV_HEADS
ATT_WIDTH = ATT_HEADS * ATT_HEAD_DIM
KV_WIDTH = ATT_KV_HEADS * ATT_HEAD_DIM
ATT_IN_WIDTH = 2 * ATT_WIDTH + 2 * KV_WIDTH
AXIS_DIM = ATT_HEAD_DIM // 2
ROPE_THETA = 10000.0
EPS = 1e-6
F_MIN = 1e-6

MOD_ROWS = 16
TOKEN_TILE = 1024
MIN_PROJ_STEPS = 8
PROJ_ROW_GROUPS = 2
POOL_ROWS = 256
IN_PROJ_STEP = 512
MOD_COLS = 1536
SCAN_CHUNK = 128
SCAN_UNITS = 8
POOL_HALO = 8
ATT_Q_TILE = 128
ATT_KEY_TILE = 128
LOG2_E = 1.4426950408889634
VMEM_LIMIT = 56 * 1024 * 1024

F32 = jnp.float32
BF16 = jnp.bfloat16


def _cparams(n_axes):
    return pltpu.CompilerParams(
        dimension_semantics=("arbitrary",) * n_axes, vmem_limit_bytes=VMEM_LIMIT)


def _rms(x, g):
    return x * lax.rsqrt(jnp.mean(x * x, axis=-1, keepdims=True) + EPS) * g


def _dot(a, b):
    return jnp.dot(a, b, preferred_element_type=F32)


def _dot_nt(a, b):
    return lax.dot_general(a, b, (((1,), (1,)), ((), ())), preferred_element_type=F32)


def _mod_kernel(cv_ref, w_ref, b_ref, o_ref):
    cv = cv_ref[...]
    a = (cv * jax.nn.sigmoid(cv)).astype(BF16)
    o_ref[...] = _dot(a, w_ref[...].astype(BF16)) + b_ref[...]


def _modulation(cvec, ada_w, ada_b):
    tn = MOD_COLS
    out = pl.pallas_call(
        _mod_kernel,
        grid=(DEPTH, 3 * D_MODEL // tn),
        in_specs=[
            pl.BlockSpec((MOD_ROWS, D_MODEL), lambda l, n: (0, 0)),
            pl.BlockSpec((None, D_MODEL, tn), lambda l, n: (l, 0, n)),
            pl.BlockSpec((None, 1, tn), lambda l, n: (l, 0, n)),
        ],
        out_specs=pl.BlockSpec((None, MOD_ROWS, tn), lambda l, n: (l, 0, n)),
        out_shape=jax.ShapeDtypeStruct((DEPTH, MOD_ROWS, 3 * D_MODEL), F32),
        compiler_params=_cparams(2),
        name="modulation",
    )(cvec, ada_w, ada_b.reshape(DEPTH, 1, 3 * D_MODEL))
    return out.reshape(DEPTH, MOD_ROWS, 3, 1, D_MODEL)


def _mod_spec(layer, seq_len, row0, tile):
    assert row0 == 0 or seq_len % tile == 0
    tiles_per_seq = seq_len // tile
    if row0 == 0:
        index = lambda i: (layer, 0, 0, 0, 0)
    else:
        index = lambda i: (layer, row0 + i // tiles_per_seq, 0, 0, 0)
    return pl.BlockSpec((None, None, 3, 1, D_MODEL), index)


def _proj_kernel(*refs, post, pre, n_out, f32_cols):
    refs = list(refs)
    x_ref = refs.pop(0)
    if post:
        ya_ref, yb_ref, wo_ref, modp_ref, gpost_ref = (refs.pop(0) for _ in range(5))
    if pre:
        modn_ref, gpre_ref, wi_ref = (refs.pop(0) for _ in range(3))
    xo_ref = refs.pop(0) if post else None
    rows_all = x_ref.shape[0]
    for r0 in range(0, rows_all, rows_all // PROJ_ROW_GROUPS):
        rows = slice(r0, r0 + rows_all // PROJ_ROW_GROUPS)
        x = x_ref[rows, :]
        if post:
            half = D_MODEL // 2
            p = _dot(ya_ref[rows, :], wo_ref[0:half, :]) + _dot(yb_ref[rows, :], wo_ref[half:D_MODEL, :])
            x = x + modp_ref[2] * _rms(p, gpost_ref[...])
            xo_ref[rows, :] = x
        if pre:
            h = (_rms(x, gpre_ref[...]) * (1.0 + modn_ref[1]) + modn_ref[0]).astype(BF16)
            lo, hi = f32_cols
            for c0 in range(0, n_out, IN_PROJ_STEP):
                y = _dot(h, wi_ref[:, c0:c0 + IN_PROJ_STEP])
                if lo <= c0 < hi:
                    refs[1][rows, c0 - lo:c0 - lo + IN_PROJ_STEP] = y
                else:
                    c1 = c0 if c0 < lo else c0 - (hi - lo)
                    refs[0][rows, c1:c1 + IN_PROJ_STEP] = y.astype(BF16)


def _proj(x, mod, seq_len, row0, post=None, pre=None):
    t = x.shape[0]
    tile = min(TOKEN_TILE, t // MIN_PROJ_STEPS)
    half = D_MODEL // 2
    row = lambda i: (i, 0)
    fixed = lambda i: (0, 0)
    resident = dict(index_map=fixed, pipeline_mode=pl.Buffered(1))
    in_specs = [pl.BlockSpec((tile, D_MODEL), row)]
    args = [x]
    out_specs, out_shape = [], []
    n_out, f32_cols = 0, (0, 0)
    if post:
        ya, ia, yb, ib, w_out, layer, gain = post
        in_specs += [pl.BlockSpec((tile, half), lambda i: (i, ia)),
                     pl.BlockSpec((tile, half), lambda i: (i, ib)),
                     pl.BlockSpec((D_MODEL, D_MODEL), **resident),
                     _mod_spec(layer, seq_len, row0, tile),
                     pl.BlockSpec((1, D_MODEL), fixed)]
        args += [ya, yb, w_out, mod, gain]
        out_specs.append(pl.BlockSpec((tile, D_MODEL), row))
        out_shape.append(jax.ShapeDtypeStruct((t, D_MODEL), F32))
    if pre:
        w_in, layer, gain, f32_cols = pre
        n_out = w_in.shape[1]
        lo, hi = f32_cols
        assert lo % IN_PROJ_STEP == 0 and hi % IN_PROJ_STEP == 0 and n_out % IN_PROJ_STEP == 0
        in_specs += [_mod_spec(layer, seq_len, row0, tile),
                     pl.BlockSpec((1, D_MODEL), fixed),
                     pl.BlockSpec((D_MODEL, n_out), **resident)]
        args += [mod, gain, w_in]
        out_specs.append(pl.BlockSpec((tile, n_out - (hi - lo)), row))
        out_shape.append(jax.ShapeDtypeStruct((t, n_out - (hi - lo)), BF16))
        if hi > lo:
            out_specs.append(pl.BlockSpec((tile, hi - lo), row))
            out_shape.append(jax.ShapeDtypeStruct((t, hi - lo), F32))
    return pl.pallas_call(
        functools.partial(_proj_kernel, post=bool(post), pre=bool(pre), n_out=n_out, f32_cols=f32_cols),
        grid=(t // tile,),
        in_specs=in_specs,
        out_specs=out_specs,
        out_shape=out_shape,
        compiler_params=_cparams(1),
        name="proj",
    )(*args)


def _pool_kernel(u_ref, gp_ref, pw_ref, ps_ref, o_ref, pad_ref, *, seq_len):
    gd = POOL_GROUP_DIM
    zeros = jnp.zeros((POOL_HALO, POOL_WIDTH), F32)
    pad_ref[0:POOL_HALO, :] = zeros
    pad_ref[POOL_HALO + seq_len:2 * POOL_HALO + seq_len, :] = zeros
    pad_ref[POOL_HALO:POOL_HALO + seq_len, :] = u_ref[...].astype(F32)
    rows = POOL_ROWS
    for g, win in enumerate(POOL_WINDOWS):
        cols = slice(g * gd, (g + 1) * gd)
        for r in range(seq_len // rows):
            base = r * rows
            acc = None
            for j in range(-(win // 2), win // 2):
                start = POOL_HALO + base + j
                part = pad_ref[start:start + rows, cols]
                acc = part if acc is None else acc + part
            t = base + lax.broadcasted_iota(jnp.int32, (rows, 1), 0)
            lo = jnp.clip(t - win // 2, 0, seq_len)
            hi = jnp.clip(t + win // 2, 0, seq_len)
            mean = acc / (hi - lo).astype(F32)
            d = mean - pad_ref[POOL_HALO + base:POOL_HALO + base + rows, cols]
            y = _dot(d.astype(BF16), pw_ref[g]) * ps_ref[:, cols]
            gate = gp_ref[base:base + rows, cols].astype(F32)
            o_ref[base:base + rows, cols] = (y * (gate * jax.nn.sigmoid(gate))).astype(BF16)


def _pool_mixer(z, batch, seq_len, pool_w, pool_scale):
    return pl.pallas_call(
        functools.partial(_pool_kernel, seq_len=seq_len),
        grid=(batch,),
        in_specs=[
            pl.BlockSpec((seq_len, POOL_WIDTH), lambda b: (b, 0)),
            pl.BlockSpec((seq_len, POOL_WIDTH), lambda b: (b, 1)),
            pl.BlockSpec((len(POOL_WINDOWS), POOL_GROUP_DIM, POOL_GROUP_DIM), lambda b: (0, 0, 0)),
            pl.BlockSpec((1, POOL_WIDTH), lambda b: (0, 0)),
        ],
        out_specs=pl.BlockSpec((seq_len, POOL_WIDTH), lambda b: (b, 0)),
        out_shape=jax.ShapeDtypeStruct((batch * seq_len, POOL_WIDTH), BF16),
        scratch_shapes=[pltpu.VMEM((seq_len + 2 * POOL_HALO, POOL_WIDTH), F32)],
        compiler_params=_cparams(1),
        name="pool_mixer",
    )(z, z, pool_w, pool_scale)


def _block_row(p, block, row):
    c, w = p.shape
    p3 = p.reshape(c // block, block, w)
    return jnp.broadcast_to(p3[:, row:row + 1, :], p3.shape).reshape(c, w)


def _interleave(lo, hi, block):
    half = block // 2
    parts = []
    for b in range(SCAN_CHUNK // block):
        parts.append(lo[b * block:b * block + half])
        parts.append(hi[b * block + half:(b + 1) * block])
    return jnp.concatenate(parts, axis=0)


def _rec_kernel(*refs, seq_len, heads, has_s0, out_state):
    q_ref, ff_ref, fb_ref, v_ref, gr_ref, lb_ref, hn_ref = refs[:7]
    rest = list(refs[7:])
    s0_ref = rest.pop(0) if has_s0 else None
    o_ref = rest.pop(0)
    st_ref = rest.pop(0) if out_state else None
    acc_ref, qs_ref, k_ref, p_ref, a_ref, qin_ref, u_ref, dec_ref, tri_ref, lvl_ref = rest

    c = SCAN_CHUNK
    hd = REC_HEAD_DIM
    n_chunks = seq_len // c
    zf_refs = (ff_ref, fb_ref)
    units = [(h, n) for h in range(heads) for n in range(n_chunks)]

    def rows_in(n):
        return slice(n * c, (n + 1) * c)

    def rows_sc(h, n):
        return slice((h * n_chunks + n) * c, (h * n_chunks + n + 1) * c)

    def cols(h):
        return slice(h * hd, (h + 1) * hd)

    ti = lax.broadcasted_iota(jnp.int32, (c, c), 0)
    si = lax.broadcasted_iota(jnp.int32, (c, c), 1)
    x = ti ^ si
    lvl = jnp.where(x < 8, 0, jnp.where(x < 16, 1, jnp.where(x < 32, 2, jnp.where(x < 64, 3, 4))))
    for d, causal in enumerate((si <= ti, si >= ti)):
        tri_ref[d] = jnp.where(causal, 1.0, 0.0).astype(BF16)
        lvl_ref[d] = jnp.where(causal, lvl, 5)

    for h, n in units:
        qz = q_ref[rows_in(n), cols(h)].astype(F32)
        qs_ref[rows_sc(h, n), :] = qz * jax.nn.sigmoid(qz)
        for d in range(2):
            lower = lb_ref[d, h]
            f = jnp.clip(lower + (1.0 - lower) * jax.nn.sigmoid(zf_refs[d][rows_in(n), cols(h)]), F_MIN, 1.0)
            k_ref[d, rows_sc(h, n), :] = 1.0 - f
            g = jnp.log(f) * LOG2_E
            g_hi = g.astype(BF16)
            rest = g - g_hi.astype(F32)
            g_mid = rest.astype(BF16)
            g_lo = (rest - g_mid.astype(F32)).astype(BF16)
            sums = _dot(tri_ref[d], jnp.concatenate([g_hi, g_mid, g_lo], axis=1))
            p_ref[d, rows_sc(h, n), :] = (sums[:, 0:hd] + sums[:, hd:2 * hd]) + sums[:, 2 * hd:3 * hd]

    for h, n in units:
        rows = rows_sc(h, n)
        q = qs_ref[rows, :]
        v_t = v_ref[rows_in(n), cols(h)].astype(F32).T.astype(BF16)
        for d in range(2):
            rev = d == 1
            level = lvl_ref[d]
            k = k_ref[d, rows, :]
            p = p_ref[d, rows, :]
            e0 = p - _block_row(p, 8, 4 if rev else 3)
            a = _dot_nt((q * jnp.exp2(e0)).astype(BF16), (k * jnp.exp2(-e0)).astype(BF16))
            a = jnp.where(level == 0, a, 0.0)
            for lv, block in enumerate((16, 32, 64, 128), 1):
                beta = _block_row(p, block, block // 2 if rev else block // 2 - 1)
                if rev:
                    e = _interleave(p, beta, block) - _interleave(beta, p, block)
                    src = _interleave(q, k, block)
                else:
                    e = _interleave(beta, p, block) - _interleave(p, beta, block)
                    src = _interleave(k, q, block)
                m = (src * jnp.exp2(e)).astype(BF16)
                a = jnp.where(level == lv, _dot_nt(m, m), a)
            a_ref[rows, d * c:(d + 1) * c] = a.astype(BF16)

            edge = p[0:1, :] if rev else p[c - 1:c, :]
            qin_ref[d, rows, :] = (q * jnp.exp2(p)).astype(BF16)
            u_ref[d, h * n_chunks + n] = _dot(v_t, (k * jnp.exp2(edge - p)).astype(BF16))
            dec_ref[d, h * n_chunks + n] = jnp.broadcast_to(jnp.exp2(edge), (8, hd))

    for h, n in units:
        v_b = v_ref[rows_in(n), cols(h)]
        acc_ref[rows_sc(h, n), :] = _dot(a_ref[rows_sc(h, n), :], jnp.concatenate([v_b, v_b], axis=0))

    for h in range(heads):
        if has_s0:
            states = [s0_ref[0, h].T, s0_ref[1, h].T]
        else:
            states = [jnp.zeros((hd, hd), F32)] * 2
        inter = [[None] * n_chunks, [None] * n_chunks]
        for i in range(n_chunks):
            for d, n in ((0, i), (1, n_chunks - 1 - i)):
                inter[d][n] = _dot_nt(qin_ref[d, rows_sc(h, n), :], states[d].astype(BF16))
                states[d] = dec_ref[d, h * n_chunks + n][0:1, :] * states[d] + u_ref[d, h * n_chunks + n]
        for n in range(n_chunks):
            gate = gr_ref[rows_in(n), cols(h)].astype(F32)
            o = acc_ref[rows_sc(h, n), :] + inter[0][n] + inter[1][n]
            o_ref[rows_in(n), cols(h)] = (_rms(o, hn_ref[...]) * (gate * jax.nn.sigmoid(gate))).astype(BF16)
        if out_state:
            st_ref[0, h] = states[0].T
            st_ref[1, h] = states[1].T


def _rec_mixer(z, zf, batch, seq_len, lower, head_norm, s0):
    hd = REC_HEAD_DIM
    nh = REC_HEADS
    n_chunks = seq_len // SCAN_CHUNK
    heads = min(nh, max(1, SCAN_UNITS // n_chunks))
    width = heads * hd
    col0 = 2 * POOL_WIDTH // width

    def zspec(part):
        return pl.BlockSpec((seq_len, width), lambda b, g: (b, col0 + part * (nh // heads) + g))

    def fspec(part):
        return pl.BlockSpec((seq_len, width), lambda b, g: (b, part * (nh // heads) + g))

    in_specs = [zspec(0), fspec(0), fspec(1), zspec(1), zspec(2),
                pl.BlockSpec((2, heads, 1, hd), lambda b, g: (0, g, 0, 0)),
                pl.BlockSpec((1, hd), lambda b, g: (0, 0))]
    args = [z, zf, zf, z, z, lower, head_norm]
    has_s0 = s0 is not None
    if has_s0:
        s0_arr, j = s0
        in_specs.append(pl.BlockSpec((None, None, 2, heads, hd, hd), lambda b, g: (b, j, 0, g, 0, 0)))
        args.append(s0_arr)
    out_state = not has_s0
    out_specs = [pl.BlockSpec((seq_len, width), lambda b, g: (b, g))]
    out_shape = [jax.ShapeDtypeStruct((batch * seq_len, REC_WIDTH), BF16)]
    if out_state:
        out_specs.append(pl.BlockSpec((None, 2, heads, hd, hd), lambda b, g: (b, 0, g, 0, 0)))
        out_shape.append(jax.ShapeDtypeStruct((batch, 2, nh, hd, hd), F32))
    tokens = heads * seq_len
    outs = pl.pallas_call(
        functools.partial(_rec_kernel, seq_len=seq_len, heads=heads, has_s0=has_s0, out_state=out_state),
        grid=(batch, nh // heads),
        in_specs=in_specs,
        out_specs=out_specs,
        out_shape=out_shape,
        scratch_shapes=[pltpu.VMEM((tokens, hd), F32),
                        pltpu.VMEM((tokens, hd), F32),
                        pltpu.VMEM((2, tokens, hd), F32),
                        pltpu.VMEM((2, tokens, hd), F32),
                        pltpu.VMEM((tokens, 2 * SCAN_CHUNK), BF16),
                        pltpu.VMEM((2, tokens, hd), BF16),
                        pltpu.VMEM((2, tokens // SCAN_CHUNK, hd, hd), F32),
                        pltpu.VMEM((2, tokens // SCAN_CHUNK, 8, hd), F32),
                        pltpu.VMEM((2, SCAN_CHUNK, SCAN_CHUNK), BF16),
                        pltpu.VMEM((2, SCAN_CHUNK, SCAN_CHUNK), jnp.int32)],
        compiler_params=_cparams(2),
        name="rec_mixer",
    )(*args)
    return outs if out_state else (outs[0], None)


def _rope(x, cos, sin_signed):
    lane = lax.broadcasted_iota(jnp.int32, x.shape, 1)
    partner = jnp.where((lane & 1) == 0, pltpu.roll(x, ATT_HEAD_DIM - 1, 1), pltpu.roll(x, 1, 1))
    return x * cos + partner * sin_signed


def _att_kernel(*refs, seq_len, n_keys, latent):
    q_ref, k_ref, v_ref, qn_ref, kn_ref = refs[:5]
    g_refs = refs[5:5 + ATT_KV_HEADS]
    if latent:
        ck_ref, cv_ref, cos_ref, sin_ref, y_ref, kall, vall = refs[5 + ATT_KV_HEADS:-8]
    else:
        y_ref, ko_ref, vo_ref, kall, vall = refs[5 + ATT_KV_HEADS:-8]
    qs_bufs, s_bufs, m_bufs, o_bufs = refs[-8:-6], refs[-6:-4], refs[-4:-2], refs[-2:]
    hd = ATT_HEAD_DIM
    qw = ATT_REP * hd
    tq = ATT_Q_TILE
    tk = min(ATT_KEY_TILE, n_keys)
    n_kb = n_keys // tk
    n_past = n_keys - seq_len
    n_items = ATT_KV_HEADS * (seq_len // tq)
    exp2_scale = hd ** -0.5 * LOG2_E
    assert ATT_KV_HEADS == 2

    lane = lax.broadcasted_iota(jnp.int32, (n_keys, hd), 1)
    for g in range(ATT_KV_HEADS):
        k = _rms(k_ref[:, g * hd:(g + 1) * hd].astype(F32), kn_ref[...])
        v = v_ref[:, g * hd:(g + 1) * hd].astype(F32)
        if latent:
            k = _rope(k, cos_ref[...], sin_ref[...])
            kall[g, 0:n_past, :] = ck_ref[:, g * hd:(g + 1) * hd].astype(BF16)
            vall[g, 0:n_past, 0:hd] = cv_ref[:, g * hd:(g + 1) * hd].astype(BF16)
        else:
            ko_ref[:, g * hd:(g + 1) * hd] = k
            vo_ref[:, g * hd:(g + 1) * hd] = v
        kall[g, n_past:n_keys, :] = k.astype(BF16)
        vall[g, n_past:n_keys, 0:hd] = v.astype(BF16)
        vall[g, :, hd:2 * hd] = (lane == 0).astype(BF16)

    def tile_rows(item):
        return pl.ds(pl.multiple_of((item // ATT_KV_HEADS) * tq, tq), tq)

    def prep(item, g):
        rows = tile_rows(item)
        for h in range(ATT_REP):
            q = _rms(q_ref[rows, g * qw + h * hd:g * qw + (h + 1) * hd].astype(F32), qn_ref[...])
            if latent:
                q = _rope(q, cos_ref[rows, :], sin_ref[rows, :])
            qs_bufs[g][h * tq:(h + 1) * tq, :] = q.astype(BF16)

    def scores(item, g):
        qs = qs_bufs[g][...]
        mx = None
        for kb in range(n_kb):
            s = _dot_nt(qs, kall[g, kb * tk:(kb + 1) * tk, :])
            s_bufs[g][:, kb * tk:(kb + 1) * tk] = s
            for c in range(tk // hd):
                part = s[:, c * hd:(c + 1) * hd]
                mx = part if mx is None else jnp.maximum(mx, part)
        m = jnp.max(mx, axis=-1, keepdims=True) * exp2_scale
        m_bufs[g][...] = jnp.broadcast_to(m, (ATT_REP * tq, hd))

    def mix(item, g):
        m = m_bufs[g][...]
        ps = []
        for c in range(n_keys // hd):
            s = s_bufs[g][:, c * hd:(c + 1) * hd]
            ps.append(jnp.exp2(s * exp2_scale - m).astype(BF16))
        o_bufs[g][...] = _dot(jnp.concatenate(ps, axis=1), vall[g])

    def finish(item, g):
        rows = tile_rows(item)
        o = o_bufs[g][:, 0:hd] / o_bufs[g][:, hd:hd + 1]
        for h in range(ATT_REP):
            gate = g_refs[g][rows, h * hd:(h + 1) * hd].astype(F32)
            y = o[h * tq:(h + 1) * tq, :] * (gate * jax.nn.sigmoid(gate))
            y_ref[rows, g * qw + h * hd:g * qw + (h + 1) * hd] = y.astype(BF16)

    stages = (prep, scores, mix, finish)

    def step(k, parity):
        for a, stage in enumerate(stages):
            item = k - a
            if isinstance(item, int) and not 0 <= item < n_items:
                continue
            stage(item, (parity + a) % 2)

    depth = len(stages) - 1
    for k in range(min(depth, n_items + depth)):
        step(k, k % 2)
    n_steady = max(n_items - depth, 0)

    def steady_pair(j, carry):
        k = depth + 2 * j
        step(k, depth % 2)
        step(k + 1, (depth + 1) % 2)
        return carry

    lax.fori_loop(0, n_steady // 2, steady_pair, 0)
    for k in range(depth + 2 * (n_steady // 2), n_items + depth):
        step(k, k % 2)


def _attention(z, batch, seq_len, q_norm, k_norm, cache=None):
    hd = ATT_HEAD_DIM
    latent = cache is not None
    qw = ATT_REP * hd
    kv0 = ATT_WIDTH // KV_WIDTH
    gate0 = (ATT_WIDTH + 2 * KV_WIDTH) // qw
    assert (ATT_WIDTH + 2 * KV_WIDTH) % qw == 0 and ATT_WIDTH % KV_WIDTH == 0
    in_specs = [
        pl.BlockSpec((seq_len, ATT_WIDTH), lambda b: (b, 0)),
        pl.BlockSpec((seq_len, KV_WIDTH), lambda b: (b, kv0)),
        pl.BlockSpec((seq_len, KV_WIDTH), lambda b: (b, kv0 + 1)),
        pl.BlockSpec((1, hd), lambda b: (0, 0)),
        pl.BlockSpec((1, hd), lambda b: (0, 0)),
    ] + [pl.BlockSpec((seq_len, qw), lambda b, g=g: (b, gate0 + g)) for g in range(ATT_KV_HEADS)]
    args = [z, z, z, q_norm, k_norm] + [z] * ATT_KV_HEADS
    out_specs = [pl.BlockSpec((seq_len, ATT_WIDTH), lambda b: (b, 0))]
    out_shape = [jax.ShapeDtypeStruct((batch * seq_len, ATT_WIDTH), BF16)]
    n_keys = seq_len
    if latent:
        cache_k, cache_v, j, cos, sin_signed = cache
        past = cache_k.shape[2]
        n_keys += past
        ck = cache_k.reshape(batch, N_ATT, past, KV_WIDTH)
        cv = cache_v.reshape(batch, N_ATT, past, KV_WIDTH)
        cspec = pl.BlockSpec((None, None, past, KV_WIDTH), lambda b: (b, j, 0, 0))
        tspec = pl.BlockSpec((seq_len, hd), lambda b: (0, 0))
        in_specs += [cspec, cspec, tspec, tspec]
        args += [ck, cv, cos, sin_signed]
    else:
        kv_spec = pl.BlockSpec((None, seq_len, KV_WIDTH), lambda b: (b, 0, 0))
        out_specs += [kv_spec, kv_spec]
        out_shape += [jax.ShapeDtypeStruct((batch, seq_len, KV_WIDTH), F32)] * 2
    rows = ATT_REP * ATT_Q_TILE
    return pl.pallas_call(
        functools.partial(_att_kernel, seq_len=seq_len, n_keys=n_keys, latent=latent),
        grid=(batch,),
        in_specs=in_specs,
        out_specs=out_specs,
        out_shape=out_shape,
        scratch_shapes=[pltpu.VMEM((ATT_KV_HEADS, n_keys, hd), BF16),
                        pltpu.VMEM((ATT_KV_HEADS, n_keys, 2 * hd), BF16),
                        pltpu.VMEM((rows, hd), BF16), pltpu.VMEM((rows, hd), BF16),
                        pltpu.VMEM((rows, n_keys), F32), pltpu.VMEM((rows, n_keys), F32),
                        pltpu.VMEM((rows, hd), F32), pltpu.VMEM((rows, hd), F32),
                        pltpu.VMEM((rows, 2 * hd), F32), pltpu.VMEM((rows, 2 * hd), F32)],
        compiler_params=_cparams(1),
        name="attention",
    )(*args)


def _rope_tables(n_tokens):
    t = jnp.arange(n_tokens)
    row = (t // GRID_W).astype(F32)
    col = (t % GRID_W).astype(F32)
    inv = ROPE_THETA ** (-jnp.arange(0, AXIS_DIM, 2, dtype=F32) / AXIS_DIM)
    ang = jnp.concatenate([row[:, None] * inv[None, :], col[:, None] * inv[None, :]], axis=-1)
    cos = jnp.repeat(jnp.cos(ang), 2, axis=-1)
    sin = jnp.repeat(jnp.sin(ang), 2, axis=-1)
    sign = jnp.where(jnp.arange(ATT_HEAD_DIM) % 2 == 0, -1.0, 1.0).astype(F32)
    return cos, sin * sign


def kernel(x_prompt, x_sample, c, state_hgrn, cache_k, cache_v, c_ctx, ada_w, ada_b, norm_pre, norm_post, rec_w_in, rec_lb_logits, rec_head_norm, pool_w, pool_scale, rec_w_out, att_w_in, att_q_norm, att_k_norm, att_w_out):
    nb_c, len_c, _ = x_prompt.shape
    nb_l, len_l, _ = x_sample.shape

    lb_p = jax.nn.softmax(rec_lb_logits.astype(F32), axis=0)
    lower_bounds = jnp.clip(jnp.cumsum(lb_p, axis=0) - lb_p[0], 0.0, 1.0)
    lower_bounds = lower_bounds.reshape(N_REC, 2, REC_HEADS, 1, REC_HEAD_DIM)
    cos, sin_signed = _rope_tables(len_l)

    cvec = jnp.zeros((MOD_ROWS, D_MODEL), F32).at[0].set(c_ctx).at[1:1 + nb_l].set(c)
    mod = _modulation(cvec, ada_w, ada_b)

    forget_cols = (2 * POOL_WIDTH + REC_WIDTH, 2 * POOL_WIDTH + 3 * REC_WIDTH)
    layers = []
    for i in range(DEPTH):
        j = i // 2
        rec = i % 2 == 0
        w_in, w_out = (rec_w_in, rec_w_out) if rec else (att_w_in, att_w_out)
        layers.append(dict(
            w_in=w_in[j].astype(BF16), w_out=w_out[j].astype(BF16), f32_cols=forget_cols if rec else (0, 0),
            gain_pre=norm_pre[i].reshape(1, D_MODEL), gain_post=norm_post[i].reshape(1, D_MODEL)))

    def pre_args(i):
        return (layers[i]["w_in"], i, layers[i]["gain_pre"], layers[i]["f32_cols"])

    xs = [x_prompt.reshape(nb_c * len_c, D_MODEL), x_sample.reshape(nb_l * len_l, D_MODEL)]
    streams = ((nb_c, len_c, 0), (nb_l, len_l, 1))
    new_states, new_k, new_v = [], [], []
    for s, (nb, sl, row0) in enumerate(streams):
        x = xs[s]
        zs = _proj(x, mod, sl, row0, pre=pre_args(0))
        for i in range(DEPTH):
            j = i // 2
            if i % 2 == 0:
                z, zf = zs
                pw = pool_w[j].astype(BF16)
                ps = pool_scale[j].reshape(1, POOL_WIDTH)
                hn = rec_head_norm[j].reshape(1, REC_HEAD_DIM)
                y_pool = _pool_mixer(z, nb, sl, pw, ps)
                s0 = (state_hgrn, j) if s == 1 else None
                y_rec, st = _rec_mixer(z, zf, nb, sl, lower_bounds[j], hn, s0)
                if st is not None:
                    new_states.append(st)
                halves = (y_pool, 0, y_rec, 0)
            else:
                (z,) = zs
                qn = att_q_norm[j].reshape(1, ATT_HEAD_DIM)
                kn = att_k_norm[j].reshape(1, ATT_HEAD_DIM)
                if s == 0:
                    y, k_new, v_new = _attention(z, nb, sl, qn, kn)
                    new_k.append(k_new.reshape(nb, sl, ATT_KV_HEADS, ATT_HEAD_DIM))
                    new_v.append(v_new.reshape(nb, sl, ATT_KV_HEADS, ATT_HEAD_DIM))
                else:
                    (y,) = _attention(z, nb, sl, qn, kn, (cache_k, cache_v, j, cos, sin_signed))
                halves = (y, 0, y, 1)
            post = halves + (layers[i]["w_out"], i, layers[i]["gain_post"])
            x, *zs = _proj(x, mod, sl, row0, post=post, pre=pre_args(i + 1) if i + 1 < DEPTH else None)
        xs[s] = x
    xc, xl = xs

    return (xc.reshape(nb_c, len_c, D_MODEL), xl.reshape(nb_l, len_l, D_MODEL),
            jnp.stack(new_states, axis=1), jnp.stack(new_k, axis=1), jnp.stack(new_v, axis=1))
```

```python
import functools

import jax
import jax.numpy as jnp
from jax import lax
from jax.experimental import pallas as pl
from jax.experimental.pallas import tpu as pltpu

D_MODEL = 1024
DEPTH = 4
GRID_W = 64
N_REC = (DEPTH + 1) // 2
N_ATT = DEPTH // 2
POOL_WIDTH = D_MODEL // 2
POOL_WINDOWS = (2, 4, 8, 16)
POOL_GROUP_DIM = POOL_WIDTH // len(POOL_WINDOWS)
REC_WIDTH = D_MODEL // 2
REC_HEAD_DIM = 128
REC_HEADS = REC_WIDTH // REC_HEAD_DIM
REC_IN_WIDTH = 2 * POOL_WIDTH + 5 * REC_WIDTH
ATT_HEAD_DIM = 128
ATT_HEADS = D_MODEL // ATT_HEAD_DIM
ATT_KV_HEADS = 2
ATT_REP = ATT_HEADS // ATT_KV_HEADS
ATT_WIDTH = ATT_HEADS * ATT_HEAD_DIM
KV_WIDTH = ATT_KV_HEADS * ATT_HEAD_DIM
ATT_IN_WIDTH = 2 * ATT_WIDTH + 2 * KV_WIDTH
AXIS_DIM = ATT_HEAD_DIM // 2
ROPE_THETA = 10000.0
EPS = 1e-6
F_MIN = 1e-6

MOD_ROWS = 16
TOKEN_TILE = 1024
MIN_PROJ_STEPS = 8
PROJ_ROW_GROUPS = 1
POOL_ROWS = 256
IN_PROJ_STEP = 512
MOD_COLS = 1536
SCAN_CHUNK = 128
SCAN_UNITS = 8
POOL_HALO = 8
ATT_Q_TILE = 128
ATT_KEY_TILE = 128
LOG2_E = 1.4426950408889634
VMEM_LIMIT = 56 * 1024 * 1024

F32 = jnp.float32
BF16 = jnp.bfloat16


def _cparams(n_axes):
    return pltpu.CompilerParams(
        dimension_semantics=("arbitrary",) * n_axes, vmem_limit_bytes=VMEM_LIMIT)


def _rms(x, g):
    return x * lax.rsqrt(jnp.mean(x * x, axis=-1, keepdims=True) + EPS) * g


def _dot(a, b):
    return jnp.dot(a, b, preferred_element_type=F32)


def _dot_nt(a, b):
    return lax.dot_general(a, b, (((1,), (1,)), ((), ())), preferred_element_type=F32)


def _mod_kernel(cv_ref, w_ref, b_ref, o_ref):
    cv = cv_ref[...]
    a = (cv * jax.nn.sigmoid(cv)).astype(BF16)
    o_ref[...] = _dot(a, w_ref[...].astype(BF16)) + b_ref[...]


def _modulation(cvec, ada_w, ada_b):
    tn = MOD_COLS
    out = pl.pallas_call(
        _mod_kernel,
        grid=(DEPTH, 3 * D_MODEL // tn),
        in_specs=[
            pl.BlockSpec((MOD_ROWS, D_MODEL), lambda l, n: (0, 0)),
            pl.BlockSpec((None, D_MODEL, tn), lambda l, n: (l, 0, n)),
            pl.BlockSpec((None, 1, tn), lambda l, n: (l, 0, n)),
        ],
        out_specs=pl.BlockSpec((None, MOD_ROWS, tn), lambda l, n: (l, 0, n)),
        out_shape=jax.ShapeDtypeStruct((DEPTH, MOD_ROWS, 3 * D_MODEL), F32),
        compiler_params=_cparams(2),
        name="modulation",
    )(cvec, ada_w, ada_b.reshape(DEPTH, 1, 3 * D_MODEL))
    return out.reshape(DEPTH, MOD_ROWS, 3, 1, D_MODEL)


def _mod_spec(layer, seq_len, row0, tile):
    assert row0 == 0 or seq_len % tile == 0
    tiles_per_seq = seq_len // tile
    if row0 == 0:
        index = lambda i: (layer, 0, 0, 0, 0)
    else:
        index = lambda i: (layer, row0 + i // tiles_per_seq, 0, 0, 0)
    return pl.BlockSpec((None, None, 3, 1, D_MODEL), index)


def _proj_kernel(*refs, post, pre, n_out, f32_cols):
    refs = list(refs)
    x_ref = refs.pop(0)
    if post:
        ya_ref, yb_ref, wo_ref, modp_ref, gpost_ref = (refs.pop(0) for _ in range(5))
    if pre:
        modn_ref, gpre_ref, wi_ref = (refs.pop(0) for _ in range(3))
    xo_ref = refs.pop(0) if post else None
    rows_all = x_ref.shape[0]
    for r0 in range(0, rows_all, rows_all // PROJ_ROW_GROUPS):
        rows = slice(r0, r0 + rows_all // PROJ_ROW_GROUPS)
        x = x_ref[rows, :]
        if post:
            half = D_MODEL // 2
            p = _dot(ya_ref[rows, :], wo_ref[0:half, :]) + _dot(yb_ref[rows, :], wo_ref[half:D_MODEL, :])
            x = x + modp_ref[2] * _rms(p, gpost_ref[...])
            xo_ref[rows, :] = x
        if pre:
            h = (_rms(x, gpre_ref[...]) * (1.0 + modn_ref[1]) + modn_ref[0]).astype(BF16)
            lo, hi = f32_cols
            for c0 in range(0, n_out, IN_PROJ_STEP):
                y = _dot(h, wi_ref[:, c0:c0 + IN_PROJ_STEP])
                if lo <= c0 < hi:
                    refs[1][rows, c0 - lo:c0 - lo + IN_PROJ_STEP] = y
                else:
                    c1 = c0 if c0 < lo else c0 - (hi - lo)
                    refs[0][rows, c1:c1 + IN_PROJ_STEP] = y.astype(BF16)


def _proj(x, mod, seq_len, row0, post=None, pre=None):
    t = x.shape[0]
    tile = min(TOKEN_TILE, t // MIN_PROJ_STEPS)
    half = D_MODEL // 2
    row = lambda i: (i, 0)
    fixed = lambda i: (0, 0)
    resident = dict(index_map=fixed, pipeline_mode=pl.Buffered(1))
    in_specs = [pl.BlockSpec((tile, D_MODEL), row)]
    args = [x]
    out_specs, out_shape = [], []
    n_out, f32_cols = 0, (0, 0)
    if post:
        ya, ia, yb, ib, w_out, layer, gain = post
        in_specs += [pl.BlockSpec((tile, half), lambda i: (i, ia)),
                     pl.BlockSpec((tile, half), lambda i: (i, ib)),
                     pl.BlockSpec((D_MODEL, D_MODEL), **resident),
                     _mod_spec(layer, seq_len, row0, tile),
                     pl.BlockSpec((1, D_MODEL), fixed)]
        args += [ya, yb, w_out, mod, gain]
        out_specs.append(pl.BlockSpec((tile, D_MODEL), row))
        out_shape.append(jax.ShapeDtypeStruct((t, D_MODEL), F32))
    if pre:
        w_in, layer, gain, f32_cols = pre
        n_out = w_in.shape[1]
        lo, hi = f32_cols
        assert lo % IN_PROJ_STEP == 0 and hi % IN_PROJ_STEP == 0 and n_out % IN_PROJ_STEP == 0
        in_specs += [_mod_spec(layer, seq_len, row0, tile),
                     pl.BlockSpec((1, D_MODEL), fixed),
                     pl.BlockSpec((D_MODEL, n_out), **resident)]
        args += [mod, gain, w_in]
        out_specs.append(pl.BlockSpec((tile, n_out - (hi - lo)), row))
        out_shape.append(jax.ShapeDtypeStruct((t, n_out - (hi - lo)), BF16))
        if hi > lo:
            out_specs.append(pl.BlockSpec((tile, hi - lo), row))
            out_shape.append(jax.ShapeDtypeStruct((t, hi - lo), F32))
    return pl.pallas_call(
        functools.partial(_proj_kernel, post=bool(post), pre=bool(pre), n_out=n_out, f32_cols=f32_cols),
        grid=(t // tile,),
        in_specs=in_specs,
        out_specs=out_specs,
        out_shape=out_shape,
        compiler_params=_cparams(1),
        name="proj",
    )(*args)


def _pool_kernel(u_ref, gp_ref, pw_ref, ps_ref, o_ref, pad_ref, *, seq_len):
    gd = POOL_GROUP_DIM
    zeros = jnp.zeros((POOL_HALO, POOL_WIDTH), F32)
    pad_ref[0:POOL_HALO, :] = zeros
    pad_ref[POOL_HALO + seq_len:2 * POOL_HALO + seq_len, :] = zeros
    pad_ref[POOL_HALO:POOL_HALO + seq_len, :] = u_ref[...].astype(F32)
    rows = POOL_ROWS
    for g, win in enumerate(POOL_WINDOWS):
        cols = slice(g * gd, (g + 1) * gd)
        for r in range(seq_len // rows):
            base = r * rows
            acc = None
            for j in range(-(win // 2), win // 2):
                start = POOL_HALO + base + j
                part = pad_ref[start:start + rows, cols]
                acc = part if acc is None else acc + part
            t = base + lax.broadcasted_iota(jnp.int32, (rows, 1), 0)
            lo = jnp.clip(t - win // 2, 0, seq_len)
            hi = jnp.clip(t + win // 2, 0, seq_len)
            mean = acc / (hi - lo).astype(F32)
            d = mean - pad_ref[POOL_HALO + base:POOL_HALO + base + rows, cols]
            y = _dot(d.astype(BF16), pw_ref[g]) * ps_ref[:, cols]
            gate = gp_ref[base:base + rows, cols].astype(F32)
            o_ref[base:base + rows, cols] = (y * (gate * jax.nn.sigmoid(gate))).astype(BF16)


def _pool_mixer(z, batch, seq_len, pool_w, pool_scale):
    return pl.pallas_call(
        functools.partial(_pool_kernel, seq_len=seq_len),
        grid=(batch,),
        in_specs=[
            pl.BlockSpec((seq_len, POOL_WIDTH), lambda b: (b, 0)),
            pl.BlockSpec((seq_len, POOL_WIDTH), lambda b: (b, 1)),
            pl.BlockSpec((len(POOL_WINDOWS), POOL_GROUP_DIM, POOL_GROUP_DIM), lambda b: (0, 0, 0)),
            pl.BlockSpec((1, POOL_WIDTH), lambda b: (0, 0)),
        ],
        out_specs=pl.BlockSpec((seq_len, POOL_WIDTH), lambda b: (b, 0)),
        out_shape=jax.ShapeDtypeStruct((batch * seq_len, POOL_WIDTH), BF16),
        scratch_shapes=[pltpu.VMEM((seq_len + 2 * POOL_HALO, POOL_WIDTH), F32)],
        compiler_params=_cparams(1),
        name="pool_mixer",
    )(z, z, pool_w, pool_scale)


def _block_row(p, block, row):
    c, w = p.shape
    p3 = p.reshape(c // block, block, w)
    return jnp.broadcast_to(p3[:, row:row + 1, :], p3.shape).reshape(c, w)


def _interleave(lo, hi, block):
    half = block // 2
    parts = []
    for b in range(SCAN_CHUNK // block):
        parts.append(lo[b * block:b * block + half])
        parts.append(hi[b * block + half:(b + 1) * block])
    return jnp.concatenate(parts, axis=0)


def _rec_kernel(*refs, seq_len, heads, has_s0, out_state):
    q_ref, ff_ref, fb_ref, v_ref, gr_ref, lb_ref, hn_ref = refs[:7]
    rest = list(refs[7:])
    s0_ref = rest.pop(0) if has_s0 else None
    o_ref = rest.pop(0)
    st_ref = rest.pop(0) if out_state else None
    acc_ref, qs_ref, k_ref, p_ref, a_ref, qin_ref, u_ref, dec_ref, tri_ref, lvl_ref = rest

    c = SCAN_CHUNK
    hd = REC_HEAD_DIM
    n_chunks = seq_len // c
    zf_refs = (ff_ref, fb_ref)
    units = [(h, n) for h in range(heads) for n in range(n_chunks)]

    def rows_in(n):
        return slice(n * c, (n + 1) * c)

    def rows_sc(h, n):
        return slice((h * n_chunks + n) * c, (h * n_chunks + n + 1) * c)

    def cols(h):
        return slice(h * hd, (h + 1) * hd)

    ti = lax.broadcasted_iota(jnp.int32, (c, c), 0)
    si = lax.broadcasted_iota(jnp.int32, (c, c), 1)
    x = ti ^ si
    lvl = jnp.where(x < 8, 0, jnp.where(x < 16, 1, jnp.where(x < 32, 2, jnp.where(x < 64, 3, 4))))
    for d, causal in enumerate((si <= ti, si >= ti)):
        tri_ref[d] = jnp.where(causal, 1.0, 0.0).astype(BF16)
        lvl_ref[d] = jnp.where(causal, lvl, 5)

    for h, n in units:
        qz = q_ref[rows_in(n), cols(h)].astype(F32)
        qs_ref[rows_sc(h, n), :] = qz * jax.nn.sigmoid(qz)
        for d in range(2):
            lower = lb_ref[d, h]
            f = jnp.clip(lower + (1.0 - lower) * jax.nn.sigmoid(zf_refs[d][rows_in(n), cols(h)]), F_MIN, 1.0)
            k_ref[d, rows_sc(h, n), :] = 1.0 - f
            g = jnp.log(f) * LOG2_E
            g_hi = g.astype(BF16)
            rest = g - g_hi.astype(F32)
            g_mid = rest.astype(BF16)
            g_lo = (rest - g_mid.astype(F32)).astype(BF16)
            sums = _dot(tri_ref[d], jnp.concatenate([g_hi, g_mid, g_lo], axis=1))
            p_ref[d, rows_sc(h, n), :] = (sums[:, 0:hd] + sums[:, hd:2 * hd]) + sums[:, 2 * hd:3 * hd]

    for h, n in units:
        rows = rows_sc(h, n)
        q = qs_ref[rows, :]
        v_t = v_ref[rows_in(n), cols(h)].astype(F32).T.astype(BF16)
        for d in range(2):
            rev = d == 1
            level = lvl_ref[d]
            k = k_ref[d, rows, :]
            p = p_ref[d, rows, :]
            e0 = p - _block_row(p, 8, 4 if rev else 3)
            a = _dot_nt((q * jnp.exp2(e0)).astype(BF16), (k * jnp.exp2(-e0)).astype(BF16))
            a = jnp.where(level == 0, a, 0.0)
            for lv, block in enumerate((16, 32, 64, 128), 1):
                beta = _block_row(p, block, block // 2 if rev else block // 2 - 1)
                if rev:
                    e = _interleave(p, beta, block) - _interleave(beta, p, block)
                    src = _interleave(q, k, block)
                else:
                    e = _interleave(beta, p, block) - _interleave(p, beta, block)
                    src = _interleave(k, q, block)
                m = (src * jnp.exp2(e)).astype(BF16)
                a = jnp.where(level == lv, _dot_nt(m, m), a)
            a_ref[rows, d * c:(d + 1) * c] = a.astype(BF16)

            edge = p[0:1, :] if rev else p[c - 1:c, :]
            qin_ref[d, rows, :] = (q * jnp.exp2(p)).astype(BF16)
            u_ref[d, h * n_chunks + n] = _dot(v_t, (k * jnp.exp2(edge - p)).astype(BF16))
            dec_ref[d, h * n_chunks + n] = jnp.broadcast_to(jnp.exp2(edge), (8, hd))

    for h, n in units:
        v_b = v_ref[rows_in(n), cols(h)]
        acc_ref[rows_sc(h, n), :] = _dot(a_ref[rows_sc(h, n), :], jnp.concatenate([v_b, v_b], axis=0))

    for h in range(heads):
        if has_s0:
            states = [s0_ref[0, h].T, s0_ref[1, h].T]
        else:
            states = [jnp.zeros((hd, hd), F32)] * 2
        inter = [[None] * n_chunks, [None] * n_chunks]
        for i in range(n_chunks):
            for d, n in ((0, i), (1, n_chunks - 1 - i)):
                inter[d][n] = _dot_nt(qin_ref[d, rows_sc(h, n), :], states[d].astype(BF16))
                states[d] = dec_ref[d, h * n_chunks + n][0:1, :] * states[d] + u_ref[d, h * n_chunks + n]
        for n in range(n_chunks):
            gate = gr_ref[rows_in(n), cols(h)].astype(F32)
            o = acc_ref[rows_sc(h, n), :] + inter[0][n] + inter[1][n]
            o_ref[rows_in(n), cols(h)] = (_rms(o, hn_ref[...]) * (gate * jax.nn.sigmoid(gate))).astype(BF16)
        if out_state:
            st_ref[0, h] = states[0].T
            st_ref[1, h] = states[1].T


def _rec_mixer(z, zf, batch, seq_len, lower, head_norm, s0):
    hd = REC_HEAD_DIM
    nh = REC_HEADS
    n_chunks = seq_len // SCAN_CHUNK
    heads = min(nh, max(1, SCAN_UNITS // n_chunks))
    width = heads * hd
    col0 = 2 * POOL_WIDTH // width

    def zspec(part):
        return pl.BlockSpec((seq_len, width), lambda b, g: (b, col0 + part * (nh // heads) + g))

    def fspec(part):
        return pl.BlockSpec((seq_len, width), lambda b, g: (b, part * (nh // heads) + g))

    in_specs = [zspec(0), fspec(0), fspec(1), zspec(1), zspec(2),
                pl.BlockSpec((2, heads, 1, hd), lambda b, g: (0, g, 0, 0)),
                pl.BlockSpec((1, hd), lambda b, g: (0, 0))]
    args = [z, zf, zf, z, z, lower, head_norm]
    has_s0 = s0 is not None
    if has_s0:
        s0_arr, j = s0
        in_specs.append(pl.BlockSpec((None, None, 2, heads, hd, hd), lambda b, g: (b, j, 0, g, 0, 0)))
        args.append(s0_arr)
    out_state = not has_s0
    out_specs = [pl.BlockSpec((seq_len, width), lambda b, g: (b, g))]
    out_shape = [jax.ShapeDtypeStruct((batch * seq_len, REC_WIDTH), BF16)]
    if out_state:
        out_specs.append(pl.BlockSpec((None, 2, heads, hd, hd), lambda b, g: (b, 0, g, 0, 0)))
        out_shape.append(jax.ShapeDtypeStruct((batch, 2, nh, hd, hd), F32))
    tokens = heads * seq_len
    outs = pl.pallas_call(
        functools.partial(_rec_kernel, seq_len=seq_len, heads=heads, has_s0=has_s0, out_state=out_state),
        grid=(batch, nh // heads),
        in_specs=in_specs,
        out_specs=out_specs,
        out_shape=out_shape,
        scratch_shapes=[pltpu.VMEM((tokens, hd), F32),
                        pltpu.VMEM((tokens, hd), F32),
                        pltpu.VMEM((2, tokens, hd), F32),
                        pltpu.VMEM((2, tokens, hd), F32),
                        pltpu.VMEM((tokens, 2 * SCAN_CHUNK), BF16),
                        pltpu.VMEM((2, tokens, hd), BF16),
                        pltpu.VMEM((2, tokens // SCAN_CHUNK, hd, hd), F32),
                        pltpu.VMEM((2, tokens // SCAN_CHUNK, 8, hd), F32),
                        pltpu.VMEM((2, SCAN_CHUNK, SCAN_CHUNK), BF16),
                        pltpu.VMEM((2, SCAN_CHUNK, SCAN_CHUNK), jnp.int32)],
        compiler_params=_cparams(2),
        name="rec_mixer",
    )(*args)
    return outs if out_state else (outs[0], None)


def _rope(x, cos, sin_signed):
    lane = lax.broadcasted_iota(jnp.int32, x.shape, 1)
    partner = jnp.where((lane & 1) == 0, pltpu.roll(x, ATT_HEAD_DIM - 1, 1), pltpu.roll(x, 1, 1))
    return x * cos + partner * sin_signed


def _att_kernel(*refs, seq_len, n_keys, latent):
    q_ref, k_ref, v_ref, qn_ref, kn_ref = refs[:5]
    g_refs = refs[5:5 + ATT_KV_HEADS]
    if latent:
        ck_ref, cv_ref, cos_ref, sin_ref, y_ref, kall, vall = refs[5 + ATT_KV_HEADS:-8]
    else:
        y_ref, ko_ref, vo_ref, kall, vall = refs[5 + ATT_KV_HEADS:-8]
    qs_bufs, s_bufs, m_bufs, o_bufs = refs[-8:-6], refs[-6:-4], refs[-4:-2], refs[-2:]
    hd = ATT_HEAD_DIM
    qw = ATT_REP * hd
    tq = ATT_Q_TILE
    tk = min(ATT_KEY_TILE, n_keys)
    n_kb = n_keys // tk
    n_past = n_keys - seq_len
    n_items = ATT_KV_HEADS * (seq_len // tq)
    exp2_scale = hd ** -0.5 * LOG2_E
    assert ATT_KV_HEADS == 2

    lane = lax.broadcasted_iota(jnp.int32, (n_keys, hd), 1)
    for g in range(ATT_KV_HEADS):
        k = _rms(k_ref[:, g * hd:(g + 1) * hd].astype(F32), kn_ref[...])
        v = v_ref[:, g * hd:(g + 1) * hd].astype(F32)
        if latent:
            k = _rope(k, cos_ref[...], sin_ref[...])
            kall[g, 0:n_past, :] = ck_ref[:, g, :].astype(BF16)
            vall[g, 0:n_past, 0:hd] = cv_ref[:, g, :].astype(BF16)
        else:
            ko_ref[:, g, :] = k
            vo_ref[:, g, :] = v
        kall[g, n_past:n_keys, :] = k.astype(BF16)
        vall[g, n_past:n_keys, 0:hd] = v.astype(BF16)
        vall[g, :, hd:2 * hd] = (lane == 0).astype(BF16)

    def tile_rows(item):
        return pl.ds(pl.multiple_of((item // ATT_KV_HEADS) * tq, tq), tq)

    def prep(item, g):
        rows = tile_rows(item)
        for h in range(ATT_REP):
            q = _rms(q_ref[rows, g * qw + h * hd:g * qw + (h + 1) * hd].astype(F32), qn_ref[...])
            if latent:
                q = _rope(q, cos_ref[rows, :], sin_ref[rows, :])
            qs_bufs[g][h * tq:(h + 1) * tq, :] = q.astype(BF16)

    def scores(item, g):
        qs = qs_bufs[g][...]
        mx = None
        for kb in range(n_kb):
            s = _dot_nt(qs, kall[g, kb * tk:(kb + 1) * tk, :])
            s_bufs[g][:, kb * tk:(kb + 1) * tk] = s
            for c in range(tk // hd):
                part = s[:, c * hd:(c + 1) * hd]
                mx = part if mx is None else jnp.maximum(mx, part)
        m = jnp.max(mx, axis=-1, keepdims=True) * exp2_scale
        m_bufs[g][...] = jnp.broadcast_to(m, (ATT_REP * tq, hd))

    def mix(item, g):
        m = m_bufs[g][...]
        ps = []
        for c in range(n_keys // hd):
            s = s_bufs[g][:, c * hd:(c + 1) * hd]
            ps.append(jnp.exp2(s * exp2_scale - m).astype(BF16))
        o_bufs[g][...] = _dot(jnp.concatenate(ps, axis=1), vall[g])

    def finish(item, g):
        rows = tile_rows(item)
        o = o_bufs[g][:, 0:hd] / o_bufs[g][:, hd:hd + 1]
        for h in range(ATT_REP):
            gate = g_refs[g][rows, h * hd:(h + 1) * hd].astype(F32)
            y = o[h * tq:(h + 1) * tq, :] * (gate * jax.nn.sigmoid(gate))
            y_ref[rows, g * qw + h * hd:g * qw + (h + 1) * hd] = y.astype(BF16)

    stages = (prep, scores, mix, finish)

    def step(k, parity):
        for a, stage in enumerate(stages):
            item = k - a
            if isinstance(item, int) and not 0 <= item < n_items:
                continue
            stage(item, (parity + a) % 2)

    depth = len(stages) - 1
    for k in range(min(depth, n_items + depth)):
        step(k, k % 2)
    n_steady = max(n_items - depth, 0)

    def steady_pair(j, carry):
        k = depth + 2 * j
        step(k, depth % 2)
        step(k + 1, (depth + 1) % 2)
        return carry

    lax.fori_loop(0, n_steady // 2, steady_pair, 0)
    for k in range(depth + 2 * (n_steady // 2), n_items + depth):
        step(k, k % 2)


def _attention(z, batch, seq_len, q_norm, k_norm, cache=None):
    hd = ATT_HEAD_DIM
    latent = cache is not None
    qw = ATT_REP * hd
    kv0 = ATT_WIDTH // KV_WIDTH
    gate0 = (ATT_WIDTH + 2 * KV_WIDTH) // qw
    assert (ATT_WIDTH + 2 * KV_WIDTH) % qw == 0 and ATT_WIDTH % KV_WIDTH == 0
    in_specs = [
        pl.BlockSpec((seq_len, ATT_WIDTH), lambda b: (b, 0)),
        pl.BlockSpec((seq_len, KV_WIDTH), lambda b: (b, kv0)),
        pl.BlockSpec((seq_len, KV_WIDTH), lambda b: (b, kv0 + 1)),
        pl.BlockSpec((1, hd), lambda b: (0, 0)),
        pl.BlockSpec((1, hd), lambda b: (0, 0)),
    ] + [pl.BlockSpec((seq_len, qw), lambda b, g=g: (b, gate0 + g)) for g in range(ATT_KV_HEADS)]
    args = [z, z, z, q_norm, k_norm] + [z] * ATT_KV_HEADS
    out_specs = [pl.BlockSpec((seq_len, ATT_WIDTH), lambda b: (b, 0))]
    out_shape = [jax.ShapeDtypeStruct((batch * seq_len, ATT_WIDTH), BF16)]
    n_keys = seq_len
    if latent:
        cache_k, cache_v, j, cos, sin_signed = cache
        past = cache_k.shape[2]
        n_keys += past
        cspec = pl.BlockSpec((None, None, past, ATT_KV_HEADS, hd), lambda b: (b, j, 0, 0, 0))
        tspec = pl.BlockSpec((seq_len, hd), lambda b: (0, 0))
        in_specs += [cspec, cspec, tspec, tspec]
        args += [cache_k, cache_v, cos, sin_signed]
    else:
        kv_spec = pl.BlockSpec((None, seq_len, ATT_KV_HEADS, hd), lambda b: (b, 0, 0, 0))
        out_specs += [kv_spec, kv_spec]
        out_shape += [jax.ShapeDtypeStruct((batch, seq_len, ATT_KV_HEADS, hd), F32)] * 2
    rows = ATT_REP * ATT_Q_TILE
    return pl.pallas_call(
        functools.partial(_att_kernel, seq_len=seq_len, n_keys=n_keys, latent=latent),
        grid=(batch,),
        in_specs=in_specs,
        out_specs=out_specs,
        out_shape=out_shape,
        scratch_shapes=[pltpu.VMEM((ATT_KV_HEADS, n_keys, hd), BF16),
                        pltpu.VMEM((ATT_KV_HEADS, n_keys, 2 * hd), BF16),
                        pltpu.VMEM((rows, hd), BF16), pltpu.VMEM((rows, hd), BF16),
                        pltpu.VMEM((rows, n_keys), F32), pltpu.VMEM((rows, n_keys), F32),
                        pltpu.VMEM((rows, hd), F32), pltpu.VMEM((rows, hd), F32),
                        pltpu.VMEM((rows, 2 * hd), F32), pltpu.VMEM((rows, 2 * hd), F32)],
        compiler_params=_cparams(1),
        name="attention",
    )(*args)


def _rope_tables(n_tokens):
    t = jnp.arange(n_tokens)
    row = (t // GRID_W).astype(F32)
    col = (t % GRID_W).astype(F32)
    inv = ROPE_THETA ** (-jnp.arange(0, AXIS_DIM, 2, dtype=F32) / AXIS_DIM)
    ang = jnp.concatenate([row[:, None] * inv[None, :], col[:, None] * inv[None, :]], axis=-1)
    cos = jnp.repeat(jnp.cos(ang), 2, axis=-1)
    sin = jnp.repeat(jnp.sin(ang), 2, axis=-1)
    sign = jnp.where(jnp.arange(ATT_HEAD_DIM) % 2 == 0, -1.0, 1.0).astype(F32)
    return cos, sin * sign


def kernel(x_prompt, x_sample, c, state_hgrn, cache_k, cache_v, c_ctx, ada_w, ada_b, norm_pre, norm_post, rec_w_in, rec_lb_logits, rec_head_norm, pool_w, pool_scale, rec_w_out, att_w_in, att_q_norm, att_k_norm, att_w_out):
    nb_c, len_c, _ = x_prompt.shape
    nb_l, len_l, _ = x_sample.shape

    lb_p = jax.nn.softmax(rec_lb_logits.astype(F32), axis=0)
    lower_bounds = jnp.clip(jnp.cumsum(lb_p, axis=0) - lb_p[0], 0.0, 1.0)
    lower_bounds = lower_bounds.reshape(N_REC, 2, REC_HEADS, 1, REC_HEAD_DIM)
    cos, sin_signed = _rope_tables(len_l)

    cvec = jnp.zeros((MOD_ROWS, D_MODEL), F32).at[0].set(c_ctx).at[1:1 + nb_l].set(c)
    mod = _modulation(cvec, ada_w, ada_b)

    forget_cols = (2 * POOL_WIDTH + REC_WIDTH, 2 * POOL_WIDTH + 3 * REC_WIDTH)
    layers = []
    for i in range(DEPTH):
        j = i // 2
        rec = i % 2 == 0
        w_in, w_out = (rec_w_in, rec_w_out) if rec else (att_w_in, att_w_out)
        layers.append(dict(
            w_in=w_in[j].astype(BF16), w_out=w_out[j].astype(BF16), f32_cols=forget_cols if rec else (0, 0),
            gain_pre=norm_pre[i].reshape(1, D_MODEL), gain_post=norm_post[i].reshape(1, D_MODEL)))

    def pre_args(i):
        return (layers[i]["w_in"], i, layers[i]["gain_pre"], layers[i]["f32_cols"])

    xs = [x_prompt.reshape(nb_c * len_c, D_MODEL), x_sample.reshape(nb_l * len_l, D_MODEL)]
    streams = ((nb_c, len_c, 0), (nb_l, len_l, 1))
    new_states, new_k, new_v = [], [], []
    for s, (nb, sl, row0) in enumerate(streams):
        x = xs[s]
        zs = _proj(x, mod, sl, row0, pre=pre_args(0))
        for i in range(DEPTH):
            j = i // 2
            if i % 2 == 0:
                z, zf = zs
                pw = pool_w[j].astype(BF16)
                ps = pool_scale[j].reshape(1, POOL_WIDTH)
                hn = rec_head_norm[j].reshape(1, REC_HEAD_DIM)
                y_pool = _pool_mixer(z, nb, sl, pw, ps)
                s0 = (state_hgrn, j) if s == 1 else None
                y_rec, st = _rec_mixer(z, zf, nb, sl, lower_bounds[j], hn, s0)
                if st is not None:
                    new_states.append(st)
                halves = (y_pool, 0, y_rec, 0)
            else:
                (z,) = zs
                qn = att_q_norm[j].reshape(1, ATT_HEAD_DIM)
                kn = att_k_norm[j].reshape(1, ATT_HEAD_DIM)
                if s == 0:
                    y, k_new, v_new = _attention(z, nb, sl, qn, kn)
                    new_k.append(k_new)
                    new_v.append(v_new)
                else:
                    (y,) = _attention(z, nb, sl, qn, kn, (cache_k, cache_v, j, cos, sin_signed))
                halves = (y, 0, y, 1)
            post = halves + (layers[i]["w_out"], i, layers[i]["gain_post"])
            x, *zs = _proj(x, mod, sl, row0, post=post, pre=pre_args(i + 1) if i + 1 < DEPTH else None)
        xs[s] = x
    xc, xl = xs

    return (xc.reshape(nb_c, len_c, D_MODEL), xl.reshape(nb_l, len_l, D_MODEL),
            jnp.stack(new_states, axis=1), jnp.stack(new_k, axis=1), jnp.stack(new_v, axis=1))
```

```python
import functools

import jax
import jax.numpy as jnp
from jax import lax
from jax.experimental import pallas as pl
from jax.experimental.pallas import tpu as pltpu

D_MODEL = 1024
DEPTH = 4
GRID_W = 64
N_REC = (DEPTH + 1) // 2
N_ATT = DEPTH // 2
POOL_WIDTH = D_MODEL // 2
POOL_WINDOWS = (2, 4, 8, 16)
POOL_GROUP_DIM = POOL_WIDTH // len(POOL_WINDOWS)
REC_WIDTH = D_MODEL // 2
REC_HEAD_DIM = 128
REC_HEADS = REC_WIDTH // REC_HEAD_DIM
REC_IN_WIDTH = 2 * POOL_WIDTH + 5 * REC_WIDTH
ATT_HEAD_DIM = 128
ATT_HEADS = D_MODEL // ATT_HEAD_DIM
ATT_KV_HEADS = 2
ATT_REP = ATT_HEADS // ATT_KV_HEADS
ATT_WIDTH = ATT_HEADS * ATT_HEAD_DIM
KV_WIDTH = ATT_KV_HEADS * ATT_HEAD_DIM
ATT_IN_WIDTH = 2 * ATT_WIDTH + 2 * KV_WIDTH
AXIS_DIM = ATT_HEAD_DIM // 2
ROPE_THETA = 10000.0
EPS = 1e-6
F_MIN = 1e-6

MOD_ROWS = 16
TOKEN_TILE = 1024
MIN_PROJ_STEPS = 8
PROJ_ROW_GROUPS = 1
POOL_ROWS = 256
IN_PROJ_STEP = 512
MOD_COLS = 1536
SCAN_CHUNK = 128
SCAN_UNITS = 8
POOL_HALO = 8
ATT_Q_TILE = 128
ATT_KEY_TILE = 128
LOG2_E = 1.4426950408889634
VMEM_LIMIT = 56 * 1024 * 1024

F32 = jnp.float32
BF16 = jnp.bfloat16


def _cparams(n_axes):
    return pltpu.CompilerParams(
        dimension_semantics=("arbitrary",) * n_axes, vmem_limit_bytes=VMEM_LIMIT)


def _rms(x, g):
    return x * lax.rsqrt(jnp.mean(x * x, axis=-1, keepdims=True) + EPS) * g


def _dot(a, b):
    return jnp.dot(a, b, preferred_element_type=F32)


def _dot_nt(a, b):
    return lax.dot_general(a, b, (((1,), (1,)), ((), ())), preferred_element_type=F32)


def _mod_kernel(cv_ref, w_ref, b_ref, o_ref):
    cv = cv_ref[...]
    a = (cv * jax.nn.sigmoid(cv)).astype(BF16)
    o_ref[...] = _dot(a, w_ref[...].astype(BF16)) + b_ref[...]


def _modulation(cvec, ada_w, ada_b):
    tn = MOD_COLS
    out = pl.pallas_call(
        _mod_kernel,
        grid=(DEPTH, 3 * D_MODEL // tn),
        in_specs=[
            pl.BlockSpec((MOD_ROWS, D_MODEL), lambda l, n: (0, 0)),
            pl.BlockSpec((None, D_MODEL, tn), lambda l, n: (l, 0, n)),
            pl.BlockSpec((None, 1, tn), lambda l, n: (l, 0, n)),
        ],
        out_specs=pl.BlockSpec((None, MOD_ROWS, tn), lambda l, n: (l, 0, n)),
        out_shape=jax.ShapeDtypeStruct((DEPTH, MOD_ROWS, 3 * D_MODEL), F32),
        compiler_params=_cparams(2),
        name="modulation",
    )(cvec, ada_w, ada_b.reshape(DEPTH, 1, 3 * D_MODEL))
    return out.reshape(DEPTH, MOD_ROWS, 3, 1, D_MODEL)


def _mod_spec(layer, seq_len, row0, tile):
    assert row0 == 0 or seq_len % tile == 0
    tiles_per_seq = seq_len // tile
    if row0 == 0:
        index = lambda i: (layer, 0, 0, 0, 0)
    else:
        index = lambda i: (layer, row0 + i // tiles_per_seq, 0, 0, 0)
    return pl.BlockSpec((None, None, 3, 1, D_MODEL), index)


def _proj_kernel(*refs, post, pre, n_out, f32_cols):
    refs = list(refs)
    x_ref = refs.pop(0)
    if post:
        ya_ref, yb_ref, wo_ref, modp_ref, gpost_ref = (refs.pop(0) for _ in range(5))
    if pre:
        modn_ref, gpre_ref, wi_ref = (refs.pop(0) for _ in range(3))
    xo_ref = refs.pop(0) if post else None
    rows_all = x_ref.shape[0]
    for r0 in range(0, rows_all, rows_all // PROJ_ROW_GROUPS):
        rows = slice(r0, r0 + rows_all // PROJ_ROW_GROUPS)
        x = x_ref[rows, :]
        if post:
            half = D_MODEL // 2
            p = _dot(ya_ref[rows, :], wo_ref[0:half, :]) + _dot(yb_ref[rows, :], wo_ref[half:D_MODEL, :])
            x = x + modp_ref[2] * _rms(p, gpost_ref[...])
            xo_ref[rows, :] = x
        if pre:
            h = (_rms(x, gpre_ref[...]) * (1.0 + modn_ref[1]) + modn_ref[0]).astype(BF16)
            lo, hi = f32_cols
            for c0 in range(0, n_out, IN_PROJ_STEP):
                y = _dot(h, wi_ref[:, c0:c0 + IN_PROJ_STEP])
                if lo <= c0 < hi:
                    refs[1][rows, c0 - lo:c0 - lo + IN_PROJ_STEP] = y
                else:
                    c1 = c0 if c0 < lo else c0 - (hi - lo)
                    refs[0][rows, c1:c1 + IN_PROJ_STEP] = y.astype(BF16)


def _proj(x, mod, seq_len, row0, post=None, pre=None):
    t = x.shape[0]
    tile = min(TOKEN_TILE, t // MIN_PROJ_STEPS)
    half = D_MODEL // 2
    row = lambda i: (i, 0)
    fixed = lambda i: (0, 0)
    resident = dict(index_map=fixed, pipeline_mode=pl.Buffered(1))
    in_specs = [pl.BlockSpec((tile, D_MODEL), row)]
    args = [x]
    out_specs, out_shape = [], []
    n_out, f32_cols = 0, (0, 0)
    if post:
        ya, ia, yb, ib, w_out, layer, gain = post
        in_specs += [pl.BlockSpec((tile, half), lambda i: (i, ia)),
                     pl.BlockSpec((tile, half), lambda i: (i, ib)),
                     pl.BlockSpec((D_MODEL, D_MODEL), **resident),
                     _mod_spec(layer, seq_len, row0, tile),
                     pl.BlockSpec((1, D_MODEL), fixed)]
        args += [ya, yb, w_out, mod, gain]
        out_specs.append(pl.BlockSpec((tile, D_MODEL), row))
        out_shape.append(jax.ShapeDtypeStruct((t, D_MODEL), F32))
    if pre:
        w_in, layer, gain, f32_cols = pre
        n_out = w_in.shape[1]
        lo, hi = f32_cols
        assert lo % IN_PROJ_STEP == 0 and hi % IN_PROJ_STEP == 0 and n_out % IN_PROJ_STEP == 0
        in_specs += [_mod_spec(layer, seq_len, row0, tile),
                     pl.BlockSpec((1, D_MODEL), fixed),
                     pl.BlockSpec((D_MODEL, n_out), **resident)]
        args += [mod, gain, w_in]
        out_specs.append(pl.BlockSpec((tile, n_out - (hi - lo)), row))
        out_shape.append(jax.ShapeDtypeStruct((t, n_out - (hi - lo)), BF16))
        if hi > lo:
            out_specs.append(pl.BlockSpec((tile, hi - lo), row))
            out_shape.append(jax.ShapeDtypeStruct((t, hi - lo), F32))
    return pl.pallas_call(
        functools.partial(_proj_kernel, post=bool(post), pre=bool(pre), n_out=n_out, f32_cols=f32_cols),
        grid=(t // tile,),
        in_specs=in_specs,
        out_specs=out_specs,
        out_shape=out_shape,
        compiler_params=_cparams(1),
        name="proj",
    )(*args)


def _pool_kernel(u_ref, gp_ref, pw_ref, ps_ref, o_ref, pad_ref, *, seq_len):
    gd = POOL_GROUP_DIM
    zeros = jnp.zeros((POOL_HALO, POOL_WIDTH), F32)
    pad_ref[0:POOL_HALO, :] = zeros
    pad_ref[POOL_HALO + seq_len:2 * POOL_HALO + seq_len, :] = zeros
    pad_ref[POOL_HALO:POOL_HALO + seq_len, :] = u_ref[...].astype(F32)
    rows = POOL_ROWS
    for g, win in enumerate(POOL_WINDOWS):
        cols = slice(g * gd, (g + 1) * gd)
        for r in range(seq_len // rows):
            base = r * rows
            acc = None
            for j in range(-(win // 2), win // 2):
                start = POOL_HALO + base + j
                part = pad_ref[start:start + rows, cols]
                acc = part if acc is None else acc + part
            t = base + lax.broadcasted_iota(jnp.int32, (rows, 1), 0)
            lo = jnp.clip(t - win // 2, 0, seq_len)
            hi = jnp.clip(t + win // 2, 0, seq_len)
            mean = acc / (hi - lo).astype(F32)
            d = mean - pad_ref[POOL_HALO + base:POOL_HALO + base + rows, cols]
            y = _dot(d.astype(BF16), pw_ref[g]) * ps_ref[:, cols]
            gate = gp_ref[base:base + rows, cols].astype(F32)
            o_ref[base:base + rows, cols] = (y * (gate * jax.nn.sigmoid(gate))).astype(BF16)


def _pool_mixer(z, batch, seq_len, pool_w, pool_scale):
    return pl.pallas_call(
        functools.partial(_pool_kernel, seq_len=seq_len),
        grid=(batch,),
        in_specs=[
            pl.BlockSpec((seq_len, POOL_WIDTH), lambda b: (b, 0)),
            pl.BlockSpec((seq_len, POOL_WIDTH), lambda b: (b, 1)),
            pl.BlockSpec((len(POOL_WINDOWS), POOL_GROUP_DIM, POOL_GROUP_DIM), lambda b: (0, 0, 0)),
            pl.BlockSpec((1, POOL_WIDTH), lambda b: (0, 0)),
        ],
        out_specs=pl.BlockSpec((seq_len, POOL_WIDTH), lambda b: (b, 0)),
        out_shape=jax.ShapeDtypeStruct((batch * seq_len, POOL_WIDTH), BF16),
        scratch_shapes=[pltpu.VMEM((seq_len + 2 * POOL_HALO, POOL_WIDTH), F32)],
        compiler_params=_cparams(1),
        name="pool_mixer",
    )(z, z, pool_w, pool_scale)


def _block_row(p, block, row):
    c, w = p.shape
    p3 = p.reshape(c // block, block, w)
    return jnp.broadcast_to(p3[:, row:row + 1, :], p3.shape).reshape(c, w)


def _interleave(lo, hi, block):
    half = block // 2
    parts = []
    for b in range(SCAN_CHUNK // block):
        parts.append(lo[b * block:b * block + half])
        parts.append(hi[b * block + half:(b + 1) * block])
    return jnp.concatenate(parts, axis=0)


def _rec_kernel(*refs, seq_len, heads, has_s0, out_state, n_aliased):
    q_ref, ff_ref, fb_ref, v_ref, gr_ref, lb_ref, hn_ref = refs[:7]
    rest = list(refs[7:])
    s0_ref = rest.pop(0) if has_s0 else None
    del rest[:n_aliased]
    o_ref = rest.pop(0)
    st_ref = rest.pop(0) if out_state else None
    acc_ref, qs_ref, k_ref, p_ref, a_ref, qin_ref, u_ref, dec_ref, tri_ref, lvl_ref = rest

    c = SCAN_CHUNK
    hd = REC_HEAD_DIM
    n_chunks = seq_len // c
    zf_refs = (ff_ref, fb_ref)
    units = [(h, n) for h in range(heads) for n in range(n_chunks)]

    def rows_in(n):
        return slice(n * c, (n + 1) * c)

    def rows_sc(h, n):
        return slice((h * n_chunks + n) * c, (h * n_chunks + n + 1) * c)

    def cols(h):
        return slice(h * hd, (h + 1) * hd)

    ti = lax.broadcasted_iota(jnp.int32, (c, c), 0)
    si = lax.broadcasted_iota(jnp.int32, (c, c), 1)
    x = ti ^ si
    lvl = jnp.where(x < 8, 0, jnp.where(x < 16, 1, jnp.where(x < 32, 2, jnp.where(x < 64, 3, 4))))
    for d, causal in enumerate((si <= ti, si >= ti)):
        tri_ref[d] = jnp.where(causal, 1.0, 0.0).astype(BF16)
        lvl_ref[d] = jnp.where(causal, lvl, 5)

    for h, n in units:
        qz = q_ref[rows_in(n), cols(h)].astype(F32)
        qs_ref[rows_sc(h, n), :] = qz * jax.nn.sigmoid(qz)
        for d in range(2):
            lower = lb_ref[d, h]
            f = jnp.clip(lower + (1.0 - lower) * jax.nn.sigmoid(zf_refs[d][rows_in(n), cols(h)]), F_MIN, 1.0)
            k_ref[d, rows_sc(h, n), :] = 1.0 - f
            g = jnp.log(f) * LOG2_E
            g_hi = g.astype(BF16)
            rest = g - g_hi.astype(F32)
            g_mid = rest.astype(BF16)
            g_lo = (rest - g_mid.astype(F32)).astype(BF16)
            sums = _dot(tri_ref[d], jnp.concatenate([g_hi, g_mid, g_lo], axis=1))
            p_ref[d, rows_sc(h, n), :] = (sums[:, 0:hd] + sums[:, hd:2 * hd]) + sums[:, 2 * hd:3 * hd]

    for h, n in units:
        rows = rows_sc(h, n)
        q = qs_ref[rows, :]
        v_t = v_ref[rows_in(n), cols(h)].astype(F32).T.astype(BF16)
        for d in range(2):
            rev = d == 1
            level = lvl_ref[d]
            k = k_ref[d, rows, :]
            p = p_ref[d, rows, :]
            e0 = p - _block_row(p, 8, 4 if rev else 3)
            a = _dot_nt((q * jnp.exp2(e0)).astype(BF16), (k * jnp.exp2(-e0)).astype(BF16))
            a = jnp.where(level == 0, a, 0.0)
            for lv, block in enumerate((16, 32, 64, 128), 1):
                beta = _block_row(p, block, block // 2 if rev else block // 2 - 1)
                if rev:
                    e = _interleave(p, beta, block) - _interleave(beta, p, block)
                    src = _interleave(q, k, block)
                else:
                    e = _interleave(beta, p, block) - _interleave(p, beta, block)
                    src = _interleave(k, q, block)
                m = (src * jnp.exp2(e)).astype(BF16)
                a = jnp.where(level == lv, _dot_nt(m, m), a)
            a_ref[rows, d * c:(d + 1) * c] = a.astype(BF16)

            edge = p[0:1, :] if rev else p[c - 1:c, :]
            qin_ref[d, rows, :] = (q * jnp.exp2(p)).astype(BF16)
            u_ref[d, h * n_chunks + n] = _dot(v_t, (k * jnp.exp2(edge - p)).astype(BF16))
            dec_ref[d, h * n_chunks + n] = jnp.broadcast_to(jnp.exp2(edge), (8, hd))

    for h, n in units:
        v_b = v_ref[rows_in(n), cols(h)]
        acc_ref[rows_sc(h, n), :] = _dot(a_ref[rows_sc(h, n), :], jnp.concatenate([v_b, v_b], axis=0))

    for h in range(heads):
        if has_s0:
            states = [s0_ref[0, h].T, s0_ref[1, h].T]
        else:
            states = [jnp.zeros((hd, hd), F32)] * 2
        inter = [[None] * n_chunks, [None] * n_chunks]
        for i in range(n_chunks):
            for d, n in ((0, i), (1, n_chunks - 1 - i)):
                inter[d][n] = _dot_nt(qin_ref[d, rows_sc(h, n), :], states[d].astype(BF16))
                states[d] = dec_ref[d, h * n_chunks + n][0:1, :] * states[d] + u_ref[d, h * n_chunks + n]
        for n in range(n_chunks):
            gate = gr_ref[rows_in(n), cols(h)].astype(F32)
            o = acc_ref[rows_sc(h, n), :] + inter[0][n] + inter[1][n]
            o_ref[rows_in(n), cols(h)] = (_rms(o, hn_ref[...]) * (gate * jax.nn.sigmoid(gate))).astype(BF16)
        if out_state:
            st_ref[0, h] = states[0].T
            st_ref[1, h] = states[1].T


def _rec_mixer(z, zf, batch, seq_len, lower, head_norm, j, s0=None, states=None):
    hd = REC_HEAD_DIM
    nh = REC_HEADS
    n_chunks = seq_len // SCAN_CHUNK
    heads = min(nh, max(1, SCAN_UNITS // n_chunks))
    width = heads * hd
    col0 = 2 * POOL_WIDTH // width

    def zspec(part):
        return pl.BlockSpec((seq_len, width), lambda b, g: (b, col0 + part * (nh // heads) + g))

    def fspec(part):
        return pl.BlockSpec((seq_len, width), lambda b, g: (b, part * (nh // heads) + g))

    in_specs = [zspec(0), fspec(0), fspec(1), zspec(1), zspec(2),
                pl.BlockSpec((2, heads, 1, hd), lambda b, g: (0, g, 0, 0)),
                pl.BlockSpec((1, hd), lambda b, g: (0, 0))]
    args = [z, zf, zf, z, z, lower, head_norm]
    has_s0 = s0 is not None
    state_spec = pl.BlockSpec((None, None, 2, heads, hd, hd), lambda b, g: (b, j, 0, g, 0, 0))
    if has_s0:
        in_specs.append(state_spec)
        args.append(s0)
    out_state = not has_s0
    out_specs = [pl.BlockSpec((seq_len, width), lambda b, g: (b, g))]
    out_shape = [jax.ShapeDtypeStruct((batch * seq_len, REC_WIDTH), BF16)]
    aliases = {}
    if out_state:
        out_specs.append(state_spec)
        out_shape.append(jax.ShapeDtypeStruct((batch, N_REC, 2, nh, hd, hd), F32))
        if states is not None:
            aliases[len(args)] = 1
            in_specs.append(pl.BlockSpec(memory_space=pl.ANY))
            args.append(states)
    tokens = heads * seq_len
    outs = pl.pallas_call(
        functools.partial(_rec_kernel, seq_len=seq_len, heads=heads, has_s0=has_s0, out_state=out_state,
                          n_aliased=len(aliases)),
        grid=(batch, nh // heads),
        in_specs=in_specs,
        out_specs=out_specs,
        out_shape=out_shape,
        input_output_aliases=aliases,
        scratch_shapes=[pltpu.VMEM((tokens, hd), F32),
                        pltpu.VMEM((tokens, hd), F32),
                        pltpu.VMEM((2, tokens, hd), F32),
                        pltpu.VMEM((2, tokens, hd), F32),
                        pltpu.VMEM((tokens, 2 * SCAN_CHUNK), BF16),
                        pltpu.VMEM((2, tokens, hd), BF16),
                        pltpu.VMEM((2, tokens // SCAN_CHUNK, hd, hd), F32),
                        pltpu.VMEM((2, tokens // SCAN_CHUNK, 8, hd), F32),
                        pltpu.VMEM((2, SCAN_CHUNK, SCAN_CHUNK), BF16),
                        pltpu.VMEM((2, SCAN_CHUNK, SCAN_CHUNK), jnp.int32)],
        compiler_params=_cparams(2),
        name="rec_mixer",
    )(*args)
    return outs if out_state else (outs[0], None)


def _rope(x, cos, sin_signed):
    lane = lax.broadcasted_iota(jnp.int32, x.shape, 1)
    partner = jnp.where((lane & 1) == 0, pltpu.roll(x, ATT_HEAD_DIM - 1, 1), pltpu.roll(x, 1, 1))
    return x * cos + partner * sin_signed


def _att_kernel(*refs, seq_len, n_keys, latent, n_aliased):
    q_ref, k_ref, v_ref, qn_ref, kn_ref = refs[:5]
    g_refs = refs[5:5 + ATT_KV_HEADS]
    if latent:
        ck_ref, cv_ref, cos_ref, sin_ref, y_ref, kall, vall = refs[5 + ATT_KV_HEADS:-8]
    else:
        y_ref, ko_ref, vo_ref, kall, vall = refs[5 + ATT_KV_HEADS + n_aliased:-8]
    qs_bufs, s_bufs, m_bufs, o_bufs = refs[-8:-6], refs[-6:-4], refs[-4:-2], refs[-2:]
    hd = ATT_HEAD_DIM
    qw = ATT_REP * hd
    tq = ATT_Q_TILE
    tk = min(ATT_KEY_TILE, n_keys)
    n_kb = n_keys // tk
    n_past = n_keys - seq_len
    n_items = ATT_KV_HEADS * (seq_len // tq)
    exp2_scale = hd ** -0.5 * LOG2_E
    assert ATT_KV_HEADS == 2

    lane = lax.broadcasted_iota(jnp.int32, (n_keys, hd), 1)
    for g in range(ATT_KV_HEADS):
        k = _rms(k_ref[:, g * hd:(g + 1) * hd].astype(F32), kn_ref[...])
        v = v_ref[:, g * hd:(g + 1) * hd].astype(F32)
        if latent:
            k = _rope(k, cos_ref[...], sin_ref[...])
            kall[g, 0:n_past, :] = ck_ref[:, g, :].astype(BF16)
            vall[g, 0:n_past, 0:hd] = cv_ref[:, g, :].astype(BF16)
        else:
            ko_ref[:, g, :] = k
            vo_ref[:, g, :] = v
        kall[g, n_past:n_keys, :] = k.astype(BF16)
        vall[g, n_past:n_keys, 0:hd] = v.astype(BF16)
        vall[g, :, hd:2 * hd] = (lane == 0).astype(BF16)

    def tile_rows(item):
        return pl.ds(pl.multiple_of((item // ATT_KV_HEADS) * tq, tq), tq)

    def prep(item, g):
        rows = tile_rows(item)
        for h in range(ATT_REP):
            q = _rms(q_ref[rows, g * qw + h * hd:g * qw + (h + 1) * hd].astype(F32), qn_ref[...])
            if latent:
                q = _rope(q, cos_ref[rows, :], sin_ref[rows, :])
            qs_bufs[g][h * tq:(h + 1) * tq, :] = q.astype(BF16)

    def scores(item, g):
        qs = qs_bufs[g][...]
        mx = None
        for kb in range(n_kb):
            s = _dot_nt(qs, kall[g, kb * tk:(kb + 1) * tk, :])
            s_bufs[g][:, kb * tk:(kb + 1) * tk] = s
            for c in range(tk // hd):
                part = s[:, c * hd:(c + 1) * hd]
                mx = part if mx is None else jnp.maximum(mx, part)
        m = jnp.max(mx, axis=-1, keepdims=True) * exp2_scale
        m_bufs[g][...] = jnp.broadcast_to(m, (ATT_REP * tq, hd))

    def mix(item, g):
        m = m_bufs[g][...]
        ps = []
        for c in range(n_keys // hd):
            s = s_bufs[g][:, c * hd:(c + 1) * hd]
            ps.append(jnp.exp2(s * exp2_scale - m).astype(BF16))
        o_bufs[g][...] = _dot(jnp.concatenate(ps, axis=1), vall[g])

    def finish(item, g):
        rows = tile_rows(item)
        o = o_bufs[g][:, 0:hd] / o_bufs[g][:, hd:hd + 1]
        for h in range(ATT_REP):
            gate = g_refs[g][rows, h * hd:(h + 1) * hd].astype(F32)
            y = o[h * tq:(h + 1) * tq, :] * (gate * jax.nn.sigmoid(gate))
            y_ref[rows, g * qw + h * hd:g * qw + (h + 1) * hd] = y.astype(BF16)

    stages = (prep, scores, mix, finish)

    def step(k, parity):
        for a, stage in enumerate(stages):
            item = k - a
            if isinstance(item, int) and not 0 <= item < n_items:
                continue
            stage(item, (parity + a) % 2)

    depth = len(stages) - 1
    for k in range(min(depth, n_items + depth)):
        step(k, k % 2)
    n_steady = max(n_items - depth, 0)

    def steady_pair(j, carry):
        k = depth + 2 * j
        step(k, depth % 2)
        step(k + 1, (depth + 1) % 2)
        return carry

    lax.fori_loop(0, n_steady // 2, steady_pair, 0)
    for k in range(depth + 2 * (n_steady // 2), n_items + depth):
        step(k, k % 2)


def _attention(z, batch, seq_len, q_norm, k_norm, j, cache=None, new_kv=None):
    hd = ATT_HEAD_DIM
    latent = cache is not None
    qw = ATT_REP * hd
    kv0 = ATT_WIDTH // KV_WIDTH
    gate0 = (ATT_WIDTH + 2 * KV_WIDTH) // qw
    assert (ATT_WIDTH + 2 * KV_WIDTH) % qw == 0 and ATT_WIDTH % KV_WIDTH == 0
    in_specs = [
        pl.BlockSpec((seq_len, ATT_WIDTH), lambda b: (b, 0)),
        pl.BlockSpec((seq_len, KV_WIDTH), lambda b: (b, kv0)),
        pl.BlockSpec((seq_len, KV_WIDTH), lambda b: (b, kv0 + 1)),
        pl.BlockSpec((1, hd), lambda b: (0, 0)),
        pl.BlockSpec((1, hd), lambda b: (0, 0)),
    ] + [pl.BlockSpec((seq_len, qw), lambda b, g=g: (b, gate0 + g)) for g in range(ATT_KV_HEADS)]
    args = [z, z, z, q_norm, k_norm] + [z] * ATT_KV_HEADS
    out_specs = [pl.BlockSpec((seq_len, ATT_WIDTH), lambda b: (b, 0))]
    out_shape = [jax.ShapeDtypeStruct((batch * seq_len, ATT_WIDTH), BF16)]
    n_keys = seq_len
    aliases = {}
    if latent:
        cache_k, cache_v, cos, sin_signed = cache
        past = cache_k.shape[2]
        n_keys += past
        cspec = pl.BlockSpec((None, None, past, ATT_KV_HEADS, hd), lambda b: (b, j, 0, 0, 0))
        tspec = pl.BlockSpec((seq_len, hd), lambda b: (0, 0))
        in_specs += [cspec, cspec, tspec, tspec]
        args += [cache_k, cache_v, cos, sin_signed]
    else:
        kv_spec = pl.BlockSpec((None, None, seq_len, ATT_KV_HEADS, hd), lambda b: (b, j, 0, 0, 0))
        out_specs += [kv_spec, kv_spec]
        out_shape += [jax.ShapeDtypeStruct((batch, N_ATT, seq_len, ATT_KV_HEADS, hd), F32)] * 2
        if new_kv is not None:
            aliases = {len(args): 1, len(args) + 1: 2}
            in_specs += [pl.BlockSpec(memory_space=pl.ANY)] * 2
            args += list(new_kv)
    rows = ATT_REP * ATT_Q_TILE
    return pl.pallas_call(
        functools.partial(_att_kernel, seq_len=seq_len, n_keys=n_keys, latent=latent, n_aliased=len(aliases)),
        grid=(batch,),
        in_specs=in_specs,
        out_specs=out_specs,
        out_shape=out_shape,
        input_output_aliases=aliases,
        scratch_shapes=[pltpu.VMEM((ATT_KV_HEADS, n_keys, hd), BF16),
                        pltpu.VMEM((ATT_KV_HEADS, n_keys, 2 * hd), BF16),
                        pltpu.VMEM((rows, hd), BF16), pltpu.VMEM((rows, hd), BF16),
                        pltpu.VMEM((rows, n_keys), F32), pltpu.VMEM((rows, n_keys), F32),
                        pltpu.VMEM((rows, hd), F32), pltpu.VMEM((rows, hd), F32),
                        pltpu.VMEM((rows, 2 * hd), F32), pltpu.VMEM((rows, 2 * hd), F32)],
        compiler_params=_cparams(1),
        name="attention",
    )(*args)


def _rope_tables(n_tokens):
    t = jnp.arange(n_tokens)
    row = (t // GRID_W).astype(F32)
    col = (t % GRID_W).astype(F32)
    inv = ROPE_THETA ** (-jnp.arange(0, AXIS_DIM, 2, dtype=F32) / AXIS_DIM)
    ang = jnp.concatenate([row[:, None] * inv[None, :], col[:, None] * inv[None, :]], axis=-1)
    cos = jnp.repeat(jnp.cos(ang), 2, axis=-1)
    sin = jnp.repeat(jnp.sin(ang), 2, axis=-1)
    sign = jnp.where(jnp.arange(ATT_HEAD_DIM) % 2 == 0, -1.0, 1.0).astype(F32)
    return cos, sin * sign


def kernel(x_prompt, x_sample, c, state_hgrn, cache_k, cache_v, c_ctx, ada_w, ada_b, norm_pre, norm_post, rec_w_in, rec_lb_logits, rec_head_norm, pool_w, pool_scale, rec_w_out, att_w_in, att_q_norm, att_k_norm, att_w_out):
    nb_c, len_c, _ = x_prompt.shape
    nb_l, len_l, _ = x_sample.shape

    lb_p = jax.nn.softmax(rec_lb_logits.astype(F32), axis=0)
    lower_bounds = jnp.clip(jnp.cumsum(lb_p, axis=0) - lb_p[0], 0.0, 1.0)
    lower_bounds = lower_bounds.reshape(N_REC, 2, REC_HEADS, 1, REC_HEAD_DIM)
    cos, sin_signed = _rope_tables(len_l)

    cvec = jnp.zeros((MOD_ROWS, D_MODEL), F32).at[0].set(c_ctx).at[1:1 + nb_l].set(c)
    mod = _modulation(cvec, ada_w, ada_b)

    forget_cols = (2 * POOL_WIDTH + REC_WIDTH, 2 * POOL_WIDTH + 3 * REC_WIDTH)
    layers = []
    for i in range(DEPTH):
        j = i // 2
        rec = i % 2 == 0
        w_in, w_out = (rec_w_in, rec_w_out) if rec else (att_w_in, att_w_out)
        layers.append(dict(
            w_in=w_in[j].astype(BF16), w_out=w_out[j].astype(BF16), f32_cols=forget_cols if rec else (0, 0),
            gain_pre=norm_pre[i].reshape(1, D_MODEL), gain_post=norm_post[i].reshape(1, D_MODEL)))

    def pre_args(i):
        return (layers[i]["w_in"], i, layers[i]["gain_pre"], layers[i]["f32_cols"])

    xs = [x_prompt.reshape(nb_c * len_c, D_MODEL), x_sample.reshape(nb_l * len_l, D_MODEL)]
    streams = ((nb_c, len_c, 0), (nb_l, len_l, 1))
    new_states, new_kv = None, None
    for s, (nb, sl, row0) in enumerate(streams):
        x = xs[s]
        zs = _proj(x, mod, sl, row0, pre=pre_args(0))
        for i in range(DEPTH):
            j = i // 2
            if i % 2 == 0:
                z, zf = zs
                pw = pool_w[j].astype(BF16)
                ps = pool_scale[j].reshape(1, POOL_WIDTH)
                hn = rec_head_norm[j].reshape(1, REC_HEAD_DIM)
                y_pool = _pool_mixer(z, nb, sl, pw, ps)
                if s == 0:
                    y_rec, new_states = _rec_mixer(z, zf, nb, sl, lower_bounds[j], hn, j, states=new_states)
                else:
                    y_rec, _ = _rec_mixer(z, zf, nb, sl, lower_bounds[j], hn, j, s0=state_hgrn)
                halves = (y_pool, 0, y_rec, 0)
            else:
                (z,) = zs
                qn = att_q_norm[j].reshape(1, ATT_HEAD_DIM)
                kn = att_k_norm[j].reshape(1, ATT_HEAD_DIM)
                if s == 0:
                    y, *new_kv = _attention(z, nb, sl, qn, kn, j, new_kv=new_kv)
                else:
                    (y,) = _attention(z, nb, sl, qn, kn, j, cache=(cache_k, cache_v, cos, sin_signed))
                halves = (y, 0, y, 1)
            post = halves + (layers[i]["w_out"], i, layers[i]["gain_post"])
            x, *zs = _proj(x, mod, sl, row0, post=post, pre=pre_args(i + 1) if i + 1 < DEPTH else None)
        xs[s] = x
    xc, xl = xs

    return (xc.reshape(nb_c, len_c, D_MODEL), xl.reshape(nb_l, len_l, D_MODEL),
            new_states, new_kv[0], new_kv[1])
```

```python
import functools

import jax
import jax.numpy as jnp
from jax import lax
from jax.experimental import pallas as pl
from jax.experimental.pallas import tpu as pltpu

D_MODEL = 1024
DEPTH = 4
GRID_W = 64
N_REC = (DEPTH + 1) // 2
N_ATT = DEPTH // 2
POOL_WIDTH = D_MODEL // 2
POOL_WINDOWS = (2, 4, 8, 16)
POOL_GROUP_DIM = POOL_WIDTH // len(POOL_WINDOWS)
REC_WIDTH = D_MODEL // 2
REC_HEAD_DIM = 128
REC_HEADS = REC_WIDTH // REC_HEAD_DIM
REC_IN_WIDTH = 2 * POOL_WIDTH + 5 * REC_WIDTH
ATT_HEAD_DIM = 128
ATT_HEADS = D_MODEL // ATT_HEAD_DIM
ATT_KV_HEADS = 2
ATT_REP = ATT_HEADS // ATT_KV_HEADS
ATT_WIDTH = ATT_HEADS * ATT_HEAD_DIM
KV_WIDTH = ATT_KV_HEADS * ATT_HEAD_DIM
ATT_IN_WIDTH = 2 * ATT_WIDTH + 2 * KV_WIDTH
AXIS_DIM = ATT_HEAD_DIM // 2
ROPE_THETA = 10000.0
EPS = 1e-6
F_MIN = 1e-6

MOD_ROWS = 16
TOKEN_TILE = 1024
MIN_PROJ_STEPS = 8
PROJ_ROW_GROUPS = 1
POOL_ROWS = 256
IN_PROJ_STEP = 512
MOD_COLS = 1536
SCAN_CHUNK = 128
SCAN_UNITS = 8
POOL_HALO = 8
ATT_Q_TILE = 128
ATT_KEY_TILE = 128
LOG2_E = 1.4426950408889634
VMEM_LIMIT = 56 * 1024 * 1024

F32 = jnp.float32
BF16 = jnp.bfloat16


def _cparams(n_axes):
    return pltpu.CompilerParams(
        dimension_semantics=("arbitrary",) * n_axes, vmem_limit_bytes=VMEM_LIMIT)


def _rms(x, g):
    return x * lax.rsqrt(jnp.mean(x * x, axis=-1, keepdims=True) + EPS) * g


def _dot(a, b):
    return jnp.dot(a, b, preferred_element_type=F32)


def _dot_nt(a, b):
    return lax.dot_general(a, b, (((1,), (1,)), ((), ())), preferred_element_type=F32)


def _mod_kernel(cv_ref, w_ref, b_ref, o_ref):
    cv = cv_ref[...]
    a = (cv * jax.nn.sigmoid(cv)).astype(BF16)
    o_ref[...] = _dot(a, w_ref[...].astype(BF16)) + b_ref[...]


def _modulation(cvec, ada_w, ada_b):
    tn = MOD_COLS
    out = pl.pallas_call(
        _mod_kernel,
        grid=(DEPTH, 3 * D_MODEL // tn),
        in_specs=[
            pl.BlockSpec((MOD_ROWS, D_MODEL), lambda l, n: (0, 0)),
            pl.BlockSpec((None, D_MODEL, tn), lambda l, n: (l, 0, n)),
            pl.BlockSpec((None, 1, tn), lambda l, n: (l, 0, n)),
        ],
        out_specs=pl.BlockSpec((None, MOD_ROWS, tn), lambda l, n: (l, 0, n)),
        out_shape=jax.ShapeDtypeStruct((DEPTH, MOD_ROWS, 3 * D_MODEL), F32),
        compiler_params=_cparams(2),
        name="modulation",
    )(cvec, ada_w, ada_b.reshape(DEPTH, 1, 3 * D_MODEL))
    return out.reshape(DEPTH, MOD_ROWS, 3, 1, D_MODEL)


def _mod_spec(layer, seq_len, row0, tile):
    assert row0 == 0 or seq_len % tile == 0
    tiles_per_seq = seq_len // tile
    if row0 == 0:
        index = lambda i: (layer, 0, 0, 0, 0)
    else:
        index = lambda i: (layer, row0 + i // tiles_per_seq, 0, 0, 0)
    return pl.BlockSpec((None, None, 3, 1, D_MODEL), index)


def _proj_kernel(*refs, post, pre, n_out, f32_cols):
    refs = list(refs)
    x_ref = refs.pop(0)
    if post:
        ya_ref, yb_ref, wo_ref, modp_ref, gpost_ref = (refs.pop(0) for _ in range(5))
    if pre:
        modn_ref, gpre_ref, wi_ref = (refs.pop(0) for _ in range(3))
    xo_ref = refs.pop(0) if post else None
    rows_all = x_ref.shape[0]
    for r0 in range(0, rows_all, rows_all // PROJ_ROW_GROUPS):
        rows = slice(r0, r0 + rows_all // PROJ_ROW_GROUPS)
        x = x_ref[rows, :]
        if post:
            half = D_MODEL // 2
            p = _dot(ya_ref[rows, :], wo_ref[0:half, :]) + _dot(yb_ref[rows, :], wo_ref[half:D_MODEL, :])
            x = x + modp_ref[2] * _rms(p, gpost_ref[...])
            xo_ref[rows, :] = x
        if pre:
            h = (_rms(x, gpre_ref[...]) * (1.0 + modn_ref[1]) + modn_ref[0]).astype(BF16)
            lo, hi = f32_cols
            for c0 in range(0, n_out, IN_PROJ_STEP):
                y = _dot(h, wi_ref[:, c0:c0 + IN_PROJ_STEP])
                if lo <= c0 < hi:
                    refs[1][rows, c0 - lo:c0 - lo + IN_PROJ_STEP] = y
                else:
                    c1 = c0 if c0 < lo else c0 - (hi - lo)
                    refs[0][rows, c1:c1 + IN_PROJ_STEP] = y.astype(BF16)


def _proj(x, mod, seq_len, row0, post=None, pre=None):
    t = x.shape[0]
    tile = min(TOKEN_TILE, t // MIN_PROJ_STEPS)
    half = D_MODEL // 2
    row = lambda i: (i, 0)
    fixed = lambda i: (0, 0)
    resident = dict(index_map=fixed, pipeline_mode=pl.Buffered(1))
    in_specs = [pl.BlockSpec((tile, D_MODEL), row)]
    args = [x]
    out_specs, out_shape = [], []
    n_out, f32_cols = 0, (0, 0)
    if post:
        ya, ia, yb, ib, w_out, layer, gain = post
        in_specs += [pl.BlockSpec((tile, half), lambda i: (i, ia)),
                     pl.BlockSpec((tile, half), lambda i: (i, ib)),
                     pl.BlockSpec((D_MODEL, D_MODEL), **resident),
                     _mod_spec(layer, seq_len, row0, tile),
                     pl.BlockSpec((1, D_MODEL), fixed)]
        args += [ya, yb, w_out, mod, gain]
        out_specs.append(pl.BlockSpec((tile, D_MODEL), row))
        out_shape.append(jax.ShapeDtypeStruct((t, D_MODEL), F32))
    if pre:
        w_in, layer, gain, f32_cols = pre
        n_out = w_in.shape[1]
        lo, hi = f32_cols
        assert lo % IN_PROJ_STEP == 0 and hi % IN_PROJ_STEP == 0 and n_out % IN_PROJ_STEP == 0
        in_specs += [_mod_spec(layer, seq_len, row0, tile),
                     pl.BlockSpec((1, D_MODEL), fixed),
                     pl.BlockSpec((D_MODEL, n_out), **resident)]
        args += [mod, gain, w_in]
        out_specs.append(pl.BlockSpec((tile, n_out - (hi - lo)), row))
        out_shape.append(jax.ShapeDtypeStruct((t, n_out - (hi - lo)), BF16))
        if hi > lo:
            out_specs.append(pl.BlockSpec((tile, hi - lo), row))
            out_shape.append(jax.ShapeDtypeStruct((t, hi - lo), F32))
    return pl.pallas_call(
        functools.partial(_proj_kernel, post=bool(post), pre=bool(pre), n_out=n_out, f32_cols=f32_cols),
        grid=(t // tile,),
        in_specs=in_specs,
        out_specs=out_specs,
        out_shape=out_shape,
        compiler_params=_cparams(1),
        name="proj",
    )(*args)


def _pool_kernel(u_ref, gp_ref, pw_ref, ps_ref, o_ref, pad_ref, *, seq_len):
    gd = POOL_GROUP_DIM
    zeros = jnp.zeros((POOL_HALO, POOL_WIDTH), F32)
    pad_ref[0:POOL_HALO, :] = zeros
    pad_ref[POOL_HALO + seq_len:2 * POOL_HALO + seq_len, :] = zeros
    pad_ref[POOL_HALO:POOL_HALO + seq_len, :] = u_ref[...].astype(F32)
    rows = POOL_ROWS
    for g, win in enumerate(POOL_WINDOWS):
        cols = slice(g * gd, (g + 1) * gd)
        for r in range(seq_len // rows):
            base = r * rows
            acc = None
            for j in range(-(win // 2), win // 2):
                start = POOL_HALO + base + j
                part = pad_ref[start:start + rows, cols]
                acc = part if acc is None else acc + part
            t = base + lax.broadcasted_iota(jnp.int32, (rows, 1), 0)
            lo = jnp.clip(t - win // 2, 0, seq_len)
            hi = jnp.clip(t + win // 2, 0, seq_len)
            mean = acc / (hi - lo).astype(F32)
            d = mean - pad_ref[POOL_HALO + base:POOL_HALO + base + rows, cols]
            y = _dot(d.astype(BF16), pw_ref[g]) * ps_ref[:, cols]
            gate = gp_ref[base:base + rows, cols].astype(F32)
            o_ref[base:base + rows, cols] = (y * (gate * jax.nn.sigmoid(gate))).astype(BF16)


def _pool_mixer(z, batch, seq_len, pool_w, pool_scale):
    return pl.pallas_call(
        functools.partial(_pool_kernel, seq_len=seq_len),
        grid=(batch,),
        in_specs=[
            pl.BlockSpec((seq_len, POOL_WIDTH), lambda b: (b, 0)),
            pl.BlockSpec((seq_len, POOL_WIDTH), lambda b: (b, 1)),
            pl.BlockSpec((len(POOL_WINDOWS), POOL_GROUP_DIM, POOL_GROUP_DIM), lambda b: (0, 0, 0)),
            pl.BlockSpec((1, POOL_WIDTH), lambda b: (0, 0)),
        ],
        out_specs=pl.BlockSpec((seq_len, POOL_WIDTH), lambda b: (b, 0)),
        out_shape=jax.ShapeDtypeStruct((batch * seq_len, POOL_WIDTH), BF16),
        scratch_shapes=[pltpu.VMEM((seq_len + 2 * POOL_HALO, POOL_WIDTH), F32)],
        compiler_params=_cparams(1),
        name="pool_mixer",
    )(z, z, pool_w, pool_scale)


def _block_row(p, block, row):
    c, w = p.shape
    p3 = p.reshape(c // block, block, w)
    return jnp.broadcast_to(p3[:, row:row + 1, :], p3.shape).reshape(c, w)


def _interleave(lo, hi, block):
    half = block // 2
    parts = []
    for b in range(SCAN_CHUNK // block):
        parts.append(lo[b * block:b * block + half])
        parts.append(hi[b * block + half:(b + 1) * block])
    return jnp.concatenate(parts, axis=0)


def _rec_kernel(*refs, seq_len, heads, has_s0, out_state, n_aliased):
    q_ref, ff_ref, fb_ref, v_ref, gr_ref, lb_ref, hn_ref = refs[:7]
    rest = list(refs[7:])
    s0_ref = rest.pop(0) if has_s0 else None
    del rest[:n_aliased]
    o_ref = rest.pop(0)
    st_ref = rest.pop(0) if out_state else None
    acc_ref, qs_ref, k_ref, p_ref, a_ref, qin_ref, u_ref, dec_ref, tri_ref, lvl_ref = rest

    c = SCAN_CHUNK
    hd = REC_HEAD_DIM
    n_chunks = seq_len // c
    zf_refs = (ff_ref, fb_ref)
    units = [(h, n) for h in range(heads) for n in range(n_chunks)]

    def rows_in(n):
        return slice(n * c, (n + 1) * c)

    def rows_sc(h, n):
        return slice((h * n_chunks + n) * c, (h * n_chunks + n + 1) * c)

    def cols(h):
        return slice(h * hd, (h + 1) * hd)

    ti = lax.broadcasted_iota(jnp.int32, (c, c), 0)
    si = lax.broadcasted_iota(jnp.int32, (c, c), 1)
    x = ti ^ si
    lvl = jnp.where(x < 8, 0, jnp.where(x < 16, 1, jnp.where(x < 32, 2, jnp.where(x < 64, 3, 4))))
    for d, causal in enumerate((si <= ti, si >= ti)):
        tri_ref[d] = jnp.where(causal, 1.0, 0.0).astype(BF16)
        lvl_ref[d] = jnp.where(causal, lvl, 5)

    for h, n in units:
        qz = q_ref[rows_in(n), cols(h)].astype(F32)
        qs_ref[rows_sc(h, n), :] = qz * jax.nn.sigmoid(qz)
        for d in range(2):
            lower = lb_ref[d, h]
            f = jnp.clip(lower + (1.0 - lower) * jax.nn.sigmoid(zf_refs[d][rows_in(n), cols(h)]), F_MIN, 1.0)
            k_ref[d, rows_sc(h, n), :] = 1.0 - f
            g = jnp.log(f) * LOG2_E
            g_hi = g.astype(BF16)
            rest = g - g_hi.astype(F32)
            g_mid = rest.astype(BF16)
            g_lo = (rest - g_mid.astype(F32)).astype(BF16)
            sums = _dot(tri_ref[d], jnp.concatenate([g_hi, g_mid, g_lo], axis=1))
            p_ref[d, rows_sc(h, n), :] = (sums[:, 0:hd] + sums[:, hd:2 * hd]) + sums[:, 2 * hd:3 * hd]

    for h, n in units:
        rows = rows_sc(h, n)
        q = qs_ref[rows, :]
        v_t = v_ref[rows_in(n), cols(h)].astype(F32).T.astype(BF16)
        for d in range(2):
            rev = d == 1
            level = lvl_ref[d]
            k = k_ref[d, rows, :]
            p = p_ref[d, rows, :]
            e0 = p - _block_row(p, 8, 4 if rev else 3)
            a = _dot_nt((q * jnp.exp2(e0)).astype(BF16), (k * jnp.exp2(-e0)).astype(BF16))
            a = jnp.where(level == 0, a, 0.0)
            for lv, block in enumerate((16, 32, 64, 128), 1):
                beta = _block_row(p, block, block // 2 if rev else block // 2 - 1)
                if rev:
                    e = _interleave(p, beta, block) - _interleave(beta, p, block)
                    src = _interleave(q, k, block)
                else:
                    e = _interleave(beta, p, block) - _interleave(p, beta, block)
                    src = _interleave(k, q, block)
                m = (src * jnp.exp2(e)).astype(BF16)
                a = jnp.where(level == lv, _dot_nt(m, m), a)
            a_ref[rows, d * c:(d + 1) * c] = a.astype(BF16)

            edge = p[0:1, :] if rev else p[c - 1:c, :]
            qin_ref[d, rows, :] = (q * jnp.exp2(p)).astype(BF16)
            u_ref[d, h * n_chunks + n] = _dot(v_t, (k * jnp.exp2(edge - p)).astype(BF16))
            dec_ref[d, h * n_chunks + n] = jnp.broadcast_to(jnp.exp2(edge), (8, hd))

    for h, n in units:
        v_b = v_ref[rows_in(n), cols(h)]
        acc_ref[rows_sc(h, n), :] = _dot(a_ref[rows_sc(h, n), :], jnp.concatenate([v_b, v_b], axis=0))

    for h in range(heads):
        if has_s0:
            states = [s0_ref[0, h].T, s0_ref[1, h].T]
        else:
            states = [jnp.zeros((hd, hd), F32)] * 2
        inter = [[None] * n_chunks, [None] * n_chunks]
        for i in range(n_chunks):
            for d, n in ((0, i), (1, n_chunks - 1 - i)):
                inter[d][n] = _dot_nt(qin_ref[d, rows_sc(h, n), :], states[d].astype(BF16))
                states[d] = dec_ref[d, h * n_chunks + n][0:1, :] * states[d] + u_ref[d, h * n_chunks + n]
        for n in range(n_chunks):
            gate = gr_ref[rows_in(n), cols(h)].astype(F32)
            o = acc_ref[rows_sc(h, n), :] + inter[0][n] + inter[1][n]
            o_ref[rows_in(n), cols(h)] = (_rms(o, hn_ref[...]) * (gate * jax.nn.sigmoid(gate))).astype(BF16)
        if out_state:
            st_ref[0, h] = states[0].T
            st_ref[1, h] = states[1].T


def _rec_mixer(z, zf, batch, seq_len, lower, head_norm, j, s0=None, states=None):
    hd = REC_HEAD_DIM
    nh = REC_HEADS
    n_chunks = seq_len // SCAN_CHUNK
    heads = min(nh, max(1, SCAN_UNITS // n_chunks))
    width = heads * hd
    col0 = 2 * POOL_WIDTH // width

    def zspec(part):
        return pl.BlockSpec((seq_len, width), lambda b, g: (b, col0 + part * (nh // heads) + g))

    def fspec(part):
        return pl.BlockSpec((seq_len, width), lambda b, g: (b, part * (nh // heads) + g))

    in_specs = [zspec(0), fspec(0), fspec(1), zspec(1), zspec(2),
                pl.BlockSpec((2, heads, 1, hd), lambda b, g: (0, g, 0, 0)),
                pl.BlockSpec((1, hd), lambda b, g: (0, 0))]
    args = [z, zf, zf, z, z, lower, head_norm]
    has_s0 = s0 is not None
    state_spec = pl.BlockSpec((None, None, 2, heads, hd, hd), lambda b, g: (b, j, 0, g, 0, 0))
    if has_s0:
        in_specs.append(state_spec)
        args.append(s0)
    out_state = not has_s0
    out_specs = [pl.BlockSpec((seq_len, width), lambda b, g: (b, g))]
    out_shape = [jax.ShapeDtypeStruct((batch * seq_len, REC_WIDTH), BF16)]
    aliases = {}
    if out_state:
        out_specs.append(state_spec)
        out_shape.append(jax.ShapeDtypeStruct((batch, N_REC, 2, nh, hd, hd), F32))
        if states is not None:
            aliases[len(args)] = 1
            in_specs.append(pl.BlockSpec(memory_space=pl.ANY))
            args.append(states)
    tokens = heads * seq_len
    outs = pl.pallas_call(
        functools.partial(_rec_kernel, seq_len=seq_len, heads=heads, has_s0=has_s0, out_state=out_state,
                          n_aliased=len(aliases)),
        grid=(batch, nh // heads),
        in_specs=in_specs,
        out_specs=out_specs,
        out_shape=out_shape,
        input_output_aliases=aliases,
        scratch_shapes=[pltpu.VMEM((tokens, hd), F32),
                        pltpu.VMEM((tokens, hd), F32),
                        pltpu.VMEM((2, tokens, hd), F32),
                        pltpu.VMEM((2, tokens, hd), F32),
                        pltpu.VMEM((tokens, 2 * SCAN_CHUNK), BF16),
                        pltpu.VMEM((2, tokens, hd), BF16),
                        pltpu.VMEM((2, tokens // SCAN_CHUNK, hd, hd), F32),
                        pltpu.VMEM((2, tokens // SCAN_CHUNK, 8, hd), F32),
                        pltpu.VMEM((2, SCAN_CHUNK, SCAN_CHUNK), BF16),
                        pltpu.VMEM((2, SCAN_CHUNK, SCAN_CHUNK), jnp.int32)],
        compiler_params=_cparams(2),
        name="rec_mixer",
    )(*args)
    return outs if out_state else (outs[0], None)


def _rope(x, cos, sin_signed):
    lane = lax.broadcasted_iota(jnp.int32, x.shape, 1)
    partner = jnp.where((lane & 1) == 0, pltpu.roll(x, ATT_HEAD_DIM - 1, 1), pltpu.roll(x, 1, 1))
    return x * cos + partner * sin_signed


def _att_kernel(*refs, seq_len, n_keys, latent, n_aliased):
    q_ref, k_ref, v_ref, qn_ref, kn_ref = refs[:5]
    g_refs = refs[5:5 + ATT_KV_HEADS]
    if latent:
        ck_ref, cv_ref, cos_ref, sin_ref, y_ref, kall, vall = refs[5 + ATT_KV_HEADS:-8]
    else:
        y_ref, ko_ref, vo_ref, kall, vall = refs[5 + ATT_KV_HEADS + n_aliased:-8]
    qs_bufs, s_bufs, m_bufs, o_bufs = refs[-8:-6], refs[-6:-4], refs[-4:-2], refs[-2:]
    hd = ATT_HEAD_DIM
    qw = ATT_REP * hd
    tq = ATT_Q_TILE
    tk = min(ATT_KEY_TILE, n_keys)
    n_kb = n_keys // tk
    n_past = n_keys - seq_len
    n_items = ATT_KV_HEADS * (seq_len // tq)
    exp2_scale = hd ** -0.5 * LOG2_E
    assert ATT_KV_HEADS == 2

    lane = lax.broadcasted_iota(jnp.int32, (n_keys, hd), 1)
    for g in range(ATT_KV_HEADS):
        k = _rms(k_ref[:, g * hd:(g + 1) * hd].astype(F32), kn_ref[...])
        v = v_ref[:, g * hd:(g + 1) * hd].astype(F32)
        if latent:
            k = _rope(k, cos_ref[...], sin_ref[...])
            kall[g, 0:n_past, :] = ck_ref[pl.ds(g, n_past, stride=ATT_KV_HEADS), :].astype(BF16)
            vall[g, 0:n_past, 0:hd] = cv_ref[pl.ds(g, n_past, stride=ATT_KV_HEADS), :].astype(BF16)
        else:
            ko_ref[:, g, :] = k
            vo_ref[:, g, :] = v
        kall[g, n_past:n_keys, :] = k.astype(BF16)
        vall[g, n_past:n_keys, 0:hd] = v.astype(BF16)
        vall[g, :, hd:2 * hd] = (lane == 0).astype(BF16)

    def tile_rows(item):
        return pl.ds(pl.multiple_of((item // ATT_KV_HEADS) * tq, tq), tq)

    def prep(item, g):
        rows = tile_rows(item)
        for h in range(ATT_REP):
            q = _rms(q_ref[rows, g * qw + h * hd:g * qw + (h + 1) * hd].astype(F32), qn_ref[...])
            if latent:
                q = _rope(q, cos_ref[rows, :], sin_ref[rows, :])
            qs_bufs[g][h * tq:(h + 1) * tq, :] = q.astype(BF16)

    def scores(item, g):
        qs = qs_bufs[g][...]
        mx = None
        for kb in range(n_kb):
            s = _dot_nt(qs, kall[g, kb * tk:(kb + 1) * tk, :])
            s_bufs[g][:, kb * tk:(kb + 1) * tk] = s
            for c in range(tk // hd):
                part = s[:, c * hd:(c + 1) * hd]
                mx = part if mx is None else jnp.maximum(mx, part)
        m = jnp.max(mx, axis=-1, keepdims=True) * exp2_scale
        m_bufs[g][...] = jnp.broadcast_to(m, (ATT_REP * tq, hd))

    def mix(item, g):
        m = m_bufs[g][...]
        ps = []
        for c in range(n_keys // hd):
            s = s_bufs[g][:, c * hd:(c + 1) * hd]
            ps.append(jnp.exp2(s * exp2_scale - m).astype(BF16))
        o_bufs[g][...] = _dot(jnp.concatenate(ps, axis=1), vall[g])

    def finish(item, g):
        rows = tile_rows(item)
        o = o_bufs[g][:, 0:hd] / o_bufs[g][:, hd:hd + 1]
        for h in range(ATT_REP):
            gate = g_refs[g][rows, h * hd:(h + 1) * hd].astype(F32)
            y = o[h * tq:(h + 1) * tq, :] * (gate * jax.nn.sigmoid(gate))
            y_ref[rows, g * qw + h * hd:g * qw + (h + 1) * hd] = y.astype(BF16)

    stages = (prep, scores, mix, finish)

    def step(k, parity):
        for a, stage in enumerate(stages):
            item = k - a
            if isinstance(item, int) and not 0 <= item < n_items:
                continue
            stage(item, (parity + a) % 2)

    depth = len(stages) - 1
    for k in range(min(depth, n_items + depth)):
        step(k, k % 2)
    n_steady = max(n_items - depth, 0)

    def steady_pair(j, carry):
        k = depth + 2 * j
        step(k, depth % 2)
        step(k + 1, (depth + 1) % 2)
        return carry

    lax.fori_loop(0, n_steady // 2, steady_pair, 0)
    for k in range(depth + 2 * (n_steady // 2), n_items + depth):
        step(k, k % 2)


def _attention(z, batch, seq_len, q_norm, k_norm, j, cache=None, new_kv=None):
    hd = ATT_HEAD_DIM
    latent = cache is not None
    qw = ATT_REP * hd
    kv0 = ATT_WIDTH // KV_WIDTH
    gate0 = (ATT_WIDTH + 2 * KV_WIDTH) // qw
    assert (ATT_WIDTH + 2 * KV_WIDTH) % qw == 0 and ATT_WIDTH % KV_WIDTH == 0
    in_specs = [
        pl.BlockSpec((seq_len, ATT_WIDTH), lambda b: (b, 0)),
        pl.BlockSpec((seq_len, KV_WIDTH), lambda b: (b, kv0)),
        pl.BlockSpec((seq_len, KV_WIDTH), lambda b: (b, kv0 + 1)),
        pl.BlockSpec((1, hd), lambda b: (0, 0)),
        pl.BlockSpec((1, hd), lambda b: (0, 0)),
    ] + [pl.BlockSpec((seq_len, qw), lambda b, g=g: (b, gate0 + g)) for g in range(ATT_KV_HEADS)]
    args = [z, z, z, q_norm, k_norm] + [z] * ATT_KV_HEADS
    out_specs = [pl.BlockSpec((seq_len, ATT_WIDTH), lambda b: (b, 0))]
    out_shape = [jax.ShapeDtypeStruct((batch * seq_len, ATT_WIDTH), BF16)]
    n_keys = seq_len
    aliases = {}
    if latent:
        cache_k, cache_v, cos, sin_signed = cache
        past = cache_k.shape[2]
        n_keys += past
        cspec = pl.BlockSpec((None, None, past * ATT_KV_HEADS, hd), lambda b: (b, j, 0, 0))
        tspec = pl.BlockSpec((seq_len, hd), lambda b: (0, 0))
        in_specs += [cspec, cspec, tspec, tspec]
        args += [cache_k.reshape(batch, N_ATT, past * ATT_KV_HEADS, hd),
                 cache_v.reshape(batch, N_ATT, past * ATT_KV_HEADS, hd), cos, sin_signed]
    else:
        kv_spec = pl.BlockSpec((None, None, seq_len, ATT_KV_HEADS, hd), lambda b: (b, j, 0, 0, 0))
        out_specs += [kv_spec, kv_spec]
        out_shape += [jax.ShapeDtypeStruct((batch, N_ATT, seq_len, ATT_KV_HEADS, hd), F32)] * 2
        if new_kv is not None:
            aliases = {len(args): 1, len(args) + 1: 2}
            in_specs += [pl.BlockSpec(memory_space=pl.ANY)] * 2
            args += list(new_kv)
    rows = ATT_REP * ATT_Q_TILE
    return pl.pallas_call(
        functools.partial(_att_kernel, seq_len=seq_len, n_keys=n_keys, latent=latent, n_aliased=len(aliases)),
        grid=(batch,),
        in_specs=in_specs,
        out_specs=out_specs,
        out_shape=out_shape,
        input_output_aliases=aliases,
        scratch_shapes=[pltpu.VMEM((ATT_KV_HEADS, n_keys, hd), BF16),
                        pltpu.VMEM((ATT_KV_HEADS, n_keys, 2 * hd), BF16),
                        pltpu.VMEM((rows, hd), BF16), pltpu.VMEM((rows, hd), BF16),
                        pltpu.VMEM((rows, n_keys), F32), pltpu.VMEM((rows, n_keys), F32),
                        pltpu.VMEM((rows, hd), F32), pltpu.VMEM((rows, hd), F32),
                        pltpu.VMEM((rows, 2 * hd), F32), pltpu.VMEM((rows, 2 * hd), F32)],
        compiler_params=_cparams(1),
        name="attention",
    )(*args)


def _rope_tables(n_tokens):
    t = jnp.arange(n_tokens)
    row = (t // GRID_W).astype(F32)
    col = (t % GRID_W).astype(F32)
    inv = ROPE_THETA ** (-jnp.arange(0, AXIS_DIM, 2, dtype=F32) / AXIS_DIM)
    ang = jnp.concatenate([row[:, None] * inv[None, :], col[:, None] * inv[None, :]], axis=-1)
    cos = jnp.repeat(jnp.cos(ang), 2, axis=-1)
    sin = jnp.repeat(jnp.sin(ang), 2, axis=-1)
    sign = jnp.where(jnp.arange(ATT_HEAD_DIM) % 2 == 0, -1.0, 1.0).astype(F32)
    return cos, sin * sign


def kernel(x_prompt, x_sample, c, state_hgrn, cache_k, cache_v, c_ctx, ada_w, ada_b, norm_pre, norm_post, rec_w_in, rec_lb_logits, rec_head_norm, pool_w, pool_scale, rec_w_out, att_w_in, att_q_norm, att_k_norm, att_w_out):
    nb_c, len_c, _ = x_prompt.shape
    nb_l, len_l, _ = x_sample.shape

    lb_p = jax.nn.softmax(rec_lb_logits.astype(F32), axis=0)
    lower_bounds = jnp.clip(jnp.cumsum(lb_p, axis=0) - lb_p[0], 0.0, 1.0)
    lower_bounds = lower_bounds.reshape(N_REC, 2, REC_HEADS, 1, REC_HEAD_DIM)
    cos, sin_signed = _rope_tables(len_l)

    cvec = jnp.zeros((MOD_ROWS, D_MODEL), F32).at[0].set(c_ctx).at[1:1 + nb_l].set(c)
    mod = _modulation(cvec, ada_w, ada_b)

    forget_cols = (2 * POOL_WIDTH + REC_WIDTH, 2 * POOL_WIDTH + 3 * REC_WIDTH)
    layers = []
    for i in range(DEPTH):
        j = i // 2
        rec = i % 2 == 0
        w_in, w_out = (rec_w_in, rec_w_out) if rec else (att_w_in, att_w_out)
        layers.append(dict(
            w_in=w_in[j].astype(BF16), w_out=w_out[j].astype(BF16), f32_cols=forget_cols if rec else (0, 0),
            gain_pre=norm_pre[i].reshape(1, D_MODEL), gain_post=norm_post[i].reshape(1, D_MODEL)))

    def pre_args(i):
        return (layers[i]["w_in"], i, layers[i]["gain_pre"], layers[i]["f32_cols"])

    xs = [x_prompt.reshape(nb_c * len_c, D_MODEL), x_sample.reshape(nb_l * len_l, D_MODEL)]
    streams = ((nb_c, len_c, 0), (nb_l, len_l, 1))
    new_states, new_kv = None, None
    for s, (nb, sl, row0) in enumerate(streams):
        x = xs[s]
        zs = _proj(x, mod, sl, row0, pre=pre_args(0))
        for i in range(DEPTH):
            j = i // 2
            if i % 2 == 0:
                z, zf = zs
                pw = pool_w[j].astype(BF16)
                ps = pool_scale[j].reshape(1, POOL_WIDTH)
                hn = rec_head_norm[j].reshape(1, REC_HEAD_DIM)
                y_pool = _pool_mixer(z, nb, sl, pw, ps)
                if s == 0:
                    y_rec, new_states = _rec_mixer(z, zf, nb, sl, lower_bounds[j], hn, j, states=new_states)
                else:
                    y_rec, _ = _rec_mixer(z, zf, nb, sl, lower_bounds[j], hn, j, s0=state_hgrn)
                halves = (y_pool, 0, y_rec, 0)
            else:
                (z,) = zs
                qn = att_q_norm[j].reshape(1, ATT_HEAD_DIM)
                kn = att_k_norm[j].reshape(1, ATT_HEAD_DIM)
                if s == 0:
                    y, *new_kv = _attention(z, nb, sl, qn, kn, j, new_kv=new_kv)
                else:
                    (y,) = _attention(z, nb, sl, qn, kn, j, cache=(cache_k, cache_v, cos, sin_signed))
                halves = (y, 0, y, 1)
            post = halves + (layers[i]["w_out"], i, layers[i]["gain_post"])
            x, *zs = _proj(x, mod, sl, row0, post=post, pre=pre_args(i + 1) if i + 1 < DEPTH else None)
        xs[s] = x
    xc, xl = xs

    return (xc.reshape(nb_c, len_c, D_MODEL), xl.reshape(nb_l, len_l, D_MODEL),
            new_states, new_kv[0], new_kv[1])
```

```python
import functools
from typing import Any, NamedTuple

import jax
import jax.numpy as jnp
from jax import lax
from jax.experimental import pallas as pl
from jax.experimental.pallas import tpu as pltpu

D_MODEL = 1024
DEPTH = 4
GRID_W = 64
N_REC = (DEPTH + 1) // 2
N_ATT = DEPTH // 2
POOL_WIDTH = D_MODEL // 2
POOL_WINDOWS = (2, 4, 8, 16)
POOL_GROUP_DIM = POOL_WIDTH // len(POOL_WINDOWS)
REC_WIDTH = D_MODEL // 2
REC_HEAD_DIM = 128
REC_HEADS = REC_WIDTH // REC_HEAD_DIM
REC_IN_WIDTH = 2 * POOL_WIDTH + 5 * REC_WIDTH
ATT_HEAD_DIM = 128
ATT_HEADS = D_MODEL // ATT_HEAD_DIM
ATT_KV_HEADS = 2
ATT_REP = ATT_HEADS // ATT_KV_HEADS
ATT_WIDTH = ATT_HEADS * ATT_HEAD_DIM
KV_WIDTH = ATT_KV_HEADS * ATT_HEAD_DIM
ATT_IN_WIDTH = 2 * ATT_WIDTH + 2 * KV_WIDTH
AXIS_DIM = ATT_HEAD_DIM // 2
ROPE_THETA = 10000.0
EPS = 1e-6
F_MIN = 1e-6

MOD_ROWS = 16
TOKEN_TILE = 1024
MIN_PROJ_STEPS = 8
PROJ_ROW_GROUPS = 1
POOL_ROWS = 256
IN_PROJ_STEP = 512
MOD_COLS = 1536
SCAN_CHUNK = 128
SCAN_UNITS = 8
POOL_HALO = 8
ATT_Q_TILE = 128
ATT_KEY_TILE = 128
LOG2_E = 1.4426950408889634
VMEM_LIMIT = 56 * 1024 * 1024

F32 = jnp.float32
BF16 = jnp.bfloat16


def _cparams(n_axes):
    return pltpu.CompilerParams(
        dimension_semantics=("arbitrary",) * n_axes, vmem_limit_bytes=VMEM_LIMIT)


def _rms(x, g):
    return x * lax.rsqrt(jnp.mean(x * x, axis=-1, keepdims=True) + EPS) * g


def _dot(a, b):
    return jnp.dot(a, b, preferred_element_type=F32)


def _dot_nt(a, b):
    return lax.dot_general(a, b, (((1,), (1,)), ((), ())), preferred_element_type=F32)


class _Part(NamedTuple):
    kernel: Any
    steps: int
    in_specs: list
    args: list
    out_specs: list
    out_shape: list
    scratch_shapes: list
    aliases: dict


def _run(*parts, name):
    steps = parts[0].steps
    assert all(p.steps == steps for p in parts)
    n_in = [len(p.args) for p in parts]
    n_out = [len(p.out_shape) for p in parts]
    n_scr = [len(p.scratch_shapes) for p in parts]

    def body(*refs):
        ins, outs, scr = refs[:sum(n_in)], refs[sum(n_in):sum(n_in) + sum(n_out)], refs[sum(n_in) + sum(n_out):]
        pending = {}
        for k, p in enumerate(parts):
            i0, o0, s0 = sum(n_in[:k]), sum(n_out[:k]), sum(n_scr[:k])
            pending[k] = (0.0, p.kernel(*ins[i0:i0 + n_in[k]], *outs[o0:o0 + n_out[k]], *scr[s0:s0 + n_scr[k]]))
        while pending:
            k = min(pending, key=lambda k: pending[k][0])
            try:
                pending[k] = (next(pending[k][1]), pending[k][1])
            except StopIteration:
                del pending[k]

    aliases = {}
    for k, p in enumerate(parts):
        for i, o in p.aliases.items():
            aliases[sum(n_in[:k]) + i] = sum(n_out[:k]) + o
    outs = pl.pallas_call(
        body,
        grid=(steps,),
        in_specs=[s for p in parts for s in p.in_specs],
        out_specs=[s for p in parts for s in p.out_specs],
        out_shape=[s for p in parts for s in p.out_shape],
        scratch_shapes=[s for p in parts for s in p.scratch_shapes],
        input_output_aliases=aliases,
        compiler_params=_cparams(1),
        name=name,
    )(*[a for p in parts for a in p.args])
    return [list(outs[sum(n_out[:k]):sum(n_out[:k]) + n_out[k]]) for k in range(len(parts))]


def _mod_kernel(cv_ref, w_ref, b_ref, o_ref):
    cv = cv_ref[...]
    a = (cv * jax.nn.sigmoid(cv)).astype(BF16)
    o_ref[...] = _dot(a, w_ref[...].astype(BF16)) + b_ref[...]


def _modulation(cvec, ada_w, ada_b):
    tn = MOD_COLS
    out = pl.pallas_call(
        _mod_kernel,
        grid=(DEPTH, 3 * D_MODEL // tn),
        in_specs=[
            pl.BlockSpec((MOD_ROWS, D_MODEL), lambda l, n: (0, 0)),
            pl.BlockSpec((None, D_MODEL, tn), lambda l, n: (l, 0, n)),
            pl.BlockSpec((None, 1, tn), lambda l, n: (l, 0, n)),
        ],
        out_specs=pl.BlockSpec((None, MOD_ROWS, tn), lambda l, n: (l, 0, n)),
        out_shape=jax.ShapeDtypeStruct((DEPTH, MOD_ROWS, 3 * D_MODEL), F32),
        compiler_params=_cparams(2),
        name="modulation",
    )(cvec, ada_w, ada_b.reshape(DEPTH, 1, 3 * D_MODEL))
    return out.reshape(DEPTH, MOD_ROWS, 3, 1, D_MODEL)


def _mod_spec(layer, seq_len, row0, tile):
    assert row0 == 0 or seq_len % tile == 0
    tiles_per_seq = seq_len // tile
    if row0 == 0:
        index = lambda i: (layer, 0, 0, 0, 0)
    else:
        index = lambda i: (layer, row0 + i // tiles_per_seq, 0, 0, 0)
    return pl.BlockSpec((None, None, 3, 1, D_MODEL), index)


def _proj_kernel(*refs, post, pre, n_out, f32_cols):
    refs = list(refs)
    x_ref = refs.pop(0)
    if post:
        ya_ref, yb_ref, wo_ref, modp_ref, gpost_ref = (refs.pop(0) for _ in range(5))
    if pre:
        modn_ref, gpre_ref, wi_ref = (refs.pop(0) for _ in range(3))
    xo_ref = refs.pop(0) if post else None
    rows_all = x_ref.shape[0]
    total = PROJ_ROW_GROUPS * (2 * post + n_out // IN_PROJ_STEP)
    done = 0
    for r0 in range(0, rows_all, rows_all // PROJ_ROW_GROUPS):
        rows = slice(r0, r0 + rows_all // PROJ_ROW_GROUPS)
        x = x_ref[rows, :]
        if post:
            half = D_MODEL // 2
            p = _dot(ya_ref[rows, :], wo_ref[0:half, :]) + _dot(yb_ref[rows, :], wo_ref[half:D_MODEL, :])
            x = x + modp_ref[2] * _rms(p, gpost_ref[...])
            xo_ref[rows, :] = x
            done += 2
            yield done / total
        if pre:
            h = (_rms(x, gpre_ref[...]) * (1.0 + modn_ref[1]) + modn_ref[0]).astype(BF16)
            lo, hi = f32_cols
            for c0 in range(0, n_out, IN_PROJ_STEP):
                y = _dot(h, wi_ref[:, c0:c0 + IN_PROJ_STEP])
                if lo <= c0 < hi:
                    refs[1][rows, c0 - lo:c0 - lo + IN_PROJ_STEP] = y
                else:
                    c1 = c0 if c0 < lo else c0 - (hi - lo)
                    refs[0][rows, c1:c1 + IN_PROJ_STEP] = y.astype(BF16)
                done += 1
                yield done / total


def _proj_part(x, mod, seq_len, row0, post=None, pre=None, steps=None):
    t = x.shape[0]
    tile = t // steps if steps else min(TOKEN_TILE, t // MIN_PROJ_STEPS)
    half = D_MODEL // 2
    row = lambda i: (i, 0)
    fixed = lambda i: (0, 0)
    resident = dict(index_map=fixed, pipeline_mode=pl.Buffered(1))
    in_specs = [pl.BlockSpec((tile, D_MODEL), row)]
    args = [x]
    out_specs, out_shape = [], []
    n_out, f32_cols = 0, (0, 0)
    if post:
        ya, ia, yb, ib, w_out, layer, gain = post
        in_specs += [pl.BlockSpec((tile, half), lambda i: (i, ia)),
                     pl.BlockSpec((tile, half), lambda i: (i, ib)),
                     pl.BlockSpec((D_MODEL, D_MODEL), **resident),
                     _mod_spec(layer, seq_len, row0, tile),
                     pl.BlockSpec((1, D_MODEL), fixed)]
        args += [ya, yb, w_out, mod, gain]
        out_specs.append(pl.BlockSpec((tile, D_MODEL), row))
        out_shape.append(jax.ShapeDtypeStruct((t, D_MODEL), F32))
    if pre:
        w_in, layer, gain, f32_cols = pre
        n_out = w_in.shape[1]
        lo, hi = f32_cols
        assert lo % IN_PROJ_STEP == 0 and hi % IN_PROJ_STEP == 0 and n_out % IN_PROJ_STEP == 0
        in_specs += [_mod_spec(layer, seq_len, row0, tile),
                     pl.BlockSpec((1, D_MODEL), fixed),
                     pl.BlockSpec((D_MODEL, n_out), **resident)]
        args += [mod, gain, w_in]
        out_specs.append(pl.BlockSpec((tile, n_out - (hi - lo)), row))
        out_shape.append(jax.ShapeDtypeStruct((t, n_out - (hi - lo)), BF16))
        if hi > lo:
            out_specs.append(pl.BlockSpec((tile, hi - lo), row))
            out_shape.append(jax.ShapeDtypeStruct((t, hi - lo), F32))
    kernel = functools.partial(_proj_kernel, post=bool(post), pre=bool(pre), n_out=n_out, f32_cols=f32_cols)
    return _Part(kernel, t // tile, in_specs, args, out_specs, out_shape, [], {})


def _pool_kernel(u_ref, gp_ref, pw_ref, ps_ref, o_ref, pad_ref, *, seq_len):
    gd = POOL_GROUP_DIM
    zeros = jnp.zeros((POOL_HALO, POOL_WIDTH), F32)
    pad_ref[0:POOL_HALO, :] = zeros
    pad_ref[POOL_HALO + seq_len:2 * POOL_HALO + seq_len, :] = zeros
    pad_ref[POOL_HALO:POOL_HALO + seq_len, :] = u_ref[...].astype(F32)
    rows = POOL_ROWS
    for g, win in enumerate(POOL_WINDOWS):
        cols = slice(g * gd, (g + 1) * gd)
        for r in range(seq_len // rows):
            base = r * rows
            acc = None
            for j in range(-(win // 2), win // 2):
                start = POOL_HALO + base + j
                part = pad_ref[start:start + rows, cols]
                acc = part if acc is None else acc + part
            t = base + lax.broadcasted_iota(jnp.int32, (rows, 1), 0)
            lo = jnp.clip(t - win // 2, 0, seq_len)
            hi = jnp.clip(t + win // 2, 0, seq_len)
            mean = acc / (hi - lo).astype(F32)
            d = mean - pad_ref[POOL_HALO + base:POOL_HALO + base + rows, cols]
            y = _dot(d.astype(BF16), pw_ref[g]) * ps_ref[:, cols]
            gate = gp_ref[base:base + rows, cols].astype(F32)
            o_ref[base:base + rows, cols] = (y * (gate * jax.nn.sigmoid(gate))).astype(BF16)


def _pool_mixer(z, batch, seq_len, pool_w, pool_scale):
    return pl.pallas_call(
        functools.partial(_pool_kernel, seq_len=seq_len),
        grid=(batch,),
        in_specs=[
            pl.BlockSpec((seq_len, POOL_WIDTH), lambda b: (b, 0)),
            pl.BlockSpec((seq_len, POOL_WIDTH), lambda b: (b, 1)),
            pl.BlockSpec((len(POOL_WINDOWS), POOL_GROUP_DIM, POOL_GROUP_DIM), lambda b: (0, 0, 0)),
            pl.BlockSpec((1, POOL_WIDTH), lambda b: (0, 0)),
        ],
        out_specs=pl.BlockSpec((seq_len, POOL_WIDTH), lambda b: (b, 0)),
        out_shape=jax.ShapeDtypeStruct((batch * seq_len, POOL_WIDTH), BF16),
        scratch_shapes=[pltpu.VMEM((seq_len + 2 * POOL_HALO, POOL_WIDTH), F32)],
        compiler_params=_cparams(1),
        name="pool_mixer",
    )(z, z, pool_w, pool_scale)


def _block_row(p, block, row):
    c, w = p.shape
    p3 = p.reshape(c // block, block, w)
    return jnp.broadcast_to(p3[:, row:row + 1, :], p3.shape).reshape(c, w)


def _interleave(lo, hi, block):
    half = block // 2
    parts = []
    for b in range(SCAN_CHUNK // block):
        parts.append(lo[b * block:b * block + half])
        parts.append(hi[b * block + half:(b + 1) * block])
    return jnp.concatenate(parts, axis=0)


def _rec_kernel(*refs, seq_len, heads, has_s0, out_state, n_aliased, layer):
    q_ref, ff_ref, fb_ref, v_ref, gr_ref, lb_ref, hn_ref = refs[:7]
    rest = list(refs[7:])
    s0_ref = rest.pop(0) if has_s0 else None
    del rest[:n_aliased]
    o_ref = rest.pop(0)
    st_ref = rest.pop(0) if out_state else None
    acc_ref, qs_ref, k_ref, p_ref, a_ref, qin_ref, u_ref, dec_ref, tri_ref, lvl_ref = rest

    c = SCAN_CHUNK
    hd = REC_HEAD_DIM
    n_chunks = seq_len // c
    zf_refs = (ff_ref, fb_ref)
    units = [(h, n) for h in range(heads) for n in range(n_chunks)]

    def rows_in(n):
        return slice(n * c, (n + 1) * c)

    def rows_sc(h, n):
        return slice((h * n_chunks + n) * c, (h * n_chunks + n + 1) * c)

    def cols(h):
        return slice(h * hd, (h + 1) * hd)

    ti = lax.broadcasted_iota(jnp.int32, (c, c), 0)
    si = lax.broadcasted_iota(jnp.int32, (c, c), 1)
    x = ti ^ si
    lvl = jnp.where(x < 8, 0, jnp.where(x < 16, 1, jnp.where(x < 32, 2, jnp.where(x < 64, 3, 4))))
    for d, causal in enumerate((si <= ti, si >= ti)):
        tri_ref[d] = jnp.where(causal, 1.0, 0.0).astype(BF16)
        lvl_ref[d] = jnp.where(causal, lvl, 5)

    weights = (0.26, 0.43, 0.09, 0.22)
    done = [0.0]

    def progress(stage, share):
        done[0] += weights[stage] * share
        return done[0]

    for h, n in units:
        qz = q_ref[rows_in(n), cols(h)].astype(F32)
        qs_ref[rows_sc(h, n), :] = qz * jax.nn.sigmoid(qz)
        for d in range(2):
            lower = lb_ref[d, h]
            f = jnp.clip(lower + (1.0 - lower) * jax.nn.sigmoid(zf_refs[d][rows_in(n), cols(h)]), F_MIN, 1.0)
            k_ref[d, rows_sc(h, n), :] = 1.0 - f
            g = jnp.log(f) * LOG2_E
            g_hi = g.astype(BF16)
            rest = g - g_hi.astype(F32)
            g_mid = rest.astype(BF16)
            g_lo = (rest - g_mid.astype(F32)).astype(BF16)
            sums = _dot(tri_ref[d], jnp.concatenate([g_hi, g_mid, g_lo], axis=1))
            p_ref[d, rows_sc(h, n), :] = (sums[:, 0:hd] + sums[:, hd:2 * hd]) + sums[:, 2 * hd:3 * hd]
        yield progress(0, 1 / len(units))

    for h, n in units:
        rows = rows_sc(h, n)
        q = qs_ref[rows, :]
        v_t = v_ref[rows_in(n), cols(h)].astype(F32).T.astype(BF16)
        for d in range(2):
            rev = d == 1
            level = lvl_ref[d]
            k = k_ref[d, rows, :]
            p = p_ref[d, rows, :]
            e0 = p - _block_row(p, 8, 4 if rev else 3)
            a = _dot_nt((q * jnp.exp2(e0)).astype(BF16), (k * jnp.exp2(-e0)).astype(BF16))
            a = jnp.where(level == 0, a, 0.0)
            for lv, block in enumerate((16, 32, 64, 128), 1):
                beta = _block_row(p, block, block // 2 if rev else block // 2 - 1)
                if rev:
                    e = _interleave(p, beta, block) - _interleave(beta, p, block)
                    src = _interleave(q, k, block)
                else:
                    e = _interleave(beta, p, block) - _interleave(p, beta, block)
                    src = _interleave(k, q, block)
                m = (src * jnp.exp2(e)).astype(BF16)
                a = jnp.where(level == lv, _dot_nt(m, m), a)
            a_ref[rows, d * c:(d + 1) * c] = a.astype(BF16)

            edge = p[0:1, :] if rev else p[c - 1:c, :]
            qin_ref[d, rows, :] = (q * jnp.exp2(p)).astype(BF16)
            u_ref[d, h * n_chunks + n] = _dot(v_t, (k * jnp.exp2(edge - p)).astype(BF16))
            dec_ref[d, h * n_chunks + n] = jnp.broadcast_to(jnp.exp2(edge), (8, hd))
        yield progress(1, 1 / len(units))

    for h, n in units:
        v_b = v_ref[rows_in(n), cols(h)]
        acc_ref[rows_sc(h, n), :] = _dot(a_ref[rows_sc(h, n), :], jnp.concatenate([v_b, v_b], axis=0))
        yield progress(2, 1 / len(units))

    for h in range(heads):
        if has_s0:
            states = [s0_ref[0, h].T, s0_ref[1, h].T]
        else:
            states = [jnp.zeros((hd, hd), F32)] * 2
        inter = [[None] * n_chunks, [None] * n_chunks]
        for i in range(n_chunks):
            for d, n in ((0, i), (1, n_chunks - 1 - i)):
                inter[d][n] = _dot_nt(qin_ref[d, rows_sc(h, n), :], states[d].astype(BF16))
                states[d] = dec_ref[d, h * n_chunks + n][0:1, :] * states[d] + u_ref[d, h * n_chunks + n]
        for n in range(n_chunks):
            gate = gr_ref[rows_in(n), cols(h)].astype(F32)
            o = acc_ref[rows_sc(h, n), :] + inter[0][n] + inter[1][n]
            o_ref[rows_in(n), cols(h)] = (_rms(o, hn_ref[...]) * (gate * jax.nn.sigmoid(gate))).astype(BF16)
        if out_state:
            st_ref[layer, 0, h] = states[0].T
            st_ref[layer, 1, h] = states[1].T
        yield progress(3, 1 / heads)
    if out_state:
        for other in range(st_ref.shape[0]):
            if other != layer:
                st_ref[other] = jnp.zeros(st_ref.shape[1:], F32)


def _rec_part(z, zf, batch, seq_len, lower, head_norm, j, s0=None, states=None, heads=None):
    hd = REC_HEAD_DIM
    nh = REC_HEADS
    n_chunks = seq_len // SCAN_CHUNK
    heads = heads or min(nh, max(1, SCAN_UNITS // n_chunks))
    width = heads * hd
    col0 = 2 * POOL_WIDTH // width
    groups = nh // heads

    def zspec(part):
        return pl.BlockSpec((seq_len, width), lambda i: (i // groups, col0 + part * groups + i % groups))

    def fspec(part):
        return pl.BlockSpec((seq_len, width), lambda i: (i // groups, part * groups + i % groups))

    in_specs = [zspec(0), fspec(0), fspec(1), zspec(1), zspec(2),
                pl.BlockSpec((2, heads, 1, hd), lambda i: (0, i % groups, 0, 0)),
                pl.BlockSpec((1, hd), lambda i: (0, 0))]
    args = [z, zf, zf, z, z, lower, head_norm]
    has_s0 = s0 is not None
    if has_s0:
        in_specs.append(pl.BlockSpec((None, None, 2, heads, hd, hd),
                                     lambda i: (i // groups, j, 0, i % groups, 0, 0)))
        args.append(s0)
    out_state = not has_s0
    out_specs = [pl.BlockSpec((seq_len, width), lambda i: (i // groups, i % groups))]
    out_shape = [jax.ShapeDtypeStruct((batch * seq_len, REC_WIDTH), BF16)]
    aliases = {}
    layer = j
    if out_state:
        out_shape.append(jax.ShapeDtypeStruct((batch, N_REC, 2, nh, hd, hd), F32))
        if states is None:
            out_specs.append(pl.BlockSpec((None, N_REC, 2, heads, hd, hd),
                                          lambda i: (i // groups, 0, 0, i % groups, 0, 0)))
        else:
            out_specs.append(pl.BlockSpec((None, 1, 2, heads, hd, hd),
                                          lambda i: (i // groups, j, 0, i % groups, 0, 0)))
            layer = 0
            aliases[len(args)] = 1
            in_specs.append(pl.BlockSpec(memory_space=pl.ANY))
            args.append(states)
    tokens = heads * seq_len
    kernel = functools.partial(_rec_kernel, seq_len=seq_len, heads=heads, has_s0=has_s0, out_state=out_state,
                               n_aliased=len(aliases), layer=layer)
    scratch_shapes = [pltpu.VMEM((tokens, hd), F32),
                      pltpu.VMEM((tokens, hd), F32),
                      pltpu.VMEM((2, tokens, hd), F32),
                      pltpu.VMEM((2, tokens, hd), F32),
                      pltpu.VMEM((tokens, 2 * SCAN_CHUNK), BF16),
                      pltpu.VMEM((2, tokens, hd), BF16),
                      pltpu.VMEM((2, tokens // SCAN_CHUNK, hd, hd), F32),
                      pltpu.VMEM((2, tokens // SCAN_CHUNK, 8, hd), F32),
                      pltpu.VMEM((2, SCAN_CHUNK, SCAN_CHUNK), BF16),
                      pltpu.VMEM((2, SCAN_CHUNK, SCAN_CHUNK), jnp.int32)]
    return _Part(kernel, batch * groups, in_specs, args, out_specs, out_shape, scratch_shapes, aliases)


def _rope(x, cos, sin_signed):
    lane = lax.broadcasted_iota(jnp.int32, x.shape, 1)
    partner = jnp.where((lane & 1) == 0, pltpu.roll(x, ATT_HEAD_DIM - 1, 1), pltpu.roll(x, 1, 1))
    return x * cos + partner * sin_signed


def _att_kernel(*refs, seq_len, n_keys, latent, n_aliased, layer):
    q_ref, k_ref, v_ref, qn_ref, kn_ref = refs[:5]
    g_refs = refs[5:5 + ATT_KV_HEADS]
    if latent:
        ck_ref, cv_ref, cos_ref, sin_ref, y_ref, kall, vall = refs[5 + ATT_KV_HEADS:-8]
    else:
        y_ref, ko_ref, vo_ref, kall, vall = refs[5 + ATT_KV_HEADS + n_aliased:-8]
    qs_bufs, s_bufs, m_bufs, o_bufs = refs[-8:-6], refs[-6:-4], refs[-4:-2], refs[-2:]
    hd = ATT_HEAD_DIM
    qw = ATT_REP * hd
    tq = ATT_Q_TILE
    tk = min(ATT_KEY_TILE, n_keys)
    n_kb = n_keys // tk
    n_past = n_keys - seq_len
    n_items = ATT_KV_HEADS * (seq_len // tq)
    exp2_scale = hd ** -0.5 * LOG2_E
    assert ATT_KV_HEADS == 2

    if not latent:
        for other in range(ko_ref.shape[0]):
            if other != layer:
                ko_ref[other] = jnp.zeros(ko_ref.shape[1:], F32)
                vo_ref[other] = jnp.zeros(vo_ref.shape[1:], F32)
    lane = lax.broadcasted_iota(jnp.int32, (n_keys, hd), 1)
    for g in range(ATT_KV_HEADS):
        k = _rms(k_ref[:, g * hd:(g + 1) * hd].astype(F32), kn_ref[...])
        v = v_ref[:, g * hd:(g + 1) * hd].astype(F32)
        if latent:
            k = _rope(k, cos_ref[...], sin_ref[...])
            kall[g, 0:n_past, :] = ck_ref[pl.ds(g, n_past, stride=ATT_KV_HEADS), :].astype(BF16)
            vall[g, 0:n_past, 0:hd] = cv_ref[pl.ds(g, n_past, stride=ATT_KV_HEADS), :].astype(BF16)
        else:
            ko_ref[layer, :, g, :] = k
            vo_ref[layer, :, g, :] = v
        kall[g, n_past:n_keys, :] = k.astype(BF16)
        vall[g, n_past:n_keys, 0:hd] = v.astype(BF16)
        vall[g, :, hd:2 * hd] = (lane == 0).astype(BF16)

    def tile_rows(item):
        return pl.ds(pl.multiple_of((item // ATT_KV_HEADS) * tq, tq), tq)

    def prep(item, g):
        rows = tile_rows(item)
        for h in range(ATT_REP):
            q = _rms(q_ref[rows, g * qw + h * hd:g * qw + (h + 1) * hd].astype(F32), qn_ref[...])
            if latent:
                q = _rope(q, cos_ref[rows, :], sin_ref[rows, :])
            qs_bufs[g][h * tq:(h + 1) * tq, :] = q.astype(BF16)

    def scores(item, g):
        qs = qs_bufs[g][...]
        mx = None
        for kb in range(n_kb):
            s = _dot_nt(qs, kall[g, kb * tk:(kb + 1) * tk, :])
            s_bufs[g][:, kb * tk:(kb + 1) * tk] = s
            for c in range(tk // hd):
                part = s[:, c * hd:(c + 1) * hd]
                mx = part if mx is None else jnp.maximum(mx, part)
        m = jnp.max(mx, axis=-1, keepdims=True) * exp2_scale
        m_bufs[g][...] = jnp.broadcast_to(m, (ATT_REP * tq, hd))

    def mix(item, g):
        m = m_bufs[g][...]
        ps = []
        for c in range(n_keys // hd):
            s = s_bufs[g][:, c * hd:(c + 1) * hd]
            ps.append(jnp.exp2(s * exp2_scale - m).astype(BF16))
        o_bufs[g][...] = _dot(jnp.concatenate(ps, axis=1), vall[g])

    def finish(item, g):
        rows = tile_rows(item)
        o = o_bufs[g][:, 0:hd] / o_bufs[g][:, hd:hd + 1]
        for h in range(ATT_REP):
            gate = g_refs[g][rows, h * hd:(h + 1) * hd].astype(F32)
            y = o[h * tq:(h + 1) * tq, :] * (gate * jax.nn.sigmoid(gate))
            y_ref[rows, g * qw + h * hd:g * qw + (h + 1) * hd] = y.astype(BF16)

    stages = (prep, scores, mix, finish)

    def step(k, parity):
        for a, stage in enumerate(stages):
            item = k - a
            if isinstance(item, int) and not 0 <= item < n_items:
                continue
            stage(item, (parity + a) % 2)

    depth = len(stages) - 1
    for k in range(min(depth, n_items + depth)):
        step(k, k % 2)
    n_steady = max(n_items - depth, 0)

    def steady_pair(j, carry):
        k = depth + 2 * j
        step(k, depth % 2)
        step(k + 1, (depth + 1) % 2)
        return carry

    lax.fori_loop(0, n_steady // 2, steady_pair, 0)
    for k in range(depth + 2 * (n_steady // 2), n_items + depth):
        step(k, k % 2)


def _attention(z, batch, seq_len, q_norm, k_norm, j, cache=None, new_kv=None):
    hd = ATT_HEAD_DIM
    latent = cache is not None
    qw = ATT_REP * hd
    kv0 = ATT_WIDTH // KV_WIDTH
    gate0 = (ATT_WIDTH + 2 * KV_WIDTH) // qw
    assert (ATT_WIDTH + 2 * KV_WIDTH) % qw == 0 and ATT_WIDTH % KV_WIDTH == 0
    in_specs = [
        pl.BlockSpec((seq_len, ATT_WIDTH), lambda b: (b, 0)),
        pl.BlockSpec((seq_len, KV_WIDTH), lambda b: (b, kv0)),
        pl.BlockSpec((seq_len, KV_WIDTH), lambda b: (b, kv0 + 1)),
        pl.BlockSpec((1, hd), lambda b: (0, 0)),
        pl.BlockSpec((1, hd), lambda b: (0, 0)),
    ] + [pl.BlockSpec((seq_len, qw), lambda b, g=g: (b, gate0 + g)) for g in range(ATT_KV_HEADS)]
    args = [z, z, z, q_norm, k_norm] + [z] * ATT_KV_HEADS
    out_specs = [pl.BlockSpec((seq_len, ATT_WIDTH), lambda b: (b, 0))]
    out_shape = [jax.ShapeDtypeStruct((batch * seq_len, ATT_WIDTH), BF16)]
    n_keys = seq_len
    aliases = {}
    layer = j
    if latent:
        cache_k, cache_v, cos, sin_signed = cache
        past = cache_k.shape[2]
        n_keys += past
        cspec = pl.BlockSpec((None, None, past * ATT_KV_HEADS, hd), lambda b: (b, j, 0, 0))
        tspec = pl.BlockSpec((seq_len, hd), lambda b: (0, 0))
        in_specs += [cspec, cspec, tspec, tspec]
        args += [cache_k.reshape(batch, N_ATT, past * ATT_KV_HEADS, hd),
                 cache_v.reshape(batch, N_ATT, past * ATT_KV_HEADS, hd), cos, sin_signed]
    else:
        out_shape += [jax.ShapeDtypeStruct((batch, N_ATT, seq_len, ATT_KV_HEADS, hd), F32)] * 2
        if new_kv is None:
            kv_spec = pl.BlockSpec((None, N_ATT, seq_len, ATT_KV_HEADS, hd), lambda b: (b, 0, 0, 0, 0))
        else:
            kv_spec = pl.BlockSpec((None, 1, seq_len, ATT_KV_HEADS, hd), lambda b: (b, j, 0, 0, 0))
            layer = 0
            aliases = {len(args): 1, len(args) + 1: 2}
            in_specs += [pl.BlockSpec(memory_space=pl.ANY)] * 2
            args += list(new_kv)
        out_specs += [kv_spec, kv_spec]
    rows = ATT_REP * ATT_Q_TILE
    return pl.pallas_call(
        functools.partial(_att_kernel, seq_len=seq_len, n_keys=n_keys, latent=latent, n_aliased=len(aliases),
                          layer=layer),
        grid=(batch,),
        in_specs=in_specs,
        out_specs=out_specs,
        out_shape=out_shape,
        input_output_aliases=aliases,
        scratch_shapes=[pltpu.VMEM((ATT_KV_HEADS, n_keys, hd), BF16),
                        pltpu.VMEM((ATT_KV_HEADS, n_keys, 2 * hd), BF16),
                        pltpu.VMEM((rows, hd), BF16), pltpu.VMEM((rows, hd), BF16),
                        pltpu.VMEM((rows, n_keys), F32), pltpu.VMEM((rows, n_keys), F32),
                        pltpu.VMEM((rows, hd), F32), pltpu.VMEM((rows, hd), F32),
                        pltpu.VMEM((rows, 2 * hd), F32), pltpu.VMEM((rows, 2 * hd), F32)],
        compiler_params=_cparams(1),
        name="attention",
    )(*args)


def _rope_tables(n_tokens):
    t = jnp.arange(n_tokens)
    row = (t // GRID_W).astype(F32)
    col = (t % GRID_W).astype(F32)
    inv = ROPE_THETA ** (-jnp.arange(0, AXIS_DIM, 2, dtype=F32) / AXIS_DIM)
    ang = jnp.concatenate([row[:, None] * inv[None, :], col[:, None] * inv[None, :]], axis=-1)
    cos = jnp.repeat(jnp.cos(ang), 2, axis=-1)
    sin = jnp.repeat(jnp.sin(ang), 2, axis=-1)
    sign = jnp.where(jnp.arange(ATT_HEAD_DIM) % 2 == 0, -1.0, 1.0).astype(F32)
    return cos, sin * sign


def kernel(x_prompt, x_sample, c, state_hgrn, cache_k, cache_v, c_ctx, ada_w, ada_b, norm_pre, norm_post, rec_w_in, rec_lb_logits, rec_head_norm, pool_w, pool_scale, rec_w_out, att_w_in, att_q_norm, att_k_norm, att_w_out):
    nb_c, len_c, _ = x_prompt.shape
    nb_l, len_l, _ = x_sample.shape

    lb_p = jax.nn.softmax(rec_lb_logits.astype(F32), axis=0)
    lower_bounds = jnp.clip(jnp.cumsum(lb_p, axis=0) - lb_p[0], 0.0, 1.0)
    lower_bounds = lower_bounds.reshape(N_REC, 2, REC_HEADS, 1, REC_HEAD_DIM)
    cos, sin_signed = _rope_tables(len_l)

    cvec = jnp.zeros((MOD_ROWS, D_MODEL), F32).at[0].set(c_ctx).at[1:1 + nb_l].set(c)
    mod = _modulation(cvec, ada_w, ada_b)

    forget_cols = (2 * POOL_WIDTH + REC_WIDTH, 2 * POOL_WIDTH + 3 * REC_WIDTH)
    layers = []
    for i in range(DEPTH):
        j = i // 2
        rec = i % 2 == 0
        w_in, w_out = (rec_w_in, rec_w_out) if rec else (att_w_in, att_w_out)
        layers.append(dict(
            w_in=w_in[j].astype(BF16), w_out=w_out[j].astype(BF16), f32_cols=forget_cols if rec else (0, 0),
            gain_pre=norm_pre[i].reshape(1, D_MODEL), gain_post=norm_post[i].reshape(1, D_MODEL)))

    def pre_args(i):
        return (layers[i]["w_in"], i, layers[i]["gain_pre"], layers[i]["f32_cols"])

    xs = [x_prompt.reshape(nb_c * len_c, D_MODEL), x_sample.reshape(nb_l * len_l, D_MODEL)]
    streams = ((nb_c, len_c, 0), (nb_l, len_l, 1))
    zs = [None, None]
    ys = [None, None]
    new_states, new_kv = None, None

    def proj_part(s, i, steps=None):
        nb, sl, row0 = streams[s]
        post = ys[s] + (layers[i - 1]["w_out"], i - 1, layers[i - 1]["gain_post"]) if i > 0 else None
        return _proj_part(xs[s], mod, sl, row0, post=post, pre=pre_args(i) if i < DEPTH else None, steps=steps)

    def take_proj(s, i, outs):
        if i > 0:
            xs[s] = outs.pop(0)
        zs[s] = outs

    def rec_part(s, i, heads=None):
        nb, sl, _ = streams[s]
        j = i // 2
        hn = rec_head_norm[j].reshape(1, REC_HEAD_DIM)
        z, zf = zs[s]
        if s == 0:
            return _rec_part(z, zf, nb, sl, lower_bounds[j], hn, j, states=new_states, heads=heads)
        return _rec_part(z, zf, nb, sl, lower_bounds[j], hn, j, s0=state_hgrn, heads=heads)

    def pool(s, i):
        nb, sl, _ = streams[s]
        j = i // 2
        return _pool_mixer(zs[s][0], nb, sl, pool_w[j].astype(BF16), pool_scale[j].reshape(1, POOL_WIDTH))

    take_proj(0, 0, _run(proj_part(0, 0), name="proj")[0])
    for i in range(DEPTH):
        j = i // 2
        if i % 2 == 0:
            rec_c = rec_part(0, i)
            (y_rec, new_states), outs = _run(rec_c, proj_part(1, i, steps=rec_c.steps), name="rec_proj")
            ys[0] = (pool(0, i), 0, y_rec, 0)
            take_proj(1, i, outs)
            rec_l = rec_part(1, i, heads=REC_HEADS * streams[1][0] // rec_c.steps)
            (y_rec,), outs = _run(rec_l, proj_part(0, i + 1, steps=rec_l.steps), name="rec_proj")
            ys[1] = (pool(1, i), 0, y_rec, 0)
            take_proj(0, i + 1, outs)
        else:
            qn = att_q_norm[j].reshape(1, ATT_HEAD_DIM)
            kn = att_k_norm[j].reshape(1, ATT_HEAD_DIM)
            y, *new_kv = _attention(zs[0][0], nb_c, len_c, qn, kn, j, new_kv=new_kv)
            ys[0] = (y, 0, y, 1)
            take_proj(1, i, _run(proj_part(1, i), name="proj")[0])
            (y,) = _attention(zs[1][0], nb_l, len_l, qn, kn, j, cache=(cache_k, cache_v, cos, sin_signed))
            ys[1] = (y, 0, y, 1)
            take_proj(0, i + 1, _run(proj_part(0, i + 1), name="proj")[0])
    take_proj(1, DEPTH, _run(proj_part(1, DEPTH), name="proj")[0])
    xc, xl = xs

    return (xc.reshape(nb_c, len_c, D_MODEL), xl.reshape(nb_l, len_l, D_MODEL),
            new_states, new_kv[0], new_kv[1])
```

```python
import functools
from typing import Any, NamedTuple

import jax
import jax.numpy as jnp
from jax import lax
from jax.experimental import pallas as pl
from jax.experimental.pallas import tpu as pltpu

D_MODEL = 1024
DEPTH = 4
GRID_W = 64
N_REC = (DEPTH + 1) // 2
N_ATT = DEPTH // 2
POOL_WIDTH = D_MODEL // 2
POOL_WINDOWS = (2, 4, 8, 16)
POOL_GROUP_DIM = POOL_WIDTH // len(POOL_WINDOWS)
REC_WIDTH = D_MODEL // 2
REC_HEAD_DIM = 128
REC_HEADS = REC_WIDTH // REC_HEAD_DIM
REC_IN_WIDTH = 2 * POOL_WIDTH + 5 * REC_WIDTH
ATT_HEAD_DIM = 128
ATT_HEADS = D_MODEL // ATT_HEAD_DIM
ATT_KV_HEADS = 2
ATT_REP = ATT_HEADS // ATT_KV_HEADS
ATT_WIDTH = ATT_HEADS * ATT_HEAD_DIM
KV_WIDTH = ATT_KV_HEADS * ATT_HEAD_DIM
ATT_IN_WIDTH = 2 * ATT_WIDTH + 2 * KV_WIDTH
AXIS_DIM = ATT_HEAD_DIM // 2
ROPE_THETA = 10000.0
EPS = 1e-6
F_MIN = 1e-6

MOD_ROWS = 16
TOKEN_TILE = 1024
MIN_PROJ_STEPS = 8
PROJ_ROW_GROUPS = 1
POOL_ROWS = 256
IN_PROJ_STEP = 512
MOD_COLS = 1536
SCAN_CHUNK = 128
SCAN_UNITS = 8
POOL_HALO = 8
ATT_Q_TILE = 128
ATT_KEY_TILE = 128
ATT_STEADY_STEPS = 2
LOG2_E = 1.4426950408889634
VMEM_LIMIT = 56 * 1024 * 1024

F32 = jnp.float32
BF16 = jnp.bfloat16


def _cparams(n_axes):
    return pltpu.CompilerParams(
        dimension_semantics=("arbitrary",) * n_axes, vmem_limit_bytes=VMEM_LIMIT)


def _rms(x, g):
    return x * lax.rsqrt(jnp.mean(x * x, axis=-1, keepdims=True) + EPS) * g


def _dot(a, b):
    return jnp.dot(a, b, preferred_element_type=F32)


def _dot_nt(a, b):
    return lax.dot_general(a, b, (((1,), (1,)), ((), ())), preferred_element_type=F32)


class _Part(NamedTuple):
    kernel: Any
    steps: int
    in_specs: list
    args: list
    out_specs: list
    out_shape: list
    scratch_shapes: list
    aliases: dict


def _run(*parts, name):
    steps = parts[0].steps
    assert all(p.steps == steps for p in parts)
    n_in = [len(p.args) for p in parts]
    n_out = [len(p.out_shape) for p in parts]
    n_scr = [len(p.scratch_shapes) for p in parts]

    def body(*refs):
        ins, outs, scr = refs[:sum(n_in)], refs[sum(n_in):sum(n_in) + sum(n_out)], refs[sum(n_in) + sum(n_out):]
        pending = {}
        for k, p in enumerate(parts):
            i0, o0, s0 = sum(n_in[:k]), sum(n_out[:k]), sum(n_scr[:k])
            pending[k] = (0.0, p.kernel(*ins[i0:i0 + n_in[k]], *outs[o0:o0 + n_out[k]], *scr[s0:s0 + n_scr[k]]))
        while pending:
            k = min(pending, key=lambda k: pending[k][0])
            try:
                pending[k] = (next(pending[k][1]), pending[k][1])
            except StopIteration:
                del pending[k]

    aliases = {}
    for k, p in enumerate(parts):
        for i, o in p.aliases.items():
            aliases[sum(n_in[:k]) + i] = sum(n_out[:k]) + o
    outs = pl.pallas_call(
        body,
        grid=(steps,),
        in_specs=[s for p in parts for s in p.in_specs],
        out_specs=[s for p in parts for s in p.out_specs],
        out_shape=[s for p in parts for s in p.out_shape],
        scratch_shapes=[s for p in parts for s in p.scratch_shapes],
        input_output_aliases=aliases,
        compiler_params=_cparams(1),
        name=name,
    )(*[a for p in parts for a in p.args])
    return [list(outs[sum(n_out[:k]):sum(n_out[:k]) + n_out[k]]) for k in range(len(parts))]


def _mod_kernel(cv_ref, w_ref, b_ref, o_ref):
    cv = cv_ref[...]
    a = (cv * jax.nn.sigmoid(cv)).astype(BF16)
    o_ref[...] = _dot(a, w_ref[...].astype(BF16)) + b_ref[...]


def _modulation(cvec, ada_w, ada_b):
    tn = MOD_COLS
    out = pl.pallas_call(
        _mod_kernel,
        grid=(DEPTH, 3 * D_MODEL // tn),
        in_specs=[
            pl.BlockSpec((MOD_ROWS, D_MODEL), lambda l, n: (0, 0)),
            pl.BlockSpec((None, D_MODEL, tn), lambda l, n: (l, 0, n)),
            pl.BlockSpec((None, 1, tn), lambda l, n: (l, 0, n)),
        ],
        out_specs=pl.BlockSpec((None, MOD_ROWS, tn), lambda l, n: (l, 0, n)),
        out_shape=jax.ShapeDtypeStruct((DEPTH, MOD_ROWS, 3 * D_MODEL), F32),
        compiler_params=_cparams(2),
        name="modulation",
    )(cvec, ada_w, ada_b.reshape(DEPTH, 1, 3 * D_MODEL))
    return out.reshape(DEPTH, MOD_ROWS, 3, 1, D_MODEL)


def _mod_spec(layer, seq_len, row0, tile):
    assert row0 == 0 or seq_len % tile == 0
    tiles_per_seq = seq_len // tile
    if row0 == 0:
        index = lambda i: (layer, 0, 0, 0, 0)
    else:
        index = lambda i: (layer, row0 + i // tiles_per_seq, 0, 0, 0)
    return pl.BlockSpec((None, None, 3, 1, D_MODEL), index)


def _proj_kernel(*refs, post, pre, n_out, f32_cols):
    refs = list(refs)
    x_ref = refs.pop(0)
    if post:
        ya_ref, yb_ref, wo_ref, modp_ref, gpost_ref = (refs.pop(0) for _ in range(5))
    if pre:
        modn_ref, gpre_ref, wi_ref = (refs.pop(0) for _ in range(3))
    xo_ref = refs.pop(0) if post else None
    rows_all = x_ref.shape[0]
    total = PROJ_ROW_GROUPS * (2 * post + n_out // IN_PROJ_STEP)
    done = 0
    for r0 in range(0, rows_all, rows_all // PROJ_ROW_GROUPS):
        rows = slice(r0, r0 + rows_all // PROJ_ROW_GROUPS)
        x = x_ref[rows, :]
        if post:
            half = D_MODEL // 2
            p = _dot(ya_ref[rows, :], wo_ref[0:half, :]) + _dot(yb_ref[rows, :], wo_ref[half:D_MODEL, :])
            x = x + modp_ref[2] * _rms(p, gpost_ref[...])
            xo_ref[rows, :] = x
            done += 2
            yield done / total
        if pre:
            h = (_rms(x, gpre_ref[...]) * (1.0 + modn_ref[1]) + modn_ref[0]).astype(BF16)
            lo, hi = f32_cols
            for c0 in range(0, n_out, IN_PROJ_STEP):
                y = _dot(h, wi_ref[:, c0:c0 + IN_PROJ_STEP])
                if lo <= c0 < hi:
                    refs[1][rows, c0 - lo:c0 - lo + IN_PROJ_STEP] = y
                else:
                    c1 = c0 if c0 < lo else c0 - (hi - lo)
                    refs[0][rows, c1:c1 + IN_PROJ_STEP] = y.astype(BF16)
                done += 1
                yield done / total


def _proj_part(x, mod, seq_len, row0, post=None, pre=None, steps=None):
    t = x.shape[0]
    tile = t // steps if steps else min(TOKEN_TILE, t // MIN_PROJ_STEPS)
    half = D_MODEL // 2
    row = lambda i: (i, 0)
    fixed = lambda i: (0, 0)
    resident = dict(index_map=fixed, pipeline_mode=pl.Buffered(1))
    in_specs = [pl.BlockSpec((tile, D_MODEL), row)]
    args = [x]
    out_specs, out_shape = [], []
    n_out, f32_cols = 0, (0, 0)
    if post:
        ya, ia, yb, ib, w_out, layer, gain = post
        in_specs += [pl.BlockSpec((tile, half), lambda i: (i, ia)),
                     pl.BlockSpec((tile, half), lambda i: (i, ib)),
                     pl.BlockSpec((D_MODEL, D_MODEL), **resident),
                     _mod_spec(layer, seq_len, row0, tile),
                     pl.BlockSpec((1, D_MODEL), fixed)]
        args += [ya, yb, w_out, mod, gain]
        out_specs.append(pl.BlockSpec((tile, D_MODEL), row))
        out_shape.append(jax.ShapeDtypeStruct((t, D_MODEL), F32))
    if pre:
        w_in, layer, gain, f32_cols = pre
        n_out = w_in.shape[1]
        lo, hi = f32_cols
        assert lo % IN_PROJ_STEP == 0 and hi % IN_PROJ_STEP == 0 and n_out % IN_PROJ_STEP == 0
        in_specs += [_mod_spec(layer, seq_len, row0, tile),
                     pl.BlockSpec((1, D_MODEL), fixed),
                     pl.BlockSpec((D_MODEL, n_out), **resident)]
        args += [mod, gain, w_in]
        out_specs.append(pl.BlockSpec((tile, n_out - (hi - lo)), row))
        out_shape.append(jax.ShapeDtypeStruct((t, n_out - (hi - lo)), BF16))
        if hi > lo:
            out_specs.append(pl.BlockSpec((tile, hi - lo), row))
            out_shape.append(jax.ShapeDtypeStruct((t, hi - lo), F32))
    kernel = functools.partial(_proj_kernel, post=bool(post), pre=bool(pre), n_out=n_out, f32_cols=f32_cols)
    return _Part(kernel, t // tile, in_specs, args, out_specs, out_shape, [], {})


def _pool_kernel(u_ref, gp_ref, pw_ref, ps_ref, o_ref, pad_ref, *, seq_len):
    gd = POOL_GROUP_DIM
    zeros = jnp.zeros((POOL_HALO, POOL_WIDTH), F32)
    pad_ref[0:POOL_HALO, :] = zeros
    pad_ref[POOL_HALO + seq_len:2 * POOL_HALO + seq_len, :] = zeros
    pad_ref[POOL_HALO:POOL_HALO + seq_len, :] = u_ref[...].astype(F32)
    rows = POOL_ROWS
    for g, win in enumerate(POOL_WINDOWS):
        cols = slice(g * gd, (g + 1) * gd)
        for r in range(seq_len // rows):
            base = r * rows
            acc = None
            for j in range(-(win // 2), win // 2):
                start = POOL_HALO + base + j
                part = pad_ref[start:start + rows, cols]
                acc = part if acc is None else acc + part
            t = base + lax.broadcasted_iota(jnp.int32, (rows, 1), 0)
            lo = jnp.clip(t - win // 2, 0, seq_len)
            hi = jnp.clip(t + win // 2, 0, seq_len)
            mean = acc / (hi - lo).astype(F32)
            d = mean - pad_ref[POOL_HALO + base:POOL_HALO + base + rows, cols]
            y = _dot(d.astype(BF16), pw_ref[g]) * ps_ref[:, cols]
            gate = gp_ref[base:base + rows, cols].astype(F32)
            o_ref[base:base + rows, cols] = (y * (gate * jax.nn.sigmoid(gate))).astype(BF16)


def _pool_mixer(z, batch, seq_len, pool_w, pool_scale):
    return pl.pallas_call(
        functools.partial(_pool_kernel, seq_len=seq_len),
        grid=(batch,),
        in_specs=[
            pl.BlockSpec((seq_len, POOL_WIDTH), lambda b: (b, 0)),
            pl.BlockSpec((seq_len, POOL_WIDTH), lambda b: (b, 1)),
            pl.BlockSpec((len(POOL_WINDOWS), POOL_GROUP_DIM, POOL_GROUP_DIM), lambda b: (0, 0, 0)),
            pl.BlockSpec((1, POOL_WIDTH), lambda b: (0, 0)),
        ],
        out_specs=pl.BlockSpec((seq_len, POOL_WIDTH), lambda b: (b, 0)),
        out_shape=jax.ShapeDtypeStruct((batch * seq_len, POOL_WIDTH), BF16),
        scratch_shapes=[pltpu.VMEM((seq_len + 2 * POOL_HALO, POOL_WIDTH), F32)],
        compiler_params=_cparams(1),
        name="pool_mixer",
    )(z, z, pool_w, pool_scale)


def _block_row(p, block, row):
    c, w = p.shape
    p3 = p.reshape(c // block, block, w)
    return jnp.broadcast_to(p3[:, row:row + 1, :], p3.shape).reshape(c, w)


def _interleave(lo, hi, block):
    half = block // 2
    parts = []
    for b in range(SCAN_CHUNK // block):
        parts.append(lo[b * block:b * block + half])
        parts.append(hi[b * block + half:(b + 1) * block])
    return jnp.concatenate(parts, axis=0)


def _select_levels(products, rev):
    c = SCAN_CHUNK
    lane = lax.broadcasted_iota(jnp.int32, (8, c), 1)
    sub = lax.broadcasted_iota(jnp.int32, (8, c), 0)
    out = []
    for t0 in range(0, c, 8):
        if rev:
            keep = (lane >= t0 + sub) & (lane < t0 + 8)
        else:
            keep = (lane >= t0) & (lane <= t0 + sub)
        row = jnp.where(keep, products[0][t0:t0 + 8, :], 0.0)
        for lv, block in enumerate((16, 32, 64, 128), 1):
            half = block // 2
            start = t0 // block * block
            upper = t0 - start >= half
            if upper == rev:
                continue
            lo = start + half if rev else start
            row = jnp.where((lane >= lo) & (lane < lo + half), products[lv][t0:t0 + 8, :], row)
        out.append(row)
    return jnp.concatenate(out, axis=0)


def _rec_kernel(*refs, seq_len, heads, has_s0, out_state, n_aliased, layer):
    q_ref, ff_ref, fb_ref, v_ref, gr_ref, lb_ref, hn_ref = refs[:7]
    rest = list(refs[7:])
    s0_ref = rest.pop(0) if has_s0 else None
    del rest[:n_aliased]
    o_ref = rest.pop(0)
    st_ref = rest.pop(0) if out_state else None
    acc_ref, qs_ref, k_ref, p_ref, a_ref, qin_ref, u_ref, dec_ref, tri_ref = rest

    c = SCAN_CHUNK
    hd = REC_HEAD_DIM
    n_chunks = seq_len // c
    zf_refs = (ff_ref, fb_ref)
    units = [(h, n) for h in range(heads) for n in range(n_chunks)]

    def rows_in(n):
        return slice(n * c, (n + 1) * c)

    def rows_sc(h, n):
        return slice((h * n_chunks + n) * c, (h * n_chunks + n + 1) * c)

    def cols(h):
        return slice(h * hd, (h + 1) * hd)

    ti = lax.broadcasted_iota(jnp.int32, (c, c), 0)
    si = lax.broadcasted_iota(jnp.int32, (c, c), 1)
    for d, causal in enumerate((si <= ti, si >= ti)):
        tri_ref[d] = jnp.where(causal, 1.0, 0.0).astype(BF16)

    weights = (0.26, 0.43, 0.09, 0.22)
    done = [0.0]

    def progress(stage, share):
        done[0] += weights[stage] * share
        return done[0]

    for h, n in units:
        qz = q_ref[rows_in(n), cols(h)].astype(F32)
        qs_ref[rows_sc(h, n), :] = qz * jax.nn.sigmoid(qz)
        for d in range(2):
            lower = lb_ref[d, h]
            f = jnp.clip(lower + (1.0 - lower) * jax.nn.sigmoid(zf_refs[d][rows_in(n), cols(h)]), F_MIN, 1.0)
            k_ref[d, rows_sc(h, n), :] = 1.0 - f
            g = jnp.log(f) * LOG2_E
            g_hi = g.astype(BF16)
            rest = g - g_hi.astype(F32)
            g_mid = rest.astype(BF16)
            g_lo = (rest - g_mid.astype(F32)).astype(BF16)
            sums = _dot(tri_ref[d], jnp.concatenate([g_hi, g_mid, g_lo], axis=1))
            p_ref[d, rows_sc(h, n), :] = (sums[:, 0:hd] + sums[:, hd:2 * hd]) + sums[:, 2 * hd:3 * hd]
        yield progress(0, 1 / len(units))

    for h, n in units:
        rows = rows_sc(h, n)
        q = qs_ref[rows, :]
        v_t = v_ref[rows_in(n), cols(h)].astype(F32).T.astype(BF16)
        for d in range(2):
            rev = d == 1
            k = k_ref[d, rows, :]
            p = p_ref[d, rows, :]
            e0 = p - _block_row(p, 8, 4 if rev else 3)
            products = [_dot_nt((q * jnp.exp2(e0)).astype(BF16), (k * jnp.exp2(-e0)).astype(BF16))]
            for block in (16, 32, 64, 128):
                beta = _block_row(p, block, block // 2 if rev else block // 2 - 1)
                if rev:
                    e = _interleave(p, beta, block) - _interleave(beta, p, block)
                    src = _interleave(q, k, block)
                else:
                    e = _interleave(beta, p, block) - _interleave(p, beta, block)
                    src = _interleave(k, q, block)
                m = (src * jnp.exp2(e)).astype(BF16)
                products.append(_dot_nt(m, m))
            a_ref[rows, d * c:(d + 1) * c] = _select_levels(products, rev).astype(BF16)

            edge = p[0:1, :] if rev else p[c - 1:c, :]
            qin_ref[d, rows, :] = (q * jnp.exp2(p)).astype(BF16)
            u_ref[d, h * n_chunks + n] = _dot(v_t, (k * jnp.exp2(edge - p)).astype(BF16))
            dec_ref[d, h * n_chunks + n] = jnp.broadcast_to(jnp.exp2(edge), (8, hd))
        yield progress(1, 1 / len(units))

    for h, n in units:
        v_b = v_ref[rows_in(n), cols(h)]
        acc_ref[rows_sc(h, n), :] = _dot(a_ref[rows_sc(h, n), :], jnp.concatenate([v_b, v_b], axis=0))
        yield progress(2, 1 / len(units))

    for h in range(heads):
        if has_s0:
            states = [s0_ref[0, h].T, s0_ref[1, h].T]
        else:
            states = [jnp.zeros((hd, hd), F32)] * 2
        inter = [[None] * n_chunks, [None] * n_chunks]
        for i in range(n_chunks):
            for d, n in ((0, i), (1, n_chunks - 1 - i)):
                inter[d][n] = _dot_nt(qin_ref[d, rows_sc(h, n), :], states[d].astype(BF16))
                states[d] = dec_ref[d, h * n_chunks + n][0:1, :] * states[d] + u_ref[d, h * n_chunks + n]
        for n in range(n_chunks):
            gate = gr_ref[rows_in(n), cols(h)].astype(F32)
            o = acc_ref[rows_sc(h, n), :] + inter[0][n] + inter[1][n]
            o_ref[rows_in(n), cols(h)] = (_rms(o, hn_ref[...]) * (gate * jax.nn.sigmoid(gate))).astype(BF16)
        if out_state:
            st_ref[layer, 0, h] = states[0].T
            st_ref[layer, 1, h] = states[1].T
        yield progress(3, 1 / heads)
    if out_state:
        for other in range(st_ref.shape[0]):
            if other != layer:
                st_ref[other] = jnp.zeros(st_ref.shape[1:], F32)


def _rec_part(z, zf, batch, seq_len, lower, head_norm, j, s0=None, states=None, heads=None):
    hd = REC_HEAD_DIM
    nh = REC_HEADS
    n_chunks = seq_len // SCAN_CHUNK
    heads = heads or min(nh, max(1, SCAN_UNITS // n_chunks))
    width = heads * hd
    col0 = 2 * POOL_WIDTH // width
    groups = nh // heads

    def zspec(part):
        return pl.BlockSpec((seq_len, width), lambda i: (i // groups, col0 + part * groups + i % groups))

    def fspec(part):
        return pl.BlockSpec((seq_len, width), lambda i: (i // groups, part * groups + i % groups))

    in_specs = [zspec(0), fspec(0), fspec(1), zspec(1), zspec(2),
                pl.BlockSpec((2, heads, 1, hd), lambda i: (0, i % groups, 0, 0)),
                pl.BlockSpec((1, hd), lambda i: (0, 0))]
    args = [z, zf, zf, z, z, lower, head_norm]
    has_s0 = s0 is not None
    if has_s0:
        in_specs.append(pl.BlockSpec((None, None, 2, heads, hd, hd),
                                     lambda i: (i // groups, j, 0, i % groups, 0, 0)))
        args.append(s0)
    out_state = not has_s0
    out_specs = [pl.BlockSpec((seq_len, width), lambda i: (i // groups, i % groups))]
    out_shape = [jax.ShapeDtypeStruct((batch * seq_len, REC_WIDTH), BF16)]
    aliases = {}
    layer = j
    if out_state:
        out_shape.append(jax.ShapeDtypeStruct((batch, N_REC, 2, nh, hd, hd), F32))
        if states is None:
            out_specs.append(pl.BlockSpec((None, N_REC, 2, heads, hd, hd),
                                          lambda i: (i // groups, 0, 0, i % groups, 0, 0)))
        else:
            out_specs.append(pl.BlockSpec((None, 1, 2, heads, hd, hd),
                                          lambda i: (i // groups, j, 0, i % groups, 0, 0)))
            layer = 0
            aliases[len(args)] = 1
            in_specs.append(pl.BlockSpec(memory_space=pl.ANY))
            args.append(states)
    tokens = heads * seq_len
    kernel = functools.partial(_rec_kernel, seq_len=seq_len, heads=heads, has_s0=has_s0, out_state=out_state,
                               n_aliased=len(aliases), layer=layer)
    scratch_shapes = [pltpu.VMEM((tokens, hd), F32),
                      pltpu.VMEM((tokens, hd), F32),
                      pltpu.VMEM((2, tokens, hd), F32),
                      pltpu.VMEM((2, tokens, hd), F32),
                      pltpu.VMEM((tokens, 2 * SCAN_CHUNK), BF16),
                      pltpu.VMEM((2, tokens, hd), BF16),
                      pltpu.VMEM((2, tokens // SCAN_CHUNK, hd, hd), F32),
                      pltpu.VMEM((2, tokens // SCAN_CHUNK, 8, hd), F32),
                      pltpu.VMEM((2, SCAN_CHUNK, SCAN_CHUNK), BF16)]
    return _Part(kernel, batch * groups, in_specs, args, out_specs, out_shape, scratch_shapes, aliases)


def _rope(x, cos, sin_signed):
    lane = lax.broadcasted_iota(jnp.int32, x.shape, 1)
    partner = jnp.where((lane & 1) == 0, pltpu.roll(x, ATT_HEAD_DIM - 1, 1), pltpu.roll(x, 1, 1))
    return x * cos + partner * sin_signed


def _att_kernel(*refs, seq_len, n_keys, latent, n_aliased, layer):
    q_ref, k_ref, v_ref, qn_ref, kn_ref = refs[:5]
    g_refs = refs[5:5 + ATT_KV_HEADS]
    if latent:
        ck_ref, cv_ref, cos_ref, sin_ref, y_ref, kall, vall = refs[5 + ATT_KV_HEADS:-8]
    else:
        y_ref, ko_ref, vo_ref, kall, vall = refs[5 + ATT_KV_HEADS + n_aliased:-8]
    qs_bufs, s_bufs, m_bufs, o_bufs = refs[-8:-6], refs[-6:-4], refs[-4:-2], refs[-2:]
    hd = ATT_HEAD_DIM
    qw = ATT_REP * hd
    tq = ATT_Q_TILE
    tk = min(ATT_KEY_TILE, n_keys)
    n_kb = n_keys // tk
    n_past = n_keys - seq_len
    n_items = ATT_KV_HEADS * (seq_len // tq)
    exp2_scale = hd ** -0.5 * LOG2_E
    assert ATT_KV_HEADS == 2

    if not latent:
        for other in range(ko_ref.shape[0]):
            if other != layer:
                ko_ref[other] = jnp.zeros(ko_ref.shape[1:], F32)
                vo_ref[other] = jnp.zeros(vo_ref.shape[1:], F32)
    lane = lax.broadcasted_iota(jnp.int32, (n_keys, hd), 1)
    for g in range(ATT_KV_HEADS):
        k = _rms(k_ref[:, g * hd:(g + 1) * hd].astype(F32), kn_ref[...])
        v = v_ref[:, g * hd:(g + 1) * hd].astype(F32)
        if latent:
            k = _rope(k, cos_ref[...], sin_ref[...])
            kall[g, 0:n_past, :] = ck_ref[pl.ds(g, n_past, stride=ATT_KV_HEADS), :].astype(BF16)
            vall[g, 0:n_past, 0:hd] = cv_ref[pl.ds(g, n_past, stride=ATT_KV_HEADS), :].astype(BF16)
        else:
            ko_ref[layer, :, g, :] = k
            vo_ref[layer, :, g, :] = v
        kall[g, n_past:n_keys, :] = k.astype(BF16)
        vall[g, n_past:n_keys, 0:hd] = v.astype(BF16)
        vall[g, :, hd:2 * hd] = (lane == 0).astype(BF16)

    def tile_rows(item):
        return pl.ds(pl.multiple_of((item // ATT_KV_HEADS) * tq, tq), tq)

    def prep(item, g):
        rows = tile_rows(item)
        for h in range(ATT_REP):
            q = _rms(q_ref[rows, g * qw + h * hd:g * qw + (h + 1) * hd].astype(F32), qn_ref[...])
            if latent:
                q = _rope(q, cos_ref[rows, :], sin_ref[rows, :])
            qs_bufs[g][h * tq:(h + 1) * tq, :] = q.astype(BF16)

    def scores(item, g):
        qs = qs_bufs[g][...]
        mx = None
        for kb in range(n_kb):
            s = _dot_nt(qs, kall[g, kb * tk:(kb + 1) * tk, :])
            s_bufs[g][:, kb * tk:(kb + 1) * tk] = s
            for c in range(tk // hd):
                part = s[:, c * hd:(c + 1) * hd]
                mx = part if mx is None else jnp.maximum(mx, part)
        m = jnp.max(mx, axis=-1, keepdims=True) * exp2_scale
        m_bufs[g][...] = jnp.broadcast_to(m, (ATT_REP * tq, hd))

    def mix(item, g):
        m = m_bufs[g][...]
        ps = []
        for c in range(n_keys // hd):
            s = s_bufs[g][:, c * hd:(c + 1) * hd]
            ps.append(jnp.exp2(s * exp2_scale - m).astype(BF16))
        o_bufs[g][...] = _dot(jnp.concatenate(ps, axis=1), vall[g])

    def finish(item, g):
        rows = tile_rows(item)
        o = o_bufs[g][:, 0:hd] / o_bufs[g][:, hd:hd + 1]
        for h in range(ATT_REP):
            gate = g_refs[g][rows, h * hd:(h + 1) * hd].astype(F32)
            y = o[h * tq:(h + 1) * tq, :] * (gate * jax.nn.sigmoid(gate))
            y_ref[rows, g * qw + h * hd:g * qw + (h + 1) * hd] = y.astype(BF16)

    stages = (prep, scores, mix, finish)

    def step(k, parity):
        for a, stage in enumerate(stages):
            item = k - a
            if isinstance(item, int) and not 0 <= item < n_items:
                continue
            stage(item, (parity + a) % 2)

    depth = len(stages) - 1
    for k in range(min(depth, n_items + depth)):
        step(k, k % 2)
    n_steady = max(n_items - depth, 0)

    group = ATT_STEADY_STEPS

    def steady_group(j, carry):
        for u in range(group):
            step(depth + group * j + u, (depth + u) % 2)
        return carry

    lax.fori_loop(0, n_steady // group, steady_group, 0)
    for k in range(depth + group * (n_steady // group), n_items + depth):
        step(k, k % 2)


def _attention(z, batch, seq_len, q_norm, k_norm, j, cache=None, new_kv=None):
    hd = ATT_HEAD_DIM
    latent = cache is not None
    qw = ATT_REP * hd
    kv0 = ATT_WIDTH // KV_WIDTH
    gate0 = (ATT_WIDTH + 2 * KV_WIDTH) // qw
    assert (ATT_WIDTH + 2 * KV_WIDTH) % qw == 0 and ATT_WIDTH % KV_WIDTH == 0
    in_specs = [
        pl.BlockSpec((seq_len, ATT_WIDTH), lambda b: (b, 0)),
        pl.BlockSpec((seq_len, KV_WIDTH), lambda b: (b, kv0)),
        pl.BlockSpec((seq_len, KV_WIDTH), lambda b: (b, kv0 + 1)),
        pl.BlockSpec((1, hd), lambda b: (0, 0)),
        pl.BlockSpec((1, hd), lambda b: (0, 0)),
    ] + [pl.BlockSpec((seq_len, qw), lambda b, g=g: (b, gate0 + g)) for g in range(ATT_KV_HEADS)]
    args = [z, z, z, q_norm, k_norm] + [z] * ATT_KV_HEADS
    out_specs = [pl.BlockSpec((seq_len, ATT_WIDTH), lambda b: (b, 0))]
    out_shape = [jax.ShapeDtypeStruct((batch * seq_len, ATT_WIDTH), BF16)]
    n_keys = seq_len
    aliases = {}
    layer = j
    if latent:
        cache_k, cache_v, cos, sin_signed = cache
        past = cache_k.shape[2]
        n_keys += past
        cspec = pl.BlockSpec((None, None, past * ATT_KV_HEADS, hd), lambda b: (b, j, 0, 0))
        tspec = pl.BlockSpec((seq_len, hd), lambda b: (0, 0))
        in_specs += [cspec, cspec, tspec, tspec]
        args += [cache_k.reshape(batch, N_ATT, past * ATT_KV_HEADS, hd),
                 cache_v.reshape(batch, N_ATT, past * ATT_KV_HEADS, hd), cos, sin_signed]
    else:
        out_shape += [jax.ShapeDtypeStruct((batch, N_ATT, seq_len, ATT_KV_HEADS, hd), F32)] * 2
        if new_kv is None:
            kv_spec = pl.BlockSpec((None, N_ATT, seq_len, ATT_KV_HEADS, hd), lambda b: (b, 0, 0, 0, 0))
        else:
            kv_spec = pl.BlockSpec((None, 1, seq_len, ATT_KV_HEADS, hd), lambda b: (b, j, 0, 0, 0))
            layer = 0
            aliases = {len(args): 1, len(args) + 1: 2}
            in_specs += [pl.BlockSpec(memory_space=pl.ANY)] * 2
            args += list(new_kv)
        out_specs += [kv_spec, kv_spec]
    rows = ATT_REP * ATT_Q_TILE
    return pl.pallas_call(
        functools.partial(_att_kernel, seq_len=seq_len, n_keys=n_keys, latent=latent, n_aliased=len(aliases),
                          layer=layer),
        grid=(batch,),
        in_specs=in_specs,
        out_specs=out_specs,
        out_shape=out_shape,
        input_output_aliases=aliases,
        scratch_shapes=[pltpu.VMEM((ATT_KV_HEADS, n_keys, hd), BF16),
                        pltpu.VMEM((ATT_KV_HEADS, n_keys, 2 * hd), BF16),
                        pltpu.VMEM((rows, hd), BF16), pltpu.VMEM((rows, hd), BF16),
                        pltpu.VMEM((rows, n_keys), F32), pltpu.VMEM((rows, n_keys), F32),
                        pltpu.VMEM((rows, hd), F32), pltpu.VMEM((rows, hd), F32),
                        pltpu.VMEM((rows, 2 * hd), F32), pltpu.VMEM((rows, 2 * hd), F32)],
        compiler_params=_cparams(1),
        name="attention",
    )(*args)


def _rope_tables(n_tokens):
    t = jnp.arange(n_tokens)
    row = (t // GRID_W).astype(F32)
    col = (t % GRID_W).astype(F32)
    inv = ROPE_THETA ** (-jnp.arange(0, AXIS_DIM, 2, dtype=F32) / AXIS_DIM)
    ang = jnp.concatenate([row[:, None] * inv[None, :], col[:, None] * inv[None, :]], axis=-1)
    cos = jnp.repeat(jnp.cos(ang), 2, axis=-1)
    sin = jnp.repeat(jnp.sin(ang), 2, axis=-1)
    sign = jnp.where(jnp.arange(ATT_HEAD_DIM) % 2 == 0, -1.0, 1.0).astype(F32)
    return cos, sin * sign


def kernel(x_prompt, x_sample, c, state_hgrn, cache_k, cache_v, c_ctx, ada_w, ada_b, norm_pre, norm_post, rec_w_in, rec_lb_logits, rec_head_norm, pool_w, pool_scale, rec_w_out, att_w_in, att_q_norm, att_k_norm, att_w_out):
    nb_c, len_c, _ = x_prompt.shape
    nb_l, len_l, _ = x_sample.shape

    lb_p = jax.nn.softmax(rec_lb_logits.astype(F32), axis=0)
    lower_bounds = jnp.clip(jnp.cumsum(lb_p, axis=0) - lb_p[0], 0.0, 1.0)
    lower_bounds = lower_bounds.reshape(N_REC, 2, REC_HEADS, 1, REC_HEAD_DIM)
    cos, sin_signed = _rope_tables(len_l)

    cvec = jnp.zeros((MOD_ROWS, D_MODEL), F32).at[0].set(c_ctx).at[1:1 + nb_l].set(c)
    mod = _modulation(cvec, ada_w, ada_b)

    forget_cols = (2 * POOL_WIDTH + REC_WIDTH, 2 * POOL_WIDTH + 3 * REC_WIDTH)
    layers = []
    for i in range(DEPTH):
        j = i // 2
        rec = i % 2 == 0
        w_in, w_out = (rec_w_in, rec_w_out) if rec else (att_w_in, att_w_out)
        layers.append(dict(
            w_in=w_in[j].astype(BF16), w_out=w_out[j].astype(BF16), f32_cols=forget_cols if rec else (0, 0),
            gain_pre=norm_pre[i].reshape(1, D_MODEL), gain_post=norm_post[i].reshape(1, D_MODEL)))

    def pre_args(i):
        return (layers[i]["w_in"], i, layers[i]["gain_pre"], layers[i]["f32_cols"])

    xs = [x_prompt.reshape(nb_c * len_c, D_MODEL), x_sample.reshape(nb_l * len_l, D_MODEL)]
    streams = ((nb_c, len_c, 0), (nb_l, len_l, 1))
    zs = [None, None]
    ys = [None, None]
    new_states, new_kv = None, None

    def proj_part(s, i, steps=None):
        nb, sl, row0 = streams[s]
        post = ys[s] + (layers[i - 1]["w_out"], i - 1, layers[i - 1]["gain_post"]) if i > 0 else None
        return _proj_part(xs[s], mod, sl, row0, post=post, pre=pre_args(i) if i < DEPTH else None, steps=steps)

    def take_proj(s, i, outs):
        if i > 0:
            xs[s] = outs.pop(0)
        zs[s] = outs

    def rec_part(s, i, heads=None):
        nb, sl, _ = streams[s]
        j = i // 2
        hn = rec_head_norm[j].reshape(1, REC_HEAD_DIM)
        z, zf = zs[s]
        if s == 0:
            return _rec_part(z, zf, nb, sl, lower_bounds[j], hn, j, states=new_states, heads=heads)
        return _rec_part(z, zf, nb, sl, lower_bounds[j], hn, j, s0=state_hgrn, heads=heads)

    def pool(s, i):
        nb, sl, _ = streams[s]
        j = i // 2
        return _pool_mixer(zs[s][0], nb, sl, pool_w[j].astype(BF16), pool_scale[j].reshape(1, POOL_WIDTH))

    take_proj(0, 0, _run(proj_part(0, 0), name="proj")[0])
    for i in range(DEPTH):
        j = i // 2
        if i % 2 == 0:
            rec_c = rec_part(0, i)
            (y_rec, new_states), outs = _run(rec_c, proj_part(1, i, steps=rec_c.steps), name="rec_proj")
            ys[0] = (pool(0, i), 0, y_rec, 0)
            take_proj(1, i, outs)
            rec_l = rec_part(1, i, heads=REC_HEADS * streams[1][0] // rec_c.steps)
            (y_rec,), outs = _run(rec_l, proj_part(0, i + 1, steps=rec_l.steps), name="rec_proj")
            ys[1] = (pool(1, i), 0, y_rec, 0)
            take_proj(0, i + 1, outs)
        else:
            qn = att_q_norm[j].reshape(1, ATT_HEAD_DIM)
            kn = att_k_norm[j].reshape(1, ATT_HEAD_DIM)
            y, *new_kv = _attention(zs[0][0], nb_c, len_c, qn, kn, j, new_kv=new_kv)
            ys[0] = (y, 0, y, 1)
            take_proj(1, i, _run(proj_part(1, i), name="proj")[0])
            (y,) = _attention(zs[1][0], nb_l, len_l, qn, kn, j, cache=(cache_k, cache_v, cos, sin_signed))
            ys[1] = (y, 0, y, 1)
            take_proj(0, i + 1, _run(proj_part(0, i + 1), name="proj")[0])
    take_proj(1, DEPTH, _run(proj_part(1, DEPTH), name="proj")[0])
    xc, xl = xs

    return (xc.reshape(nb_c, len_c, D_MODEL), xl.reshape(nb_l, len_l, D_MODEL),
            new_states, new_kv[0], new_kv[1])
```

```python
import functools
from typing import Any, NamedTuple

import jax
import jax.numpy as jnp
from jax import lax
from jax.experimental import pallas as pl
from jax.experimental.pallas import tpu as pltpu

D_MODEL = 1024
DEPTH = 4
GRID_W = 64
N_REC = (DEPTH + 1) // 2
N_ATT = DEPTH // 2
POOL_WIDTH = D_MODEL // 2
POOL_WINDOWS = (2, 4, 8, 16)
POOL_GROUP_DIM = POOL_WIDTH // len(POOL_WINDOWS)
REC_WIDTH = D_MODEL // 2
REC_HEAD_DIM = 128
REC_HEADS = REC_WIDTH // REC_HEAD_DIM
REC_IN_WIDTH = 2 * POOL_WIDTH + 5 * REC_WIDTH
ATT_HEAD_DIM = 128
ATT_HEADS = D_MODEL // ATT_HEAD_DIM
ATT_KV_HEADS = 2
ATT_REP = ATT_HEADS // ATT_KV_HEADS
ATT_WIDTH = ATT_HEADS * ATT_HEAD_DIM
KV_WIDTH = ATT_KV_HEADS * ATT_HEAD_DIM
ATT_IN_WIDTH = 2 * ATT_WIDTH + 2 * KV_WIDTH
AXIS_DIM = ATT_HEAD_DIM // 2
ROPE_THETA = 10000.0
EPS = 1e-6
F_MIN = 1e-6

MOD_ROWS = 16
TOKEN_TILE = 1024
MIN_PROJ_STEPS = 8
PROJ_ROW_GROUPS = 1
POOL_ROWS = 256
IN_PROJ_STEP = 512
MOD_COLS = 1536
SCAN_CHUNK = 128
SCAN_UNITS = 8
POOL_HALO = 128
ATT_Q_TILE = 128
ATT_KEY_TILE = 128
ATT_STEADY_STEPS = 2
LOG2_E = 1.4426950408889634
VMEM_LIMIT = 56 * 1024 * 1024

F32 = jnp.float32
BF16 = jnp.bfloat16


def _cparams(n_axes):
    return pltpu.CompilerParams(
        dimension_semantics=("arbitrary",) * n_axes, vmem_limit_bytes=VMEM_LIMIT)


def _rms(x, g):
    return x * lax.rsqrt(jnp.mean(x * x, axis=-1, keepdims=True) + EPS) * g


def _dot(a, b):
    return jnp.dot(a, b, preferred_element_type=F32)


def _dot_nt(a, b):
    return lax.dot_general(a, b, (((1,), (1,)), ((), ())), preferred_element_type=F32)


class _Part(NamedTuple):
    kernel: Any
    steps: int
    in_specs: list
    args: list
    out_specs: list
    out_shape: list
    scratch_shapes: list
    aliases: dict


def _run(*parts, name):
    steps = parts[0].steps
    assert all(p.steps == steps for p in parts)
    n_in = [len(p.args) for p in parts]
    n_out = [len(p.out_shape) for p in parts]
    n_scr = [len(p.scratch_shapes) for p in parts]

    def body(*refs):
        ins, outs, scr = refs[:sum(n_in)], refs[sum(n_in):sum(n_in) + sum(n_out)], refs[sum(n_in) + sum(n_out):]
        pending = {}
        for k, p in enumerate(parts):
            i0, o0, s0 = sum(n_in[:k]), sum(n_out[:k]), sum(n_scr[:k])
            pending[k] = (0.0, p.kernel(*ins[i0:i0 + n_in[k]], *outs[o0:o0 + n_out[k]], *scr[s0:s0 + n_scr[k]]))
        while pending:
            k = min(pending, key=lambda k: pending[k][0])
            try:
                pending[k] = (next(pending[k][1]), pending[k][1])
            except StopIteration:
                del pending[k]

    aliases = {}
    for k, p in enumerate(parts):
        for i, o in p.aliases.items():
            aliases[sum(n_in[:k]) + i] = sum(n_out[:k]) + o
    outs = pl.pallas_call(
        body,
        grid=(steps,),
        in_specs=[s for p in parts for s in p.in_specs],
        out_specs=[s for p in parts for s in p.out_specs],
        out_shape=[s for p in parts for s in p.out_shape],
        scratch_shapes=[s for p in parts for s in p.scratch_shapes],
        input_output_aliases=aliases,
        compiler_params=_cparams(1),
        name=name,
    )(*[a for p in parts for a in p.args])
    return [list(outs[sum(n_out[:k]):sum(n_out[:k]) + n_out[k]]) for k in range(len(parts))]


def _mod_kernel(cv_ref, w_ref, b_ref, o_ref):
    cv = cv_ref[...]
    a = (cv * jax.nn.sigmoid(cv)).astype(BF16)
    o_ref[...] = _dot(a, w_ref[...].astype(BF16)) + b_ref[...]


def _modulation(cvec, ada_w, ada_b):
    tn = MOD_COLS
    out = pl.pallas_call(
        _mod_kernel,
        grid=(DEPTH, 3 * D_MODEL // tn),
        in_specs=[
            pl.BlockSpec((MOD_ROWS, D_MODEL), lambda l, n: (0, 0)),
            pl.BlockSpec((None, D_MODEL, tn), lambda l, n: (l, 0, n)),
            pl.BlockSpec((None, 1, tn), lambda l, n: (l, 0, n)),
        ],
        out_specs=pl.BlockSpec((None, MOD_ROWS, tn), lambda l, n: (l, 0, n)),
        out_shape=jax.ShapeDtypeStruct((DEPTH, MOD_ROWS, 3 * D_MODEL), F32),
        compiler_params=_cparams(2),
        name="modulation",
    )(cvec, ada_w, ada_b.reshape(DEPTH, 1, 3 * D_MODEL))
    return out.reshape(DEPTH, MOD_ROWS, 3, 1, D_MODEL)


def _mod_spec(layer, seq_len, row0, tile):
    assert row0 == 0 or seq_len % tile == 0
    tiles_per_seq = seq_len // tile
    if row0 == 0:
        index = lambda i: (layer, 0, 0, 0, 0)
    else:
        index = lambda i: (layer, row0 + i // tiles_per_seq, 0, 0, 0)
    return pl.BlockSpec((None, None, 3, 1, D_MODEL), index)


def _proj_kernel(*refs, post, pre, n_out, f32_cols):
    refs = list(refs)
    x_ref = refs.pop(0)
    if post:
        ya_ref, yb_ref, wo_ref, modp_ref, gpost_ref = (refs.pop(0) for _ in range(5))
    if pre:
        modn_ref, gpre_ref, wi_ref = (refs.pop(0) for _ in range(3))
    xo_ref = refs.pop(0) if post else None
    rows_all = x_ref.shape[0]
    total = PROJ_ROW_GROUPS * (2 * post + n_out // IN_PROJ_STEP)
    done = 0
    for r0 in range(0, rows_all, rows_all // PROJ_ROW_GROUPS):
        rows = slice(r0, r0 + rows_all // PROJ_ROW_GROUPS)
        x = x_ref[rows, :]
        if post:
            half = D_MODEL // 2
            p = _dot(ya_ref[rows, :], wo_ref[0:half, :]) + _dot(yb_ref[rows, :], wo_ref[half:D_MODEL, :])
            x = x + modp_ref[2] * _rms(p, gpost_ref[...])
            xo_ref[rows, :] = x
            done += 2
            yield done / total
        if pre:
            h = (_rms(x, gpre_ref[...]) * (1.0 + modn_ref[1]) + modn_ref[0]).astype(BF16)
            lo, hi = f32_cols
            for c0 in range(0, n_out, IN_PROJ_STEP):
                y = _dot(h, wi_ref[:, c0:c0 + IN_PROJ_STEP])
                if lo <= c0 < hi:
                    refs[1][rows, c0 - lo:c0 - lo + IN_PROJ_STEP] = y
                else:
                    c1 = c0 if c0 < lo else c0 - (hi - lo)
                    refs[0][rows, c1:c1 + IN_PROJ_STEP] = y.astype(BF16)
                done += 1
                yield done / total


def _proj_part(x, mod, seq_len, row0, post=None, pre=None, steps=None):
    t = x.shape[0]
    tile = t // steps if steps else min(TOKEN_TILE, t // MIN_PROJ_STEPS)
    half = D_MODEL // 2
    row = lambda i: (i, 0)
    fixed = lambda i: (0, 0)
    resident = dict(index_map=fixed, pipeline_mode=pl.Buffered(1))
    in_specs = [pl.BlockSpec((tile, D_MODEL), row)]
    args = [x]
    out_specs, out_shape = [], []
    n_out, f32_cols = 0, (0, 0)
    if post:
        ya, ia, yb, ib, w_out, layer, gain = post
        in_specs += [pl.BlockSpec((tile, half), lambda i: (i, ia)),
                     pl.BlockSpec((tile, half), lambda i: (i, ib)),
                     pl.BlockSpec((D_MODEL, D_MODEL), **resident),
                     _mod_spec(layer, seq_len, row0, tile),
                     pl.BlockSpec((1, D_MODEL), fixed)]
        args += [ya, yb, w_out, mod, gain]
        out_specs.append(pl.BlockSpec((tile, D_MODEL), row))
        out_shape.append(jax.ShapeDtypeStruct((t, D_MODEL), F32))
    if pre:
        w_in, layer, gain, f32_cols = pre
        n_out = w_in.shape[1]
        lo, hi = f32_cols
        assert lo % IN_PROJ_STEP == 0 and hi % IN_PROJ_STEP == 0 and n_out % IN_PROJ_STEP == 0
        in_specs += [_mod_spec(layer, seq_len, row0, tile),
                     pl.BlockSpec((1, D_MODEL), fixed),
                     pl.BlockSpec((D_MODEL, n_out), **resident)]
        args += [mod, gain, w_in]
        out_specs.append(pl.BlockSpec((tile, n_out - (hi - lo)), row))
        out_shape.append(jax.ShapeDtypeStruct((t, n_out - (hi - lo)), BF16))
        if hi > lo:
            out_specs.append(pl.BlockSpec((tile, hi - lo), row))
            out_shape.append(jax.ShapeDtypeStruct((t, hi - lo), F32))
    kernel = functools.partial(_proj_kernel, post=bool(post), pre=bool(pre), n_out=n_out, f32_cols=f32_cols)
    return _Part(kernel, t // tile, in_specs, args, out_specs, out_shape, [], {})


def _pool_kernel(u_ref, gp_ref, pw_ref, ps_ref, o_ref, pad_ref, band_ref, sum_ref, dif_ref, *, seq_len):
    gd = POOL_GROUP_DIM
    rows = POOL_ROWS
    halo = POOL_HALO
    span = rows + 2 * halo

    @pl.when(pl.program_id(0) == 0)
    def _():
        r = lax.broadcasted_iota(jnp.int32, (rows, span), 0)
        c = lax.broadcasted_iota(jnp.int32, (rows, span), 1)
        offset = c - halo - r
        for g, win in enumerate(POOL_WINDOWS):
            band_ref[g] = jnp.where((offset >= -(win // 2)) & (offset < win // 2), 1.0, 0.0).astype(BF16)

    zeros = jnp.zeros((halo, POOL_WIDTH), BF16)
    pad_ref[0:halo, :] = zeros
    pad_ref[halo + seq_len:2 * halo + seq_len, :] = zeros
    pad_ref[halo:halo + seq_len, :] = u_ref[...]
    units = [(g, r * rows) for g in range(len(POOL_WINDOWS)) for r in range(seq_len // rows)]

    def cols(g):
        return slice(g * gd, (g + 1) * gd)

    for g, base in units:
        sum_ref[base:base + rows, cols(g)] = _dot(band_ref[g], pad_ref[base:base + span, cols(g)])
    for g, base in units:
        win = POOL_WINDOWS[g]
        t = base + lax.broadcasted_iota(jnp.int32, (rows, 1), 0)
        count = jnp.clip(t + win // 2, 0, seq_len) - jnp.clip(t - win // 2, 0, seq_len)
        mean = sum_ref[base:base + rows, cols(g)] / count.astype(F32)
        d = mean - pad_ref[halo + base:halo + base + rows, cols(g)].astype(F32)
        dif_ref[base:base + rows, cols(g)] = d.astype(BF16)
    for g, base in units:
        y = _dot(dif_ref[base:base + rows, cols(g)], pw_ref[g]) * ps_ref[:, cols(g)]
        gate = gp_ref[base:base + rows, cols(g)].astype(F32)
        o_ref[base:base + rows, cols(g)] = (y * (gate * jax.nn.sigmoid(gate))).astype(BF16)


def _pool_mixer(z, batch, seq_len, pool_w, pool_scale):
    return pl.pallas_call(
        functools.partial(_pool_kernel, seq_len=seq_len),
        grid=(batch,),
        in_specs=[
            pl.BlockSpec((seq_len, POOL_WIDTH), lambda b: (b, 0)),
            pl.BlockSpec((seq_len, POOL_WIDTH), lambda b: (b, 1)),
            pl.BlockSpec((len(POOL_WINDOWS), POOL_GROUP_DIM, POOL_GROUP_DIM), lambda b: (0, 0, 0)),
            pl.BlockSpec((1, POOL_WIDTH), lambda b: (0, 0)),
        ],
        out_specs=pl.BlockSpec((seq_len, POOL_WIDTH), lambda b: (b, 0)),
        out_shape=jax.ShapeDtypeStruct((batch * seq_len, POOL_WIDTH), BF16),
        scratch_shapes=[pltpu.VMEM((seq_len + 2 * POOL_HALO, POOL_WIDTH), BF16),
                        pltpu.VMEM((len(POOL_WINDOWS), POOL_ROWS, POOL_ROWS + 2 * POOL_HALO), BF16),
                        pltpu.VMEM((seq_len, POOL_WIDTH), F32), pltpu.VMEM((seq_len, POOL_WIDTH), BF16)],
        compiler_params=_cparams(1),
        name="pool_mixer",
    )(z, z, pool_w, pool_scale)


def _block_row(p, block, row):
    c, w = p.shape
    p3 = p.reshape(c // block, block, w)
    return jnp.broadcast_to(p3[:, row:row + 1, :], p3.shape).reshape(c, w)


def _interleave(lo, hi, block):
    half = block // 2
    parts = []
    for b in range(SCAN_CHUNK // block):
        parts.append(lo[b * block:b * block + half])
        parts.append(hi[b * block + half:(b + 1) * block])
    return jnp.concatenate(parts, axis=0)


def _select_levels(products, rev):
    c = SCAN_CHUNK
    lane = lax.broadcasted_iota(jnp.int32, (8, c), 1)
    sub = lax.broadcasted_iota(jnp.int32, (8, c), 0)
    out = []
    for t0 in range(0, c, 8):
        if rev:
            keep = (lane >= t0 + sub) & (lane < t0 + 8)
        else:
            keep = (lane >= t0) & (lane <= t0 + sub)
        row = jnp.where(keep, products[0][t0:t0 + 8, :], 0.0)
        for lv, block in enumerate((16, 32, 64, 128), 1):
            half = block // 2
            start = t0 // block * block
            upper = t0 - start >= half
            if upper == rev:
                continue
            lo = start + half if rev else start
            row = jnp.where((lane >= lo) & (lane < lo + half), products[lv][t0:t0 + 8, :], row)
        out.append(row)
    return jnp.concatenate(out, axis=0)


def _rec_kernel(*refs, seq_len, heads, has_s0, out_state, n_aliased, layer):
    q_ref, ff_ref, fb_ref, v_ref, gr_ref, lb_ref, hn_ref = refs[:7]
    rest = list(refs[7:])
    s0_ref = rest.pop(0) if has_s0 else None
    del rest[:n_aliased]
    o_ref = rest.pop(0)
    st_ref = rest.pop(0) if out_state else None
    acc_ref, qs_ref, k_ref, p_ref, a_ref, qin_ref, u_ref, dec_ref, tri_ref = rest

    c = SCAN_CHUNK
    hd = REC_HEAD_DIM
    n_chunks = seq_len // c
    zf_refs = (ff_ref, fb_ref)
    units = [(h, n) for h in range(heads) for n in range(n_chunks)]

    def rows_in(n):
        return slice(n * c, (n + 1) * c)

    def rows_sc(h, n):
        return slice((h * n_chunks + n) * c, (h * n_chunks + n + 1) * c)

    def cols(h):
        return slice(h * hd, (h + 1) * hd)

    ti = lax.broadcasted_iota(jnp.int32, (c, c), 0)
    si = lax.broadcasted_iota(jnp.int32, (c, c), 1)
    for d, causal in enumerate((si <= ti, si >= ti)):
        tri_ref[d] = jnp.where(causal, 1.0, 0.0).astype(BF16)

    weights = (0.26, 0.43, 0.09, 0.22)
    done = [0.0]

    def progress(stage, share):
        done[0] += weights[stage] * share
        return done[0]

    for h, n in units:
        qz = q_ref[rows_in(n), cols(h)].astype(F32)
        qs_ref[rows_sc(h, n), :] = qz * jax.nn.sigmoid(qz)
        for d in range(2):
            lower = lb_ref[d, h]
            f = jnp.clip(lower + (1.0 - lower) * jax.nn.sigmoid(zf_refs[d][rows_in(n), cols(h)]), F_MIN, 1.0)
            k_ref[d, rows_sc(h, n), :] = 1.0 - f
            g = jnp.log(f) * LOG2_E
            g_hi = g.astype(BF16)
            rest = g - g_hi.astype(F32)
            g_mid = rest.astype(BF16)
            g_lo = (rest - g_mid.astype(F32)).astype(BF16)
            sums = _dot(tri_ref[d], jnp.concatenate([g_hi, g_mid, g_lo], axis=1))
            p_ref[d, rows_sc(h, n), :] = (sums[:, 0:hd] + sums[:, hd:2 * hd]) + sums[:, 2 * hd:3 * hd]
        yield progress(0, 1 / len(units))

    for h, n in units:
        rows = rows_sc(h, n)
        q = qs_ref[rows, :]
        v_t = v_ref[rows_in(n), cols(h)].astype(F32).T.astype(BF16)
        for d in range(2):
            rev = d == 1
            k = k_ref[d, rows, :]
            p = p_ref[d, rows, :]
            e0 = p - _block_row(p, 8, 4 if rev else 3)
            products = [_dot_nt((q * jnp.exp2(e0)).astype(BF16), (k * jnp.exp2(-e0)).astype(BF16))]
            for block in (16, 32, 64, 128):
                beta = _block_row(p, block, block // 2 if rev else block // 2 - 1)
                if rev:
                    e = _interleave(p, beta, block) - _interleave(beta, p, block)
                    src = _interleave(q, k, block)
                else:
                    e = _interleave(beta, p, block) - _interleave(p, beta, block)
                    src = _interleave(k, q, block)
                m = (src * jnp.exp2(e)).astype(BF16)
                products.append(_dot_nt(m, m))
            a_ref[rows, d * c:(d + 1) * c] = _select_levels(products, rev).astype(BF16)

            edge = p[0:1, :] if rev else p[c - 1:c, :]
            qin_ref[d, rows, :] = (q * jnp.exp2(p)).astype(BF16)
            u_ref[d, h * n_chunks + n] = _dot(v_t, (k * jnp.exp2(edge - p)).astype(BF16))
            dec_ref[d, h * n_chunks + n] = jnp.broadcast_to(jnp.exp2(edge), (8, hd))
        yield progress(1, 1 / len(units))

    for h, n in units:
        v_b = v_ref[rows_in(n), cols(h)]
        acc_ref[rows_sc(h, n), :] = _dot(a_ref[rows_sc(h, n), :], jnp.concatenate([v_b, v_b], axis=0))
        yield progress(2, 1 / len(units))

    for h in range(heads):
        if has_s0:
            states = [s0_ref[0, h].T, s0_ref[1, h].T]
        else:
            states = [jnp.zeros((hd, hd), F32)] * 2
        inter = [[None] * n_chunks, [None] * n_chunks]
        for i in range(n_chunks):
            for d, n in ((0, i), (1, n_chunks - 1 - i)):
                inter[d][n] = _dot_nt(qin_ref[d, rows_sc(h, n), :], states[d].astype(BF16))
                states[d] = dec_ref[d, h * n_chunks + n][0:1, :] * states[d] + u_ref[d, h * n_chunks + n]
        for n in range(n_chunks):
            gate = gr_ref[rows_in(n), cols(h)].astype(F32)
            o = acc_ref[rows_sc(h, n), :] + inter[0][n] + inter[1][n]
            o_ref[rows_in(n), cols(h)] = (_rms(o, hn_ref[...]) * (gate * jax.nn.sigmoid(gate))).astype(BF16)
        if out_state:
            st_ref[layer, 0, h] = states[0].T
            st_ref[layer, 1, h] = states[1].T
        yield progress(3, 1 / heads)
    if out_state:
        for other in range(st_ref.shape[0]):
            if other != layer:
                st_ref[other] = jnp.zeros(st_ref.shape[1:], F32)


def _rec_part(z, zf, batch, seq_len, lower, head_norm, j, s0=None, states=None, heads=None):
    hd = REC_HEAD_DIM
    nh = REC_HEADS
    n_chunks = seq_len // SCAN_CHUNK
    heads = heads or min(nh, max(1, SCAN_UNITS // n_chunks))
    width = heads * hd
    col0 = 2 * POOL_WIDTH // width
    groups = nh // heads

    def zspec(part):
        return pl.BlockSpec((seq_len, width), lambda i: (i // groups, col0 + part * groups + i % groups))

    def fspec(part):
        return pl.BlockSpec((seq_len, width), lambda i: (i // groups, part * groups + i % groups))

    in_specs = [zspec(0), fspec(0), fspec(1), zspec(1), zspec(2),
                pl.BlockSpec((2, heads, 1, hd), lambda i: (0, i % groups, 0, 0)),
                pl.BlockSpec((1, hd), lambda i: (0, 0))]
    args = [z, zf, zf, z, z, lower, head_norm]
    has_s0 = s0 is not None
    if has_s0:
        in_specs.append(pl.BlockSpec((None, None, 2, heads, hd, hd),
                                     lambda i: (i // groups, j, 0, i % groups, 0, 0)))
        args.append(s0)
    out_state = not has_s0
    out_specs = [pl.BlockSpec((seq_len, width), lambda i: (i // groups, i % groups))]
    out_shape = [jax.ShapeDtypeStruct((batch * seq_len, REC_WIDTH), BF16)]
    aliases = {}
    layer = j
    if out_state:
        out_shape.append(jax.ShapeDtypeStruct((batch, N_REC, 2, nh, hd, hd), F32))
        if states is None:
            out_specs.append(pl.BlockSpec((None, N_REC, 2, heads, hd, hd),
                                          lambda i: (i // groups, 0, 0, i % groups, 0, 0)))
        else:
            out_specs.append(pl.BlockSpec((None, 1, 2, heads, hd, hd),
                                          lambda i: (i // groups, j, 0, i % groups, 0, 0)))
            layer = 0
            aliases[len(args)] = 1
            in_specs.append(pl.BlockSpec(memory_space=pl.ANY))
            args.append(states)
    tokens = heads * seq_len
    kernel = functools.partial(_rec_kernel, seq_len=seq_len, heads=heads, has_s0=has_s0, out_state=out_state,
                               n_aliased=len(aliases), layer=layer)
    scratch_shapes = [pltpu.VMEM((tokens, hd), F32),
                      pltpu.VMEM((tokens, hd), F32),
                      pltpu.VMEM((2, tokens, hd), F32),
                      pltpu.VMEM((2, tokens, hd), F32),
                      pltpu.VMEM((tokens, 2 * SCAN_CHUNK), BF16),
                      pltpu.VMEM((2, tokens, hd), BF16),
                      pltpu.VMEM((2, tokens // SCAN_CHUNK, hd, hd), F32),
                      pltpu.VMEM((2, tokens // SCAN_CHUNK, 8, hd), F32),
                      pltpu.VMEM((2, SCAN_CHUNK, SCAN_CHUNK), BF16)]
    return _Part(kernel, batch * groups, in_specs, args, out_specs, out_shape, scratch_shapes, aliases)


def _rope(x, cos, sin_signed):
    lane = lax.broadcasted_iota(jnp.int32, x.shape, 1)
    partner = jnp.where((lane & 1) == 0, pltpu.roll(x, ATT_HEAD_DIM - 1, 1), pltpu.roll(x, 1, 1))
    return x * cos + partner * sin_signed


def _att_kernel(*refs, seq_len, n_keys, latent, n_aliased, layer):
    q_ref, k_ref, v_ref, qn_ref, kn_ref = refs[:5]
    g_refs = refs[5:5 + ATT_KV_HEADS]
    if latent:
        ck_ref, cv_ref, cos_ref, sin_ref, y_ref, kall, vall = refs[5 + ATT_KV_HEADS:-8]
    else:
        y_ref, ko_ref, vo_ref, kall, vall = refs[5 + ATT_KV_HEADS + n_aliased:-8]
    qs_bufs, s_bufs, m_bufs, o_bufs = refs[-8:-6], refs[-6:-4], refs[-4:-2], refs[-2:]
    hd = ATT_HEAD_DIM
    qw = ATT_REP * hd
    tq = ATT_Q_TILE
    tk = min(ATT_KEY_TILE, n_keys)
    n_kb = n_keys // tk
    n_past = n_keys - seq_len
    n_items = ATT_KV_HEADS * (seq_len // tq)
    exp2_scale = hd ** -0.5 * LOG2_E
    assert ATT_KV_HEADS == 2

    if not latent:
        for other in range(ko_ref.shape[0]):
            if other != layer:
                ko_ref[other] = jnp.zeros(ko_ref.shape[1:], F32)
                vo_ref[other] = jnp.zeros(vo_ref.shape[1:], F32)
    lane = lax.broadcasted_iota(jnp.int32, (n_keys, hd), 1)
    for g in range(ATT_KV_HEADS):
        k = _rms(k_ref[:, g * hd:(g + 1) * hd].astype(F32), kn_ref[...])
        v = v_ref[:, g * hd:(g + 1) * hd].astype(F32)
        if latent:
            k = _rope(k, cos_ref[...], sin_ref[...])
            kall[g, 0:n_past, :] = ck_ref[pl.ds(g, n_past, stride=ATT_KV_HEADS), :].astype(BF16)
            vall[g, 0:n_past, 0:hd] = cv_ref[pl.ds(g, n_past, stride=ATT_KV_HEADS), :].astype(BF16)
        else:
            ko_ref[layer, :, g, :] = k
            vo_ref[layer, :, g, :] = v
        kall[g, n_past:n_keys, :] = k.astype(BF16)
        vall[g, n_past:n_keys, 0:hd] = v.astype(BF16)
        vall[g, :, hd:2 * hd] = (lane == 0).astype(BF16)

    def tile_rows(item):
        return pl.ds(pl.multiple_of((item // ATT_KV_HEADS) * tq, tq), tq)

    def prep(item, g):
        rows = tile_rows(item)
        for h in range(ATT_REP):
            q = _rms(q_ref[rows, g * qw + h * hd:g * qw + (h + 1) * hd].astype(F32), qn_ref[...])
            if latent:
                q = _rope(q, cos_ref[rows, :], sin_ref[rows, :])
            qs_bufs[g][h * tq:(h + 1) * tq, :] = q.astype(BF16)

    def scores(item, g):
        qs = qs_bufs[g][...]
        mx = None
        for kb in range(n_kb):
            s = _dot_nt(qs, kall[g, kb * tk:(kb + 1) * tk, :])
            s_bufs[g][:, kb * tk:(kb + 1) * tk] = s
            for c in range(tk // hd):
                part = s[:, c * hd:(c + 1) * hd]
                mx = part if mx is None else jnp.maximum(mx, part)
        m = jnp.max(mx, axis=-1, keepdims=True) * exp2_scale
        m_bufs[g][...] = jnp.broadcast_to(m, (ATT_REP * tq, hd))

    def mix(item, g):
        m = m_bufs[g][...]
        ps = []
        for c in range(n_keys // hd):
            s = s_bufs[g][:, c * hd:(c + 1) * hd]
            ps.append(jnp.exp2(s * exp2_scale - m).astype(BF16))
        o_bufs[g][...] = _dot(jnp.concatenate(ps, axis=1), vall[g])

    def finish(item, g):
        rows = tile_rows(item)
        o = o_bufs[g][:, 0:hd] / o_bufs[g][:, hd:hd + 1]
        for h in range(ATT_REP):
            gate = g_refs[g][rows, h * hd:(h + 1) * hd].astype(F32)
            y = o[h * tq:(h + 1) * tq, :] * (gate * jax.nn.sigmoid(gate))
            y_ref[rows, g * qw + h * hd:g * qw + (h + 1) * hd] = y.astype(BF16)

    stages = (prep, scores, mix, finish)

    def step(k, parity):
        for a, stage in enumerate(stages):
            item = k - a
            if isinstance(item, int) and not 0 <= item < n_items:
                continue
            stage(item, (parity + a) % 2)

    depth = len(stages) - 1
    for k in range(min(depth, n_items + depth)):
        step(k, k % 2)
    n_steady = max(n_items - depth, 0)

    group = ATT_STEADY_STEPS

    def steady_group(j, carry):
        for u in range(group):
            step(depth + group * j + u, (depth + u) % 2)
        return carry

    lax.fori_loop(0, n_steady // group, steady_group, 0)
    for k in range(depth + group * (n_steady // group), n_items + depth):
        step(k, k % 2)


def _attention(z, batch, seq_len, q_norm, k_norm, j, cache=None, new_kv=None):
    hd = ATT_HEAD_DIM
    latent = cache is not None
    qw = ATT_REP * hd
    kv0 = ATT_WIDTH // KV_WIDTH
    gate0 = (ATT_WIDTH + 2 * KV_WIDTH) // qw
    assert (ATT_WIDTH + 2 * KV_WIDTH) % qw == 0 and ATT_WIDTH % KV_WIDTH == 0
    in_specs = [
        pl.BlockSpec((seq_len, ATT_WIDTH), lambda b: (b, 0)),
        pl.BlockSpec((seq_len, KV_WIDTH), lambda b: (b, kv0)),
        pl.BlockSpec((seq_len, KV_WIDTH), lambda b: (b, kv0 + 1)),
        pl.BlockSpec((1, hd), lambda b: (0, 0)),
        pl.BlockSpec((1, hd), lambda b: (0, 0)),
    ] + [pl.BlockSpec((seq_len, qw), lambda b, g=g: (b, gate0 + g)) for g in range(ATT_KV_HEADS)]
    args = [z, z, z, q_norm, k_norm] + [z] * ATT_KV_HEADS
    out_specs = [pl.BlockSpec((seq_len, ATT_WIDTH), lambda b: (b, 0))]
    out_shape = [jax.ShapeDtypeStruct((batch * seq_len, ATT_WIDTH), BF16)]
    n_keys = seq_len
    aliases = {}
    layer = j
    if latent:
        cache_k, cache_v, cos, sin_signed = cache
        past = cache_k.shape[2]
        n_keys += past
        cspec = pl.BlockSpec((None, None, past * ATT_KV_HEADS, hd), lambda b: (b, j, 0, 0))
        tspec = pl.BlockSpec((seq_len, hd), lambda b: (0, 0))
        in_specs += [cspec, cspec, tspec, tspec]
        args += [cache_k.reshape(batch, N_ATT, past * ATT_KV_HEADS, hd),
                 cache_v.reshape(batch, N_ATT, past * ATT_KV_HEADS, hd), cos, sin_signed]
    else:
        out_shape += [jax.ShapeDtypeStruct((batch, N_ATT, seq_len, ATT_KV_HEADS, hd), F32)] * 2
        if new_kv is None:
            kv_spec = pl.BlockSpec((None, N_ATT, seq_len, ATT_KV_HEADS, hd), lambda b: (b, 0, 0, 0, 0))
        else:
            kv_spec = pl.BlockSpec((None, 1, seq_len, ATT_KV_HEADS, hd), lambda b: (b, j, 0, 0, 0))
            layer = 0
            aliases = {len(args): 1, len(args) + 1: 2}
            in_specs += [pl.BlockSpec(memory_space=pl.ANY)] * 2
            args += list(new_kv)
        out_specs += [kv_spec, kv_spec]
    rows = ATT_REP * ATT_Q_TILE
    return pl.pallas_call(
        functools.partial(_att_kernel, seq_len=seq_len, n_keys=n_keys, latent=latent, n_aliased=len(aliases),
                          layer=layer),
        grid=(batch,),
        in_specs=in_specs,
        out_specs=out_specs,
        out_shape=out_shape,
        input_output_aliases=aliases,
        scratch_shapes=[pltpu.VMEM((ATT_KV_HEADS, n_keys, hd), BF16),
                        pltpu.VMEM((ATT_KV_HEADS, n_keys, 2 * hd), BF16),
                        pltpu.VMEM((rows, hd), BF16), pltpu.VMEM((rows, hd), BF16),
                        pltpu.VMEM((rows, n_keys), F32), pltpu.VMEM((rows, n_keys), F32),
                        pltpu.VMEM((rows, hd), F32), pltpu.VMEM((rows, hd), F32),
                        pltpu.VMEM((rows, 2 * hd), F32), pltpu.VMEM((rows, 2 * hd), F32)],
        compiler_params=_cparams(1),
        name="attention",
    )(*args)


def _rope_tables(n_tokens):
    t = jnp.arange(n_tokens)
    row = (t // GRID_W).astype(F32)
    col = (t % GRID_W).astype(F32)
    inv = ROPE_THETA ** (-jnp.arange(0, AXIS_DIM, 2, dtype=F32) / AXIS_DIM)
    ang = jnp.concatenate([row[:, None] * inv[None, :], col[:, None] * inv[None, :]], axis=-1)
    cos = jnp.repeat(jnp.cos(ang), 2, axis=-1)
    sin = jnp.repeat(jnp.sin(ang), 2, axis=-1)
    sign = jnp.where(jnp.arange(ATT_HEAD_DIM) % 2 == 0, -1.0, 1.0).astype(F32)
    return cos, sin * sign


def kernel(x_prompt, x_sample, c, state_hgrn, cache_k, cache_v, c_ctx, ada_w, ada_b, norm_pre, norm_post, rec_w_in, rec_lb_logits, rec_head_norm, pool_w, pool_scale, rec_w_out, att_w_in, att_q_norm, att_k_norm, att_w_out):
    nb_c, len_c, _ = x_prompt.shape
    nb_l, len_l, _ = x_sample.shape

    lb_p = jax.nn.softmax(rec_lb_logits.astype(F32), axis=0)
    lower_bounds = jnp.clip(jnp.cumsum(lb_p, axis=0) - lb_p[0], 0.0, 1.0)
    lower_bounds = lower_bounds.reshape(N_REC, 2, REC_HEADS, 1, REC_HEAD_DIM)
    cos, sin_signed = _rope_tables(len_l)

    cvec = jnp.zeros((MOD_ROWS, D_MODEL), F32).at[0].set(c_ctx).at[1:1 + nb_l].set(c)
    mod = _modulation(cvec, ada_w, ada_b)

    forget_cols = (2 * POOL_WIDTH + REC_WIDTH, 2 * POOL_WIDTH + 3 * REC_WIDTH)
    layers = []
    for i in range(DEPTH):
        j = i // 2
        rec = i % 2 == 0
        w_in, w_out = (rec_w_in, rec_w_out) if rec else (att_w_in, att_w_out)
        layers.append(dict(
            w_in=w_in[j].astype(BF16), w_out=w_out[j].astype(BF16), f32_cols=forget_cols if rec else (0, 0),
            gain_pre=norm_pre[i].reshape(1, D_MODEL), gain_post=norm_post[i].reshape(1, D_MODEL)))

    def pre_args(i):
        return (layers[i]["w_in"], i, layers[i]["gain_pre"], layers[i]["f32_cols"])

    xs = [x_prompt.reshape(nb_c * len_c, D_MODEL), x_sample.reshape(nb_l * len_l, D_MODEL)]
    streams = ((nb_c, len_c, 0), (nb_l, len_l, 1))
    zs = [None, None]
    ys = [None, None]
    new_states, new_kv = None, None

    def proj_part(s, i, steps=None):
        nb, sl, row0 = streams[s]
        post = ys[s] + (layers[i - 1]["w_out"], i - 1, layers[i - 1]["gain_post"]) if i > 0 else None
        return _proj_part(xs[s], mod, sl, row0, post=post, pre=pre_args(i) if i < DEPTH else None, steps=steps)

    def take_proj(s, i, outs):
        if i > 0:
            xs[s] = outs.pop(0)
        zs[s] = outs

    def rec_part(s, i, heads=None):
        nb, sl, _ = streams[s]
        j = i // 2
        hn = rec_head_norm[j].reshape(1, REC_HEAD_DIM)
        z, zf = zs[s]
        if s == 0:
            return _rec_part(z, zf, nb, sl, lower_bounds[j], hn, j, states=new_states, heads=heads)
        return _rec_part(z, zf, nb, sl, lower_bounds[j], hn, j, s0=state_hgrn, heads=heads)

    def pool(s, i):
        nb, sl, _ = streams[s]
        j = i // 2
        return _pool_mixer(zs[s][0], nb, sl, pool_w[j].astype(BF16), pool_scale[j].reshape(1, POOL_WIDTH))

    take_proj(0, 0, _run(proj_part(0, 0), name="proj")[0])
    for i in range(DEPTH):
        j = i // 2
        if i % 2 == 0:
            rec_c = rec_part(0, i)
            (y_rec, new_states), outs = _run(rec_c, proj_part(1, i, steps=rec_c.steps), name="rec_proj")
            ys[0] = (pool(0, i), 0, y_rec, 0)
            take_proj(1, i, outs)
            rec_l = rec_part(1, i, heads=REC_HEADS * streams[1][0] // rec_c.steps)
            (y_rec,), outs = _run(rec_l, proj_part(0, i + 1, steps=rec_l.steps), name="rec_proj")
            ys[1] = (pool(1, i), 0, y_rec, 0)
            take_proj(0, i + 1, outs)
        else:
            qn = att_q_norm[j].reshape(1, ATT_HEAD_DIM)
            kn = att_k_norm[j].reshape(1, ATT_HEAD_DIM)
            y, *new_kv = _attention(zs[0][0], nb_c, len_c, qn, kn, j, new_kv=new_kv)
            ys[0] = (y, 0, y, 1)
            take_proj(1, i, _run(proj_part(1, i), name="proj")[0])
            (y,) = _attention(zs[1][0], nb_l, len_l, qn, kn, j, cache=(cache_k, cache_v, cos, sin_signed))
            ys[1] = (y, 0, y, 1)
            take_proj(0, i + 1, _run(proj_part(0, i + 1), name="proj")[0])
    take_proj(1, DEPTH, _run(proj_part(1, DEPTH), name="proj")[0])
    xc, xl = xs

    return (xc.reshape(nb_c, len_c, D_MODEL), xl.reshape(nb_l, len_l, D_MODEL),
            new_states, new_kv[0], new_kv[1])
```

```python
import functools
from typing import Any, NamedTuple

import jax
import jax.numpy as jnp
from jax import lax
from jax.experimental import pallas as pl
from jax.experimental.pallas import tpu as pltpu

D_MODEL = 1024
DEPTH = 4
GRID_W = 64
N_REC = (DEPTH + 1) // 2
N_ATT = DEPTH // 2
POOL_WIDTH = D_MODEL // 2
POOL_WINDOWS = (2, 4, 8, 16)
POOL_GROUP_DIM = POOL_WIDTH // len(POOL_WINDOWS)
REC_WIDTH = D_MODEL // 2
REC_HEAD_DIM = 128
REC_HEADS = REC_WIDTH // REC_HEAD_DIM
REC_IN_WIDTH = 2 * POOL_WIDTH + 5 * REC_WIDTH
ATT_HEAD_DIM = 128
ATT_HEADS = D_MODEL // ATT_HEAD_DIM
ATT_KV_HEADS = 2
ATT_REP = ATT_HEADS // ATT_KV_HEADS
ATT_WIDTH = ATT_HEADS * ATT_HEAD_DIM
KV_WIDTH = ATT_KV_HEADS * ATT_HEAD_DIM
ATT_IN_WIDTH = 2 * ATT_WIDTH + 2 * KV_WIDTH
AXIS_DIM = ATT_HEAD_DIM // 2
ROPE_THETA = 10000.0
EPS = 1e-6
F_MIN = 1e-6

MOD_ROWS = 16
TOKEN_TILE = 1024
MIN_PROJ_STEPS = 8
PROJ_GROUP_ROWS = 256
POOL_ROWS = 256
IN_PROJ_STEP = 512
MOD_COLS = 1536
SCAN_CHUNK = 128
SCAN_UNITS = 8
POOL_HALO = 128
ATT_Q_TILE = 128
ATT_KEY_TILE = 128
ATT_STEADY_STEPS = 2
LOG2_E = 1.4426950408889634
VMEM_LIMIT = 56 * 1024 * 1024

F32 = jnp.float32
BF16 = jnp.bfloat16


def _cparams(n_axes):
    return pltpu.CompilerParams(
        dimension_semantics=("arbitrary",) * n_axes, vmem_limit_bytes=VMEM_LIMIT)


def _rms(x, g):
    return x * lax.rsqrt(jnp.mean(x * x, axis=-1, keepdims=True) + EPS) * g


def _dot(a, b):
    return jnp.dot(a, b, preferred_element_type=F32)


def _dot_nt(a, b):
    return lax.dot_general(a, b, (((1,), (1,)), ((), ())), preferred_element_type=F32)


class _Part(NamedTuple):
    kernel: Any
    steps: int
    in_specs: list
    args: list
    out_specs: list
    out_shape: list
    scratch_shapes: list
    aliases: dict


def _run(*parts, name):
    steps = parts[0].steps
    assert all(p.steps == steps for p in parts)
    n_in = [len(p.args) for p in parts]
    n_out = [len(p.out_shape) for p in parts]
    n_scr = [len(p.scratch_shapes) for p in parts]

    def body(*refs):
        ins, outs, scr = refs[:sum(n_in)], refs[sum(n_in):sum(n_in) + sum(n_out)], refs[sum(n_in) + sum(n_out):]
        pending = {}
        for k, p in enumerate(parts):
            i0, o0, s0 = sum(n_in[:k]), sum(n_out[:k]), sum(n_scr[:k])
            pending[k] = (0.0, p.kernel(*ins[i0:i0 + n_in[k]], *outs[o0:o0 + n_out[k]], *scr[s0:s0 + n_scr[k]]))
        while pending:
            k = min(pending, key=lambda k: pending[k][0])
            try:
                pending[k] = (next(pending[k][1]), pending[k][1])
            except StopIteration:
                del pending[k]

    aliases = {}
    for k, p in enumerate(parts):
        for i, o in p.aliases.items():
            aliases[sum(n_in[:k]) + i] = sum(n_out[:k]) + o
    outs = pl.pallas_call(
        body,
        grid=(steps,),
        in_specs=[s for p in parts for s in p.in_specs],
        out_specs=[s for p in parts for s in p.out_specs],
        out_shape=[s for p in parts for s in p.out_shape],
        scratch_shapes=[s for p in parts for s in p.scratch_shapes],
        input_output_aliases=aliases,
        compiler_params=_cparams(1),
        name=name,
    )(*[a for p in parts for a in p.args])
    return [list(outs[sum(n_out[:k]):sum(n_out[:k]) + n_out[k]]) for k in range(len(parts))]


def _mod_kernel(cv_ref, w_ref, b_ref, o_ref):
    cv = cv_ref[...]
    a = (cv * jax.nn.sigmoid(cv)).astype(BF16)
    o_ref[...] = _dot(a, w_ref[...].astype(BF16)) + b_ref[...]


def _modulation(cvec, ada_w, ada_b):
    tn = MOD_COLS
    out = pl.pallas_call(
        _mod_kernel,
        grid=(DEPTH, 3 * D_MODEL // tn),
        in_specs=[
            pl.BlockSpec((MOD_ROWS, D_MODEL), lambda l, n: (0, 0)),
            pl.BlockSpec((None, D_MODEL, tn), lambda l, n: (l, 0, n)),
            pl.BlockSpec((None, 1, tn), lambda l, n: (l, 0, n)),
        ],
        out_specs=pl.BlockSpec((None, MOD_ROWS, tn), lambda l, n: (l, 0, n)),
        out_shape=jax.ShapeDtypeStruct((DEPTH, MOD_ROWS, 3 * D_MODEL), F32),
        compiler_params=_cparams(2),
        name="modulation",
    )(cvec, ada_w, ada_b.reshape(DEPTH, 1, 3 * D_MODEL))
    return out.reshape(DEPTH, MOD_ROWS, 3, 1, D_MODEL)


def _mod_spec(layer, seq_len, row0, tile):
    assert row0 == 0 or seq_len % tile == 0
    tiles_per_seq = seq_len // tile
    if row0 == 0:
        index = lambda i: (layer, 0, 0, 0, 0)
    else:
        index = lambda i: (layer, row0 + i // tiles_per_seq, 0, 0, 0)
    return pl.BlockSpec((None, None, 3, 1, D_MODEL), index)


def _proj_kernel(*refs, post, pre, n_out, f32_cols):
    refs = list(refs)
    x_ref = refs.pop(0)
    if post:
        ya_ref, yb_ref, wo_ref, modp_ref, gpost_ref = (refs.pop(0) for _ in range(5))
    if pre:
        modn_ref, gpre_ref, wi_ref = (refs.pop(0) for _ in range(3))
    xo_ref = refs.pop(0) if post else None
    rows_all = x_ref.shape[0]
    n_groups = max(1, rows_all // PROJ_GROUP_ROWS)

    def group(rows):
        x = x_ref[rows, :]
        if post:
            half = D_MODEL // 2
            p = _dot(ya_ref[rows, :], wo_ref[0:half, :]) + _dot(yb_ref[rows, :], wo_ref[half:D_MODEL, :])
            x = x + modp_ref[2] * _rms(p, gpost_ref[...])
            xo_ref[rows, :] = x
            yield
        if pre:
            h = (_rms(x, gpre_ref[...]) * (1.0 + modn_ref[1]) + modn_ref[0]).astype(BF16)
            lo, hi = f32_cols
            for c0 in range(0, n_out, IN_PROJ_STEP):
                y = _dot(h, wi_ref[:, c0:c0 + IN_PROJ_STEP])
                if lo <= c0 < hi:
                    refs[1][rows, c0 - lo:c0 - lo + IN_PROJ_STEP] = y
                else:
                    c1 = c0 if c0 < lo else c0 - (hi - lo)
                    refs[0][rows, c1:c1 + IN_PROJ_STEP] = y.astype(BF16)
                yield

    size = rows_all // n_groups
    waiting = [group(slice(k * size, (k + 1) * size)) for k in range(n_groups)]
    running = []
    pieces = n_groups * (post + n_out // IN_PROJ_STEP)
    done = 0
    while waiting or running:
        if waiting:
            running.append(waiting.pop(0))
        for g in list(running):
            try:
                next(g)
                done += 1
            except StopIteration:
                running.remove(g)
        yield min(done / pieces, 1.0)


def _proj_part(x, mod, seq_len, row0, post=None, pre=None, steps=None):
    t = x.shape[0]
    tile = t // steps if steps else min(TOKEN_TILE, t // MIN_PROJ_STEPS)
    half = D_MODEL // 2
    row = lambda i: (i, 0)
    fixed = lambda i: (0, 0)
    resident = dict(index_map=fixed, pipeline_mode=pl.Buffered(1))
    in_specs = [pl.BlockSpec((tile, D_MODEL), row)]
    args = [x]
    out_specs, out_shape = [], []
    n_out, f32_cols = 0, (0, 0)
    if post:
        ya, ia, yb, ib, w_out, layer, gain = post
        in_specs += [pl.BlockSpec((tile, half), lambda i: (i, ia)),
                     pl.BlockSpec((tile, half), lambda i: (i, ib)),
                     pl.BlockSpec((D_MODEL, D_MODEL), **resident),
                     _mod_spec(layer, seq_len, row0, tile),
                     pl.BlockSpec((1, D_MODEL), fixed)]
        args += [ya, yb, w_out, mod, gain]
        out_specs.append(pl.BlockSpec((tile, D_MODEL), row))
        out_shape.append(jax.ShapeDtypeStruct((t, D_MODEL), F32))
    if pre:
        w_in, layer, gain, f32_cols = pre
        n_out = w_in.shape[1]
        lo, hi = f32_cols
        assert lo % IN_PROJ_STEP == 0 and hi % IN_PROJ_STEP == 0 and n_out % IN_PROJ_STEP == 0
        in_specs += [_mod_spec(layer, seq_len, row0, tile),
                     pl.BlockSpec((1, D_MODEL), fixed),
                     pl.BlockSpec((D_MODEL, n_out), **resident)]
        args += [mod, gain, w_in]
        out_specs.append(pl.BlockSpec((tile, n_out - (hi - lo)), row))
        out_shape.append(jax.ShapeDtypeStruct((t, n_out - (hi - lo)), BF16))
        if hi > lo:
            out_specs.append(pl.BlockSpec((tile, hi - lo), row))
            out_shape.append(jax.ShapeDtypeStruct((t, hi - lo), F32))
    kernel = functools.partial(_proj_kernel, post=bool(post), pre=bool(pre), n_out=n_out, f32_cols=f32_cols)
    return _Part(kernel, t // tile, in_specs, args, out_specs, out_shape, [], {})


def _pool_kernel(u_ref, gp_ref, pw_ref, ps_ref, o_ref, pad_ref, band_ref, sum_ref, dif_ref, *, seq_len):
    gd = POOL_GROUP_DIM
    rows = POOL_ROWS
    halo = POOL_HALO
    span = rows + 2 * halo

    @pl.when(pl.program_id(0) == 0)
    def _():
        r = lax.broadcasted_iota(jnp.int32, (rows, span), 0)
        c = lax.broadcasted_iota(jnp.int32, (rows, span), 1)
        offset = c - halo - r
        for g, win in enumerate(POOL_WINDOWS):
            band_ref[g] = jnp.where((offset >= -(win // 2)) & (offset < win // 2), 1.0, 0.0).astype(BF16)

    zeros = jnp.zeros((halo, POOL_WIDTH), BF16)
    pad_ref[0:halo, :] = zeros
    pad_ref[halo + seq_len:2 * halo + seq_len, :] = zeros
    pad_ref[halo:halo + seq_len, :] = u_ref[...]
    units = [(g, r * rows) for g in range(len(POOL_WINDOWS)) for r in range(seq_len // rows)]

    def cols(g):
        return slice(g * gd, (g + 1) * gd)

    for g, base in units:
        sum_ref[base:base + rows, cols(g)] = _dot(band_ref[g], pad_ref[base:base + span, cols(g)])
    for g, base in units:
        win = POOL_WINDOWS[g]
        t = base + lax.broadcasted_iota(jnp.int32, (rows, 1), 0)
        count = jnp.clip(t + win // 2, 0, seq_len) - jnp.clip(t - win // 2, 0, seq_len)
        mean = sum_ref[base:base + rows, cols(g)] / count.astype(F32)
        d = mean - pad_ref[halo + base:halo + base + rows, cols(g)].astype(F32)
        dif_ref[base:base + rows, cols(g)] = d.astype(BF16)
    for g, base in units:
        y = _dot(dif_ref[base:base + rows, cols(g)], pw_ref[g]) * ps_ref[:, cols(g)]
        gate = gp_ref[base:base + rows, cols(g)].astype(F32)
        o_ref[base:base + rows, cols(g)] = (y * (gate * jax.nn.sigmoid(gate))).astype(BF16)


def _pool_mixer(z, batch, seq_len, pool_w, pool_scale):
    return pl.pallas_call(
        functools.partial(_pool_kernel, seq_len=seq_len),
        grid=(batch,),
        in_specs=[
            pl.BlockSpec((seq_len, POOL_WIDTH), lambda b: (b, 0)),
            pl.BlockSpec((seq_len, POOL_WIDTH), lambda b: (b, 1)),
            pl.BlockSpec((len(POOL_WINDOWS), POOL_GROUP_DIM, POOL_GROUP_DIM), lambda b: (0, 0, 0)),
            pl.BlockSpec((1, POOL_WIDTH), lambda b: (0, 0)),
        ],
        out_specs=pl.BlockSpec((seq_len, POOL_WIDTH), lambda b: (b, 0)),
        out_shape=jax.ShapeDtypeStruct((batch * seq_len, POOL_WIDTH), BF16),
        scratch_shapes=[pltpu.VMEM((seq_len + 2 * POOL_HALO, POOL_WIDTH), BF16),
                        pltpu.VMEM((len(POOL_WINDOWS), POOL_ROWS, POOL_ROWS + 2 * POOL_HALO), BF16),
                        pltpu.VMEM((seq_len, POOL_WIDTH), F32), pltpu.VMEM((seq_len, POOL_WIDTH), BF16)],
        compiler_params=_cparams(1),
        name="pool_mixer",
    )(z, z, pool_w, pool_scale)


def _block_row(p, block, row):
    c, w = p.shape
    p3 = p.reshape(c // block, block, w)
    return jnp.broadcast_to(p3[:, row:row + 1, :], p3.shape).reshape(c, w)


def _interleave(lo, hi, block):
    half = block // 2
    parts = []
    for b in range(SCAN_CHUNK // block):
        parts.append(lo[b * block:b * block + half])
        parts.append(hi[b * block + half:(b + 1) * block])
    return jnp.concatenate(parts, axis=0)


def _select_levels(products, rev):
    c = SCAN_CHUNK
    lane = lax.broadcasted_iota(jnp.int32, (8, c), 1)
    sub = lax.broadcasted_iota(jnp.int32, (8, c), 0)
    out = []
    for t0 in range(0, c, 8):
        if rev:
            keep = (lane >= t0 + sub) & (lane < t0 + 8)
        else:
            keep = (lane >= t0) & (lane <= t0 + sub)
        row = jnp.where(keep, products[0][t0:t0 + 8, :], 0.0)
        for lv, block in enumerate((16, 32, 64, 128), 1):
            half = block // 2
            start = t0 // block * block
            upper = t0 - start >= half
            if upper == rev:
                continue
            lo = start + half if rev else start
            row = jnp.where((lane >= lo) & (lane < lo + half), products[lv][t0:t0 + 8, :], row)
        out.append(row)
    return jnp.concatenate(out, axis=0)


def _rec_kernel(*refs, seq_len, heads, has_s0, out_state, n_aliased, layer):
    q_ref, ff_ref, fb_ref, v_ref, gr_ref, lb_ref, hn_ref = refs[:7]
    rest = list(refs[7:])
    s0_ref = rest.pop(0) if has_s0 else None
    del rest[:n_aliased]
    o_ref = rest.pop(0)
    st_ref = rest.pop(0) if out_state else None
    acc_ref, qs_ref, k_ref, p_ref, a_ref, qin_ref, u_ref, dec_ref, tri_ref = rest

    c = SCAN_CHUNK
    hd = REC_HEAD_DIM
    n_chunks = seq_len // c
    zf_refs = (ff_ref, fb_ref)
    units = [(h, n) for h in range(heads) for n in range(n_chunks)]

    def rows_in(n):
        return slice(n * c, (n + 1) * c)

    def rows_sc(h, n):
        return slice((h * n_chunks + n) * c, (h * n_chunks + n + 1) * c)

    def cols(h):
        return slice(h * hd, (h + 1) * hd)

    ti = lax.broadcasted_iota(jnp.int32, (c, c), 0)
    si = lax.broadcasted_iota(jnp.int32, (c, c), 1)
    for d, causal in enumerate((si <= ti, si >= ti)):
        tri_ref[d] = jnp.where(causal, 1.0, 0.0).astype(BF16)

    weights = (0.26, 0.43, 0.09, 0.22)
    done = [0.0]

    def progress(stage, share):
        done[0] += weights[stage] * share
        return done[0]

    for h, n in units:
        qz = q_ref[rows_in(n), cols(h)].astype(F32)
        qs_ref[rows_sc(h, n), :] = qz * jax.nn.sigmoid(qz)
        for d in range(2):
            lower = lb_ref[d, h]
            f = jnp.clip(lower + (1.0 - lower) * jax.nn.sigmoid(zf_refs[d][rows_in(n), cols(h)]), F_MIN, 1.0)
            k_ref[d, rows_sc(h, n), :] = 1.0 - f
            g = jnp.log(f) * LOG2_E
            g_hi = g.astype(BF16)
            rest = g - g_hi.astype(F32)
            g_mid = rest.astype(BF16)
            g_lo = (rest - g_mid.astype(F32)).astype(BF16)
            sums = _dot(tri_ref[d], jnp.concatenate([g_hi, g_mid, g_lo], axis=1))
            p_ref[d, rows_sc(h, n), :] = (sums[:, 0:hd] + sums[:, hd:2 * hd]) + sums[:, 2 * hd:3 * hd]
        yield progress(0, 1 / len(units))

    for h, n in units:
        rows = rows_sc(h, n)
        q = qs_ref[rows, :]
        v_t = v_ref[rows_in(n), cols(h)].astype(F32).T.astype(BF16)
        for d in range(2):
            rev = d == 1
            k = k_ref[d, rows, :]
            p = p_ref[d, rows, :]
            e0 = p - _block_row(p, 8, 4 if rev else 3)
            products = [_dot_nt((q * jnp.exp2(e0)).astype(BF16), (k * jnp.exp2(-e0)).astype(BF16))]
            for block in (16, 32, 64, 128):
                beta = _block_row(p, block, block // 2 if rev else block // 2 - 1)
                if rev:
                    e = _interleave(p, beta, block) - _interleave(beta, p, block)
                    src = _interleave(q, k, block)
                else:
                    e = _interleave(beta, p, block) - _interleave(p, beta, block)
                    src = _interleave(k, q, block)
                m = (src * jnp.exp2(e)).astype(BF16)
                products.append(_dot_nt(m, m))
            a_ref[rows, d * c:(d + 1) * c] = _select_levels(products, rev).astype(BF16)

            edge = p[0:1, :] if rev else p[c - 1:c, :]
            qin_ref[d, rows, :] = (q * jnp.exp2(p)).astype(BF16)
            u_ref[d, h * n_chunks + n] = _dot(v_t, (k * jnp.exp2(edge - p)).astype(BF16))
            dec_ref[d, h * n_chunks + n] = jnp.broadcast_to(jnp.exp2(edge), (8, hd))
        yield progress(1, 1 / len(units))

    for h, n in units:
        v_b = v_ref[rows_in(n), cols(h)]
        acc_ref[rows_sc(h, n), :] = _dot(a_ref[rows_sc(h, n), :], jnp.concatenate([v_b, v_b], axis=0))
        yield progress(2, 1 / len(units))

    for h in range(heads):
        if has_s0:
            states = [s0_ref[0, h].T, s0_ref[1, h].T]
        else:
            states = [jnp.zeros((hd, hd), F32)] * 2
        inter = [[None] * n_chunks, [None] * n_chunks]
        for i in range(n_chunks):
            for d, n in ((0, i), (1, n_chunks - 1 - i)):
                inter[d][n] = _dot_nt(qin_ref[d, rows_sc(h, n), :], states[d].astype(BF16))
                states[d] = dec_ref[d, h * n_chunks + n][0:1, :] * states[d] + u_ref[d, h * n_chunks + n]
        for n in range(n_chunks):
            gate = gr_ref[rows_in(n), cols(h)].astype(F32)
            o = acc_ref[rows_sc(h, n), :] + inter[0][n] + inter[1][n]
            o_ref[rows_in(n), cols(h)] = (_rms(o, hn_ref[...]) * (gate * jax.nn.sigmoid(gate))).astype(BF16)
        if out_state:
            st_ref[layer, 0, h] = states[0].T
            st_ref[layer, 1, h] = states[1].T
        yield progress(3, 1 / heads)
    if out_state:
        for other in range(st_ref.shape[0]):
            if other != layer:
                st_ref[other] = jnp.zeros(st_ref.shape[1:], F32)


def _rec_part(z, zf, batch, seq_len, lower, head_norm, j, s0=None, states=None, heads=None):
    hd = REC_HEAD_DIM
    nh = REC_HEADS
    n_chunks = seq_len // SCAN_CHUNK
    heads = heads or min(nh, max(1, SCAN_UNITS // n_chunks))
    width = heads * hd
    col0 = 2 * POOL_WIDTH // width
    groups = nh // heads

    def zspec(part):
        return pl.BlockSpec((seq_len, width), lambda i: (i // groups, col0 + part * groups + i % groups))

    def fspec(part):
        return pl.BlockSpec((seq_len, width), lambda i: (i // groups, part * groups + i % groups))

    in_specs = [zspec(0), fspec(0), fspec(1), zspec(1), zspec(2),
                pl.BlockSpec((2, heads, 1, hd), lambda i: (0, i % groups, 0, 0)),
                pl.BlockSpec((1, hd), lambda i: (0, 0))]
    args = [z, zf, zf, z, z, lower, head_norm]
    has_s0 = s0 is not None
    if has_s0:
        in_specs.append(pl.BlockSpec((None, None, 2, heads, hd, hd),
                                     lambda i: (i // groups, j, 0, i % groups, 0, 0)))
        args.append(s0)
    out_state = not has_s0
    out_specs = [pl.BlockSpec((seq_len, width), lambda i: (i // groups, i % groups))]
    out_shape = [jax.ShapeDtypeStruct((batch * seq_len, REC_WIDTH), BF16)]
    aliases = {}
    layer = j
    if out_state:
        out_shape.append(jax.ShapeDtypeStruct((batch, N_REC, 2, nh, hd, hd), F32))
        if states is None:
            out_specs.append(pl.BlockSpec((None, N_REC, 2, heads, hd, hd),
                                          lambda i: (i // groups, 0, 0, i % groups, 0, 0)))
        else:
            out_specs.append(pl.BlockSpec((None, 1, 2, heads, hd, hd),
                                          lambda i: (i // groups, j, 0, i % groups, 0, 0)))
            layer = 0
            aliases[len(args)] = 1
            in_specs.append(pl.BlockSpec(memory_space=pl.ANY))
            args.append(states)
    tokens = heads * seq_len
    kernel = functools.partial(_rec_kernel, seq_len=seq_len, heads=heads, has_s0=has_s0, out_state=out_state,
                               n_aliased=len(aliases), layer=layer)
    scratch_shapes = [pltpu.VMEM((tokens, hd), F32),
                      pltpu.VMEM((tokens, hd), F32),
                      pltpu.VMEM((2, tokens, hd), F32),
                      pltpu.VMEM((2, tokens, hd), F32),
                      pltpu.VMEM((tokens, 2 * SCAN_CHUNK), BF16),
                      pltpu.VMEM((2, tokens, hd), BF16),
                      pltpu.VMEM((2, tokens // SCAN_CHUNK, hd, hd), F32),
                      pltpu.VMEM((2, tokens // SCAN_CHUNK, 8, hd), F32),
                      pltpu.VMEM((2, SCAN_CHUNK, SCAN_CHUNK), BF16)]
    return _Part(kernel, batch * groups, in_specs, args, out_specs, out_shape, scratch_shapes, aliases)


def _rope(x, cos, sin_signed):
    lane = lax.broadcasted_iota(jnp.int32, x.shape, 1)
    partner = jnp.where((lane & 1) == 0, pltpu.roll(x, ATT_HEAD_DIM - 1, 1), pltpu.roll(x, 1, 1))
    return x * cos + partner * sin_signed


def _att_kernel(*refs, seq_len, n_keys, latent, n_aliased, layer):
    q_ref, k_ref, v_ref, qn_ref, kn_ref = refs[:5]
    g_refs = refs[5:5 + ATT_KV_HEADS]
    if latent:
        ck_ref, cv_ref, cos_ref, sin_ref, y_ref, kall, vall = refs[5 + ATT_KV_HEADS:-8]
    else:
        y_ref, ko_ref, vo_ref, kall, vall = refs[5 + ATT_KV_HEADS + n_aliased:-8]
    qs_bufs, s_bufs, m_bufs, o_bufs = refs[-8:-6], refs[-6:-4], refs[-4:-2], refs[-2:]
    hd = ATT_HEAD_DIM
    qw = ATT_REP * hd
    tq = ATT_Q_TILE
    tk = min(ATT_KEY_TILE, n_keys)
    n_kb = n_keys // tk
    n_past = n_keys - seq_len
    n_items = ATT_KV_HEADS * (seq_len // tq)
    exp2_scale = hd ** -0.5 * LOG2_E
    assert ATT_KV_HEADS == 2

    if not latent:
        for other in range(ko_ref.shape[0]):
            if other != layer:
                ko_ref[other] = jnp.zeros(ko_ref.shape[1:], F32)
                vo_ref[other] = jnp.zeros(vo_ref.shape[1:], F32)
    lane = lax.broadcasted_iota(jnp.int32, (n_keys, hd), 1)
    for g in range(ATT_KV_HEADS):
        k = _rms(k_ref[:, g * hd:(g + 1) * hd].astype(F32), kn_ref[...])
        v = v_ref[:, g * hd:(g + 1) * hd].astype(F32)
        if latent:
            k = _rope(k, cos_ref[...], sin_ref[...])
            kall[g, 0:n_past, :] = ck_ref[pl.ds(g, n_past, stride=ATT_KV_HEADS), :].astype(BF16)
            vall[g, 0:n_past, 0:hd] = cv_ref[pl.ds(g, n_past, stride=ATT_KV_HEADS), :].astype(BF16)
        else:
            ko_ref[layer, :, g, :] = k
            vo_ref[layer, :, g, :] = v
        kall[g, n_past:n_keys, :] = k.astype(BF16)
        vall[g, n_past:n_keys, 0:hd] = v.astype(BF16)
        vall[g, :, hd:2 * hd] = (lane == 0).astype(BF16)

    def tile_rows(item):
        return pl.ds(pl.multiple_of((item // ATT_KV_HEADS) * tq, tq), tq)

    def prep(item, g):
        rows = tile_rows(item)
        for h in range(ATT_REP):
            q = _rms(q_ref[rows, g * qw + h * hd:g * qw + (h + 1) * hd].astype(F32), qn_ref[...])
            if latent:
                q = _rope(q, cos_ref[rows, :], sin_ref[rows, :])
            qs_bufs[g][h * tq:(h + 1) * tq, :] = q.astype(BF16)

    def scores(item, g):
        qs = qs_bufs[g][...]
        mx = None
        for kb in range(n_kb):
            s = _dot_nt(qs, kall[g, kb * tk:(kb + 1) * tk, :])
            s_bufs[g][:, kb * tk:(kb + 1) * tk] = s
            for c in range(tk // hd):
                part = s[:, c * hd:(c + 1) * hd]
                mx = part if mx is None else jnp.maximum(mx, part)
        m = jnp.max(mx, axis=-1, keepdims=True) * exp2_scale
        m_bufs[g][...] = jnp.broadcast_to(m, (ATT_REP * tq, hd))

    def mix(item, g):
        m = m_bufs[g][...]
        ps = []
        for c in range(n_keys // hd):
            s = s_bufs[g][:, c * hd:(c + 1) * hd]
            ps.append(jnp.exp2(s * exp2_scale - m).astype(BF16))
        o_bufs[g][...] = _dot(jnp.concatenate(ps, axis=1), vall[g])

    def finish(item, g):
        rows = tile_rows(item)
        o = o_bufs[g][:, 0:hd] / o_bufs[g][:, hd:hd + 1]
        for h in range(ATT_REP):
            gate = g_refs[g][rows, h * hd:(h + 1) * hd].astype(F32)
            y = o[h * tq:(h + 1) * tq, :] * (gate * jax.nn.sigmoid(gate))
            y_ref[rows, g * qw + h * hd:g * qw + (h + 1) * hd] = y.astype(BF16)

    stages = (prep, scores, mix, finish)

    def step(k, parity):
        for a, stage in enumerate(stages):
            item = k - a
            if isinstance(item, int) and not 0 <= item < n_items:
                continue
            stage(item, (parity + a) % 2)

    depth = len(stages) - 1
    for k in range(min(depth, n_items + depth)):
        step(k, k % 2)
    n_steady = max(n_items - depth, 0)

    group = ATT_STEADY_STEPS

    def steady_group(j, carry):
        for u in range(group):
            step(depth + group * j + u, (depth + u) % 2)
        return carry

    lax.fori_loop(0, n_steady // group, steady_group, 0)
    for k in range(depth + group * (n_steady // group), n_items + depth):
        step(k, k % 2)


def _attention(z, batch, seq_len, q_norm, k_norm, j, cache=None, new_kv=None):
    hd = ATT_HEAD_DIM
    latent = cache is not None
    qw = ATT_REP * hd
    kv0 = ATT_WIDTH // KV_WIDTH
    gate0 = (ATT_WIDTH + 2 * KV_WIDTH) // qw
    assert (ATT_WIDTH + 2 * KV_WIDTH) % qw == 0 and ATT_WIDTH % KV_WIDTH == 0
    in_specs = [
        pl.BlockSpec((seq_len, ATT_WIDTH), lambda b: (b, 0)),
        pl.BlockSpec((seq_len, KV_WIDTH), lambda b: (b, kv0)),
        pl.BlockSpec((seq_len, KV_WIDTH), lambda b: (b, kv0 + 1)),
        pl.BlockSpec((1, hd), lambda b: (0, 0)),
        pl.BlockSpec((1, hd), lambda b: (0, 0)),
    ] + [pl.BlockSpec((seq_len, qw), lambda b, g=g: (b, gate0 + g)) for g in range(ATT_KV_HEADS)]
    args = [z, z, z, q_norm, k_norm] + [z] * ATT_KV_HEADS
    out_specs = [pl.BlockSpec((seq_len, ATT_WIDTH), lambda b: (b, 0))]
    out_shape = [jax.ShapeDtypeStruct((batch * seq_len, ATT_WIDTH), BF16)]
    n_keys = seq_len
    aliases = {}
    layer = j
    if latent:
        cache_k, cache_v, cos, sin_signed = cache
        past = cache_k.shape[2]
        n_keys += past
        cspec = pl.BlockSpec((None, None, past * ATT_KV_HEADS, hd), lambda b: (b, j, 0, 0))
        tspec = pl.BlockSpec((seq_len, hd), lambda b: (0, 0))
        in_specs += [cspec, cspec, tspec, tspec]
        args += [cache_k.reshape(batch, N_ATT, past * ATT_KV_HEADS, hd),
                 cache_v.reshape(batch, N_ATT, past * ATT_KV_HEADS, hd), cos, sin_signed]
    else:
        out_shape += [jax.ShapeDtypeStruct((batch, N_ATT, seq_len, ATT_KV_HEADS, hd), F32)] * 2
        if new_kv is None:
            kv_spec = pl.BlockSpec((None, N_ATT, seq_len, ATT_KV_HEADS, hd), lambda b: (b, 0, 0, 0, 0))
        else:
            kv_spec = pl.BlockSpec((None, 1, seq_len, ATT_KV_HEADS, hd), lambda b: (b, j, 0, 0, 0))
            layer = 0
            aliases = {len(args): 1, len(args) + 1: 2}
            in_specs += [pl.BlockSpec(memory_space=pl.ANY)] * 2
            args += list(new_kv)
        out_specs += [kv_spec, kv_spec]
    rows = ATT_REP * ATT_Q_TILE
    return pl.pallas_call(
        functools.partial(_att_kernel, seq_len=seq_len, n_keys=n_keys, latent=latent, n_aliased=len(aliases),
                          layer=layer),
        grid=(batch,),
        in_specs=in_specs,
        out_specs=out_specs,
        out_shape=out_shape,
        input_output_aliases=aliases,
        scratch_shapes=[pltpu.VMEM((ATT_KV_HEADS, n_keys, hd), BF16),
                        pltpu.VMEM((ATT_KV_HEADS, n_keys, 2 * hd), BF16),
                        pltpu.VMEM((rows, hd), BF16), pltpu.VMEM((rows, hd), BF16),
                        pltpu.VMEM((rows, n_keys), F32), pltpu.VMEM((rows, n_keys), F32),
                        pltpu.VMEM((rows, hd), F32), pltpu.VMEM((rows, hd), F32),
                        pltpu.VMEM((rows, 2 * hd), F32), pltpu.VMEM((rows, 2 * hd), F32)],
        compiler_params=_cparams(1),
        name="attention",
    )(*args)


def _rope_tables(n_tokens):
    t = jnp.arange(n_tokens)
    row = (t // GRID_W).astype(F32)
    col = (t % GRID_W).astype(F32)
    inv = ROPE_THETA ** (-jnp.arange(0, AXIS_DIM, 2, dtype=F32) / AXIS_DIM)
    ang = jnp.concatenate([row[:, None] * inv[None, :], col[:, None] * inv[None, :]], axis=-1)
    cos = jnp.repeat(jnp.cos(ang), 2, axis=-1)
    sin = jnp.repeat(jnp.sin(ang), 2, axis=-1)
    sign = jnp.where(jnp.arange(ATT_HEAD_DIM) % 2 == 0, -1.0, 1.0).astype(F32)
    return cos, sin * sign


def kernel(x_prompt, x_sample, c, state_hgrn, cache_k, cache_v, c_ctx, ada_w, ada_b, norm_pre, norm_post, rec_w_in, rec_lb_logits, rec_head_norm, pool_w, pool_scale, rec_w_out, att_w_in, att_q_norm, att_k_norm, att_w_out):
    nb_c, len_c, _ = x_prompt.shape
    nb_l, len_l, _ = x_sample.shape

    lb_p = jax.nn.softmax(rec_lb_logits.astype(F32), axis=0)
    lower_bounds = jnp.clip(jnp.cumsum(lb_p, axis=0) - lb_p[0], 0.0, 1.0)
    lower_bounds = lower_bounds.reshape(N_REC, 2, REC_HEADS, 1, REC_HEAD_DIM)
    cos, sin_signed = _rope_tables(len_l)

    cvec = jnp.zeros((MOD_ROWS, D_MODEL), F32).at[0].set(c_ctx).at[1:1 + nb_l].set(c)
    mod = _modulation(cvec, ada_w, ada_b)

    forget_cols = (2 * POOL_WIDTH + REC_WIDTH, 2 * POOL_WIDTH + 3 * REC_WIDTH)
    layers = []
    for i in range(DEPTH):
        j = i // 2
        rec = i % 2 == 0
        w_in, w_out = (rec_w_in, rec_w_out) if rec else (att_w_in, att_w_out)
        layers.append(dict(
            w_in=w_in[j].astype(BF16), w_out=w_out[j].astype(BF16), f32_cols=forget_cols if rec else (0, 0),
            gain_pre=norm_pre[i].reshape(1, D_MODEL), gain_post=norm_post[i].reshape(1, D_MODEL)))

    def pre_args(i):
        return (layers[i]["w_in"], i, layers[i]["gain_pre"], layers[i]["f32_cols"])

    xs = [x_prompt.reshape(nb_c * len_c, D_MODEL), x_sample.reshape(nb_l * len_l, D_MODEL)]
    streams = ((nb_c, len_c, 0), (nb_l, len_l, 1))
    zs = [None, None]
    ys = [None, None]
    new_states, new_kv = None, None

    def proj_part(s, i, steps=None):
        nb, sl, row0 = streams[s]
        post = ys[s] + (layers[i - 1]["w_out"], i - 1, layers[i - 1]["gain_post"]) if i > 0 else None
        return _proj_part(xs[s], mod, sl, row0, post=post, pre=pre_args(i) if i < DEPTH else None, steps=steps)

    def take_proj(s, i, outs):
        if i > 0:
            xs[s] = outs.pop(0)
        zs[s] = outs

    def rec_part(s, i, heads=None):
        nb, sl, _ = streams[s]
        j = i // 2
        hn = rec_head_norm[j].reshape(1, REC_HEAD_DIM)
        z, zf = zs[s]
        if s == 0:
            return _rec_part(z, zf, nb, sl, lower_bounds[j], hn, j, states=new_states, heads=heads)
        return _rec_part(z, zf, nb, sl, lower_bounds[j], hn, j, s0=state_hgrn, heads=heads)

    def pool(s, i):
        nb, sl, _ = streams[s]
        j = i // 2
        return _pool_mixer(zs[s][0], nb, sl, pool_w[j].astype(BF16), pool_scale[j].reshape(1, POOL_WIDTH))

    take_proj(0, 0, _run(proj_part(0, 0), name="proj")[0])
    for i in range(DEPTH):
        j = i // 2
        if i % 2 == 0:
            rec_c = rec_part(0, i)
            (y_rec, new_states), outs = _run(rec_c, proj_part(1, i, steps=rec_c.steps), name="rec_proj")
            ys[0] = (pool(0, i), 0, y_rec, 0)
            take_proj(1, i, outs)
            rec_l = rec_part(1, i, heads=REC_HEADS * streams[1][0] // rec_c.steps)
            (y_rec,), outs = _run(rec_l, proj_part(0, i + 1, steps=rec_l.steps), name="rec_proj")
            ys[1] = (pool(1, i), 0, y_rec, 0)
            take_proj(0, i + 1, outs)
        else:
            qn = att_q_norm[j].reshape(1, ATT_HEAD_DIM)
            kn = att_k_norm[j].reshape(1, ATT_HEAD_DIM)
            y, *new_kv = _attention(zs[0][0], nb_c, len_c, qn, kn, j, new_kv=new_kv)
            ys[0] = (y, 0, y, 1)
            take_proj(1, i, _run(proj_part(1, i), name="proj")[0])
            (y,) = _attention(zs[1][0], nb_l, len_l, qn, kn, j, cache=(cache_k, cache_v, cos, sin_signed))
            ys[1] = (y, 0, y, 1)
            take_proj(0, i + 1, _run(proj_part(0, i + 1), name="proj")[0])
    take_proj(1, DEPTH, _run(proj_part(1, DEPTH), name="proj")[0])
    xc, xl = xs

    return (xc.reshape(nb_c, len_c, D_MODEL), xl.reshape(nb_l, len_l, D_MODEL),
            new_states, new_kv[0], new_kv[1])
```

```python
import functools
from typing import Any, NamedTuple

import jax
import jax.numpy as jnp
from jax import lax
from jax.experimental import pallas as pl
from jax.experimental.pallas import tpu as pltpu

D_MODEL = 1024
DEPTH = 4
GRID_W = 64
N_REC = (DEPTH + 1) // 2
N_ATT = DEPTH // 2
POOL_WIDTH = D_MODEL // 2
POOL_WINDOWS = (2, 4, 8, 16)
POOL_GROUP_DIM = POOL_WIDTH // len(POOL_WINDOWS)
REC_WIDTH = D_MODEL // 2
REC_HEAD_DIM = 128
REC_HEADS = REC_WIDTH // REC_HEAD_DIM
REC_IN_WIDTH = 2 * POOL_WIDTH + 5 * REC_WIDTH
ATT_HEAD_DIM = 128
ATT_HEADS = D_MODEL // ATT_HEAD_DIM
ATT_KV_HEADS = 2
ATT_REP = ATT_HEADS // ATT_KV_HEADS
ATT_WIDTH = ATT_HEADS * ATT_HEAD_DIM
KV_WIDTH = ATT_KV_HEADS * ATT_HEAD_DIM
ATT_IN_WIDTH = 2 * ATT_WIDTH + 2 * KV_WIDTH
AXIS_DIM = ATT_HEAD_DIM // 2
ROPE_THETA = 10000.0
EPS = 1e-6
F_MIN = 1e-6

MOD_ROWS = 16
TOKEN_TILE = 1024
MIN_PROJ_STEPS = 8
PROJ_GROUP_ROWS = 512
POOL_ROWS = 256
IN_PROJ_STEP = 512
MOD_COLS = 1536
SCAN_CHUNK = 128
SCAN_UNITS = 8
POOL_HALO = 128
ATT_Q_TILE = 128
ATT_KEY_TILE = 128
ATT_STEADY_STEPS = 2
LOG2_E = 1.4426950408889634
VMEM_LIMIT = 56 * 1024 * 1024

F32 = jnp.float32
BF16 = jnp.bfloat16


def _cparams(n_axes):
    return pltpu.CompilerParams(
        dimension_semantics=("arbitrary",) * n_axes, vmem_limit_bytes=VMEM_LIMIT)


def _rms(x, g):
    return x * lax.rsqrt(jnp.mean(x * x, axis=-1, keepdims=True) + EPS) * g


def _dot(a, b):
    return jnp.dot(a, b, preferred_element_type=F32)


def _dot_nt(a, b):
    return lax.dot_general(a, b, (((1,), (1,)), ((), ())), preferred_element_type=F32)


class _Part(NamedTuple):
    kernel: Any
    steps: int
    in_specs: list
    args: list
    out_specs: list
    out_shape: list
    scratch_shapes: list
    aliases: dict


def _run(*parts, name):
    steps = parts[0].steps
    assert all(p.steps == steps for p in parts)
    n_in = [len(p.args) for p in parts]
    n_out = [len(p.out_shape) for p in parts]
    n_scr = [len(p.scratch_shapes) for p in parts]

    def body(*refs):
        ins, outs, scr = refs[:sum(n_in)], refs[sum(n_in):sum(n_in) + sum(n_out)], refs[sum(n_in) + sum(n_out):]
        pending = {}
        for k, p in enumerate(parts):
            i0, o0, s0 = sum(n_in[:k]), sum(n_out[:k]), sum(n_scr[:k])
            pending[k] = (0.0, p.kernel(*ins[i0:i0 + n_in[k]], *outs[o0:o0 + n_out[k]], *scr[s0:s0 + n_scr[k]]))
        while pending:
            k = min(pending, key=lambda k: pending[k][0])
            try:
                pending[k] = (next(pending[k][1]), pending[k][1])
            except StopIteration:
                del pending[k]

    aliases = {}
    for k, p in enumerate(parts):
        for i, o in p.aliases.items():
            aliases[sum(n_in[:k]) + i] = sum(n_out[:k]) + o
    outs = pl.pallas_call(
        body,
        grid=(steps,),
        in_specs=[s for p in parts for s in p.in_specs],
        out_specs=[s for p in parts for s in p.out_specs],
        out_shape=[s for p in parts for s in p.out_shape],
        scratch_shapes=[s for p in parts for s in p.scratch_shapes],
        input_output_aliases=aliases,
        compiler_params=_cparams(1),
        name=name,
    )(*[a for p in parts for a in p.args])
    return [list(outs[sum(n_out[:k]):sum(n_out[:k]) + n_out[k]]) for k in range(len(parts))]


def _mod_kernel(cv_ref, w_ref, b_ref, o_ref):
    cv = cv_ref[...]
    a = (cv * jax.nn.sigmoid(cv)).astype(BF16)
    o_ref[...] = _dot(a, w_ref[...].astype(BF16)) + b_ref[...]


def _modulation(cvec, ada_w, ada_b):
    tn = MOD_COLS
    out = pl.pallas_call(
        _mod_kernel,
        grid=(DEPTH, 3 * D_MODEL // tn),
        in_specs=[
            pl.BlockSpec((MOD_ROWS, D_MODEL), lambda l, n: (0, 0)),
            pl.BlockSpec((None, D_MODEL, tn), lambda l, n: (l, 0, n)),
            pl.BlockSpec((None, 1, tn), lambda l, n: (l, 0, n)),
        ],
        out_specs=pl.BlockSpec((None, MOD_ROWS, tn), lambda l, n: (l, 0, n)),
        out_shape=jax.ShapeDtypeStruct((DEPTH, MOD_ROWS, 3 * D_MODEL), F32),
        compiler_params=_cparams(2),
        name="modulation",
    )(cvec, ada_w, ada_b.reshape(DEPTH, 1, 3 * D_MODEL))
    return out.reshape(DEPTH, MOD_ROWS, 3, 1, D_MODEL)


def _mod_spec(layer, seq_len, row0, tile):
    assert row0 == 0 or seq_len % tile == 0
    tiles_per_seq = seq_len // tile
    if row0 == 0:
        index = lambda i: (layer, 0, 0, 0, 0)
    else:
        index = lambda i: (layer, row0 + i // tiles_per_seq, 0, 0, 0)
    return pl.BlockSpec((None, None, 3, 1, D_MODEL), index)


def _proj_kernel(*refs, post, pre, n_out, f32_cols):
    refs = list(refs)
    x_ref = refs.pop(0)
    if post:
        ya_ref, yb_ref, wo_ref, modp_ref, gpost_ref = (refs.pop(0) for _ in range(5))
    if pre:
        modn_ref, gpre_ref, wi_ref = (refs.pop(0) for _ in range(3))
    xo_ref = refs.pop(0) if post else None
    rows_all = x_ref.shape[0]
    n_groups = max(1, rows_all // PROJ_GROUP_ROWS)

    def group(rows):
        x = x_ref[rows, :]
        if post:
            half = D_MODEL // 2
            p = _dot(ya_ref[rows, :], wo_ref[0:half, :]) + _dot(yb_ref[rows, :], wo_ref[half:D_MODEL, :])
            x = x + modp_ref[2] * _rms(p, gpost_ref[...])
            xo_ref[rows, :] = x
            yield
        if pre:
            h = (_rms(x, gpre_ref[...]) * (1.0 + modn_ref[1]) + modn_ref[0]).astype(BF16)
            lo, hi = f32_cols
            for c0 in range(0, n_out, IN_PROJ_STEP):
                y = _dot(h, wi_ref[:, c0:c0 + IN_PROJ_STEP])
                if lo <= c0 < hi:
                    refs[1][rows, c0 - lo:c0 - lo + IN_PROJ_STEP] = y
                else:
                    c1 = c0 if c0 < lo else c0 - (hi - lo)
                    refs[0][rows, c1:c1 + IN_PROJ_STEP] = y.astype(BF16)
                yield

    size = rows_all // n_groups
    waiting = [group(slice(k * size, (k + 1) * size)) for k in range(n_groups)]
    running = []
    pieces = n_groups * (post + n_out // IN_PROJ_STEP)
    done = 0
    while waiting or running:
        if waiting:
            running.append(waiting.pop(0))
        for g in list(running):
            try:
                next(g)
                done += 1
            except StopIteration:
                running.remove(g)
        yield min(done / pieces, 1.0)


def _proj_part(x, mod, seq_len, row0, post=None, pre=None, steps=None):
    t = x.shape[0]
    tile = t // steps if steps else min(TOKEN_TILE, t // MIN_PROJ_STEPS)
    half = D_MODEL // 2
    row = lambda i: (i, 0)
    fixed = lambda i: (0, 0)
    resident = dict(index_map=fixed, pipeline_mode=pl.Buffered(1))
    in_specs = [pl.BlockSpec((tile, D_MODEL), row)]
    args = [x]
    out_specs, out_shape = [], []
    n_out, f32_cols = 0, (0, 0)
    if post:
        ya, ia, yb, ib, w_out, layer, gain = post
        in_specs += [pl.BlockSpec((tile, half), lambda i: (i, ia)),
                     pl.BlockSpec((tile, half), lambda i: (i, ib)),
                     pl.BlockSpec((D_MODEL, D_MODEL), **resident),
                     _mod_spec(layer, seq_len, row0, tile),
                     pl.BlockSpec((1, D_MODEL), fixed)]
        args += [ya, yb, w_out, mod, gain]
        out_specs.append(pl.BlockSpec((tile, D_MODEL), row))
        out_shape.append(jax.ShapeDtypeStruct((t, D_MODEL), F32))
    if pre:
        w_in, layer, gain, f32_cols = pre
        n_out = w_in.shape[1]
        lo, hi = f32_cols
        assert lo % IN_PROJ_STEP == 0 and hi % IN_PROJ_STEP == 0 and n_out % IN_PROJ_STEP == 0
        in_specs += [_mod_spec(layer, seq_len, row0, tile),
                     pl.BlockSpec((1, D_MODEL), fixed),
                     pl.BlockSpec((D_MODEL, n_out), **resident)]
        args += [mod, gain, w_in]
        out_specs.append(pl.BlockSpec((tile, n_out - (hi - lo)), row))
        out_shape.append(jax.ShapeDtypeStruct((t, n_out - (hi - lo)), BF16))
        if hi > lo:
            out_specs.append(pl.BlockSpec((tile, hi - lo), row))
            out_shape.append(jax.ShapeDtypeStruct((t, hi - lo), F32))
    kernel = functools.partial(_proj_kernel, post=bool(post), pre=bool(pre), n_out=n_out, f32_cols=f32_cols)
    return _Part(kernel, t // tile, in_specs, args, out_specs, out_shape, [], {})


def _pool_kernel(u_ref, gp_ref, pw_ref, ps_ref, o_ref, pad_ref, band_ref, sum_ref, dif_ref, *, seq_len):
    gd = POOL_GROUP_DIM
    rows = POOL_ROWS
    halo = POOL_HALO
    span = rows + 2 * halo

    @pl.when(pl.program_id(0) == 0)
    def _():
        r = lax.broadcasted_iota(jnp.int32, (rows, span), 0)
        c = lax.broadcasted_iota(jnp.int32, (rows, span), 1)
        offset = c - halo - r
        for g, win in enumerate(POOL_WINDOWS):
            band_ref[g] = jnp.where((offset >= -(win // 2)) & (offset < win // 2), 1.0, 0.0).astype(BF16)

    zeros = jnp.zeros((halo, POOL_WIDTH), BF16)
    pad_ref[0:halo, :] = zeros
    pad_ref[halo + seq_len:2 * halo + seq_len, :] = zeros
    pad_ref[halo:halo + seq_len, :] = u_ref[...]
    units = [(g, r * rows) for g in range(len(POOL_WINDOWS)) for r in range(seq_len // rows)]

    def cols(g):
        return slice(g * gd, (g + 1) * gd)

    for g, base in units:
        sum_ref[base:base + rows, cols(g)] = _dot(band_ref[g], pad_ref[base:base + span, cols(g)])
    for g, base in units:
        win = POOL_WINDOWS[g]
        t = base + lax.broadcasted_iota(jnp.int32, (rows, 1), 0)
        count = jnp.clip(t + win // 2, 0, seq_len) - jnp.clip(t - win // 2, 0, seq_len)
        mean = sum_ref[base:base + rows, cols(g)] / count.astype(F32)
        d = mean - pad_ref[halo + base:halo + base + rows, cols(g)].astype(F32)
        dif_ref[base:base + rows, cols(g)] = d.astype(BF16)
    for g, base in units:
        y = _dot(dif_ref[base:base + rows, cols(g)], pw_ref[g]) * ps_ref[:, cols(g)]
        gate = gp_ref[base:base + rows, cols(g)].astype(F32)
        o_ref[base:base + rows, cols(g)] = (y * (gate * jax.nn.sigmoid(gate))).astype(BF16)


def _pool_mixer(z, batch, seq_len, pool_w, pool_scale):
    return pl.pallas_call(
        functools.partial(_pool_kernel, seq_len=seq_len),
        grid=(batch,),
        in_specs=[
            pl.BlockSpec((seq_len, POOL_WIDTH), lambda b: (b, 0)),
            pl.BlockSpec((seq_len, POOL_WIDTH), lambda b: (b, 1)),
            pl.BlockSpec((len(POOL_WINDOWS), POOL_GROUP_DIM, POOL_GROUP_DIM), lambda b: (0, 0, 0)),
            pl.BlockSpec((1, POOL_WIDTH), lambda b: (0, 0)),
        ],
        out_specs=pl.BlockSpec((seq_len, POOL_WIDTH), lambda b: (b, 0)),
        out_shape=jax.ShapeDtypeStruct((batch * seq_len, POOL_WIDTH), BF16),
        scratch_shapes=[pltpu.VMEM((seq_len + 2 * POOL_HALO, POOL_WIDTH), BF16),
                        pltpu.VMEM((len(POOL_WINDOWS), POOL_ROWS, POOL_ROWS + 2 * POOL_HALO), BF16),
                        pltpu.VMEM((seq_len, POOL_WIDTH), F32), pltpu.VMEM((seq_len, POOL_WIDTH), BF16)],
        compiler_params=_cparams(1),
        name="pool_mixer",
    )(z, z, pool_w, pool_scale)


def _block_row(p, block, row):
    c, w = p.shape
    p3 = p.reshape(c // block, block, w)
    return jnp.broadcast_to(p3[:, row:row + 1, :], p3.shape).reshape(c, w)


def _interleave(lo, hi, block):
    half = block // 2
    parts = []
    for b in range(SCAN_CHUNK // block):
        parts.append(lo[b * block:b * block + half])
        parts.append(hi[b * block + half:(b + 1) * block])
    return jnp.concatenate(parts, axis=0)


def _select_levels(products, rev):
    c = SCAN_CHUNK
    lane = lax.broadcasted_iota(jnp.int32, (8, c), 1)
    sub = lax.broadcasted_iota(jnp.int32, (8, c), 0)
    out = []
    for t0 in range(0, c, 8):
        if rev:
            keep = (lane >= t0 + sub) & (lane < t0 + 8)
        else:
            keep = (lane >= t0) & (lane <= t0 + sub)
        row = jnp.where(keep, products[0][t0:t0 + 8, :], 0.0)
        for lv, block in enumerate((16, 32, 64, 128), 1):
            half = block // 2
            start = t0 // block * block
            upper = t0 - start >= half
            if upper == rev:
                continue
            lo = start + half if rev else start
            row = jnp.where((lane >= lo) & (lane < lo + half), products[lv][t0:t0 + 8, :], row)
        out.append(row)
    return jnp.concatenate(out, axis=0)


def _rec_kernel(*refs, seq_len, heads, has_s0, out_state, n_aliased, layer):
    q_ref, ff_ref, fb_ref, v_ref, gr_ref, lb_ref, hn_ref = refs[:7]
    rest = list(refs[7:])
    s0_ref = rest.pop(0) if has_s0 else None
    del rest[:n_aliased]
    o_ref = rest.pop(0)
    st_ref = rest.pop(0) if out_state else None
    acc_ref, qs_ref, k_ref, p_ref, a_ref, qin_ref, u_ref, dec_ref, tri_ref = rest

    c = SCAN_CHUNK
    hd = REC_HEAD_DIM
    n_chunks = seq_len // c
    zf_refs = (ff_ref, fb_ref)
    units = [(h, n) for h in range(heads) for n in range(n_chunks)]

    def rows_in(n):
        return slice(n * c, (n + 1) * c)

    def rows_sc(h, n):
        return slice((h * n_chunks + n) * c, (h * n_chunks + n + 1) * c)

    def cols(h):
        return slice(h * hd, (h + 1) * hd)

    ti = lax.broadcasted_iota(jnp.int32, (c, c), 0)
    si = lax.broadcasted_iota(jnp.int32, (c, c), 1)
    for d, causal in enumerate((si <= ti, si >= ti)):
        tri_ref[d] = jnp.where(causal, 1.0, 0.0).astype(BF16)

    weights = (0.26, 0.43, 0.09, 0.22)
    done = [0.0]

    def progress(stage, share):
        done[0] += weights[stage] * share
        return done[0]

    for h, n in units:
        qz = q_ref[rows_in(n), cols(h)].astype(F32)
        qs_ref[rows_sc(h, n), :] = qz * jax.nn.sigmoid(qz)
        for d in range(2):
            lower = lb_ref[d, h]
            f = jnp.clip(lower + (1.0 - lower) * jax.nn.sigmoid(zf_refs[d][rows_in(n), cols(h)]), F_MIN, 1.0)
            k_ref[d, rows_sc(h, n), :] = 1.0 - f
            g = jnp.log(f) * LOG2_E
            g_hi = g.astype(BF16)
            rest = g - g_hi.astype(F32)
            g_mid = rest.astype(BF16)
            g_lo = (rest - g_mid.astype(F32)).astype(BF16)
            sums = _dot(tri_ref[d], jnp.concatenate([g_hi, g_mid, g_lo], axis=1))
            p_ref[d, rows_sc(h, n), :] = (sums[:, 0:hd] + sums[:, hd:2 * hd]) + sums[:, 2 * hd:3 * hd]
        yield progress(0, 1 / len(units))

    for h, n in units:
        rows = rows_sc(h, n)
        q = qs_ref[rows, :]
        v_t = v_ref[rows_in(n), cols(h)].astype(F32).T.astype(BF16)
        for d in range(2):
            rev = d == 1
            k = k_ref[d, rows, :]
            p = p_ref[d, rows, :]
            e0 = p - _block_row(p, 8, 4 if rev else 3)
            products = [_dot_nt((q * jnp.exp2(e0)).astype(BF16), (k * jnp.exp2(-e0)).astype(BF16))]
            for block in (16, 32, 64, 128):
                beta = _block_row(p, block, block // 2 if rev else block // 2 - 1)
                if rev:
                    e = _interleave(p, beta, block) - _interleave(beta, p, block)
                    src = _interleave(q, k, block)
                else:
                    e = _interleave(beta, p, block) - _interleave(p, beta, block)
                    src = _interleave(k, q, block)
                m = (src * jnp.exp2(e)).astype(BF16)
                products.append(_dot_nt(m, m))
            a_ref[rows, d * c:(d + 1) * c] = _select_levels(products, rev).astype(BF16)

            edge = p[0:1, :] if rev else p[c - 1:c, :]
            qin_ref[d, rows, :] = (q * jnp.exp2(p)).astype(BF16)
            u_ref[d, h * n_chunks + n] = _dot(v_t, (k * jnp.exp2(edge - p)).astype(BF16))
            dec_ref[d, h * n_chunks + n] = jnp.broadcast_to(jnp.exp2(edge), (8, hd))
        yield progress(1, 1 / len(units))

    for h, n in units:
        v_b = v_ref[rows_in(n), cols(h)]
        acc_ref[rows_sc(h, n), :] = _dot(a_ref[rows_sc(h, n), :], jnp.concatenate([v_b, v_b], axis=0))
        yield progress(2, 1 / len(units))

    for h in range(heads):
        if has_s0:
            states = [s0_ref[0, h].T, s0_ref[1, h].T]
        else:
            states = [jnp.zeros((hd, hd), F32)] * 2
        inter = [[None] * n_chunks, [None] * n_chunks]
        for i in range(n_chunks):
            for d, n in ((0, i), (1, n_chunks - 1 - i)):
                inter[d][n] = _dot_nt(qin_ref[d, rows_sc(h, n), :], states[d].astype(BF16))
                states[d] = dec_ref[d, h * n_chunks + n][0:1, :] * states[d] + u_ref[d, h * n_chunks + n]
        for n in range(n_chunks):
            gate = gr_ref[rows_in(n), cols(h)].astype(F32)
            o = acc_ref[rows_sc(h, n), :] + inter[0][n] + inter[1][n]
            o_ref[rows_in(n), cols(h)] = (_rms(o, hn_ref[...]) * (gate * jax.nn.sigmoid(gate))).astype(BF16)
        if out_state:
            st_ref[layer, 0, h] = states[0].T
            st_ref[layer, 1, h] = states[1].T
        yield progress(3, 1 / heads)
    if out_state:
        for other in range(st_ref.shape[0]):
            if other != layer:
                st_ref[other] = jnp.zeros(st_ref.shape[1:], F32)


def _rec_part(z, zf, batch, seq_len, lower, head_norm, j, s0=None, states=None, heads=None):
    hd = REC_HEAD_DIM
    nh = REC_HEADS
    n_chunks = seq_len // SCAN_CHUNK
    heads = heads or min(nh, max(1, SCAN_UNITS // n_chunks))
    width = heads * hd
    col0 = 2 * POOL_WIDTH // width
    groups = nh // heads

    def zspec(part):
        return pl.BlockSpec((seq_len, width), lambda i: (i // groups, col0 + part * groups + i % groups))

    def fspec(part):
        return pl.BlockSpec((seq_len, width), lambda i: (i // groups, part * groups + i % groups))

    in_specs = [zspec(0), fspec(0), fspec(1), zspec(1), zspec(2),
                pl.BlockSpec((2, heads, 1, hd), lambda i: (0, i % groups, 0, 0)),
                pl.BlockSpec((1, hd), lambda i: (0, 0))]
    args = [z, zf, zf, z, z, lower, head_norm]
    has_s0 = s0 is not None
    if has_s0:
        in_specs.append(pl.BlockSpec((None, None, 2, heads, hd, hd),
                                     lambda i: (i // groups, j, 0, i % groups, 0, 0)))
        args.append(s0)
    out_state = not has_s0
    out_specs = [pl.BlockSpec((seq_len, width), lambda i: (i // groups, i % groups))]
    out_shape = [jax.ShapeDtypeStruct((batch * seq_len, REC_WIDTH), BF16)]
    aliases = {}
    layer = j
    if out_state:
        out_shape.append(jax.ShapeDtypeStruct((batch, N_REC, 2, nh, hd, hd), F32))
        if states is None:
            out_specs.append(pl.BlockSpec((None, N_REC, 2, heads, hd, hd),
                                          lambda i: (i // groups, 0, 0, i % groups, 0, 0)))
        else:
            out_specs.append(pl.BlockSpec((None, 1, 2, heads, hd, hd),
                                          lambda i: (i // groups, j, 0, i % groups, 0, 0)))
            layer = 0
            aliases[len(args)] = 1
            in_specs.append(pl.BlockSpec(memory_space=pl.ANY))
            args.append(states)
    tokens = heads * seq_len
    kernel = functools.partial(_rec_kernel, seq_len=seq_len, heads=heads, has_s0=has_s0, out_state=out_state,
                               n_aliased=len(aliases), layer=layer)
    scratch_shapes = [pltpu.VMEM((tokens, hd), F32),
                      pltpu.VMEM((tokens, hd), F32),
                      pltpu.VMEM((2, tokens, hd), F32),
                      pltpu.VMEM((2, tokens, hd), F32),
                      pltpu.VMEM((tokens, 2 * SCAN_CHUNK), BF16),
                      pltpu.VMEM((2, tokens, hd), BF16),
                      pltpu.VMEM((2, tokens // SCAN_CHUNK, hd, hd), F32),
                      pltpu.VMEM((2, tokens // SCAN_CHUNK, 8, hd), F32),
                      pltpu.VMEM((2, SCAN_CHUNK, SCAN_CHUNK), BF16)]
    return _Part(kernel, batch * groups, in_specs, args, out_specs, out_shape, scratch_shapes, aliases)


def _rope(x, cos, sin_signed):
    lane = lax.broadcasted_iota(jnp.int32, x.shape, 1)
    partner = jnp.where((lane & 1) == 0, pltpu.roll(x, ATT_HEAD_DIM - 1, 1), pltpu.roll(x, 1, 1))
    return x * cos + partner * sin_signed


def _att_kernel(*refs, seq_len, n_keys, latent, n_aliased, layer):
    q_ref, k_ref, v_ref, qn_ref, kn_ref = refs[:5]
    g_refs = refs[5:5 + ATT_KV_HEADS]
    if latent:
        ck_ref, cv_ref, cos_ref, sin_ref, y_ref, kall, vall = refs[5 + ATT_KV_HEADS:-8]
    else:
        y_ref, ko_ref, vo_ref, kall, vall = refs[5 + ATT_KV_HEADS + n_aliased:-8]
    qs_bufs, s_bufs, m_bufs, o_bufs = refs[-8:-6], refs[-6:-4], refs[-4:-2], refs[-2:]
    hd = ATT_HEAD_DIM
    qw = ATT_REP * hd
    tq = ATT_Q_TILE
    tk = min(ATT_KEY_TILE, n_keys)
    n_kb = n_keys // tk
    n_past = n_keys - seq_len
    n_items = ATT_KV_HEADS * (seq_len // tq)
    exp2_scale = hd ** -0.5 * LOG2_E
    assert ATT_KV_HEADS == 2

    if not latent:
        for other in range(ko_ref.shape[0]):
            if other != layer:
                ko_ref[other] = jnp.zeros(ko_ref.shape[1:], F32)
                vo_ref[other] = jnp.zeros(vo_ref.shape[1:], F32)
    lane = lax.broadcasted_iota(jnp.int32, (n_keys, hd), 1)
    for g in range(ATT_KV_HEADS):
        k = _rms(k_ref[:, g * hd:(g + 1) * hd].astype(F32), kn_ref[...])
        v = v_ref[:, g * hd:(g + 1) * hd].astype(F32)
        if latent:
            k = _rope(k, cos_ref[...], sin_ref[...])
            kall[g, 0:n_past, :] = ck_ref[pl.ds(g, n_past, stride=ATT_KV_HEADS), :].astype(BF16)
            vall[g, 0:n_past, 0:hd] = cv_ref[pl.ds(g, n_past, stride=ATT_KV_HEADS), :].astype(BF16)
        else:
            ko_ref[layer, :, g, :] = k
            vo_ref[layer, :, g, :] = v
        kall[g, n_past:n_keys, :] = k.astype(BF16)
        vall[g, n_past:n_keys, 0:hd] = v.astype(BF16)
        vall[g, :, hd:2 * hd] = (lane == 0).astype(BF16)

    def tile_rows(item):
        return pl.ds(pl.multiple_of((item // ATT_KV_HEADS) * tq, tq), tq)

    def prep(item, g):
        rows = tile_rows(item)
        for h in range(ATT_REP):
            q = _rms(q_ref[rows, g * qw + h * hd:g * qw + (h + 1) * hd].astype(F32), qn_ref[...])
            if latent:
                q = _rope(q, cos_ref[rows, :], sin_ref[rows, :])
            qs_bufs[g][h * tq:(h + 1) * tq, :] = q.astype(BF16)

    def scores(item, g):
        qs = qs_bufs[g][...]
        mx = None
        for kb in range(n_kb):
            s = _dot_nt(qs, kall[g, kb * tk:(kb + 1) * tk, :])
            s_bufs[g][:, kb * tk:(kb + 1) * tk] = s
            for c in range(tk // hd):
                part = s[:, c * hd:(c + 1) * hd]
                mx = part if mx is None else jnp.maximum(mx, part)
        m = jnp.max(mx, axis=-1, keepdims=True) * exp2_scale
        m_bufs[g][...] = jnp.broadcast_to(m, (ATT_REP * tq, hd))

    def mix(item, g):
        m = m_bufs[g][...]
        ps = []
        for c in range(n_keys // hd):
            s = s_bufs[g][:, c * hd:(c + 1) * hd]
            ps.append(jnp.exp2(s * exp2_scale - m).astype(BF16))
        o_bufs[g][...] = _dot(jnp.concatenate(ps, axis=1), vall[g])

    def finish(item, g):
        rows = tile_rows(item)
        o = o_bufs[g][:, 0:hd] / o_bufs[g][:, hd:hd + 1]
        for h in range(ATT_REP):
            gate = g_refs[g][rows, h * hd:(h + 1) * hd].astype(F32)
            y = o[h * tq:(h + 1) * tq, :] * (gate * jax.nn.sigmoid(gate))
            y_ref[rows, g * qw + h * hd:g * qw + (h + 1) * hd] = y.astype(BF16)

    stages = (prep, scores, mix, finish)

    def step(k, parity):
        for a, stage in enumerate(stages):
            item = k - a
            if isinstance(item, int) and not 0 <= item < n_items:
                continue
            stage(item, (parity + a) % 2)

    depth = len(stages) - 1
    for k in range(min(depth, n_items + depth)):
        step(k, k % 2)
    n_steady = max(n_items - depth, 0)

    group = ATT_STEADY_STEPS

    def steady_group(j, carry):
        for u in range(group):
            step(depth + group * j + u, (depth + u) % 2)
        return carry

    lax.fori_loop(0, n_steady // group, steady_group, 0)
    for k in range(depth + group * (n_steady // group), n_items + depth):
        step(k, k % 2)


def _attention(z, batch, seq_len, q_norm, k_norm, j, cache=None, new_kv=None):
    hd = ATT_HEAD_DIM
    latent = cache is not None
    qw = ATT_REP * hd
    kv0 = ATT_WIDTH // KV_WIDTH
    gate0 = (ATT_WIDTH + 2 * KV_WIDTH) // qw
    assert (ATT_WIDTH + 2 * KV_WIDTH) % qw == 0 and ATT_WIDTH % KV_WIDTH == 0
    in_specs = [
        pl.BlockSpec((seq_len, ATT_WIDTH), lambda b: (b, 0)),
        pl.BlockSpec((seq_len, KV_WIDTH), lambda b: (b, kv0)),
        pl.BlockSpec((seq_len, KV_WIDTH), lambda b: (b, kv0 + 1)),
        pl.BlockSpec((1, hd), lambda b: (0, 0)),
        pl.BlockSpec((1, hd), lambda b: (0, 0)),
    ] + [pl.BlockSpec((seq_len, qw), lambda b, g=g: (b, gate0 + g)) for g in range(ATT_KV_HEADS)]
    args = [z, z, z, q_norm, k_norm] + [z] * ATT_KV_HEADS
    out_specs = [pl.BlockSpec((seq_len, ATT_WIDTH), lambda b: (b, 0))]
    out_shape = [jax.ShapeDtypeStruct((batch * seq_len, ATT_WIDTH), BF16)]
    n_keys = seq_len
    aliases = {}
    layer = j
    if latent:
        cache_k, cache_v, cos, sin_signed = cache
        past = cache_k.shape[2]
        n_keys += past
        cspec = pl.BlockSpec((None, None, past * ATT_KV_HEADS, hd), lambda b: (b, j, 0, 0))
        tspec = pl.BlockSpec((seq_len, hd), lambda b: (0, 0))
        in_specs += [cspec, cspec, tspec, tspec]
        args += [cache_k.reshape(batch, N_ATT, past * ATT_KV_HEADS, hd),
                 cache_v.reshape(batch, N_ATT, past * ATT_KV_HEADS, hd), cos, sin_signed]
    else:
        out_shape += [jax.ShapeDtypeStruct((batch, N_ATT, seq_len, ATT_KV_HEADS, hd), F32)] * 2
        if new_kv is None:
            kv_spec = pl.BlockSpec((None, N_ATT, seq_len, ATT_KV_HEADS, hd), lambda b: (b, 0, 0, 0, 0))
        else:
            kv_spec = pl.BlockSpec((None, 1, seq_len, ATT_KV_HEADS, hd), lambda b: (b, j, 0, 0, 0))
            layer = 0
            aliases = {len(args): 1, len(args) + 1: 2}
            in_specs += [pl.BlockSpec(memory_space=pl.ANY)] * 2
            args += list(new_kv)
        out_specs += [kv_spec, kv_spec]
    rows = ATT_REP * ATT_Q_TILE
    return pl.pallas_call(
        functools.partial(_att_kernel, seq_len=seq_len, n_keys=n_keys, latent=latent, n_aliased=len(aliases),
                          layer=layer),
        grid=(batch,),
        in_specs=in_specs,
        out_specs=out_specs,
        out_shape=out_shape,
        input_output_aliases=aliases,
        scratch_shapes=[pltpu.VMEM((ATT_KV_HEADS, n_keys, hd), BF16),
                        pltpu.VMEM((ATT_KV_HEADS, n_keys, 2 * hd), BF16),
                        pltpu.VMEM((rows, hd), BF16), pltpu.VMEM((rows, hd), BF16),
                        pltpu.VMEM((rows, n_keys), F32), pltpu.VMEM((rows, n_keys), F32),
                        pltpu.VMEM((rows, hd), F32), pltpu.VMEM((rows, hd), F32),
                        pltpu.VMEM((rows, 2 * hd), F32), pltpu.VMEM((rows, 2 * hd), F32)],
        compiler_params=_cparams(1),
        name="attention",
    )(*args)


def _rope_tables(n_tokens):
    t = jnp.arange(n_tokens)
    row = (t // GRID_W).astype(F32)
    col = (t % GRID_W).astype(F32)
    inv = ROPE_THETA ** (-jnp.arange(0, AXIS_DIM, 2, dtype=F32) / AXIS_DIM)
    ang = jnp.concatenate([row[:, None] * inv[None, :], col[:, None] * inv[None, :]], axis=-1)
    cos = jnp.repeat(jnp.cos(ang), 2, axis=-1)
    sin = jnp.repeat(jnp.sin(ang), 2, axis=-1)
    sign = jnp.where(jnp.arange(ATT_HEAD_DIM) % 2 == 0, -1.0, 1.0).astype(F32)
    return cos, sin * sign


def kernel(x_prompt, x_sample, c, state_hgrn, cache_k, cache_v, c_ctx, ada_w, ada_b, norm_pre, norm_post, rec_w_in, rec_lb_logits, rec_head_norm, pool_w, pool_scale, rec_w_out, att_w_in, att_q_norm, att_k_norm, att_w_out):
    nb_c, len_c, _ = x_prompt.shape
    nb_l, len_l, _ = x_sample.shape

    lb_p = jax.nn.softmax(rec_lb_logits.astype(F32), axis=0)
    lower_bounds = jnp.clip(jnp.cumsum(lb_p, axis=0) - lb_p[0], 0.0, 1.0)
    lower_bounds = lower_bounds.reshape(N_REC, 2, REC_HEADS, 1, REC_HEAD_DIM)
    cos, sin_signed = _rope_tables(len_l)

    cvec = jnp.zeros((MOD_ROWS, D_MODEL), F32).at[0].set(c_ctx).at[1:1 + nb_l].set(c)
    mod = _modulation(cvec, ada_w, ada_b)

    forget_cols = (2 * POOL_WIDTH + REC_WIDTH, 2 * POOL_WIDTH + 3 * REC_WIDTH)
    layers = []
    for i in range(DEPTH):
        j = i // 2
        rec = i % 2 == 0
        w_in, w_out = (rec_w_in, rec_w_out) if rec else (att_w_in, att_w_out)
        layers.append(dict(
            w_in=w_in[j].astype(BF16), w_out=w_out[j].astype(BF16), f32_cols=forget_cols if rec else (0, 0),
            gain_pre=norm_pre[i].reshape(1, D_MODEL), gain_post=norm_post[i].reshape(1, D_MODEL)))

    def pre_args(i):
        return (layers[i]["w_in"], i, layers[i]["gain_pre"], layers[i]["f32_cols"])

    xs = [x_prompt.reshape(nb_c * len_c, D_MODEL), x_sample.reshape(nb_l * len_l, D_MODEL)]
    streams = ((nb_c, len_c, 0), (nb_l, len_l, 1))
    zs = [None, None]
    ys = [None, None]
    new_states, new_kv = None, None

    def proj_part(s, i, steps=None):
        nb, sl, row0 = streams[s]
        post = ys[s] + (layers[i - 1]["w_out"], i - 1, layers[i - 1]["gain_post"]) if i > 0 else None
        return _proj_part(xs[s], mod, sl, row0, post=post, pre=pre_args(i) if i < DEPTH else None, steps=steps)

    def take_proj(s, i, outs):
        if i > 0:
            xs[s] = outs.pop(0)
        zs[s] = outs

    def rec_part(s, i, heads=None):
        nb, sl, _ = streams[s]
        j = i // 2
        hn = rec_head_norm[j].reshape(1, REC_HEAD_DIM)
        z, zf = zs[s]
        if s == 0:
            return _rec_part(z, zf, nb, sl, lower_bounds[j], hn, j, states=new_states, heads=heads)
        return _rec_part(z, zf, nb, sl, lower_bounds[j], hn, j, s0=state_hgrn, heads=heads)

    def pool(s, i):
        nb, sl, _ = streams[s]
        j = i // 2
        return _pool_mixer(zs[s][0], nb, sl, pool_w[j].astype(BF16), pool_scale[j].reshape(1, POOL_WIDTH))

    take_proj(0, 0, _run(proj_part(0, 0), name="proj")[0])
    for i in range(DEPTH):
        j = i // 2
        if i % 2 == 0:
            rec_c = rec_part(0, i)
            (y_rec, new_states), outs = _run(rec_c, proj_part(1, i, steps=rec_c.steps), name="rec_proj")
            ys[0] = (pool(0, i), 0, y_rec, 0)
            take_proj(1, i, outs)
            rec_l = rec_part(1, i, heads=REC_HEADS * streams[1][0] // rec_c.steps)
            (y_rec,), outs = _run(rec_l, proj_part(0, i + 1, steps=rec_l.steps), name="rec_proj")
            ys[1] = (pool(1, i), 0, y_rec, 0)
            take_proj(0, i + 1, outs)
        else:
            qn = att_q_norm[j].reshape(1, ATT_HEAD_DIM)
            kn = att_k_norm[j].reshape(1, ATT_HEAD_DIM)
            y, *new_kv = _attention(zs[0][0], nb_c, len_c, qn, kn, j, new_kv=new_kv)
            ys[0] = (y, 0, y, 1)
            take_proj(1, i, _run(proj_part(1, i), name="proj")[0])
            (y,) = _attention(zs[1][0], nb_l, len_l, qn, kn, j, cache=(cache_k, cache_v, cos, sin_signed))
            ys[1] = (y, 0, y, 1)
            take_proj(0, i + 1, _run(proj_part(0, i + 1), name="proj")[0])
    take_proj(1, DEPTH, _run(proj_part(1, DEPTH), name="proj")[0])
    xc, xl = xs

    return (xc.reshape(nb_c, len_c, D_MODEL), xl.reshape(nb_l, len_l, D_MODEL),
            new_states, new_kv[0], new_kv[1])
```

```python
import functools
from typing import Any, NamedTuple

import jax
import jax.numpy as jnp
from jax import lax
from jax.experimental import pallas as pl
from jax.experimental.pallas import tpu as pltpu

D_MODEL = 1024
DEPTH = 4
GRID_W = 64
N_REC = (DEPTH + 1) // 2
N_ATT = DEPTH // 2
POOL_WIDTH = D_MODEL // 2
POOL_WINDOWS = (2, 4, 8, 16)
POOL_GROUP_DIM = POOL_WIDTH // len(POOL_WINDOWS)
REC_WIDTH = D_MODEL // 2
REC_HEAD_DIM = 128
REC_HEADS = REC_WIDTH // REC_HEAD_DIM
REC_IN_WIDTH = 2 * POOL_WIDTH + 5 * REC_WIDTH
ATT_HEAD_DIM = 128
ATT_HEADS = D_MODEL // ATT_HEAD_DIM
ATT_KV_HEADS = 2
ATT_REP = ATT_HEADS // ATT_KV_HEADS
ATT_WIDTH = ATT_HEADS * ATT_HEAD_DIM
KV_WIDTH = ATT_KV_HEADS * ATT_HEAD_DIM
ATT_IN_WIDTH = 2 * ATT_WIDTH + 2 * KV_WIDTH
AXIS_DIM = ATT_HEAD_DIM // 2
ROPE_THETA = 10000.0
EPS = 1e-6
F_MIN = 1e-6

MOD_ROWS = 16
TOKEN_TILE = 1024
MIN_PROJ_STEPS = 8
PROJ_GROUP_ROWS = 512
POOL_ROWS = 256
IN_PROJ_STEP = 512
MOD_COLS = 1536
SCAN_CHUNK = 128
SCAN_UNITS = 8
POOL_HALO = 128
ATT_Q_TILE = 128
ATT_KEY_TILE = 128
ATT_STEADY_STEPS = 2
LOG2_E = 1.4426950408889634
VMEM_LIMIT = 56 * 1024 * 1024

F32 = jnp.float32
BF16 = jnp.bfloat16


def _cparams(n_axes):
    return pltpu.CompilerParams(
        dimension_semantics=("arbitrary",) * n_axes, vmem_limit_bytes=VMEM_LIMIT)


def _rms(x, g):
    return x * lax.rsqrt(jnp.mean(x * x, axis=-1, keepdims=True) + EPS) * g


def _dot(a, b):
    return jnp.dot(a, b, preferred_element_type=F32)


def _dot_nt(a, b):
    return lax.dot_general(a, b, (((1,), (1,)), ((), ())), preferred_element_type=F32)


class _Part(NamedTuple):
    kernel: Any
    steps: int
    in_specs: list
    args: list
    out_specs: list
    out_shape: list
    scratch_shapes: list
    aliases: dict


def _run(*parts, name):
    steps = parts[0].steps
    assert all(p.steps == steps for p in parts)
    n_in = [len(p.args) for p in parts]
    n_out = [len(p.out_shape) for p in parts]
    n_scr = [len(p.scratch_shapes) for p in parts]

    def body(*refs):
        ins, outs, scr = refs[:sum(n_in)], refs[sum(n_in):sum(n_in) + sum(n_out)], refs[sum(n_in) + sum(n_out):]
        pending = {}
        for k, p in enumerate(parts):
            i0, o0, s0 = sum(n_in[:k]), sum(n_out[:k]), sum(n_scr[:k])
            pending[k] = (0.0, p.kernel(*ins[i0:i0 + n_in[k]], *outs[o0:o0 + n_out[k]], *scr[s0:s0 + n_scr[k]]))
        while pending:
            k = min(pending, key=lambda k: pending[k][0])
            try:
                pending[k] = (next(pending[k][1]), pending[k][1])
            except StopIteration:
                del pending[k]

    aliases = {}
    for k, p in enumerate(parts):
        for i, o in p.aliases.items():
            aliases[sum(n_in[:k]) + i] = sum(n_out[:k]) + o
    outs = pl.pallas_call(
        body,
        grid=(steps,),
        in_specs=[s for p in parts for s in p.in_specs],
        out_specs=[s for p in parts for s in p.out_specs],
        out_shape=[s for p in parts for s in p.out_shape],
        scratch_shapes=[s for p in parts for s in p.scratch_shapes],
        input_output_aliases=aliases,
        compiler_params=_cparams(1),
        name=name,
    )(*[a for p in parts for a in p.args])
    return [list(outs[sum(n_out[:k]):sum(n_out[:k]) + n_out[k]]) for k in range(len(parts))]


def _mod_kernel(cv_ref, w_ref, b_ref, o_ref):
    cv = cv_ref[...]
    a = (cv * jax.nn.sigmoid(cv)).astype(BF16)
    o_ref[...] = _dot(a, w_ref[...].astype(BF16)) + b_ref[...]


def _modulation(cvec, ada_w, ada_b):
    tn = MOD_COLS
    out = pl.pallas_call(
        _mod_kernel,
        grid=(DEPTH, 3 * D_MODEL // tn),
        in_specs=[
            pl.BlockSpec((MOD_ROWS, D_MODEL), lambda l, n: (0, 0)),
            pl.BlockSpec((None, D_MODEL, tn), lambda l, n: (l, 0, n)),
            pl.BlockSpec((None, 1, tn), lambda l, n: (l, 0, n)),
        ],
        out_specs=pl.BlockSpec((None, MOD_ROWS, tn), lambda l, n: (l, 0, n)),
        out_shape=jax.ShapeDtypeStruct((DEPTH, MOD_ROWS, 3 * D_MODEL), F32),
        compiler_params=_cparams(2),
        name="modulation",
    )(cvec, ada_w, ada_b.reshape(DEPTH, 1, 3 * D_MODEL))
    return out.reshape(DEPTH, MOD_ROWS, 3, 1, D_MODEL)


def _mod_spec(layer, seq_len, row0, tile):
    assert row0 == 0 or seq_len % tile == 0
    tiles_per_seq = seq_len // tile
    if row0 == 0:
        index = lambda i: (layer, 0, 0, 0, 0)
    else:
        index = lambda i: (layer, row0 + i // tiles_per_seq, 0, 0, 0)
    return pl.BlockSpec((None, None, 3, 1, D_MODEL), index)


def _proj_kernel(*refs, post, pre, n_out, f32_cols):
    refs = list(refs)
    x_ref = refs.pop(0)
    if post:
        ya_ref, yb_ref, wo_ref, modp_ref, gpost_ref = (refs.pop(0) for _ in range(5))
    if pre:
        modn_ref, gpre_ref, wi_ref = (refs.pop(0) for _ in range(3))
    xo_ref = refs.pop(0) if post else None
    rows_all = x_ref.shape[0]
    n_groups = max(1, rows_all // PROJ_GROUP_ROWS)

    def group(rows):
        x = x_ref[rows, :]
        if post:
            half = D_MODEL // 2
            p = _dot(ya_ref[rows, :], wo_ref[0:half, :]) + _dot(yb_ref[rows, :], wo_ref[half:D_MODEL, :])
            x = x + modp_ref[2] * _rms(p, gpost_ref[...])
            xo_ref[rows, :] = x
            yield
        if pre:
            h = (_rms(x, gpre_ref[...]) * (1.0 + modn_ref[1]) + modn_ref[0]).astype(BF16)
            lo, hi = f32_cols
            for c0 in range(0, n_out, IN_PROJ_STEP):
                y = _dot(h, wi_ref[:, c0:c0 + IN_PROJ_STEP])
                if lo <= c0 < hi:
                    refs[1][rows, c0 - lo:c0 - lo + IN_PROJ_STEP] = y
                else:
                    c1 = c0 if c0 < lo else c0 - (hi - lo)
                    refs[0][rows, c1:c1 + IN_PROJ_STEP] = y.astype(BF16)
                yield

    size = rows_all // n_groups
    waiting = [group(slice(k * size, (k + 1) * size)) for k in range(n_groups)]
    running = []
    pieces = n_groups * (post + n_out // IN_PROJ_STEP)
    done = 0
    while waiting or running:
        if waiting:
            running.append(waiting.pop(0))
        for g in list(running):
            try:
                next(g)
                done += 1
            except StopIteration:
                running.remove(g)
        yield min(done / pieces, 1.0)


def _proj_part(x, mod, seq_len, row0, post=None, pre=None, steps=None):
    t = x.shape[0]
    tile = t // steps if steps else min(TOKEN_TILE, t // MIN_PROJ_STEPS)
    half = D_MODEL // 2
    row = lambda i: (i, 0)
    fixed = lambda i: (0, 0)
    resident = dict(index_map=fixed, pipeline_mode=pl.Buffered(1))
    in_specs = [pl.BlockSpec((tile, D_MODEL), row)]
    args = [x]
    out_specs, out_shape = [], []
    n_out, f32_cols = 0, (0, 0)
    if post:
        ya, ia, yb, ib, w_out, layer, gain = post
        in_specs += [pl.BlockSpec((tile, half), lambda i: (i, ia)),
                     pl.BlockSpec((tile, half), lambda i: (i, ib)),
                     pl.BlockSpec((D_MODEL, D_MODEL), **resident),
                     _mod_spec(layer, seq_len, row0, tile),
                     pl.BlockSpec((1, D_MODEL), fixed)]
        args += [ya, yb, w_out, mod, gain]
        out_specs.append(pl.BlockSpec((tile, D_MODEL), row))
        out_shape.append(jax.ShapeDtypeStruct((t, D_MODEL), F32))
    if pre:
        w_in, layer, gain, f32_cols = pre
        n_out = w_in.shape[1]
        lo, hi = f32_cols
        assert lo % IN_PROJ_STEP == 0 and hi % IN_PROJ_STEP == 0 and n_out % IN_PROJ_STEP == 0
        in_specs += [_mod_spec(layer, seq_len, row0, tile),
                     pl.BlockSpec((1, D_MODEL), fixed),
                     pl.BlockSpec((D_MODEL, n_out), **resident)]
        args += [mod, gain, w_in]
        out_specs.append(pl.BlockSpec((tile, n_out - (hi - lo)), row))
        out_shape.append(jax.ShapeDtypeStruct((t, n_out - (hi - lo)), BF16))
        if hi > lo:
            out_specs.append(pl.BlockSpec((tile, hi - lo), row))
            out_shape.append(jax.ShapeDtypeStruct((t, hi - lo), F32))
    kernel = functools.partial(_proj_kernel, post=bool(post), pre=bool(pre), n_out=n_out, f32_cols=f32_cols)
    return _Part(kernel, t // tile, in_specs, args, out_specs, out_shape, [], {})


def _pool_kernel(u_ref, gp_ref, pw_ref, ps_ref, o_ref, pad_ref, band_ref, sum_ref, dif_ref, *, seq_len):
    gd = POOL_GROUP_DIM
    rows = POOL_ROWS
    halo = POOL_HALO
    span = rows + 2 * halo

    @pl.when(pl.program_id(0) == 0)
    def _():
        r = lax.broadcasted_iota(jnp.int32, (rows, span), 0)
        c = lax.broadcasted_iota(jnp.int32, (rows, span), 1)
        offset = c - halo - r
        for g, win in enumerate(POOL_WINDOWS):
            band_ref[g] = jnp.where((offset >= -(win // 2)) & (offset < win // 2), 1.0, 0.0).astype(BF16)

    zeros = jnp.zeros((halo, POOL_WIDTH), BF16)
    pad_ref[0:halo, :] = zeros
    pad_ref[halo + seq_len:2 * halo + seq_len, :] = zeros
    pad_ref[halo:halo + seq_len, :] = u_ref[...]
    units = [(g, r * rows) for g in range(len(POOL_WINDOWS)) for r in range(seq_len // rows)]

    def cols(g):
        return slice(g * gd, (g + 1) * gd)

    for g, base in units:
        sum_ref[base:base + rows, cols(g)] = _dot(band_ref[g], pad_ref[base:base + span, cols(g)])
    for g, base in units:
        win = POOL_WINDOWS[g]
        t = base + lax.broadcasted_iota(jnp.int32, (rows, 1), 0)
        count = jnp.clip(t + win // 2, 0, seq_len) - jnp.clip(t - win // 2, 0, seq_len)
        mean = sum_ref[base:base + rows, cols(g)] / count.astype(F32)
        d = mean - pad_ref[halo + base:halo + base + rows, cols(g)].astype(F32)
        dif_ref[base:base + rows, cols(g)] = d.astype(BF16)
    for g, base in units:
        y = _dot(dif_ref[base:base + rows, cols(g)], pw_ref[g]) * ps_ref[:, cols(g)]
        gate = gp_ref[base:base + rows, cols(g)].astype(F32)
        o_ref[base:base + rows, cols(g)] = (y * (gate * jax.nn.sigmoid(gate))).astype(BF16)


def _pool_mixer(z, batch, seq_len, pool_w, pool_scale):
    return pl.pallas_call(
        functools.partial(_pool_kernel, seq_len=seq_len),
        grid=(batch,),
        in_specs=[
            pl.BlockSpec((seq_len, POOL_WIDTH), lambda b: (b, 0)),
            pl.BlockSpec((seq_len, POOL_WIDTH), lambda b: (b, 1)),
            pl.BlockSpec((len(POOL_WINDOWS), POOL_GROUP_DIM, POOL_GROUP_DIM), lambda b: (0, 0, 0)),
            pl.BlockSpec((1, POOL_WIDTH), lambda b: (0, 0)),
        ],
        out_specs=pl.BlockSpec((seq_len, POOL_WIDTH), lambda b: (b, 0)),
        out_shape=jax.ShapeDtypeStruct((batch * seq_len, POOL_WIDTH), BF16),
        scratch_shapes=[pltpu.VMEM((seq_len + 2 * POOL_HALO, POOL_WIDTH), BF16),
                        pltpu.VMEM((len(POOL_WINDOWS), POOL_ROWS, POOL_ROWS + 2 * POOL_HALO), BF16),
                        pltpu.VMEM((seq_len, POOL_WIDTH), F32), pltpu.VMEM((seq_len, POOL_WIDTH), BF16)],
        compiler_params=_cparams(1),
        name="pool_mixer",
    )(z, z, pool_w, pool_scale)


def _block_row(p, block, row):
    c, w = p.shape
    p3 = p.reshape(c // block, block, w)
    return jnp.broadcast_to(p3[:, row:row + 1, :], p3.shape).reshape(c, w)


def _interleave(lo, hi, block):
    half = block // 2
    parts = []
    for b in range(SCAN_CHUNK // block):
        parts.append(lo[b * block:b * block + half])
        parts.append(hi[b * block + half:(b + 1) * block])
    return jnp.concatenate(parts, axis=0)


def _select_levels(products, rev):
    c = SCAN_CHUNK
    lane = lax.broadcasted_iota(jnp.int32, (8, c), 1)
    sub = lax.broadcasted_iota(jnp.int32, (8, c), 0)
    out = []
    for t0 in range(0, c, 8):
        if rev:
            keep = (lane >= t0 + sub) & (lane < t0 + 8)
        else:
            keep = (lane >= t0) & (lane <= t0 + sub)
        row = jnp.where(keep, products[0][t0:t0 + 8, :], 0.0)
        for lv, block in enumerate((16, 32, 64, 128), 1):
            half = block // 2
            start = t0 // block * block
            upper = t0 - start >= half
            if upper == rev:
                continue
            lo = start + half if rev else start
            row = jnp.where((lane >= lo) & (lane < lo + half), products[lv][t0:t0 + 8, :], row)
        out.append(row)
    return jnp.concatenate(out, axis=0)


def _rec_kernel(*refs, seq_len, heads, has_s0, out_state, n_aliased, layer):
    q_ref, ff_ref, fb_ref, v_ref, gr_ref, lb_ref, hn_ref = refs[:7]
    rest = list(refs[7:])
    s0_ref = rest.pop(0) if has_s0 else None
    del rest[:n_aliased]
    o_ref = rest.pop(0)
    st_ref = rest.pop(0) if out_state else None
    acc_ref, qs_ref, k_ref, p_ref, a_ref, qin_ref, u_ref, dec_ref, tri_ref = rest

    c = SCAN_CHUNK
    hd = REC_HEAD_DIM
    n_chunks = seq_len // c
    zf_refs = (ff_ref, fb_ref)
    units = [(h, n) for h in range(heads) for n in range(n_chunks)]

    def rows_in(n):
        return slice(n * c, (n + 1) * c)

    def rows_sc(h, n):
        return slice((h * n_chunks + n) * c, (h * n_chunks + n + 1) * c)

    def cols(h):
        return slice(h * hd, (h + 1) * hd)

    ti = lax.broadcasted_iota(jnp.int32, (c, c), 0)
    si = lax.broadcasted_iota(jnp.int32, (c, c), 1)
    for d, causal in enumerate((si <= ti, si >= ti)):
        tri_ref[d] = jnp.where(causal, 1.0, 0.0).astype(BF16)

    weights = (0.26, 0.43, 0.09, 0.22)
    done = [0.0]

    def progress(stage, share):
        done[0] += weights[stage] * share
        return done[0]

    for h, n in units:
        qz = q_ref[rows_in(n), cols(h)].astype(F32)
        qs_ref[rows_sc(h, n), :] = qz * jax.nn.sigmoid(qz)
        for d in range(2):
            lower = lb_ref[d, h]
            f = jnp.clip(lower + (1.0 - lower) * jax.nn.sigmoid(zf_refs[d][rows_in(n), cols(h)]), F_MIN, 1.0)
            k_ref[d, rows_sc(h, n), :] = 1.0 - f
            g = jnp.log(f) * LOG2_E
            g_hi = g.astype(BF16)
            rest = g - g_hi.astype(F32)
            g_mid = rest.astype(BF16)
            g_lo = (rest - g_mid.astype(F32)).astype(BF16)
            sums = _dot(tri_ref[d], jnp.concatenate([g_hi, g_mid, g_lo], axis=1))
            p_ref[d, rows_sc(h, n), :] = (sums[:, 0:hd] + sums[:, hd:2 * hd]) + sums[:, 2 * hd:3 * hd]
        yield progress(0, 1 / len(units))

    for h, n in units:
        rows = rows_sc(h, n)
        q = qs_ref[rows, :]
        v_t = v_ref[rows_in(n), cols(h)].astype(F32).T.astype(BF16)
        for d in range(2):
            rev = d == 1
            k = k_ref[d, rows, :]
            p = p_ref[d, rows, :]
            e0 = p - _block_row(p, 8, 4 if rev else 3)
            products = [_dot_nt((q * jnp.exp2(e0)).astype(BF16), (k * jnp.exp2(-e0)).astype(BF16))]
            for block in (16, 32, 64, 128):
                beta = _block_row(p, block, block // 2 if rev else block // 2 - 1)
                if rev:
                    e = _interleave(p, beta, block) - _interleave(beta, p, block)
                    src = _interleave(q, k, block)
                else:
                    e = _interleave(beta, p, block) - _interleave(p, beta, block)
                    src = _interleave(k, q, block)
                m = (src * jnp.exp2(e)).astype(BF16)
                products.append(_dot_nt(m, m))
            a_ref[rows, d * c:(d + 1) * c] = _select_levels(products, rev).astype(BF16)

            edge = p[0:1, :] if rev else p[c - 1:c, :]
            qin_ref[d, rows, :] = (q * jnp.exp2(p)).astype(BF16)
            u_ref[d, h * n_chunks + n] = _dot(v_t, (k * jnp.exp2(edge - p)).astype(BF16))
            dec_ref[d, h * n_chunks + n] = jnp.broadcast_to(jnp.exp2(edge), (8, hd))
        yield progress(1, 1 / len(units))

    for h, n in units:
        v_b = v_ref[rows_in(n), cols(h)]
        acc_ref[rows_sc(h, n), :] = _dot(a_ref[rows_sc(h, n), :], jnp.concatenate([v_b, v_b], axis=0))
        yield progress(2, 1 / len(units))

    for h in range(heads):
        if has_s0:
            states = [s0_ref[0, h].T, s0_ref[1, h].T]
        else:
            states = [jnp.zeros((hd, hd), F32)] * 2
        inter = [[None] * n_chunks, [None] * n_chunks]
        for i in range(n_chunks):
            for d, n in ((0, i), (1, n_chunks - 1 - i)):
                inter[d][n] = _dot_nt(qin_ref[d, rows_sc(h, n), :], states[d].astype(BF16))
                states[d] = dec_ref[d, h * n_chunks + n][0:1, :] * states[d] + u_ref[d, h * n_chunks + n]
        for n in range(n_chunks):
            gate = gr_ref[rows_in(n), cols(h)].astype(F32)
            o = acc_ref[rows_sc(h, n), :] + inter[0][n] + inter[1][n]
            o_ref[rows_in(n), cols(h)] = (_rms(o, hn_ref[...]) * (gate * jax.nn.sigmoid(gate))).astype(BF16)
        if out_state:
            st_ref[layer, 0, h] = states[0].T
            st_ref[layer, 1, h] = states[1].T
        yield progress(3, 1 / heads)
    if out_state:
        for other in range(st_ref.shape[0]):
            if other != layer:
                st_ref[other] = jnp.zeros(st_ref.shape[1:], F32)


def _rec_part(z, zf, batch, seq_len, lower, head_norm, j, s0=None, states=None, heads=None):
    hd = REC_HEAD_DIM
    nh = REC_HEADS
    n_chunks = seq_len // SCAN_CHUNK
    heads = heads or min(nh, max(1, SCAN_UNITS // n_chunks))
    width = heads * hd
    col0 = 2 * POOL_WIDTH // width
    groups = nh // heads

    def zspec(part):
        return pl.BlockSpec((seq_len, width), lambda i: (i // groups, col0 + part * groups + i % groups))

    def fspec(part):
        return pl.BlockSpec((seq_len, width), lambda i: (i // groups, part * groups + i % groups))

    in_specs = [zspec(0), fspec(0), fspec(1), zspec(1), zspec(2),
                pl.BlockSpec((2, heads, 1, hd), lambda i: (0, i % groups, 0, 0)),
                pl.BlockSpec((1, hd), lambda i: (0, 0))]
    args = [z, zf, zf, z, z, lower, head_norm]
    has_s0 = s0 is not None
    if has_s0:
        in_specs.append(pl.BlockSpec((None, None, 2, heads, hd, hd),
                                     lambda i: (i // groups, j, 0, i % groups, 0, 0)))
        args.append(s0)
    out_state = not has_s0
    out_specs = [pl.BlockSpec((seq_len, width), lambda i: (i // groups, i % groups))]
    out_shape = [jax.ShapeDtypeStruct((batch * seq_len, REC_WIDTH), BF16)]
    aliases = {}
    layer = j
    if out_state:
        out_shape.append(jax.ShapeDtypeStruct((batch, N_REC, 2, nh, hd, hd), F32))
        if states is None:
            out_specs.append(pl.BlockSpec((None, N_REC, 2, heads, hd, hd),
                                          lambda i: (i // groups, 0, 0, i % groups, 0, 0)))
        else:
            out_specs.append(pl.BlockSpec((None, 1, 2, heads, hd, hd),
                                          lambda i: (i // groups, j, 0, i % groups, 0, 0)))
            layer = 0
            aliases[len(args)] = 1
            in_specs.append(pl.BlockSpec(memory_space=pl.ANY))
            args.append(states)
    tokens = heads * seq_len
    kernel = functools.partial(_rec_kernel, seq_len=seq_len, heads=heads, has_s0=has_s0, out_state=out_state,
                               n_aliased=len(aliases), layer=layer)
    scratch_shapes = [pltpu.VMEM((tokens, hd), F32),
                      pltpu.VMEM((tokens, hd), F32),
                      pltpu.VMEM((2, tokens, hd), F32),
                      pltpu.VMEM((2, tokens, hd), F32),
                      pltpu.VMEM((tokens, 2 * SCAN_CHUNK), BF16),
                      pltpu.VMEM((2, tokens, hd), BF16),
                      pltpu.VMEM((2, tokens // SCAN_CHUNK, hd, hd), F32),
                      pltpu.VMEM((2, tokens // SCAN_CHUNK, 8, hd), F32),
                      pltpu.VMEM((2, SCAN_CHUNK, SCAN_CHUNK), BF16)]
    return _Part(kernel, batch * groups, in_specs, args, out_specs, out_shape, scratch_shapes, aliases)


def _rope(x, cos, sin_signed):
    lane = lax.broadcasted_iota(jnp.int32, x.shape, 1)
    partner = jnp.where((lane & 1) == 0, pltpu.roll(x, ATT_HEAD_DIM - 1, 1), pltpu.roll(x, 1, 1))
    return x * cos + partner * sin_signed


def _att_kernel(*refs, seq_len, n_keys, latent, n_aliased, layer):
    q_ref, k_ref, v_ref, qn_ref, kn_ref = refs[:5]
    g_refs = refs[5:5 + ATT_KV_HEADS]
    if latent:
        ck_ref, cv_ref, cos_ref, sin_ref, y_ref, kall, vall = refs[5 + ATT_KV_HEADS:-8]
    else:
        y_ref, ko_ref, vo_ref, kall, vall = refs[5 + ATT_KV_HEADS + n_aliased:-8]
    qs_bufs, s_bufs, m_bufs, o_bufs = refs[-8:-6], refs[-6:-4], refs[-4:-2], refs[-2:]
    hd = ATT_HEAD_DIM
    qw = ATT_REP * hd
    tq = ATT_Q_TILE
    tk = min(ATT_KEY_TILE, n_keys)
    n_kb = n_keys // tk
    n_past = n_keys - seq_len
    n_items = ATT_KV_HEADS * (seq_len // tq)
    exp2_scale = hd ** -0.5 * LOG2_E
    assert ATT_KV_HEADS == 2

    if not latent:
        for other in range(ko_ref.shape[0]):
            if other != layer:
                ko_ref[other] = jnp.zeros(ko_ref.shape[1:], F32)
                vo_ref[other] = jnp.zeros(vo_ref.shape[1:], F32)
    lane = lax.broadcasted_iota(jnp.int32, (n_keys, hd), 1)
    for g in range(ATT_KV_HEADS):
        k = _rms(k_ref[:, g * hd:(g + 1) * hd].astype(F32), kn_ref[...])
        v = v_ref[:, g * hd:(g + 1) * hd].astype(F32)
        if latent:
            k = _rope(k, cos_ref[...], sin_ref[...])
            kall[g, 0:n_past, :] = ck_ref[pl.ds(g, n_past, stride=ATT_KV_HEADS), :].astype(BF16)
            vall[g, 0:n_past, 0:hd] = cv_ref[pl.ds(g, n_past, stride=ATT_KV_HEADS), :].astype(BF16)
        else:
            ko_ref[layer, pl.ds(g, seq_len, stride=ATT_KV_HEADS), :] = k
            vo_ref[layer, pl.ds(g, seq_len, stride=ATT_KV_HEADS), :] = v
        kall[g, n_past:n_keys, :] = k.astype(BF16)
        vall[g, n_past:n_keys, 0:hd] = v.astype(BF16)
        vall[g, :, hd:2 * hd] = (lane == 0).astype(BF16)

    def tile_rows(item):
        return pl.ds(pl.multiple_of((item // ATT_KV_HEADS) * tq, tq), tq)

    def prep(item, g):
        rows = tile_rows(item)
        for h in range(ATT_REP):
            q = _rms(q_ref[rows, g * qw + h * hd:g * qw + (h + 1) * hd].astype(F32), qn_ref[...])
            if latent:
                q = _rope(q, cos_ref[rows, :], sin_ref[rows, :])
            qs_bufs[g][h * tq:(h + 1) * tq, :] = q.astype(BF16)

    def scores(item, g):
        qs = qs_bufs[g][...]
        mx = None
        for kb in range(n_kb):
            s = _dot_nt(qs, kall[g, kb * tk:(kb + 1) * tk, :])
            s_bufs[g][:, kb * tk:(kb + 1) * tk] = s
            for c in range(tk // hd):
                part = s[:, c * hd:(c + 1) * hd]
                mx = part if mx is None else jnp.maximum(mx, part)
        m = jnp.max(mx, axis=-1, keepdims=True) * exp2_scale
        m_bufs[g][...] = jnp.broadcast_to(m, (ATT_REP * tq, hd))

    def mix(item, g):
        m = m_bufs[g][...]
        ps = []
        for c in range(n_keys // hd):
            s = s_bufs[g][:, c * hd:(c + 1) * hd]
            ps.append(jnp.exp2(s * exp2_scale - m).astype(BF16))
        o_bufs[g][...] = _dot(jnp.concatenate(ps, axis=1), vall[g])

    def finish(item, g):
        rows = tile_rows(item)
        o = o_bufs[g][:, 0:hd] / o_bufs[g][:, hd:hd + 1]
        for h in range(ATT_REP):
            gate = g_refs[g][rows, h * hd:(h + 1) * hd].astype(F32)
            y = o[h * tq:(h + 1) * tq, :] * (gate * jax.nn.sigmoid(gate))
            y_ref[rows, g * qw + h * hd:g * qw + (h + 1) * hd] = y.astype(BF16)

    stages = (prep, scores, mix, finish)

    def step(k, parity):
        for a, stage in enumerate(stages):
            item = k - a
            if isinstance(item, int) and not 0 <= item < n_items:
                continue
            stage(item, (parity + a) % 2)

    depth = len(stages) - 1
    for k in range(min(depth, n_items + depth)):
        step(k, k % 2)
    n_steady = max(n_items - depth, 0)

    group = ATT_STEADY_STEPS

    def steady_group(j, carry):
        for u in range(group):
            step(depth + group * j + u, (depth + u) % 2)
        return carry

    lax.fori_loop(0, n_steady // group, steady_group, 0)
    for k in range(depth + group * (n_steady // group), n_items + depth):
        step(k, k % 2)


def _attention(z, batch, seq_len, q_norm, k_norm, j, cache=None, new_kv=None):
    hd = ATT_HEAD_DIM
    latent = cache is not None
    qw = ATT_REP * hd
    kv0 = ATT_WIDTH // KV_WIDTH
    gate0 = (ATT_WIDTH + 2 * KV_WIDTH) // qw
    assert (ATT_WIDTH + 2 * KV_WIDTH) % qw == 0 and ATT_WIDTH % KV_WIDTH == 0
    in_specs = [
        pl.BlockSpec((seq_len, ATT_WIDTH), lambda b: (b, 0)),
        pl.BlockSpec((seq_len, KV_WIDTH), lambda b: (b, kv0)),
        pl.BlockSpec((seq_len, KV_WIDTH), lambda b: (b, kv0 + 1)),
        pl.BlockSpec((1, hd), lambda b: (0, 0)),
        pl.BlockSpec((1, hd), lambda b: (0, 0)),
    ] + [pl.BlockSpec((seq_len, qw), lambda b, g=g: (b, gate0 + g)) for g in range(ATT_KV_HEADS)]
    args = [z, z, z, q_norm, k_norm] + [z] * ATT_KV_HEADS
    out_specs = [pl.BlockSpec((seq_len, ATT_WIDTH), lambda b: (b, 0))]
    out_shape = [jax.ShapeDtypeStruct((batch * seq_len, ATT_WIDTH), BF16)]
    n_keys = seq_len
    aliases = {}
    layer = j
    if latent:
        cache_k, cache_v, cos, sin_signed = cache
        past = cache_k.shape[2]
        n_keys += past
        cspec = pl.BlockSpec((None, None, past * ATT_KV_HEADS, hd), lambda b: (b, j, 0, 0))
        tspec = pl.BlockSpec((seq_len, hd), lambda b: (0, 0))
        in_specs += [cspec, cspec, tspec, tspec]
        args += [cache_k.reshape(batch, N_ATT, past * ATT_KV_HEADS, hd),
                 cache_v.reshape(batch, N_ATT, past * ATT_KV_HEADS, hd), cos, sin_signed]
    else:
        out_shape += [jax.ShapeDtypeStruct((batch, N_ATT, seq_len * ATT_KV_HEADS, hd), F32)] * 2
        if new_kv is None:
            kv_spec = pl.BlockSpec((None, N_ATT, seq_len * ATT_KV_HEADS, hd), lambda b: (b, 0, 0, 0))
        else:
            kv_spec = pl.BlockSpec((None, 1, seq_len * ATT_KV_HEADS, hd), lambda b: (b, j, 0, 0))
            layer = 0
            aliases = {len(args): 1, len(args) + 1: 2}
            in_specs += [pl.BlockSpec(memory_space=pl.ANY)] * 2
            args += list(new_kv)
        out_specs += [kv_spec, kv_spec]
    rows = ATT_REP * ATT_Q_TILE
    return pl.pallas_call(
        functools.partial(_att_kernel, seq_len=seq_len, n_keys=n_keys, latent=latent, n_aliased=len(aliases),
                          layer=layer),
        grid=(batch,),
        in_specs=in_specs,
        out_specs=out_specs,
        out_shape=out_shape,
        input_output_aliases=aliases,
        scratch_shapes=[pltpu.VMEM((ATT_KV_HEADS, n_keys, hd), BF16),
                        pltpu.VMEM((ATT_KV_HEADS, n_keys, 2 * hd), BF16),
                        pltpu.VMEM((rows, hd), BF16), pltpu.VMEM((rows, hd), BF16),
                        pltpu.VMEM((rows, n_keys), F32), pltpu.VMEM((rows, n_keys), F32),
                        pltpu.VMEM((rows, hd), F32), pltpu.VMEM((rows, hd), F32),
                        pltpu.VMEM((rows, 2 * hd), F32), pltpu.VMEM((rows, 2 * hd), F32)],
        compiler_params=_cparams(1),
        name="attention",
    )(*args)


def _rope_tables(n_tokens):
    t = jnp.arange(n_tokens)
    row = (t // GRID_W).astype(F32)
    col = (t % GRID_W).astype(F32)
    inv = ROPE_THETA ** (-jnp.arange(0, AXIS_DIM, 2, dtype=F32) / AXIS_DIM)
    ang = jnp.concatenate([row[:, None] * inv[None, :], col[:, None] * inv[None, :]], axis=-1)
    cos = jnp.repeat(jnp.cos(ang), 2, axis=-1)
    sin = jnp.repeat(jnp.sin(ang), 2, axis=-1)
    sign = jnp.where(jnp.arange(ATT_HEAD_DIM) % 2 == 0, -1.0, 1.0).astype(F32)
    return cos, sin * sign


def kernel(x_prompt, x_sample, c, state_hgrn, cache_k, cache_v, c_ctx, ada_w, ada_b, norm_pre, norm_post, rec_w_in, rec_lb_logits, rec_head_norm, pool_w, pool_scale, rec_w_out, att_w_in, att_q_norm, att_k_norm, att_w_out):
    nb_c, len_c, _ = x_prompt.shape
    nb_l, len_l, _ = x_sample.shape

    lb_p = jax.nn.softmax(rec_lb_logits.astype(F32), axis=0)
    lower_bounds = jnp.clip(jnp.cumsum(lb_p, axis=0) - lb_p[0], 0.0, 1.0)
    lower_bounds = lower_bounds.reshape(N_REC, 2, REC_HEADS, 1, REC_HEAD_DIM)
    cos, sin_signed = _rope_tables(len_l)

    cvec = jnp.zeros((MOD_ROWS, D_MODEL), F32).at[0].set(c_ctx).at[1:1 + nb_l].set(c)
    mod = _modulation(cvec, ada_w, ada_b)

    forget_cols = (2 * POOL_WIDTH + REC_WIDTH, 2 * POOL_WIDTH + 3 * REC_WIDTH)
    layers = []
    for i in range(DEPTH):
        j = i // 2
        rec = i % 2 == 0
        w_in, w_out = (rec_w_in, rec_w_out) if rec else (att_w_in, att_w_out)
        layers.append(dict(
            w_in=w_in[j].astype(BF16), w_out=w_out[j].astype(BF16), f32_cols=forget_cols if rec else (0, 0),
            gain_pre=norm_pre[i].reshape(1, D_MODEL), gain_post=norm_post[i].reshape(1, D_MODEL)))

    def pre_args(i):
        return (layers[i]["w_in"], i, layers[i]["gain_pre"], layers[i]["f32_cols"])

    xs = [x_prompt.reshape(nb_c * len_c, D_MODEL), x_sample.reshape(nb_l * len_l, D_MODEL)]
    streams = ((nb_c, len_c, 0), (nb_l, len_l, 1))
    zs = [None, None]
    ys = [None, None]
    new_states, new_kv = None, None

    def proj_part(s, i, steps=None):
        nb, sl, row0 = streams[s]
        post = ys[s] + (layers[i - 1]["w_out"], i - 1, layers[i - 1]["gain_post"]) if i > 0 else None
        return _proj_part(xs[s], mod, sl, row0, post=post, pre=pre_args(i) if i < DEPTH else None, steps=steps)

    def take_proj(s, i, outs):
        if i > 0:
            xs[s] = outs.pop(0)
        zs[s] = outs

    def rec_part(s, i, heads=None):
        nb, sl, _ = streams[s]
        j = i // 2
        hn = rec_head_norm[j].reshape(1, REC_HEAD_DIM)
        z, zf = zs[s]
        if s == 0:
            return _rec_part(z, zf, nb, sl, lower_bounds[j], hn, j, states=new_states, heads=heads)
        return _rec_part(z, zf, nb, sl, lower_bounds[j], hn, j, s0=state_hgrn, heads=heads)

    def pool(s, i):
        nb, sl, _ = streams[s]
        j = i // 2
        return _pool_mixer(zs[s][0], nb, sl, pool_w[j].astype(BF16), pool_scale[j].reshape(1, POOL_WIDTH))

    take_proj(0, 0, _run(proj_part(0, 0), name="proj")[0])
    for i in range(DEPTH):
        j = i // 2
        if i % 2 == 0:
            rec_c = rec_part(0, i)
            (y_rec, new_states), outs = _run(rec_c, proj_part(1, i, steps=rec_c.steps), name="rec_proj")
            ys[0] = (pool(0, i), 0, y_rec, 0)
            take_proj(1, i, outs)
            rec_l = rec_part(1, i, heads=REC_HEADS * streams[1][0] // rec_c.steps)
            (y_rec,), outs = _run(rec_l, proj_part(0, i + 1, steps=rec_l.steps), name="rec_proj")
            ys[1] = (pool(1, i), 0, y_rec, 0)
            take_proj(0, i + 1, outs)
        else:
            qn = att_q_norm[j].reshape(1, ATT_HEAD_DIM)
            kn = att_k_norm[j].reshape(1, ATT_HEAD_DIM)
            y, *new_kv = _attention(zs[0][0], nb_c, len_c, qn, kn, j, new_kv=new_kv)
            ys[0] = (y, 0, y, 1)
            take_proj(1, i, _run(proj_part(1, i), name="proj")[0])
            (y,) = _attention(zs[1][0], nb_l, len_l, qn, kn, j, cache=(cache_k, cache_v, cos, sin_signed))
            ys[1] = (y, 0, y, 1)
            take_proj(0, i + 1, _run(proj_part(0, i + 1), name="proj")[0])
    take_proj(1, DEPTH, _run(proj_part(1, DEPTH), name="proj")[0])
    xc, xl = xs

    kv_shape = (nb_c, N_ATT, len_c, ATT_KV_HEADS, ATT_HEAD_DIM)
    return (xc.reshape(nb_c, len_c, D_MODEL), xl.reshape(nb_l, len_l, D_MODEL),
            new_states, new_kv[0].reshape(kv_shape), new_kv[1].reshape(kv_shape))
```

```python
import functools
from typing import Any, NamedTuple

import jax
import jax.numpy as jnp
from jax import lax
from jax.experimental import pallas as pl
from jax.experimental.pallas import tpu as pltpu

D_MODEL = 1024
DEPTH = 4
GRID_W = 64
N_REC = (DEPTH + 1) // 2
N_ATT = DEPTH // 2
POOL_WIDTH = D_MODEL // 2
POOL_WINDOWS = (2, 4, 8, 16)
POOL_GROUP_DIM = POOL_WIDTH // len(POOL_WINDOWS)
REC_WIDTH = D_MODEL // 2
REC_HEAD_DIM = 128
REC_HEADS = REC_WIDTH // REC_HEAD_DIM
REC_IN_WIDTH = 2 * POOL_WIDTH + 5 * REC_WIDTH
ATT_HEAD_DIM = 128
ATT_HEADS = D_MODEL // ATT_HEAD_DIM
ATT_KV_HEADS = 2
ATT_REP = ATT_HEADS // ATT_KV_HEADS
ATT_WIDTH = ATT_HEADS * ATT_HEAD_DIM
KV_WIDTH = ATT_KV_HEADS * ATT_HEAD_DIM
ATT_IN_WIDTH = 2 * ATT_WIDTH + 2 * KV_WIDTH
AXIS_DIM = ATT_HEAD_DIM // 2
ROPE_THETA = 10000.0
EPS = 1e-6
F_MIN = 1e-6

MOD_ROWS = 16
TOKEN_TILE = 1024
MIN_PROJ_STEPS = 8
PROJ_GROUP_ROWS = 512
POOL_ROWS = 256
IN_PROJ_STEP = 512
MOD_COLS = 1536
SCAN_CHUNK = 128
SCAN_UNITS = 8
POOL_HALO = 128
ATT_Q_TILE = 128
ATT_KEY_TILE = 128
ATT_STEADY_STEPS = 2
LOG2_E = 1.4426950408889634
VMEM_LIMIT = 56 * 1024 * 1024

F32 = jnp.float32
BF16 = jnp.bfloat16


def _cparams(n_axes):
    return pltpu.CompilerParams(
        dimension_semantics=("arbitrary",) * n_axes, vmem_limit_bytes=VMEM_LIMIT)


def _rms(x, g):
    return x * lax.rsqrt(jnp.mean(x * x, axis=-1, keepdims=True) + EPS) * g


def _dot(a, b):
    return jnp.dot(a, b, preferred_element_type=F32)


def _dot_nt(a, b):
    return lax.dot_general(a, b, (((1,), (1,)), ((), ())), preferred_element_type=F32)


class _Part(NamedTuple):
    kernel: Any
    steps: int
    in_specs: list
    args: list
    out_specs: list
    out_shape: list
    scratch_shapes: list
    aliases: dict


def _run(*parts, name):
    steps = parts[0].steps
    assert all(p.steps == steps for p in parts)
    n_in = [len(p.args) for p in parts]
    n_out = [len(p.out_shape) for p in parts]
    n_scr = [len(p.scratch_shapes) for p in parts]

    def body(*refs):
        ins, outs, scr = refs[:sum(n_in)], refs[sum(n_in):sum(n_in) + sum(n_out)], refs[sum(n_in) + sum(n_out):]
        pending = {}
        for k, p in enumerate(parts):
            i0, o0, s0 = sum(n_in[:k]), sum(n_out[:k]), sum(n_scr[:k])
            pending[k] = (0.0, p.kernel(*ins[i0:i0 + n_in[k]], *outs[o0:o0 + n_out[k]], *scr[s0:s0 + n_scr[k]]))
        while pending:
            k = min(pending, key=lambda k: pending[k][0])
            try:
                pending[k] = (next(pending[k][1]), pending[k][1])
            except StopIteration:
                del pending[k]

    aliases = {}
    for k, p in enumerate(parts):
        for i, o in p.aliases.items():
            aliases[sum(n_in[:k]) + i] = sum(n_out[:k]) + o
    outs = pl.pallas_call(
        body,
        grid=(steps,),
        in_specs=[s for p in parts for s in p.in_specs],
        out_specs=[s for p in parts for s in p.out_specs],
        out_shape=[s for p in parts for s in p.out_shape],
        scratch_shapes=[s for p in parts for s in p.scratch_shapes],
        input_output_aliases=aliases,
        compiler_params=_cparams(1),
        name=name,
    )(*[a for p in parts for a in p.args])
    return [list(outs[sum(n_out[:k]):sum(n_out[:k]) + n_out[k]]) for k in range(len(parts))]


def _mod_kernel(cv_ref, w_ref, b_ref, o_ref):
    cv = cv_ref[...]
    a = (cv * jax.nn.sigmoid(cv)).astype(BF16)
    o_ref[...] = _dot(a, w_ref[...].astype(BF16)) + b_ref[...]
    yield 1.0


def _mod_part(cvec, ada_w, ada_b, layer0, n_layers):
    tn = MOD_COLS
    blocks = 3 * D_MODEL // tn
    in_specs = [
        pl.BlockSpec((MOD_ROWS, D_MODEL), lambda i: (0, 0)),
        pl.BlockSpec((None, D_MODEL, tn), lambda i: (layer0 + i // blocks, 0, i % blocks)),
        pl.BlockSpec((None, 1, tn), lambda i: (layer0 + i // blocks, 0, i % blocks)),
    ]
    out_specs = [pl.BlockSpec((None, MOD_ROWS, tn), lambda i: (i // blocks, 0, i % blocks))]
    out_shape = [jax.ShapeDtypeStruct((n_layers, MOD_ROWS, 3 * D_MODEL), F32)]
    args = [cvec, ada_w, ada_b.reshape(DEPTH, 1, 3 * D_MODEL)]
    return _Part(_mod_kernel, n_layers * blocks, in_specs, args, out_specs, out_shape, [], {})


def _mod_rows(out):
    return out.reshape(out.shape[0], MOD_ROWS, 3, 1, D_MODEL)


def _mod_spec(layer, seq_len, row0, tile):
    assert row0 == 0 or seq_len % tile == 0
    tiles_per_seq = seq_len // tile
    if row0 == 0:
        index = lambda i: (layer, 0, 0, 0, 0)
    else:
        index = lambda i: (layer, row0 + i // tiles_per_seq, 0, 0, 0)
    return pl.BlockSpec((None, None, 3, 1, D_MODEL), index)


def _proj_kernel(*refs, post, pre, n_out, f32_cols):
    refs = list(refs)
    x_ref = refs.pop(0)
    if post:
        ya_ref, yb_ref, wo_ref, modp_ref, gpost_ref = (refs.pop(0) for _ in range(5))
    if pre:
        modn_ref, gpre_ref, wi_ref = (refs.pop(0) for _ in range(3))
    xo_ref = refs.pop(0) if post else None
    rows_all = x_ref.shape[0]
    n_groups = max(1, rows_all // PROJ_GROUP_ROWS)

    def group(rows):
        x = x_ref[rows, :]
        if post:
            half = D_MODEL // 2
            p = _dot(ya_ref[rows, :], wo_ref[0:half, :]) + _dot(yb_ref[rows, :], wo_ref[half:D_MODEL, :])
            x = x + modp_ref[2] * _rms(p, gpost_ref[...])
            xo_ref[rows, :] = x
            yield
        if pre:
            h = (_rms(x, gpre_ref[...]) * (1.0 + modn_ref[1]) + modn_ref[0]).astype(BF16)
            lo, hi = f32_cols
            for c0 in range(0, n_out, IN_PROJ_STEP):
                y = _dot(h, wi_ref[:, c0:c0 + IN_PROJ_STEP])
                if lo <= c0 < hi:
                    refs[1][rows, c0 - lo:c0 - lo + IN_PROJ_STEP] = y
                else:
                    c1 = c0 if c0 < lo else c0 - (hi - lo)
                    refs[0][rows, c1:c1 + IN_PROJ_STEP] = y.astype(BF16)
                yield

    size = rows_all // n_groups
    waiting = [group(slice(k * size, (k + 1) * size)) for k in range(n_groups)]
    running = []
    pieces = n_groups * (post + n_out // IN_PROJ_STEP)
    done = 0
    while waiting or running:
        if waiting:
            running.append(waiting.pop(0))
        for g in list(running):
            try:
                next(g)
                done += 1
            except StopIteration:
                running.remove(g)
        yield min(done / pieces, 1.0)


def _proj_part(x, mod, seq_len, row0, post=None, pre=None, steps=None):
    t = x.shape[0]
    tile = t // steps if steps else min(TOKEN_TILE, t // MIN_PROJ_STEPS)
    half = D_MODEL // 2
    row = lambda i: (i, 0)
    fixed = lambda i: (0, 0)
    resident = dict(index_map=fixed, pipeline_mode=pl.Buffered(1))
    in_specs = [pl.BlockSpec((tile, D_MODEL), row)]
    args = [x]
    out_specs, out_shape = [], []
    n_out, f32_cols = 0, (0, 0)
    if post:
        ya, ia, yb, ib, w_out, layer, gain = post
        in_specs += [pl.BlockSpec((tile, half), lambda i: (i, ia)),
                     pl.BlockSpec((tile, half), lambda i: (i, ib)),
                     pl.BlockSpec((D_MODEL, D_MODEL), **resident),
                     _mod_spec(layer, seq_len, row0, tile),
                     pl.BlockSpec((1, D_MODEL), fixed)]
        args += [ya, yb, w_out, mod, gain]
        out_specs.append(pl.BlockSpec((tile, D_MODEL), row))
        out_shape.append(jax.ShapeDtypeStruct((t, D_MODEL), F32))
    if pre:
        w_in, layer, gain, f32_cols = pre
        n_out = w_in.shape[1]
        lo, hi = f32_cols
        assert lo % IN_PROJ_STEP == 0 and hi % IN_PROJ_STEP == 0 and n_out % IN_PROJ_STEP == 0
        in_specs += [_mod_spec(layer, seq_len, row0, tile),
                     pl.BlockSpec((1, D_MODEL), fixed),
                     pl.BlockSpec((D_MODEL, n_out), **resident)]
        args += [mod, gain, w_in]
        out_specs.append(pl.BlockSpec((tile, n_out - (hi - lo)), row))
        out_shape.append(jax.ShapeDtypeStruct((t, n_out - (hi - lo)), BF16))
        if hi > lo:
            out_specs.append(pl.BlockSpec((tile, hi - lo), row))
            out_shape.append(jax.ShapeDtypeStruct((t, hi - lo), F32))
    kernel = functools.partial(_proj_kernel, post=bool(post), pre=bool(pre), n_out=n_out, f32_cols=f32_cols)
    return _Part(kernel, t // tile, in_specs, args, out_specs, out_shape, [], {})


def _pool_kernel(u_ref, gp_ref, pw_ref, ps_ref, o_ref, pad_ref, band_ref, sum_ref, dif_ref, *, seq_len):
    gd = POOL_GROUP_DIM
    rows = POOL_ROWS
    halo = POOL_HALO
    span = rows + 2 * halo

    @pl.when(pl.program_id(0) == 0)
    def _():
        r = lax.broadcasted_iota(jnp.int32, (rows, span), 0)
        c = lax.broadcasted_iota(jnp.int32, (rows, span), 1)
        offset = c - halo - r
        for g, win in enumerate(POOL_WINDOWS):
            band_ref[g] = jnp.where((offset >= -(win // 2)) & (offset < win // 2), 1.0, 0.0).astype(BF16)

    zeros = jnp.zeros((halo, POOL_WIDTH), BF16)
    pad_ref[0:halo, :] = zeros
    pad_ref[halo + seq_len:2 * halo + seq_len, :] = zeros
    pad_ref[halo:halo + seq_len, :] = u_ref[...]
    units = [(g, r * rows) for g in range(len(POOL_WINDOWS)) for r in range(seq_len // rows)]

    def cols(g):
        return slice(g * gd, (g + 1) * gd)

    for g, base in units:
        sum_ref[base:base + rows, cols(g)] = _dot(band_ref[g], pad_ref[base:base + span, cols(g)])
    for g, base in units:
        win = POOL_WINDOWS[g]
        t = base + lax.broadcasted_iota(jnp.int32, (rows, 1), 0)
        count = jnp.clip(t + win // 2, 0, seq_len) - jnp.clip(t - win // 2, 0, seq_len)
        mean = sum_ref[base:base + rows, cols(g)] / count.astype(F32)
        d = mean - pad_ref[halo + base:halo + base + rows, cols(g)].astype(F32)
        dif_ref[base:base + rows, cols(g)] = d.astype(BF16)
    for g, base in units:
        y = _dot(dif_ref[base:base + rows, cols(g)], pw_ref[g]) * ps_ref[:, cols(g)]
        gate = gp_ref[base:base + rows, cols(g)].astype(F32)
        o_ref[base:base + rows, cols(g)] = (y * (gate * jax.nn.sigmoid(gate))).astype(BF16)


def _pool_mixer(z, batch, seq_len, pool_w, pool_scale):
    return pl.pallas_call(
        functools.partial(_pool_kernel, seq_len=seq_len),
        grid=(batch,),
        in_specs=[
            pl.BlockSpec((seq_len, POOL_WIDTH), lambda b: (b, 0)),
            pl.BlockSpec((seq_len, POOL_WIDTH), lambda b: (b, 1)),
            pl.BlockSpec((len(POOL_WINDOWS), POOL_GROUP_DIM, POOL_GROUP_DIM), lambda b: (0, 0, 0)),
            pl.BlockSpec((1, POOL_WIDTH), lambda b: (0, 0)),
        ],
        out_specs=pl.BlockSpec((seq_len, POOL_WIDTH), lambda b: (b, 0)),
        out_shape=jax.ShapeDtypeStruct((batch * seq_len, POOL_WIDTH), BF16),
        scratch_shapes=[pltpu.VMEM((seq_len + 2 * POOL_HALO, POOL_WIDTH), BF16),
                        pltpu.VMEM((len(POOL_WINDOWS), POOL_ROWS, POOL_ROWS + 2 * POOL_HALO), BF16),
                        pltpu.VMEM((seq_len, POOL_WIDTH), F32), pltpu.VMEM((seq_len, POOL_WIDTH), BF16)],
        compiler_params=_cparams(1),
        name="pool_mixer",
    )(z, z, pool_w, pool_scale)


def _block_row(p, block, row):
    c, w = p.shape
    p3 = p.reshape(c // block, block, w)
    return jnp.broadcast_to(p3[:, row:row + 1, :], p3.shape).reshape(c, w)


def _interleave(lo, hi, block):
    half = block // 2
    parts = []
    for b in range(SCAN_CHUNK // block):
        parts.append(lo[b * block:b * block + half])
        parts.append(hi[b * block + half:(b + 1) * block])
    return jnp.concatenate(parts, axis=0)


def _select_levels(products, rev):
    c = SCAN_CHUNK
    lane = lax.broadcasted_iota(jnp.int32, (8, c), 1)
    sub = lax.broadcasted_iota(jnp.int32, (8, c), 0)
    out = []
    for t0 in range(0, c, 8):
        if rev:
            keep = (lane >= t0 + sub) & (lane < t0 + 8)
        else:
            keep = (lane >= t0) & (lane <= t0 + sub)
        row = jnp.where(keep, products[0][t0:t0 + 8, :], 0.0)
        for lv, block in enumerate((16, 32, 64, 128), 1):
            half = block // 2
            start = t0 // block * block
            upper = t0 - start >= half
            if upper == rev:
                continue
            lo = start + half if rev else start
            row = jnp.where((lane >= lo) & (lane < lo + half), products[lv][t0:t0 + 8, :], row)
        out.append(row)
    return jnp.concatenate(out, axis=0)


def _rec_kernel(*refs, seq_len, heads, has_s0, out_state, n_aliased, layer):
    q_ref, ff_ref, fb_ref, v_ref, gr_ref, lb_ref, hn_ref = refs[:7]
    rest = list(refs[7:])
    s0_ref = rest.pop(0) if has_s0 else None
    del rest[:n_aliased]
    o_ref = rest.pop(0)
    st_ref = rest.pop(0) if out_state else None
    acc_ref, qs_ref, k_ref, p_ref, a_ref, qin_ref, u_ref, dec_ref, tri_ref = rest

    c = SCAN_CHUNK
    hd = REC_HEAD_DIM
    n_chunks = seq_len // c
    zf_refs = (ff_ref, fb_ref)
    units = [(h, n) for h in range(heads) for n in range(n_chunks)]

    def rows_in(n):
        return slice(n * c, (n + 1) * c)

    def rows_sc(h, n):
        return slice((h * n_chunks + n) * c, (h * n_chunks + n + 1) * c)

    def cols(h):
        return slice(h * hd, (h + 1) * hd)

    ti = lax.broadcasted_iota(jnp.int32, (c, c), 0)
    si = lax.broadcasted_iota(jnp.int32, (c, c), 1)
    for d, causal in enumerate((si <= ti, si >= ti)):
        tri_ref[d] = jnp.where(causal, 1.0, 0.0).astype(BF16)

    weights = (0.26, 0.43, 0.09, 0.22)
    done = [0.0]

    def progress(stage, share):
        done[0] += weights[stage] * share
        return done[0]

    for h, n in units:
        qz = q_ref[rows_in(n), cols(h)].astype(F32)
        qs_ref[rows_sc(h, n), :] = qz * jax.nn.sigmoid(qz)
        for d in range(2):
            lower = lb_ref[d, h]
            f = jnp.clip(lower + (1.0 - lower) * jax.nn.sigmoid(zf_refs[d][rows_in(n), cols(h)]), F_MIN, 1.0)
            k_ref[d, rows_sc(h, n), :] = 1.0 - f
            g = jnp.log(f) * LOG2_E
            g_hi = g.astype(BF16)
            rest = g - g_hi.astype(F32)
            g_mid = rest.astype(BF16)
            g_lo = (rest - g_mid.astype(F32)).astype(BF16)
            sums = _dot(tri_ref[d], jnp.concatenate([g_hi, g_mid, g_lo], axis=1))
            p_ref[d, rows_sc(h, n), :] = (sums[:, 0:hd] + sums[:, hd:2 * hd]) + sums[:, 2 * hd:3 * hd]
        yield progress(0, 1 / len(units))

    for h, n in units:
        rows = rows_sc(h, n)
        q = qs_ref[rows, :]
        v_t = v_ref[rows_in(n), cols(h)].astype(F32).T.astype(BF16)
        for d in range(2):
            rev = d == 1
            k = k_ref[d, rows, :]
            p = p_ref[d, rows, :]
            e0 = p - _block_row(p, 8, 4 if rev else 3)
            products = [_dot_nt((q * jnp.exp2(e0)).astype(BF16), (k * jnp.exp2(-e0)).astype(BF16))]
            for block in (16, 32, 64, 128):
                beta = _block_row(p, block, block // 2 if rev else block // 2 - 1)
                if rev:
                    e = _interleave(p, beta, block) - _interleave(beta, p, block)
                    src = _interleave(q, k, block)
                else:
                    e = _interleave(beta, p, block) - _interleave(p, beta, block)
                    src = _interleave(k, q, block)
                m = (src * jnp.exp2(e)).astype(BF16)
                products.append(_dot_nt(m, m))
            a_ref[rows, d * c:(d + 1) * c] = _select_levels(products, rev).astype(BF16)

            edge = p[0:1, :] if rev else p[c - 1:c, :]
            qin_ref[d, rows, :] = (q * jnp.exp2(p)).astype(BF16)
            u_ref[d, h * n_chunks + n] = _dot(v_t, (k * jnp.exp2(edge - p)).astype(BF16))
            dec_ref[d, h * n_chunks + n] = jnp.broadcast_to(jnp.exp2(edge), (8, hd))
        yield progress(1, 1 / len(units))

    for h, n in units:
        v_b = v_ref[rows_in(n), cols(h)]
        acc_ref[rows_sc(h, n), :] = _dot(a_ref[rows_sc(h, n), :], jnp.concatenate([v_b, v_b], axis=0))
        yield progress(2, 1 / len(units))

    for h in range(heads):
        if has_s0:
            states = [s0_ref[0, h].T, s0_ref[1, h].T]
        else:
            states = [jnp.zeros((hd, hd), F32)] * 2
        inter = [[None] * n_chunks, [None] * n_chunks]
        for i in range(n_chunks):
            for d, n in ((0, i), (1, n_chunks - 1 - i)):
                inter[d][n] = _dot_nt(qin_ref[d, rows_sc(h, n), :], states[d].astype(BF16))
                states[d] = dec_ref[d, h * n_chunks + n][0:1, :] * states[d] + u_ref[d, h * n_chunks + n]
        for n in range(n_chunks):
            gate = gr_ref[rows_in(n), cols(h)].astype(F32)
            o = acc_ref[rows_sc(h, n), :] + inter[0][n] + inter[1][n]
            o_ref[rows_in(n), cols(h)] = (_rms(o, hn_ref[...]) * (gate * jax.nn.sigmoid(gate))).astype(BF16)
        if out_state:
            st_ref[layer, 0, h] = states[0].T
            st_ref[layer, 1, h] = states[1].T
        yield progress(3, 1 / heads)
    if out_state:
        for other in range(st_ref.shape[0]):
            if other != layer:
                st_ref[other] = jnp.zeros(st_ref.shape[1:], F32)


def _rec_part(z, zf, batch, seq_len, lower, head_norm, j, s0=None, states=None, heads=None):
    hd = REC_HEAD_DIM
    nh = REC_HEADS
    n_chunks = seq_len // SCAN_CHUNK
    heads = heads or min(nh, max(1, SCAN_UNITS // n_chunks))
    width = heads * hd
    col0 = 2 * POOL_WIDTH // width
    groups = nh // heads

    def zspec(part):
        return pl.BlockSpec((seq_len, width), lambda i: (i // groups, col0 + part * groups + i % groups))

    def fspec(part):
        return pl.BlockSpec((seq_len, width), lambda i: (i // groups, part * groups + i % groups))

    in_specs = [zspec(0), fspec(0), fspec(1), zspec(1), zspec(2),
                pl.BlockSpec((2, heads, 1, hd), lambda i: (0, i % groups, 0, 0)),
                pl.BlockSpec((1, hd), lambda i: (0, 0))]
    args = [z, zf, zf, z, z, lower, head_norm]
    has_s0 = s0 is not None
    if has_s0:
        in_specs.append(pl.BlockSpec((None, None, 2, heads, hd, hd),
                                     lambda i: (i // groups, j, 0, i % groups, 0, 0)))
        args.append(s0)
    out_state = not has_s0
    out_specs = [pl.BlockSpec((seq_len, width), lambda i: (i // groups, i % groups))]
    out_shape = [jax.ShapeDtypeStruct((batch * seq_len, REC_WIDTH), BF16)]
    aliases = {}
    layer = j
    if out_state:
        out_shape.append(jax.ShapeDtypeStruct((batch, N_REC, 2, nh, hd, hd), F32))
        if states is None:
            out_specs.append(pl.BlockSpec((None, N_REC, 2, heads, hd, hd),
                                          lambda i: (i // groups, 0, 0, i % groups, 0, 0)))
        else:
            out_specs.append(pl.BlockSpec((None, 1, 2, heads, hd, hd),
                                          lambda i: (i // groups, j, 0, i % groups, 0, 0)))
            layer = 0
            aliases[len(args)] = 1
            in_specs.append(pl.BlockSpec(memory_space=pl.ANY))
            args.append(states)
    tokens = heads * seq_len
    kernel = functools.partial(_rec_kernel, seq_len=seq_len, heads=heads, has_s0=has_s0, out_state=out_state,
                               n_aliased=len(aliases), layer=layer)
    scratch_shapes = [pltpu.VMEM((tokens, hd), F32),
                      pltpu.VMEM((tokens, hd), F32),
                      pltpu.VMEM((2, tokens, hd), F32),
                      pltpu.VMEM((2, tokens, hd), F32),
                      pltpu.VMEM((tokens, 2 * SCAN_CHUNK), BF16),
                      pltpu.VMEM((2, tokens, hd), BF16),
                      pltpu.VMEM((2, tokens // SCAN_CHUNK, hd, hd), F32),
                      pltpu.VMEM((2, tokens // SCAN_CHUNK, 8, hd), F32),
                      pltpu.VMEM((2, SCAN_CHUNK, SCAN_CHUNK), BF16)]
    return _Part(kernel, batch * groups, in_specs, args, out_specs, out_shape, scratch_shapes, aliases)


def _rope(x, cos, sin_signed):
    lane = lax.broadcasted_iota(jnp.int32, x.shape, 1)
    partner = jnp.where((lane & 1) == 0, pltpu.roll(x, ATT_HEAD_DIM - 1, 1), pltpu.roll(x, 1, 1))
    return x * cos + partner * sin_signed


def _att_kernel(*refs, seq_len, n_keys, latent, n_aliased, layer):
    q_ref, k_ref, v_ref, qn_ref, kn_ref = refs[:5]
    g_refs = refs[5:5 + ATT_KV_HEADS]
    if latent:
        ck_ref, cv_ref, cos_ref, sin_ref, y_ref, kall, vall = refs[5 + ATT_KV_HEADS:-8]
    else:
        y_ref, ko_ref, vo_ref, kall, vall = refs[5 + ATT_KV_HEADS + n_aliased:-8]
    qs_bufs, s_bufs, m_bufs, o_bufs = refs[-8:-6], refs[-6:-4], refs[-4:-2], refs[-2:]
    hd = ATT_HEAD_DIM
    qw = ATT_REP * hd
    tq = ATT_Q_TILE
    tk = min(ATT_KEY_TILE, n_keys)
    n_kb = n_keys // tk
    n_past = n_keys - seq_len
    n_items = ATT_KV_HEADS * (seq_len // tq)
    exp2_scale = hd ** -0.5 * LOG2_E
    assert ATT_KV_HEADS == 2

    if not latent:
        for other in range(ko_ref.shape[0]):
            if other != layer:
                ko_ref[other] = jnp.zeros(ko_ref.shape[1:], F32)
                vo_ref[other] = jnp.zeros(vo_ref.shape[1:], F32)
    lane = lax.broadcasted_iota(jnp.int32, (n_keys, hd), 1)
    for g in range(ATT_KV_HEADS):
        k = _rms(k_ref[:, g * hd:(g + 1) * hd].astype(F32), kn_ref[...])
        v = v_ref[:, g * hd:(g + 1) * hd].astype(F32)
        if latent:
            k = _rope(k, cos_ref[...], sin_ref[...])
            kall[g, 0:n_past, :] = ck_ref[pl.ds(g, n_past, stride=ATT_KV_HEADS), :].astype(BF16)
            vall[g, 0:n_past, 0:hd] = cv_ref[pl.ds(g, n_past, stride=ATT_KV_HEADS), :].astype(BF16)
        else:
            ko_ref[layer, pl.ds(g, seq_len, stride=ATT_KV_HEADS), :] = k
            vo_ref[layer, pl.ds(g, seq_len, stride=ATT_KV_HEADS), :] = v
        kall[g, n_past:n_keys, :] = k.astype(BF16)
        vall[g, n_past:n_keys, 0:hd] = v.astype(BF16)
        vall[g, :, hd:2 * hd] = (lane == 0).astype(BF16)

    def tile_rows(item):
        return pl.ds(pl.multiple_of((item // ATT_KV_HEADS) * tq, tq), tq)

    def prep(item, g):
        rows = tile_rows(item)
        for h in range(ATT_REP):
            q = _rms(q_ref[rows, g * qw + h * hd:g * qw + (h + 1) * hd].astype(F32), qn_ref[...])
            if latent:
                q = _rope(q, cos_ref[rows, :], sin_ref[rows, :])
            qs_bufs[g][h * tq:(h + 1) * tq, :] = q.astype(BF16)

    def scores(item, g):
        qs = qs_bufs[g][...]
        mx = None
        for kb in range(n_kb):
            s = _dot_nt(qs, kall[g, kb * tk:(kb + 1) * tk, :])
            s_bufs[g][:, kb * tk:(kb + 1) * tk] = s
            for c in range(tk // hd):
                part = s[:, c * hd:(c + 1) * hd]
                mx = part if mx is None else jnp.maximum(mx, part)
        m = jnp.max(mx, axis=-1, keepdims=True) * exp2_scale
        m_bufs[g][...] = jnp.broadcast_to(m, (ATT_REP * tq, hd))

    def mix(item, g):
        m = m_bufs[g][...]
        ps = []
        for c in range(n_keys // hd):
            s = s_bufs[g][:, c * hd:(c + 1) * hd]
            ps.append(jnp.exp2(s * exp2_scale - m).astype(BF16))
        o_bufs[g][...] = _dot(jnp.concatenate(ps, axis=1), vall[g])

    def finish(item, g):
        rows = tile_rows(item)
        o = o_bufs[g][:, 0:hd] / o_bufs[g][:, hd:hd + 1]
        for h in range(ATT_REP):
            gate = g_refs[g][rows, h * hd:(h + 1) * hd].astype(F32)
            y = o[h * tq:(h + 1) * tq, :] * (gate * jax.nn.sigmoid(gate))
            y_ref[rows, g * qw + h * hd:g * qw + (h + 1) * hd] = y.astype(BF16)

    stages = (prep, scores, mix, finish)

    def step(k, parity):
        for a, stage in enumerate(stages):
            item = k - a
            if isinstance(item, int) and not 0 <= item < n_items:
                continue
            stage(item, (parity + a) % 2)

    depth = len(stages) - 1
    for k in range(min(depth, n_items + depth)):
        step(k, k % 2)
    n_steady = max(n_items - depth, 0)

    group = ATT_STEADY_STEPS

    def steady_group(j, carry):
        for u in range(group):
            step(depth + group * j + u, (depth + u) % 2)
        return carry

    lax.fori_loop(0, n_steady // group, steady_group, 0)
    for k in range(depth + group * (n_steady // group), n_items + depth):
        step(k, k % 2)


def _attention(z, batch, seq_len, q_norm, k_norm, j, cache=None, new_kv=None):
    hd = ATT_HEAD_DIM
    latent = cache is not None
    qw = ATT_REP * hd
    kv0 = ATT_WIDTH // KV_WIDTH
    gate0 = (ATT_WIDTH + 2 * KV_WIDTH) // qw
    assert (ATT_WIDTH + 2 * KV_WIDTH) % qw == 0 and ATT_WIDTH % KV_WIDTH == 0
    in_specs = [
        pl.BlockSpec((seq_len, ATT_WIDTH), lambda b: (b, 0)),
        pl.BlockSpec((seq_len, KV_WIDTH), lambda b: (b, kv0)),
        pl.BlockSpec((seq_len, KV_WIDTH), lambda b: (b, kv0 + 1)),
        pl.BlockSpec((1, hd), lambda b: (0, 0)),
        pl.BlockSpec((1, hd), lambda b: (0, 0)),
    ] + [pl.BlockSpec((seq_len, qw), lambda b, g=g: (b, gate0 + g)) for g in range(ATT_KV_HEADS)]
    args = [z, z, z, q_norm, k_norm] + [z] * ATT_KV_HEADS
    out_specs = [pl.BlockSpec((seq_len, ATT_WIDTH), lambda b: (b, 0))]
    out_shape = [jax.ShapeDtypeStruct((batch * seq_len, ATT_WIDTH), BF16)]
    n_keys = seq_len
    aliases = {}
    layer = j
    if latent:
        cache_k, cache_v, cos, sin_signed = cache
        past = cache_k.shape[2]
        n_keys += past
        cspec = pl.BlockSpec((None, None, past * ATT_KV_HEADS, hd), lambda b: (b, j, 0, 0))
        tspec = pl.BlockSpec((seq_len, hd), lambda b: (0, 0))
        in_specs += [cspec, cspec, tspec, tspec]
        args += [cache_k.reshape(batch, N_ATT, past * ATT_KV_HEADS, hd),
                 cache_v.reshape(batch, N_ATT, past * ATT_KV_HEADS, hd), cos, sin_signed]
    else:
        out_shape += [jax.ShapeDtypeStruct((batch, N_ATT, seq_len * ATT_KV_HEADS, hd), F32)] * 2
        if new_kv is None:
            kv_spec = pl.BlockSpec((None, N_ATT, seq_len * ATT_KV_HEADS, hd), lambda b: (b, 0, 0, 0))
        else:
            kv_spec = pl.BlockSpec((None, 1, seq_len * ATT_KV_HEADS, hd), lambda b: (b, j, 0, 0))
            layer = 0
            aliases = {len(args): 1, len(args) + 1: 2}
            in_specs += [pl.BlockSpec(memory_space=pl.ANY)] * 2
            args += list(new_kv)
        out_specs += [kv_spec, kv_spec]
    rows = ATT_REP * ATT_Q_TILE
    return pl.pallas_call(
        functools.partial(_att_kernel, seq_len=seq_len, n_keys=n_keys, latent=latent, n_aliased=len(aliases),
                          layer=layer),
        grid=(batch,),
        in_specs=in_specs,
        out_specs=out_specs,
        out_shape=out_shape,
        input_output_aliases=aliases,
        scratch_shapes=[pltpu.VMEM((ATT_KV_HEADS, n_keys, hd), BF16),
                        pltpu.VMEM((ATT_KV_HEADS, n_keys, 2 * hd), BF16),
                        pltpu.VMEM((rows, hd), BF16), pltpu.VMEM((rows, hd), BF16),
                        pltpu.VMEM((rows, n_keys), F32), pltpu.VMEM((rows, n_keys), F32),
                        pltpu.VMEM((rows, hd), F32), pltpu.VMEM((rows, hd), F32),
                        pltpu.VMEM((rows, 2 * hd), F32), pltpu.VMEM((rows, 2 * hd), F32)],
        compiler_params=_cparams(1),
        name="attention",
    )(*args)


def _rope_tables(n_tokens):
    t = jnp.arange(n_tokens)
    row = (t // GRID_W).astype(F32)
    col = (t % GRID_W).astype(F32)
    inv = ROPE_THETA ** (-jnp.arange(0, AXIS_DIM, 2, dtype=F32) / AXIS_DIM)
    ang = jnp.concatenate([row[:, None] * inv[None, :], col[:, None] * inv[None, :]], axis=-1)
    cos = jnp.repeat(jnp.cos(ang), 2, axis=-1)
    sin = jnp.repeat(jnp.sin(ang), 2, axis=-1)
    sign = jnp.where(jnp.arange(ATT_HEAD_DIM) % 2 == 0, -1.0, 1.0).astype(F32)
    return cos, sin * sign


def kernel(x_prompt, x_sample, c, state_hgrn, cache_k, cache_v, c_ctx, ada_w, ada_b, norm_pre, norm_post, rec_w_in, rec_lb_logits, rec_head_norm, pool_w, pool_scale, rec_w_out, att_w_in, att_q_norm, att_k_norm, att_w_out):
    nb_c, len_c, _ = x_prompt.shape
    nb_l, len_l, _ = x_sample.shape

    lb_p = jax.nn.softmax(rec_lb_logits.astype(F32), axis=0)
    lower_bounds = jnp.clip(jnp.cumsum(lb_p, axis=0) - lb_p[0], 0.0, 1.0)
    lower_bounds = lower_bounds.reshape(N_REC, 2, REC_HEADS, 1, REC_HEAD_DIM)
    cos, sin_signed = _rope_tables(len_l)

    cvec = jnp.zeros((MOD_ROWS, D_MODEL), F32).at[0].set(c_ctx).at[1:1 + nb_l].set(c)

    forget_cols = (2 * POOL_WIDTH + REC_WIDTH, 2 * POOL_WIDTH + 3 * REC_WIDTH)
    layers = []
    for i in range(DEPTH):
        j = i // 2
        rec = i % 2 == 0
        w_in, w_out = (rec_w_in, rec_w_out) if rec else (att_w_in, att_w_out)
        layers.append(dict(
            w_in=w_in[j].astype(BF16), w_out=w_out[j].astype(BF16), f32_cols=forget_cols if rec else (0, 0),
            gain_pre=norm_pre[i].reshape(1, D_MODEL), gain_post=norm_post[i].reshape(1, D_MODEL)))

    def pre_args(i):
        return (layers[i]["w_in"], i, layers[i]["gain_pre"], layers[i]["f32_cols"])

    xs = [x_prompt.reshape(nb_c * len_c, D_MODEL), x_sample.reshape(nb_l * len_l, D_MODEL)]
    streams = ((nb_c, len_c, 0), (nb_l, len_l, 1))
    zs = [None, None]
    ys = [None, None]
    new_states, new_kv = None, None

    def proj_part(s, i, steps=None, mod=None):
        nb, sl, row0 = streams[s]
        post = ys[s] + (layers[i - 1]["w_out"], i - 1, layers[i - 1]["gain_post"]) if i > 0 else None
        return _proj_part(xs[s], mods if mod is None else mod, sl, row0, post=post,
                          pre=pre_args(i) if i < DEPTH else None, steps=steps)

    def take_proj(s, i, outs):
        if i > 0:
            xs[s] = outs.pop(0)
        zs[s] = outs

    def rec_part(s, i, heads=None):
        nb, sl, _ = streams[s]
        j = i // 2
        hn = rec_head_norm[j].reshape(1, REC_HEAD_DIM)
        z, zf = zs[s]
        if s == 0:
            return _rec_part(z, zf, nb, sl, lower_bounds[j], hn, j, states=new_states, heads=heads)
        return _rec_part(z, zf, nb, sl, lower_bounds[j], hn, j, s0=state_hgrn, heads=heads)

    def pool(s, i):
        nb, sl, _ = streams[s]
        j = i // 2
        return _pool_mixer(zs[s][0], nb, sl, pool_w[j].astype(BF16), pool_scale[j].reshape(1, POOL_WIDTH))

    mod0 = _mod_rows(_run(_mod_part(cvec, ada_w, ada_b, 0, 1), name="modulation")[0][0])
    first = proj_part(0, 0, mod=mod0)
    assert first.steps == DEPTH * 3 * D_MODEL // MOD_COLS
    outs, (mods,) = _run(first, _mod_part(cvec, ada_w, ada_b, 0, DEPTH), name="proj_modulation")
    mods = _mod_rows(mods)
    take_proj(0, 0, outs)
    for i in range(DEPTH):
        j = i // 2
        if i % 2 == 0:
            rec_c = rec_part(0, i)
            (y_rec, new_states), outs = _run(rec_c, proj_part(1, i, steps=rec_c.steps), name="rec_proj")
            ys[0] = (pool(0, i), 0, y_rec, 0)
            take_proj(1, i, outs)
            rec_l = rec_part(1, i, heads=REC_HEADS * streams[1][0] // rec_c.steps)
            (y_rec,), outs = _run(rec_l, proj_part(0, i + 1, steps=rec_l.steps), name="rec_proj")
            ys[1] = (pool(1, i), 0, y_rec, 0)
            take_proj(0, i + 1, outs)
        else:
            qn = att_q_norm[j].reshape(1, ATT_HEAD_DIM)
            kn = att_k_norm[j].reshape(1, ATT_HEAD_DIM)
            y, *new_kv = _attention(zs[0][0], nb_c, len_c, qn, kn, j, new_kv=new_kv)
            ys[0] = (y, 0, y, 1)
            take_proj(1, i, _run(proj_part(1, i), name="proj")[0])
            (y,) = _attention(zs[1][0], nb_l, len_l, qn, kn, j, cache=(cache_k, cache_v, cos, sin_signed))
            ys[1] = (y, 0, y, 1)
            take_proj(0, i + 1, _run(proj_part(0, i + 1), name="proj")[0])
    take_proj(1, DEPTH, _run(proj_part(1, DEPTH), name="proj")[0])
    xc, xl = xs

    kv_shape = (nb_c, N_ATT, len_c, ATT_KV_HEADS, ATT_HEAD_DIM)
    return (xc.reshape(nb_c, len_c, D_MODEL), xl.reshape(nb_l, len_l, D_MODEL),
            new_states, new_kv[0].reshape(kv_shape), new_kv[1].reshape(kv_shape))
```

```python
import functools
from typing import Any, NamedTuple

import jax
import jax.numpy as jnp
from jax import lax
from jax.experimental import pallas as pl
from jax.experimental.pallas import tpu as pltpu

D_MODEL = 1024
DEPTH = 4
GRID_W = 64
N_REC = (DEPTH + 1) // 2
N_ATT = DEPTH // 2
POOL_WIDTH = D_MODEL // 2
POOL_WINDOWS = (2, 4, 8, 16)
POOL_GROUP_DIM = POOL_WIDTH // len(POOL_WINDOWS)
REC_WIDTH = D_MODEL // 2
REC_HEAD_DIM = 128
REC_HEADS = REC_WIDTH // REC_HEAD_DIM
REC_IN_WIDTH = 2 * POOL_WIDTH + 5 * REC_WIDTH
ATT_HEAD_DIM = 128
ATT_HEADS = D_MODEL // ATT_HEAD_DIM
ATT_KV_HEADS = 2
ATT_REP = ATT_HEADS // ATT_KV_HEADS
ATT_WIDTH = ATT_HEADS * ATT_HEAD_DIM
KV_WIDTH = ATT_KV_HEADS * ATT_HEAD_DIM
ATT_IN_WIDTH = 2 * ATT_WIDTH + 2 * KV_WIDTH
AXIS_DIM = ATT_HEAD_DIM // 2
ROPE_THETA = 10000.0
EPS = 1e-6
F_MIN = 1e-6

MOD_ROWS = 16
TOKEN_TILE = 1024
MIN_PROJ_STEPS = 8
PROJ_GROUP_ROWS = 512
POOL_ROWS = 256
IN_PROJ_STEP = 512
MOD_COLS = 1536
SCAN_CHUNK = 128
SCAN_UNITS = 8
POOL_HALO = 128
ATT_Q_TILE = 128
ATT_KEY_TILE = 128
ATT_STEADY_STEPS = 2
LOG2_E = 1.4426950408889634
VMEM_LIMIT = 56 * 1024 * 1024

F32 = jnp.float32
BF16 = jnp.bfloat16


def _cparams(n_axes):
    return pltpu.CompilerParams(
        dimension_semantics=("arbitrary",) * n_axes, vmem_limit_bytes=VMEM_LIMIT)


def _rms(x, g):
    return x * lax.rsqrt(jnp.mean(x * x, axis=-1, keepdims=True) + EPS) * g


def _dot(a, b):
    return jnp.dot(a, b, preferred_element_type=F32)


def _dot_nt(a, b):
    return lax.dot_general(a, b, (((1,), (1,)), ((), ())), preferred_element_type=F32)


class _Part(NamedTuple):
    kernel: Any
    steps: int
    in_specs: list
    args: list
    out_specs: list
    out_shape: list
    scratch_shapes: list
    aliases: dict


def _run(*parts, name):
    steps = parts[0].steps
    assert all(p.steps == steps for p in parts)
    n_in = [len(p.args) for p in parts]
    n_out = [len(p.out_shape) for p in parts]
    n_scr = [len(p.scratch_shapes) for p in parts]

    def body(*refs):
        ins, outs, scr = refs[:sum(n_in)], refs[sum(n_in):sum(n_in) + sum(n_out)], refs[sum(n_in) + sum(n_out):]
        pending = {}
        for k, p in enumerate(parts):
            i0, o0, s0 = sum(n_in[:k]), sum(n_out[:k]), sum(n_scr[:k])
            pending[k] = (0.0, p.kernel(*ins[i0:i0 + n_in[k]], *outs[o0:o0 + n_out[k]], *scr[s0:s0 + n_scr[k]]))
        while pending:
            k = min(pending, key=lambda k: pending[k][0])
            try:
                pending[k] = (next(pending[k][1]), pending[k][1])
            except StopIteration:
                del pending[k]

    aliases = {}
    for k, p in enumerate(parts):
        for i, o in p.aliases.items():
            aliases[sum(n_in[:k]) + i] = sum(n_out[:k]) + o
    outs = pl.pallas_call(
        body,
        grid=(steps,),
        in_specs=[s for p in parts for s in p.in_specs],
        out_specs=[s for p in parts for s in p.out_specs],
        out_shape=[s for p in parts for s in p.out_shape],
        scratch_shapes=[s for p in parts for s in p.scratch_shapes],
        input_output_aliases=aliases,
        compiler_params=_cparams(1),
        name=name,
    )(*[a for p in parts for a in p.args])
    return [list(outs[sum(n_out[:k]):sum(n_out[:k]) + n_out[k]]) for k in range(len(parts))]


def _mod_kernel(cv_ref, w_ref, b_ref, o_ref):
    cv = cv_ref[...]
    a = (cv * jax.nn.sigmoid(cv)).astype(BF16)
    o_ref[...] = _dot(a, w_ref[...].astype(BF16)) + b_ref[...]
    yield 1.0


def _mod_part(cvec, ada_w, ada_b, layer0, n_layers):
    tn = MOD_COLS
    blocks = 3 * D_MODEL // tn
    in_specs = [
        pl.BlockSpec((MOD_ROWS, D_MODEL), lambda i: (0, 0)),
        pl.BlockSpec((None, D_MODEL, tn), lambda i: (layer0 + i // blocks, 0, i % blocks)),
        pl.BlockSpec((None, 1, tn), lambda i: (layer0 + i // blocks, 0, i % blocks)),
    ]
    out_specs = [pl.BlockSpec((None, MOD_ROWS, tn), lambda i: (i // blocks, 0, i % blocks))]
    out_shape = [jax.ShapeDtypeStruct((n_layers, MOD_ROWS, 3 * D_MODEL), F32)]
    args = [cvec, ada_w, ada_b.reshape(DEPTH, 1, 3 * D_MODEL)]
    return _Part(_mod_kernel, n_layers * blocks, in_specs, args, out_specs, out_shape, [], {})


def _cast_kernel(*refs):
    n = len(refs) // 2
    for w_ref, o_ref in zip(refs[:n], refs[n:]):
        o_ref[...] = w_ref[...].astype(BF16)
    yield 1.0


def _cast_part(sources, steps):
    in_specs, out_specs, out_shape, args = [], [], [], []
    for stack, j in sources:
        _, k, n = stack.shape
        in_specs.append(pl.BlockSpec((None, k // steps, n), lambda i, j=j: (j, i, 0)))
        out_specs.append(pl.BlockSpec((k // steps, n), lambda i: (i, 0)))
        out_shape.append(jax.ShapeDtypeStruct((k, n), BF16))
        args.append(stack)
    return _Part(_cast_kernel, steps, in_specs, args, out_specs, out_shape, [], {})


def _mod_rows(out):
    return out.reshape(out.shape[0], MOD_ROWS, 3, 1, D_MODEL)


def _mod_spec(layer, seq_len, row0, tile):
    assert row0 == 0 or seq_len % tile == 0
    tiles_per_seq = seq_len // tile
    if row0 == 0:
        index = lambda i: (layer, 0, 0, 0, 0)
    else:
        index = lambda i: (layer, row0 + i // tiles_per_seq, 0, 0, 0)
    return pl.BlockSpec((None, None, 3, 1, D_MODEL), index)


def _proj_kernel(*refs, post, pre, n_out, f32_cols):
    refs = list(refs)
    x_ref = refs.pop(0)
    if post:
        ya_ref, yb_ref, wo_ref, modp_ref, gpost_ref = (refs.pop(0) for _ in range(5))
    if pre:
        modn_ref, gpre_ref, wi_ref = (refs.pop(0) for _ in range(3))
    xo_ref = refs.pop(0) if post else None
    rows_all = x_ref.shape[0]
    n_groups = max(1, rows_all // PROJ_GROUP_ROWS)

    def group(rows):
        x = x_ref[rows, :]
        if post:
            half = D_MODEL // 2
            p = _dot(ya_ref[rows, :], wo_ref[0:half, :]) + _dot(yb_ref[rows, :], wo_ref[half:D_MODEL, :])
            x = x + modp_ref[2] * _rms(p, gpost_ref[...])
            xo_ref[rows, :] = x
            yield
        if pre:
            h = (_rms(x, gpre_ref[...]) * (1.0 + modn_ref[1]) + modn_ref[0]).astype(BF16)
            lo, hi = f32_cols
            for c0 in range(0, n_out, IN_PROJ_STEP):
                y = _dot(h, wi_ref[:, c0:c0 + IN_PROJ_STEP])
                if lo <= c0 < hi:
                    refs[1][rows, c0 - lo:c0 - lo + IN_PROJ_STEP] = y
                else:
                    c1 = c0 if c0 < lo else c0 - (hi - lo)
                    refs[0][rows, c1:c1 + IN_PROJ_STEP] = y.astype(BF16)
                yield

    size = rows_all // n_groups
    waiting = [group(slice(k * size, (k + 1) * size)) for k in range(n_groups)]
    running = []
    pieces = n_groups * (post + n_out // IN_PROJ_STEP)
    done = 0
    while waiting or running:
        if waiting:
            running.append(waiting.pop(0))
        for g in list(running):
            try:
                next(g)
                done += 1
            except StopIteration:
                running.remove(g)
        yield min(done / pieces, 1.0)


def _proj_part(x, mod, seq_len, row0, post=None, pre=None, steps=None):
    t = x.shape[0]
    tile = t // steps if steps else min(TOKEN_TILE, t // MIN_PROJ_STEPS)
    half = D_MODEL // 2
    row = lambda i: (i, 0)
    fixed = lambda i: (0, 0)
    resident = dict(index_map=fixed, pipeline_mode=pl.Buffered(1))
    in_specs = [pl.BlockSpec((tile, D_MODEL), row)]
    args = [x]
    out_specs, out_shape = [], []
    n_out, f32_cols = 0, (0, 0)
    if post:
        ya, ia, yb, ib, w_out, layer, gain = post
        in_specs += [pl.BlockSpec((tile, half), lambda i: (i, ia)),
                     pl.BlockSpec((tile, half), lambda i: (i, ib)),
                     pl.BlockSpec((D_MODEL, D_MODEL), **resident),
                     _mod_spec(layer, seq_len, row0, tile),
                     pl.BlockSpec((1, D_MODEL), fixed)]
        args += [ya, yb, w_out, mod, gain]
        out_specs.append(pl.BlockSpec((tile, D_MODEL), row))
        out_shape.append(jax.ShapeDtypeStruct((t, D_MODEL), F32))
    if pre:
        w_in, layer, gain, f32_cols = pre
        n_out = w_in.shape[1]
        lo, hi = f32_cols
        assert lo % IN_PROJ_STEP == 0 and hi % IN_PROJ_STEP == 0 and n_out % IN_PROJ_STEP == 0
        in_specs += [_mod_spec(layer, seq_len, row0, tile),
                     pl.BlockSpec((1, D_MODEL), fixed),
                     pl.BlockSpec((D_MODEL, n_out), **resident)]
        args += [mod, gain, w_in]
        out_specs.append(pl.BlockSpec((tile, n_out - (hi - lo)), row))
        out_shape.append(jax.ShapeDtypeStruct((t, n_out - (hi - lo)), BF16))
        if hi > lo:
            out_specs.append(pl.BlockSpec((tile, hi - lo), row))
            out_shape.append(jax.ShapeDtypeStruct((t, hi - lo), F32))
    kernel = functools.partial(_proj_kernel, post=bool(post), pre=bool(pre), n_out=n_out, f32_cols=f32_cols)
    return _Part(kernel, t // tile, in_specs, args, out_specs, out_shape, [], {})


def _pool_kernel(u_ref, gp_ref, pw_ref, ps_ref, o_ref, pad_ref, band_ref, sum_ref, dif_ref, *, seq_len):
    gd = POOL_GROUP_DIM
    rows = POOL_ROWS
    halo = POOL_HALO
    span = rows + 2 * halo

    @pl.when(pl.program_id(0) == 0)
    def _():
        r = lax.broadcasted_iota(jnp.int32, (rows, span), 0)
        c = lax.broadcasted_iota(jnp.int32, (rows, span), 1)
        offset = c - halo - r
        for g, win in enumerate(POOL_WINDOWS):
            band_ref[g] = jnp.where((offset >= -(win // 2)) & (offset < win // 2), 1.0, 0.0).astype(BF16)

    zeros = jnp.zeros((halo, POOL_WIDTH), BF16)
    pad_ref[0:halo, :] = zeros
    pad_ref[halo + seq_len:2 * halo + seq_len, :] = zeros
    pad_ref[halo:halo + seq_len, :] = u_ref[...]
    units = [(g, r * rows) for g in range(len(POOL_WINDOWS)) for r in range(seq_len // rows)]

    def cols(g):
        return slice(g * gd, (g + 1) * gd)

    for g, base in units:
        sum_ref[base:base + rows, cols(g)] = _dot(band_ref[g], pad_ref[base:base + span, cols(g)])
    for g, base in units:
        win = POOL_WINDOWS[g]
        t = base + lax.broadcasted_iota(jnp.int32, (rows, 1), 0)
        count = jnp.clip(t + win // 2, 0, seq_len) - jnp.clip(t - win // 2, 0, seq_len)
        mean = sum_ref[base:base + rows, cols(g)] / count.astype(F32)
        d = mean - pad_ref[halo + base:halo + base + rows, cols(g)].astype(F32)
        dif_ref[base:base + rows, cols(g)] = d.astype(BF16)
    for g, base in units:
        y = _dot(dif_ref[base:base + rows, cols(g)], pw_ref[g]) * ps_ref[:, cols(g)]
        gate = gp_ref[base:base + rows, cols(g)].astype(F32)
        o_ref[base:base + rows, cols(g)] = (y * (gate * jax.nn.sigmoid(gate))).astype(BF16)


def _pool_mixer(z, batch, seq_len, pool_w, pool_scale):
    return pl.pallas_call(
        functools.partial(_pool_kernel, seq_len=seq_len),
        grid=(batch,),
        in_specs=[
            pl.BlockSpec((seq_len, POOL_WIDTH), lambda b: (b, 0)),
            pl.BlockSpec((seq_len, POOL_WIDTH), lambda b: (b, 1)),
            pl.BlockSpec((len(POOL_WINDOWS), POOL_GROUP_DIM, POOL_GROUP_DIM), lambda b: (0, 0, 0)),
            pl.BlockSpec((1, POOL_WIDTH), lambda b: (0, 0)),
        ],
        out_specs=pl.BlockSpec((seq_len, POOL_WIDTH), lambda b: (b, 0)),
        out_shape=jax.ShapeDtypeStruct((batch * seq_len, POOL_WIDTH), BF16),
        scratch_shapes=[pltpu.VMEM((seq_len + 2 * POOL_HALO, POOL_WIDTH), BF16),
                        pltpu.VMEM((len(POOL_WINDOWS), POOL_ROWS, POOL_ROWS + 2 * POOL_HALO), BF16),
                        pltpu.VMEM((seq_len, POOL_WIDTH), F32), pltpu.VMEM((seq_len, POOL_WIDTH), BF16)],
        compiler_params=_cparams(1),
        name="pool_mixer",
    )(z, z, pool_w, pool_scale)


def _block_row(p, block, row):
    c, w = p.shape
    p3 = p.reshape(c // block, block, w)
    return jnp.broadcast_to(p3[:, row:row + 1, :], p3.shape).reshape(c, w)


def _interleave(lo, hi, block):
    half = block // 2
    parts = []
    for b in range(SCAN_CHUNK // block):
        parts.append(lo[b * block:b * block + half])
        parts.append(hi[b * block + half:(b + 1) * block])
    return jnp.concatenate(parts, axis=0)


def _select_levels(products, rev):
    c = SCAN_CHUNK
    lane = lax.broadcasted_iota(jnp.int32, (8, c), 1)
    sub = lax.broadcasted_iota(jnp.int32, (8, c), 0)
    out = []
    for t0 in range(0, c, 8):
        if rev:
            keep = (lane >= t0 + sub) & (lane < t0 + 8)
        else:
            keep = (lane >= t0) & (lane <= t0 + sub)
        row = jnp.where(keep, products[0][t0:t0 + 8, :], 0.0)
        for lv, block in enumerate((16, 32, 64, 128), 1):
            half = block // 2
            start = t0 // block * block
            upper = t0 - start >= half
            if upper == rev:
                continue
            lo = start + half if rev else start
            row = jnp.where((lane >= lo) & (lane < lo + half), products[lv][t0:t0 + 8, :], row)
        out.append(row)
    return jnp.concatenate(out, axis=0)


def _rec_kernel(*refs, seq_len, heads, has_s0, out_state, n_aliased, layer):
    q_ref, ff_ref, fb_ref, v_ref, gr_ref, lb_ref, hn_ref = refs[:7]
    rest = list(refs[7:])
    s0_ref = rest.pop(0) if has_s0 else None
    del rest[:n_aliased]
    o_ref = rest.pop(0)
    st_ref = rest.pop(0) if out_state else None
    acc_ref, qs_ref, k_ref, p_ref, a_ref, qin_ref, u_ref, dec_ref, tri_ref = rest

    c = SCAN_CHUNK
    hd = REC_HEAD_DIM
    n_chunks = seq_len // c
    zf_refs = (ff_ref, fb_ref)
    units = [(h, n) for h in range(heads) for n in range(n_chunks)]

    def rows_in(n):
        return slice(n * c, (n + 1) * c)

    def rows_sc(h, n):
        return slice((h * n_chunks + n) * c, (h * n_chunks + n + 1) * c)

    def cols(h):
        return slice(h * hd, (h + 1) * hd)

    ti = lax.broadcasted_iota(jnp.int32, (c, c), 0)
    si = lax.broadcasted_iota(jnp.int32, (c, c), 1)
    for d, causal in enumerate((si <= ti, si >= ti)):
        tri_ref[d] = jnp.where(causal, 1.0, 0.0).astype(BF16)

    weights = (0.26, 0.43, 0.09, 0.22)
    done = [0.0]

    def progress(stage, share):
        done[0] += weights[stage] * share
        return done[0]

    for h, n in units:
        qz = q_ref[rows_in(n), cols(h)].astype(F32)
        qs_ref[rows_sc(h, n), :] = qz * jax.nn.sigmoid(qz)
        for d in range(2):
            lower = lb_ref[d, h]
            f = jnp.clip(lower + (1.0 - lower) * jax.nn.sigmoid(zf_refs[d][rows_in(n), cols(h)]), F_MIN, 1.0)
            k_ref[d, rows_sc(h, n), :] = 1.0 - f
            g = jnp.log(f) * LOG2_E
            g_hi = g.astype(BF16)
            rest = g - g_hi.astype(F32)
            g_mid = rest.astype(BF16)
            g_lo = (rest - g_mid.astype(F32)).astype(BF16)
            sums = _dot(tri_ref[d], jnp.concatenate([g_hi, g_mid, g_lo], axis=1))
            p_ref[d, rows_sc(h, n), :] = (sums[:, 0:hd] + sums[:, hd:2 * hd]) + sums[:, 2 * hd:3 * hd]
        yield progress(0, 1 / len(units))

    for h, n in units:
        rows = rows_sc(h, n)
        q = qs_ref[rows, :]
        v_t = v_ref[rows_in(n), cols(h)].astype(F32).T.astype(BF16)
        for d in range(2):
            rev = d == 1
            k = k_ref[d, rows, :]
            p = p_ref[d, rows, :]
            e0 = p - _block_row(p, 8, 4 if rev else 3)
            products = [_dot_nt((q * jnp.exp2(e0)).astype(BF16), (k * jnp.exp2(-e0)).astype(BF16))]
            for block in (16, 32, 64, 128):
                beta = _block_row(p, block, block // 2 if rev else block // 2 - 1)
                if rev:
                    e = _interleave(p, beta, block) - _interleave(beta, p, block)
                    src = _interleave(q, k, block)
                else:
                    e = _interleave(beta, p, block) - _interleave(p, beta, block)
                    src = _interleave(k, q, block)
                m = (src * jnp.exp2(e)).astype(BF16)
                products.append(_dot_nt(m, m))
            a_ref[rows, d * c:(d + 1) * c] = _select_levels(products, rev).astype(BF16)

            edge = p[0:1, :] if rev else p[c - 1:c, :]
            qin_ref[d, rows, :] = (q * jnp.exp2(p)).astype(BF16)
            u_ref[d, h * n_chunks + n] = _dot(v_t, (k * jnp.exp2(edge - p)).astype(BF16))
            dec_ref[d, h * n_chunks + n] = jnp.broadcast_to(jnp.exp2(edge), (8, hd))
        yield progress(1, 1 / len(units))

    for h, n in units:
        v_b = v_ref[rows_in(n), cols(h)]
        acc_ref[rows_sc(h, n), :] = _dot(a_ref[rows_sc(h, n), :], jnp.concatenate([v_b, v_b], axis=0))
        yield progress(2, 1 / len(units))

    for h in range(heads):
        if has_s0:
            states = [s0_ref[0, h].T, s0_ref[1, h].T]
        else:
            states = [jnp.zeros((hd, hd), F32)] * 2
        inter = [[None] * n_chunks, [None] * n_chunks]
        for i in range(n_chunks):
            for d, n in ((0, i), (1, n_chunks - 1 - i)):
                inter[d][n] = _dot_nt(qin_ref[d, rows_sc(h, n), :], states[d].astype(BF16))
                states[d] = dec_ref[d, h * n_chunks + n][0:1, :] * states[d] + u_ref[d, h * n_chunks + n]
        for n in range(n_chunks):
            gate = gr_ref[rows_in(n), cols(h)].astype(F32)
            o = acc_ref[rows_sc(h, n), :] + inter[0][n] + inter[1][n]
            o_ref[rows_in(n), cols(h)] = (_rms(o, hn_ref[...]) * (gate * jax.nn.sigmoid(gate))).astype(BF16)
        if out_state:
            st_ref[layer, 0, h] = states[0].T
            st_ref[layer, 1, h] = states[1].T
        yield progress(3, 1 / heads)
    if out_state:
        for other in range(st_ref.shape[0]):
            if other != layer:
                st_ref[other] = jnp.zeros(st_ref.shape[1:], F32)


def _rec_part(z, zf, batch, seq_len, lower, head_norm, j, s0=None, states=None, heads=None):
    hd = REC_HEAD_DIM
    nh = REC_HEADS
    n_chunks = seq_len // SCAN_CHUNK
    heads = heads or min(nh, max(1, SCAN_UNITS // n_chunks))
    width = heads * hd
    col0 = 2 * POOL_WIDTH // width
    groups = nh // heads

    def zspec(part):
        return pl.BlockSpec((seq_len, width), lambda i: (i // groups, col0 + part * groups + i % groups))

    def fspec(part):
        return pl.BlockSpec((seq_len, width), lambda i: (i // groups, part * groups + i % groups))

    in_specs = [zspec(0), fspec(0), fspec(1), zspec(1), zspec(2),
                pl.BlockSpec((2, heads, 1, hd), lambda i: (0, i % groups, 0, 0)),
                pl.BlockSpec((1, hd), lambda i: (0, 0))]
    args = [z, zf, zf, z, z, lower, head_norm]
    has_s0 = s0 is not None
    if has_s0:
        in_specs.append(pl.BlockSpec((None, None, 2, heads, hd, hd),
                                     lambda i: (i // groups, j, 0, i % groups, 0, 0)))
        args.append(s0)
    out_state = not has_s0
    out_specs = [pl.BlockSpec((seq_len, width), lambda i: (i // groups, i % groups))]
    out_shape = [jax.ShapeDtypeStruct((batch * seq_len, REC_WIDTH), BF16)]
    aliases = {}
    layer = j
    if out_state:
        out_shape.append(jax.ShapeDtypeStruct((batch, N_REC, 2, nh, hd, hd), F32))
        if states is None:
            out_specs.append(pl.BlockSpec((None, N_REC, 2, heads, hd, hd),
                                          lambda i: (i // groups, 0, 0, i % groups, 0, 0)))
        else:
            out_specs.append(pl.BlockSpec((None, 1, 2, heads, hd, hd),
                                          lambda i: (i // groups, j, 0, i % groups, 0, 0)))
            layer = 0
            aliases[len(args)] = 1
            in_specs.append(pl.BlockSpec(memory_space=pl.ANY))
            args.append(states)
    tokens = heads * seq_len
    kernel = functools.partial(_rec_kernel, seq_len=seq_len, heads=heads, has_s0=has_s0, out_state=out_state,
                               n_aliased=len(aliases), layer=layer)
    scratch_shapes = [pltpu.VMEM((tokens, hd), F32),
                      pltpu.VMEM((tokens, hd), F32),
                      pltpu.VMEM((2, tokens, hd), F32),
                      pltpu.VMEM((2, tokens, hd), F32),
                      pltpu.VMEM((tokens, 2 * SCAN_CHUNK), BF16),
                      pltpu.VMEM((2, tokens, hd), BF16),
                      pltpu.VMEM((2, tokens // SCAN_CHUNK, hd, hd), F32),
                      pltpu.VMEM((2, tokens // SCAN_CHUNK, 8, hd), F32),
                      pltpu.VMEM((2, SCAN_CHUNK, SCAN_CHUNK), BF16)]
    return _Part(kernel, batch * groups, in_specs, args, out_specs, out_shape, scratch_shapes, aliases)


def _rope(x, cos, sin_signed):
    lane = lax.broadcasted_iota(jnp.int32, x.shape, 1)
    partner = jnp.where((lane & 1) == 0, pltpu.roll(x, ATT_HEAD_DIM - 1, 1), pltpu.roll(x, 1, 1))
    return x * cos + partner * sin_signed


def _att_kernel(*refs, seq_len, n_keys, latent, n_aliased, layer):
    q_ref, k_ref, v_ref, qn_ref, kn_ref = refs[:5]
    g_refs = refs[5:5 + ATT_KV_HEADS]
    if latent:
        ck_ref, cv_ref, cos_ref, sin_ref, y_ref, kall, vall = refs[5 + ATT_KV_HEADS:-8]
    else:
        y_ref, ko_ref, vo_ref, kall, vall = refs[5 + ATT_KV_HEADS + n_aliased:-8]
    qs_bufs, s_bufs, m_bufs, o_bufs = refs[-8:-6], refs[-6:-4], refs[-4:-2], refs[-2:]
    hd = ATT_HEAD_DIM
    qw = ATT_REP * hd
    tq = ATT_Q_TILE
    tk = min(ATT_KEY_TILE, n_keys)
    n_kb = n_keys // tk
    n_past = n_keys - seq_len
    n_items = ATT_KV_HEADS * (seq_len // tq)
    exp2_scale = hd ** -0.5 * LOG2_E
    assert ATT_KV_HEADS == 2

    if not latent:
        for other in range(ko_ref.shape[0]):
            if other != layer:
                ko_ref[other] = jnp.zeros(ko_ref.shape[1:], F32)
                vo_ref[other] = jnp.zeros(vo_ref.shape[1:], F32)
    lane = lax.broadcasted_iota(jnp.int32, (n_keys, hd), 1)
    for g in range(ATT_KV_HEADS):
        k = _rms(k_ref[:, g * hd:(g + 1) * hd].astype(F32), kn_ref[...])
        v = v_ref[:, g * hd:(g + 1) * hd].astype(F32)
        if latent:
            k = _rope(k, cos_ref[...], sin_ref[...])
            kall[g, 0:n_past, :] = ck_ref[pl.ds(g, n_past, stride=ATT_KV_HEADS), :].astype(BF16)
            vall[g, 0:n_past, 0:hd] = cv_ref[pl.ds(g, n_past, stride=ATT_KV_HEADS), :].astype(BF16)
        else:
            ko_ref[layer, pl.ds(g, seq_len, stride=ATT_KV_HEADS), :] = k
            vo_ref[layer, pl.ds(g, seq_len, stride=ATT_KV_HEADS), :] = v
        kall[g, n_past:n_keys, :] = k.astype(BF16)
        vall[g, n_past:n_keys, 0:hd] = v.astype(BF16)
        vall[g, :, hd:2 * hd] = (lane == 0).astype(BF16)

    def tile_rows(item):
        return pl.ds(pl.multiple_of((item // ATT_KV_HEADS) * tq, tq), tq)

    def prep(item, g):
        rows = tile_rows(item)
        for h in range(ATT_REP):
            q = _rms(q_ref[rows, g * qw + h * hd:g * qw + (h + 1) * hd].astype(F32), qn_ref[...])
            if latent:
                q = _rope(q, cos_ref[rows, :], sin_ref[rows, :])
            qs_bufs[g][h * tq:(h + 1) * tq, :] = q.astype(BF16)

    def scores(item, g):
        qs = qs_bufs[g][...]
        mx = None
        for kb in range(n_kb):
            s = _dot_nt(qs, kall[g, kb * tk:(kb + 1) * tk, :])
            s_bufs[g][:, kb * tk:(kb + 1) * tk] = s
            for c in range(tk // hd):
                part = s[:, c * hd:(c + 1) * hd]
                mx = part if mx is None else jnp.maximum(mx, part)
        m = jnp.max(mx, axis=-1, keepdims=True) * exp2_scale
        m_bufs[g][...] = jnp.broadcast_to(m, (ATT_REP * tq, hd))

    def mix(item, g):
        m = m_bufs[g][...]
        ps = []
        for c in range(n_keys // hd):
            s = s_bufs[g][:, c * hd:(c + 1) * hd]
            ps.append(jnp.exp2(s * exp2_scale - m).astype(BF16))
        o_bufs[g][...] = _dot(jnp.concatenate(ps, axis=1), vall[g])

    def finish(item, g):
        rows = tile_rows(item)
        o = o_bufs[g][:, 0:hd] / o_bufs[g][:, hd:hd + 1]
        for h in range(ATT_REP):
            gate = g_refs[g][rows, h * hd:(h + 1) * hd].astype(F32)
            y = o[h * tq:(h + 1) * tq, :] * (gate * jax.nn.sigmoid(gate))
            y_ref[rows, g * qw + h * hd:g * qw + (h + 1) * hd] = y.astype(BF16)

    stages = (prep, scores, mix, finish)

    def step(k, parity):
        for a, stage in enumerate(stages):
            item = k - a
            if isinstance(item, int) and not 0 <= item < n_items:
                continue
            stage(item, (parity + a) % 2)

    depth = len(stages) - 1
    for k in range(min(depth, n_items + depth)):
        step(k, k % 2)
    n_steady = max(n_items - depth, 0)

    group = ATT_STEADY_STEPS

    def steady_group(j, carry):
        for u in range(group):
            step(depth + group * j + u, (depth + u) % 2)
        return carry

    lax.fori_loop(0, n_steady // group, steady_group, 0)
    for k in range(depth + group * (n_steady // group), n_items + depth):
        step(k, k % 2)


def _attention(z, batch, seq_len, q_norm, k_norm, j, cache=None, new_kv=None):
    hd = ATT_HEAD_DIM
    latent = cache is not None
    qw = ATT_REP * hd
    kv0 = ATT_WIDTH // KV_WIDTH
    gate0 = (ATT_WIDTH + 2 * KV_WIDTH) // qw
    assert (ATT_WIDTH + 2 * KV_WIDTH) % qw == 0 and ATT_WIDTH % KV_WIDTH == 0
    in_specs = [
        pl.BlockSpec((seq_len, ATT_WIDTH), lambda b: (b, 0)),
        pl.BlockSpec((seq_len, KV_WIDTH), lambda b: (b, kv0)),
        pl.BlockSpec((seq_len, KV_WIDTH), lambda b: (b, kv0 + 1)),
        pl.BlockSpec((1, hd), lambda b: (0, 0)),
        pl.BlockSpec((1, hd), lambda b: (0, 0)),
    ] + [pl.BlockSpec((seq_len, qw), lambda b, g=g: (b, gate0 + g)) for g in range(ATT_KV_HEADS)]
    args = [z, z, z, q_norm, k_norm] + [z] * ATT_KV_HEADS
    out_specs = [pl.BlockSpec((seq_len, ATT_WIDTH), lambda b: (b, 0))]
    out_shape = [jax.ShapeDtypeStruct((batch * seq_len, ATT_WIDTH), BF16)]
    n_keys = seq_len
    aliases = {}
    layer = j
    if latent:
        cache_k, cache_v, cos, sin_signed = cache
        past = cache_k.shape[2]
        n_keys += past
        cspec = pl.BlockSpec((None, None, past * ATT_KV_HEADS, hd), lambda b: (b, j, 0, 0))
        tspec = pl.BlockSpec((seq_len, hd), lambda b: (0, 0))
        in_specs += [cspec, cspec, tspec, tspec]
        args += [cache_k.reshape(batch, N_ATT, past * ATT_KV_HEADS, hd),
                 cache_v.reshape(batch, N_ATT, past * ATT_KV_HEADS, hd), cos, sin_signed]
    else:
        out_shape += [jax.ShapeDtypeStruct((batch, N_ATT, seq_len * ATT_KV_HEADS, hd), F32)] * 2
        if new_kv is None:
            kv_spec = pl.BlockSpec((None, N_ATT, seq_len * ATT_KV_HEADS, hd), lambda b: (b, 0, 0, 0))
        else:
            kv_spec = pl.BlockSpec((None, 1, seq_len * ATT_KV_HEADS, hd), lambda b: (b, j, 0, 0))
            layer = 0
            aliases = {len(args): 1, len(args) + 1: 2}
            in_specs += [pl.BlockSpec(memory_space=pl.ANY)] * 2
            args += list(new_kv)
        out_specs += [kv_spec, kv_spec]
    rows = ATT_REP * ATT_Q_TILE
    return pl.pallas_call(
        functools.partial(_att_kernel, seq_len=seq_len, n_keys=n_keys, latent=latent, n_aliased=len(aliases),
                          layer=layer),
        grid=(batch,),
        in_specs=in_specs,
        out_specs=out_specs,
        out_shape=out_shape,
        input_output_aliases=aliases,
        scratch_shapes=[pltpu.VMEM((ATT_KV_HEADS, n_keys, hd), BF16),
                        pltpu.VMEM((ATT_KV_HEADS, n_keys, 2 * hd), BF16),
                        pltpu.VMEM((rows, hd), BF16), pltpu.VMEM((rows, hd), BF16),
                        pltpu.VMEM((rows, n_keys), F32), pltpu.VMEM((rows, n_keys), F32),
                        pltpu.VMEM((rows, hd), F32), pltpu.VMEM((rows, hd), F32),
                        pltpu.VMEM((rows, 2 * hd), F32), pltpu.VMEM((rows, 2 * hd), F32)],
        compiler_params=_cparams(1),
        name="attention",
    )(*args)


def _rope_tables(n_tokens):
    t = jnp.arange(n_tokens)
    row = (t // GRID_W).astype(F32)
    col = (t % GRID_W).astype(F32)
    inv = ROPE_THETA ** (-jnp.arange(0, AXIS_DIM, 2, dtype=F32) / AXIS_DIM)
    ang = jnp.concatenate([row[:, None] * inv[None, :], col[:, None] * inv[None, :]], axis=-1)
    cos = jnp.repeat(jnp.cos(ang), 2, axis=-1)
    sin = jnp.repeat(jnp.sin(ang), 2, axis=-1)
    sign = jnp.where(jnp.arange(ATT_HEAD_DIM) % 2 == 0, -1.0, 1.0).astype(F32)
    return cos, sin * sign


def kernel(x_prompt, x_sample, c, state_hgrn, cache_k, cache_v, c_ctx, ada_w, ada_b, norm_pre, norm_post, rec_w_in, rec_lb_logits, rec_head_norm, pool_w, pool_scale, rec_w_out, att_w_in, att_q_norm, att_k_norm, att_w_out):
    nb_c, len_c, _ = x_prompt.shape
    nb_l, len_l, _ = x_sample.shape

    lb_p = jax.nn.softmax(rec_lb_logits.astype(F32), axis=0)
    lower_bounds = jnp.clip(jnp.cumsum(lb_p, axis=0) - lb_p[0], 0.0, 1.0)
    lower_bounds = lower_bounds.reshape(N_REC, 2, REC_HEADS, 1, REC_HEAD_DIM)
    cos, sin_signed = _rope_tables(len_l)

    cvec = jnp.zeros((MOD_ROWS, D_MODEL), F32).at[0].set(c_ctx).at[1:1 + nb_l].set(c)

    forget_cols = (2 * POOL_WIDTH + REC_WIDTH, 2 * POOL_WIDTH + 3 * REC_WIDTH)
    layers = []
    for i in range(DEPTH):
        rec = i % 2 == 0
        layers.append(dict(
            f32_cols=forget_cols if rec else (0, 0),
            gain_pre=norm_pre[i].reshape(1, D_MODEL), gain_post=norm_post[i].reshape(1, D_MODEL)))

    weights = {("in", 0): rec_w_in[0].astype(BF16)}

    def weight_source(kind, i):
        stacks = (rec_w_in, rec_w_out) if i % 2 == 0 else (att_w_in, att_w_out)
        return stacks[kind == "out"], i // 2

    def cast_part(keys, steps):
        keys = [k for k in keys if k not in weights and k[1] < DEPTH]
        return keys, _cast_part([weight_source(*k) for k in keys], steps)

    def pre_args(i):
        return (weights["in", i], i, layers[i]["gain_pre"], layers[i]["f32_cols"])

    xs = [x_prompt.reshape(nb_c * len_c, D_MODEL), x_sample.reshape(nb_l * len_l, D_MODEL)]
    streams = ((nb_c, len_c, 0), (nb_l, len_l, 1))
    zs = [None, None]
    ys = [None, None]
    new_states, new_kv = None, None

    def proj_part(s, i, steps=None, mod=None):
        nb, sl, row0 = streams[s]
        post = ys[s] + (weights["out", i - 1], i - 1, layers[i - 1]["gain_post"]) if i > 0 else None
        return _proj_part(xs[s], mods if mod is None else mod, sl, row0, post=post,
                          pre=pre_args(i) if i < DEPTH else None, steps=steps)

    def take_proj(s, i, outs):
        if i > 0:
            xs[s] = outs.pop(0)
        zs[s] = outs

    def rec_part(s, i, heads=None):
        nb, sl, _ = streams[s]
        j = i // 2
        hn = rec_head_norm[j].reshape(1, REC_HEAD_DIM)
        z, zf = zs[s]
        if s == 0:
            return _rec_part(z, zf, nb, sl, lower_bounds[j], hn, j, states=new_states, heads=heads)
        return _rec_part(z, zf, nb, sl, lower_bounds[j], hn, j, s0=state_hgrn, heads=heads)

    def pool(s, i):
        nb, sl, _ = streams[s]
        j = i // 2
        return _pool_mixer(zs[s][0], nb, sl, pool_w[j].astype(BF16), pool_scale[j].reshape(1, POOL_WIDTH))

    mod0 = _mod_rows(_run(_mod_part(cvec, ada_w, ada_b, 0, 1), name="modulation")[0][0])
    first = proj_part(0, 0, mod=mod0)
    assert first.steps == DEPTH * 3 * D_MODEL // MOD_COLS
    keys, casts = cast_part([("out", 0), ("in", 1)], first.steps)
    outs, (mods,), cast = _run(first, _mod_part(cvec, ada_w, ada_b, 0, DEPTH), casts, name="proj_modulation")
    weights.update(zip(keys, cast))
    mods = _mod_rows(mods)
    take_proj(0, 0, outs)
    for i in range(DEPTH):
        j = i // 2
        if i % 2 == 0:
            rec_c = rec_part(0, i)
            keys, casts = cast_part([("out", i), ("in", i + 1)], rec_c.steps)
            (y_rec, new_states), outs, cast = _run(rec_c, proj_part(1, i, steps=rec_c.steps), casts, name="rec_proj")
            weights.update(zip(keys, cast))
            ys[0] = (pool(0, i), 0, y_rec, 0)
            take_proj(1, i, outs)
            rec_l = rec_part(1, i, heads=REC_HEADS * streams[1][0] // rec_c.steps)
            keys, casts = cast_part([("out", i + 1), ("in", i + 2)], rec_l.steps)
            (y_rec,), outs, cast = _run(rec_l, proj_part(0, i + 1, steps=rec_l.steps), casts, name="rec_proj")
            weights.update(zip(keys, cast))
            ys[1] = (pool(1, i), 0, y_rec, 0)
            take_proj(0, i + 1, outs)
        else:
            qn = att_q_norm[j].reshape(1, ATT_HEAD_DIM)
            kn = att_k_norm[j].reshape(1, ATT_HEAD_DIM)
            y, *new_kv = _attention(zs[0][0], nb_c, len_c, qn, kn, j, new_kv=new_kv)
            ys[0] = (y, 0, y, 1)
            take_proj(1, i, _run(proj_part(1, i), name="proj")[0])
            (y,) = _attention(zs[1][0], nb_l, len_l, qn, kn, j, cache=(cache_k, cache_v, cos, sin_signed))
            ys[1] = (y, 0, y, 1)
            take_proj(0, i + 1, _run(proj_part(0, i + 1), name="proj")[0])
    take_proj(1, DEPTH, _run(proj_part(1, DEPTH), name="proj")[0])
    xc, xl = xs

    kv_shape = (nb_c, N_ATT, len_c, ATT_KV_HEADS, ATT_HEAD_DIM)
    return (xc.reshape(nb_c, len_c, D_MODEL), xl.reshape(nb_l, len_l, D_MODEL),
            new_states, new_kv[0].reshape(kv_shape), new_kv[1].reshape(kv_shape))
```

```python
import functools
from typing import Any, NamedTuple

import jax
import jax.numpy as jnp
from jax import lax
from jax.experimental import pallas as pl
from jax.experimental.pallas import tpu as pltpu

D_MODEL = 1024
DEPTH = 4
GRID_W = 64
N_REC = (DEPTH + 1) // 2
N_ATT = DEPTH // 2
POOL_WIDTH = D_MODEL // 2
POOL_WINDOWS = (2, 4, 8, 16)
POOL_GROUP_DIM = POOL_WIDTH // len(POOL_WINDOWS)
REC_WIDTH = D_MODEL // 2
REC_HEAD_DIM = 128
REC_HEADS = REC_WIDTH // REC_HEAD_DIM
REC_IN_WIDTH = 2 * POOL_WIDTH + 5 * REC_WIDTH
ATT_HEAD_DIM = 128
ATT_HEADS = D_MODEL // ATT_HEAD_DIM
ATT_KV_HEADS = 2
ATT_REP = ATT_HEADS // ATT_KV_HEADS
ATT_WIDTH = ATT_HEADS * ATT_HEAD_DIM
KV_WIDTH = ATT_KV_HEADS * ATT_HEAD_DIM
ATT_IN_WIDTH = 2 * ATT_WIDTH + 2 * KV_WIDTH
AXIS_DIM = ATT_HEAD_DIM // 2
ROPE_THETA = 10000.0
EPS = 1e-6
F_MIN = 1e-6

MOD_ROWS = 16
TOKEN_TILE = 1024
MIN_PROJ_STEPS = 8
PROJ_GROUP_ROWS = 512
POOL_ROWS = 256
IN_PROJ_STEP = 512
MOD_COLS = 1536
SCAN_CHUNK = 128
SCAN_UNITS = 8
POOL_HALO = 128
ATT_Q_TILE = 128
ATT_KEY_TILE = 128
ATT_STEADY_STEPS = 2
LOG2_E = 1.4426950408889634
VMEM_LIMIT = 56 * 1024 * 1024

F32 = jnp.float32
BF16 = jnp.bfloat16


def _cparams(n_axes):
    return pltpu.CompilerParams(
        dimension_semantics=("arbitrary",) * n_axes, vmem_limit_bytes=VMEM_LIMIT)


def _rms(x, g):
    return x * lax.rsqrt(jnp.mean(x * x, axis=-1, keepdims=True) + EPS) * g


def _dot(a, b):
    return jnp.dot(a, b, preferred_element_type=F32)


def _dot_nt(a, b):
    return lax.dot_general(a, b, (((1,), (1,)), ((), ())), preferred_element_type=F32)


class _Part(NamedTuple):
    kernel: Any
    steps: int
    in_specs: list
    args: list
    out_specs: list
    out_shape: list
    scratch_shapes: list
    aliases: dict


def _run(*parts, name):
    steps = parts[0].steps
    assert all(p.steps == steps for p in parts)
    n_in = [len(p.args) for p in parts]
    n_out = [len(p.out_shape) for p in parts]
    n_scr = [len(p.scratch_shapes) for p in parts]

    def body(*refs):
        ins, outs, scr = refs[:sum(n_in)], refs[sum(n_in):sum(n_in) + sum(n_out)], refs[sum(n_in) + sum(n_out):]
        pending = {}
        for k, p in enumerate(parts):
            i0, o0, s0 = sum(n_in[:k]), sum(n_out[:k]), sum(n_scr[:k])
            pending[k] = (0.0, p.kernel(*ins[i0:i0 + n_in[k]], *outs[o0:o0 + n_out[k]], *scr[s0:s0 + n_scr[k]]))
        while pending:
            k = min(pending, key=lambda k: pending[k][0])
            try:
                pending[k] = (next(pending[k][1]), pending[k][1])
            except StopIteration:
                del pending[k]

    aliases = {}
    for k, p in enumerate(parts):
        for i, o in p.aliases.items():
            aliases[sum(n_in[:k]) + i] = sum(n_out[:k]) + o
    outs = pl.pallas_call(
        body,
        grid=(steps,),
        in_specs=[s for p in parts for s in p.in_specs],
        out_specs=[s for p in parts for s in p.out_specs],
        out_shape=[s for p in parts for s in p.out_shape],
        scratch_shapes=[s for p in parts for s in p.scratch_shapes],
        input_output_aliases=aliases,
        compiler_params=_cparams(1),
        name=name,
    )(*[a for p in parts for a in p.args])
    return [list(outs[sum(n_out[:k]):sum(n_out[:k]) + n_out[k]]) for k in range(len(parts))]


def _mod_kernel(cv_ref, w_ref, b_ref, o_ref):
    cv = cv_ref[...]
    a = (cv * jax.nn.sigmoid(cv)).astype(BF16)
    o_ref[...] = _dot(a, w_ref[...].astype(BF16)) + b_ref[...]
    yield 1.0


def _mod_part(cvec, ada_w, ada_b, layer0, n_layers):
    tn = MOD_COLS
    blocks = 3 * D_MODEL // tn
    in_specs = [
        pl.BlockSpec((MOD_ROWS, D_MODEL), lambda i: (0, 0)),
        pl.BlockSpec((None, D_MODEL, tn), lambda i: (layer0 + i // blocks, 0, i % blocks)),
        pl.BlockSpec((None, 1, tn), lambda i: (layer0 + i // blocks, 0, i % blocks)),
    ]
    out_specs = [pl.BlockSpec((None, MOD_ROWS, tn), lambda i: (i // blocks, 0, i % blocks))]
    out_shape = [jax.ShapeDtypeStruct((n_layers, MOD_ROWS, 3 * D_MODEL), F32)]
    args = [cvec, ada_w, ada_b.reshape(DEPTH, 1, 3 * D_MODEL)]
    return _Part(_mod_kernel, n_layers * blocks, in_specs, args, out_specs, out_shape, [], {})


def _cast_kernel(*refs):
    n = len(refs) // 2
    for w_ref, o_ref in zip(refs[:n], refs[n:]):
        o_ref[...] = w_ref[...].astype(BF16)
    yield 1.0


def _cast_part(sources, steps):
    in_specs, out_specs, out_shape, args = [], [], [], []
    for stack, j in sources:
        _, k, n = stack.shape
        in_specs.append(pl.BlockSpec((None, k // steps, n), lambda i, j=j: (j, i, 0)))
        out_specs.append(pl.BlockSpec((k // steps, n), lambda i: (i, 0)))
        out_shape.append(jax.ShapeDtypeStruct((k, n), BF16))
        args.append(stack)
    return _Part(_cast_kernel, steps, in_specs, args, out_specs, out_shape, [], {})


def _mod_rows(out):
    return out.reshape(out.shape[0], MOD_ROWS, 3, 1, D_MODEL)


def _mod_spec(layer, seq_len, row0, tile):
    assert row0 == 0 or seq_len % tile == 0
    tiles_per_seq = seq_len // tile
    if row0 == 0:
        index = lambda i: (layer, 0, 0, 0, 0)
    else:
        index = lambda i: (layer, row0 + i // tiles_per_seq, 0, 0, 0)
    return pl.BlockSpec((None, None, 3, 1, D_MODEL), index)


def _proj_kernel(*refs, post, pre, n_out, f32_cols):
    refs = list(refs)
    x_ref = refs.pop(0)
    if post:
        ya_ref, yb_ref, wo_ref, modp_ref, gpost_ref = (refs.pop(0) for _ in range(5))
    if pre:
        modn_ref, gpre_ref, wi_ref = (refs.pop(0) for _ in range(3))
    xo_ref = refs.pop(0) if post else None
    rows_all = x_ref.shape[0]
    n_groups = max(1, rows_all // PROJ_GROUP_ROWS)

    def group(rows):
        x = x_ref[rows, :]
        if post:
            half = D_MODEL // 2
            p = _dot(ya_ref[rows, :], wo_ref[0:half, :]) + _dot(yb_ref[rows, :], wo_ref[half:D_MODEL, :])
            x = x + modp_ref[2] * _rms(p, gpost_ref[...])
            xo_ref[rows, :] = x
            yield
        if pre:
            h = (_rms(x, gpre_ref[...]) * (1.0 + modn_ref[1]) + modn_ref[0]).astype(BF16)
            lo, hi = f32_cols
            for c0 in range(0, n_out, IN_PROJ_STEP):
                y = _dot(h, wi_ref[:, c0:c0 + IN_PROJ_STEP])
                if lo <= c0 < hi:
                    refs[1][rows, c0 - lo:c0 - lo + IN_PROJ_STEP] = y
                else:
                    c1 = c0 if c0 < lo else c0 - (hi - lo)
                    refs[0][rows, c1:c1 + IN_PROJ_STEP] = y.astype(BF16)
                yield

    size = rows_all // n_groups
    waiting = [group(slice(k * size, (k + 1) * size)) for k in range(n_groups)]
    running = []
    pieces = n_groups * (post + n_out // IN_PROJ_STEP)
    done = 0
    while waiting or running:
        if waiting:
            running.append(waiting.pop(0))
        for g in list(running):
            try:
                next(g)
                done += 1
            except StopIteration:
                running.remove(g)
        yield min(done / pieces, 1.0)


def _proj_part(x, mod, seq_len, row0, post=None, pre=None, steps=None):
    t = x.shape[0]
    tile = t // steps if steps else min(TOKEN_TILE, t // MIN_PROJ_STEPS)
    half = D_MODEL // 2
    row = lambda i: (i, 0)
    fixed = lambda i: (0, 0)
    resident = dict(index_map=fixed, pipeline_mode=pl.Buffered(1))
    in_specs = [pl.BlockSpec((tile, D_MODEL), row)]
    args = [x]
    out_specs, out_shape = [], []
    n_out, f32_cols = 0, (0, 0)
    if post:
        ya, ia, yb, ib, w_out, layer, gain = post
        in_specs += [pl.BlockSpec((tile, half), lambda i: (i, ia)),
                     pl.BlockSpec((tile, half), lambda i: (i, ib)),
                     pl.BlockSpec((D_MODEL, D_MODEL), **resident),
                     _mod_spec(layer, seq_len, row0, tile),
                     pl.BlockSpec((1, D_MODEL), fixed)]
        args += [ya, yb, w_out, mod, gain]
        out_specs.append(pl.BlockSpec((tile, D_MODEL), row))
        out_shape.append(jax.ShapeDtypeStruct((t, D_MODEL), F32))
    if pre:
        w_in, layer, gain, f32_cols = pre
        n_out = w_in.shape[1]
        lo, hi = f32_cols
        assert lo % IN_PROJ_STEP == 0 and hi % IN_PROJ_STEP == 0 and n_out % IN_PROJ_STEP == 0
        in_specs += [_mod_spec(layer, seq_len, row0, tile),
                     pl.BlockSpec((1, D_MODEL), fixed),
                     pl.BlockSpec((D_MODEL, n_out), **resident)]
        args += [mod, gain, w_in]
        out_specs.append(pl.BlockSpec((tile, n_out - (hi - lo)), row))
        out_shape.append(jax.ShapeDtypeStruct((t, n_out - (hi - lo)), BF16))
        if hi > lo:
            out_specs.append(pl.BlockSpec((tile, hi - lo), row))
            out_shape.append(jax.ShapeDtypeStruct((t, hi - lo), F32))
    kernel = functools.partial(_proj_kernel, post=bool(post), pre=bool(pre), n_out=n_out, f32_cols=f32_cols)
    return _Part(kernel, t // tile, in_specs, args, out_specs, out_shape, [], {})


def _pool_kernel(u_ref, gp_ref, pw_ref, ps_ref, o_ref, pad_ref, band_ref, sum_ref, dif_ref, *, seq_len):
    gd = POOL_GROUP_DIM
    rows = POOL_ROWS
    halo = POOL_HALO
    span = rows + 2 * halo

    @pl.when(pl.program_id(0) == 0)
    def _():
        r = lax.broadcasted_iota(jnp.int32, (rows, span), 0)
        c = lax.broadcasted_iota(jnp.int32, (rows, span), 1)
        offset = c - halo - r
        for g, win in enumerate(POOL_WINDOWS):
            band_ref[g] = jnp.where((offset >= -(win // 2)) & (offset < win // 2), 1.0, 0.0).astype(BF16)

    zeros = jnp.zeros((halo, POOL_WIDTH), BF16)
    pad_ref[0:halo, :] = zeros
    pad_ref[halo + seq_len:2 * halo + seq_len, :] = zeros
    pad_ref[halo:halo + seq_len, :] = u_ref[...]
    units = [(g, r * rows) for g in range(len(POOL_WINDOWS)) for r in range(seq_len // rows)]

    def cols(g):
        return slice(g * gd, (g + 1) * gd)

    for g, base in units:
        sum_ref[base:base + rows, cols(g)] = _dot(band_ref[g], pad_ref[base:base + span, cols(g)])
    yield 0.3
    for g, base in units:
        win = POOL_WINDOWS[g]
        t = base + lax.broadcasted_iota(jnp.int32, (rows, 1), 0)
        count = jnp.clip(t + win // 2, 0, seq_len) - jnp.clip(t - win // 2, 0, seq_len)
        mean = sum_ref[base:base + rows, cols(g)] / count.astype(F32)
        d = mean - pad_ref[halo + base:halo + base + rows, cols(g)].astype(F32)
        dif_ref[base:base + rows, cols(g)] = d.astype(BF16)
    yield 0.6
    for g, base in units:
        y = _dot(dif_ref[base:base + rows, cols(g)], pw_ref[g]) * ps_ref[:, cols(g)]
        gate = gp_ref[base:base + rows, cols(g)].astype(F32)
        o_ref[base:base + rows, cols(g)] = (y * (gate * jax.nn.sigmoid(gate))).astype(BF16)
    yield 1.0


def _pool_part(z, batch, seq_len, pool_w, pool_scale):
    in_specs = [
        pl.BlockSpec((seq_len, POOL_WIDTH), lambda b: (b, 0)),
        pl.BlockSpec((seq_len, POOL_WIDTH), lambda b: (b, 1)),
        pl.BlockSpec((len(POOL_WINDOWS), POOL_GROUP_DIM, POOL_GROUP_DIM), lambda b: (0, 0, 0)),
        pl.BlockSpec((1, POOL_WIDTH), lambda b: (0, 0)),
    ]
    out_specs = [pl.BlockSpec((seq_len, POOL_WIDTH), lambda b: (b, 0))]
    out_shape = [jax.ShapeDtypeStruct((batch * seq_len, POOL_WIDTH), BF16)]
    scratch_shapes = [pltpu.VMEM((seq_len + 2 * POOL_HALO, POOL_WIDTH), BF16),
                      pltpu.VMEM((len(POOL_WINDOWS), POOL_ROWS, POOL_ROWS + 2 * POOL_HALO), BF16),
                      pltpu.VMEM((seq_len, POOL_WIDTH), F32), pltpu.VMEM((seq_len, POOL_WIDTH), BF16)]
    kernel = functools.partial(_pool_kernel, seq_len=seq_len)
    return _Part(kernel, batch, in_specs, [z, z, pool_w, pool_scale], out_specs, out_shape, scratch_shapes, {})


def _block_row(p, block, row):
    c, w = p.shape
    p3 = p.reshape(c // block, block, w)
    return jnp.broadcast_to(p3[:, row:row + 1, :], p3.shape).reshape(c, w)


def _interleave(lo, hi, block):
    half = block // 2
    parts = []
    for b in range(SCAN_CHUNK // block):
        parts.append(lo[b * block:b * block + half])
        parts.append(hi[b * block + half:(b + 1) * block])
    return jnp.concatenate(parts, axis=0)


def _select_levels(products, rev):
    c = SCAN_CHUNK
    lane = lax.broadcasted_iota(jnp.int32, (8, c), 1)
    sub = lax.broadcasted_iota(jnp.int32, (8, c), 0)
    out = []
    for t0 in range(0, c, 8):
        if rev:
            keep = (lane >= t0 + sub) & (lane < t0 + 8)
        else:
            keep = (lane >= t0) & (lane <= t0 + sub)
        row = jnp.where(keep, products[0][t0:t0 + 8, :], 0.0)
        for lv, block in enumerate((16, 32, 64, 128), 1):
            half = block // 2
            start = t0 // block * block
            upper = t0 - start >= half
            if upper == rev:
                continue
            lo = start + half if rev else start
            row = jnp.where((lane >= lo) & (lane < lo + half), products[lv][t0:t0 + 8, :], row)
        out.append(row)
    return jnp.concatenate(out, axis=0)


def _rec_kernel(*refs, seq_len, heads, has_s0, out_state, n_aliased, layer):
    q_ref, ff_ref, fb_ref, v_ref, gr_ref, lb_ref, hn_ref = refs[:7]
    rest = list(refs[7:])
    s0_ref = rest.pop(0) if has_s0 else None
    del rest[:n_aliased]
    o_ref = rest.pop(0)
    st_ref = rest.pop(0) if out_state else None
    acc_ref, qs_ref, k_ref, p_ref, a_ref, qin_ref, u_ref, dec_ref, tri_ref = rest

    c = SCAN_CHUNK
    hd = REC_HEAD_DIM
    n_chunks = seq_len // c
    zf_refs = (ff_ref, fb_ref)
    units = [(h, n) for h in range(heads) for n in range(n_chunks)]

    def rows_in(n):
        return slice(n * c, (n + 1) * c)

    def rows_sc(h, n):
        return slice((h * n_chunks + n) * c, (h * n_chunks + n + 1) * c)

    def cols(h):
        return slice(h * hd, (h + 1) * hd)

    ti = lax.broadcasted_iota(jnp.int32, (c, c), 0)
    si = lax.broadcasted_iota(jnp.int32, (c, c), 1)
    for d, causal in enumerate((si <= ti, si >= ti)):
        tri_ref[d] = jnp.where(causal, 1.0, 0.0).astype(BF16)

    weights = (0.26, 0.43, 0.09, 0.22)
    done = [0.0]

    def progress(stage, share):
        done[0] += weights[stage] * share
        return done[0]

    for h, n in units:
        qz = q_ref[rows_in(n), cols(h)].astype(F32)
        qs_ref[rows_sc(h, n), :] = qz * jax.nn.sigmoid(qz)
        for d in range(2):
            lower = lb_ref[d, h]
            f = jnp.clip(lower + (1.0 - lower) * jax.nn.sigmoid(zf_refs[d][rows_in(n), cols(h)]), F_MIN, 1.0)
            k_ref[d, rows_sc(h, n), :] = 1.0 - f
            g = jnp.log(f) * LOG2_E
            g_hi = g.astype(BF16)
            rest = g - g_hi.astype(F32)
            g_mid = rest.astype(BF16)
            g_lo = (rest - g_mid.astype(F32)).astype(BF16)
            sums = _dot(tri_ref[d], jnp.concatenate([g_hi, g_mid, g_lo], axis=1))
            p_ref[d, rows_sc(h, n), :] = (sums[:, 0:hd] + sums[:, hd:2 * hd]) + sums[:, 2 * hd:3 * hd]
        yield progress(0, 1 / len(units))

    for h, n in units:
        rows = rows_sc(h, n)
        q = qs_ref[rows, :]
        v_t = v_ref[rows_in(n), cols(h)].astype(F32).T.astype(BF16)
        for d in range(2):
            rev = d == 1
            k = k_ref[d, rows, :]
            p = p_ref[d, rows, :]
            e0 = p - _block_row(p, 8, 4 if rev else 3)
            products = [_dot_nt((q * jnp.exp2(e0)).astype(BF16), (k * jnp.exp2(-e0)).astype(BF16))]
            for block in (16, 32, 64, 128):
                beta = _block_row(p, block, block // 2 if rev else block // 2 - 1)
                if rev:
                    e = _interleave(p, beta, block) - _interleave(beta, p, block)
                    src = _interleave(q, k, block)
                else:
                    e = _interleave(beta, p, block) - _interleave(p, beta, block)
                    src = _interleave(k, q, block)
                m = (src * jnp.exp2(e)).astype(BF16)
                products.append(_dot_nt(m, m))
            a_ref[rows, d * c:(d + 1) * c] = _select_levels(products, rev).astype(BF16)

            edge = p[0:1, :] if rev else p[c - 1:c, :]
            qin_ref[d, rows, :] = (q * jnp.exp2(p)).astype(BF16)
            u_ref[d, h * n_chunks + n] = _dot(v_t, (k * jnp.exp2(edge - p)).astype(BF16))
            dec_ref[d, h * n_chunks + n] = jnp.broadcast_to(jnp.exp2(edge), (8, hd))
        yield progress(1, 1 / len(units))

    for h, n in units:
        v_b = v_ref[rows_in(n), cols(h)]
        acc_ref[rows_sc(h, n), :] = _dot(a_ref[rows_sc(h, n), :], jnp.concatenate([v_b, v_b], axis=0))
        yield progress(2, 1 / len(units))

    for h in range(heads):
        if has_s0:
            states = [s0_ref[0, h].T, s0_ref[1, h].T]
        else:
            states = [jnp.zeros((hd, hd), F32)] * 2
        inter = [[None] * n_chunks, [None] * n_chunks]
        for i in range(n_chunks):
            for d, n in ((0, i), (1, n_chunks - 1 - i)):
                inter[d][n] = _dot_nt(qin_ref[d, rows_sc(h, n), :], states[d].astype(BF16))
                states[d] = dec_ref[d, h * n_chunks + n][0:1, :] * states[d] + u_ref[d, h * n_chunks + n]
        for n in range(n_chunks):
            gate = gr_ref[rows_in(n), cols(h)].astype(F32)
            o = acc_ref[rows_sc(h, n), :] + inter[0][n] + inter[1][n]
            o_ref[rows_in(n), cols(h)] = (_rms(o, hn_ref[...]) * (gate * jax.nn.sigmoid(gate))).astype(BF16)
        if out_state:
            st_ref[layer, 0, h] = states[0].T
            st_ref[layer, 1, h] = states[1].T
        yield progress(3, 1 / heads)
    if out_state:
        for other in range(st_ref.shape[0]):
            if other != layer:
                st_ref[other] = jnp.zeros(st_ref.shape[1:], F32)


def _rec_part(z, zf, batch, seq_len, lower, head_norm, j, s0=None, states=None, heads=None):
    hd = REC_HEAD_DIM
    nh = REC_HEADS
    n_chunks = seq_len // SCAN_CHUNK
    heads = heads or min(nh, max(1, SCAN_UNITS // n_chunks))
    width = heads * hd
    col0 = 2 * POOL_WIDTH // width
    groups = nh // heads

    def zspec(part):
        return pl.BlockSpec((seq_len, width), lambda i: (i // groups, col0 + part * groups + i % groups))

    def fspec(part):
        return pl.BlockSpec((seq_len, width), lambda i: (i // groups, part * groups + i % groups))

    in_specs = [zspec(0), fspec(0), fspec(1), zspec(1), zspec(2),
                pl.BlockSpec((2, heads, 1, hd), lambda i: (0, i % groups, 0, 0)),
                pl.BlockSpec((1, hd), lambda i: (0, 0))]
    args = [z, zf, zf, z, z, lower, head_norm]
    has_s0 = s0 is not None
    if has_s0:
        in_specs.append(pl.BlockSpec((None, None, 2, heads, hd, hd),
                                     lambda i: (i // groups, j, 0, i % groups, 0, 0)))
        args.append(s0)
    out_state = not has_s0
    out_specs = [pl.BlockSpec((seq_len, width), lambda i: (i // groups, i % groups))]
    out_shape = [jax.ShapeDtypeStruct((batch * seq_len, REC_WIDTH), BF16)]
    aliases = {}
    layer = j
    if out_state:
        out_shape.append(jax.ShapeDtypeStruct((batch, N_REC, 2, nh, hd, hd), F32))
        if states is None:
            out_specs.append(pl.BlockSpec((None, N_REC, 2, heads, hd, hd),
                                          lambda i: (i // groups, 0, 0, i % groups, 0, 0)))
        else:
            out_specs.append(pl.BlockSpec((None, 1, 2, heads, hd, hd),
                                          lambda i: (i // groups, j, 0, i % groups, 0, 0)))
            layer = 0
            aliases[len(args)] = 1
            in_specs.append(pl.BlockSpec(memory_space=pl.ANY))
            args.append(states)
    tokens = heads * seq_len
    kernel = functools.partial(_rec_kernel, seq_len=seq_len, heads=heads, has_s0=has_s0, out_state=out_state,
                               n_aliased=len(aliases), layer=layer)
    scratch_shapes = [pltpu.VMEM((tokens, hd), F32),
                      pltpu.VMEM((tokens, hd), F32),
                      pltpu.VMEM((2, tokens, hd), F32),
                      pltpu.VMEM((2, tokens, hd), F32),
                      pltpu.VMEM((tokens, 2 * SCAN_CHUNK), BF16),
                      pltpu.VMEM((2, tokens, hd), BF16),
                      pltpu.VMEM((2, tokens // SCAN_CHUNK, hd, hd), F32),
                      pltpu.VMEM((2, tokens // SCAN_CHUNK, 8, hd), F32),
                      pltpu.VMEM((2, SCAN_CHUNK, SCAN_CHUNK), BF16)]
    return _Part(kernel, batch * groups, in_specs, args, out_specs, out_shape, scratch_shapes, aliases)


def _rope(x, cos, sin_signed):
    lane = lax.broadcasted_iota(jnp.int32, x.shape, 1)
    partner = jnp.where((lane & 1) == 0, pltpu.roll(x, ATT_HEAD_DIM - 1, 1), pltpu.roll(x, 1, 1))
    return x * cos + partner * sin_signed


def _att_kernel(*refs, seq_len, n_keys, latent, n_aliased, layer):
    q_ref, k_ref, v_ref, qn_ref, kn_ref = refs[:5]
    g_refs = refs[5:5 + ATT_KV_HEADS]
    if latent:
        ck_ref, cv_ref, cos_ref, sin_ref, y_ref, kall, vall = refs[5 + ATT_KV_HEADS:-8]
    else:
        y_ref, ko_ref, vo_ref, kall, vall = refs[5 + ATT_KV_HEADS + n_aliased:-8]
    qs_bufs, s_bufs, m_bufs, o_bufs = refs[-8:-6], refs[-6:-4], refs[-4:-2], refs[-2:]
    hd = ATT_HEAD_DIM
    qw = ATT_REP * hd
    tq = ATT_Q_TILE
    tk = min(ATT_KEY_TILE, n_keys)
    n_kb = n_keys // tk
    n_past = n_keys - seq_len
    n_items = ATT_KV_HEADS * (seq_len // tq)
    exp2_scale = hd ** -0.5 * LOG2_E
    assert ATT_KV_HEADS == 2

    if not latent:
        for other in range(ko_ref.shape[0]):
            if other != layer:
                ko_ref[other] = jnp.zeros(ko_ref.shape[1:], F32)
                vo_ref[other] = jnp.zeros(vo_ref.shape[1:], F32)
    lane = lax.broadcasted_iota(jnp.int32, (n_keys, hd), 1)
    for g in range(ATT_KV_HEADS):
        k = _rms(k_ref[:, g * hd:(g + 1) * hd].astype(F32), kn_ref[...])
        v = v_ref[:, g * hd:(g + 1) * hd].astype(F32)
        if latent:
            k = _rope(k, cos_ref[...], sin_ref[...])
            kall[g, 0:n_past, :] = ck_ref[pl.ds(g, n_past, stride=ATT_KV_HEADS), :].astype(BF16)
            vall[g, 0:n_past, 0:hd] = cv_ref[pl.ds(g, n_past, stride=ATT_KV_HEADS), :].astype(BF16)
        else:
            ko_ref[layer, pl.ds(g, seq_len, stride=ATT_KV_HEADS), :] = k
            vo_ref[layer, pl.ds(g, seq_len, stride=ATT_KV_HEADS), :] = v
        kall[g, n_past:n_keys, :] = k.astype(BF16)
        vall[g, n_past:n_keys, 0:hd] = v.astype(BF16)
        vall[g, :, hd:2 * hd] = (lane == 0).astype(BF16)

    def tile_rows(item):
        return pl.ds(pl.multiple_of((item // ATT_KV_HEADS) * tq, tq), tq)

    def prep(item, g):
        rows = tile_rows(item)
        for h in range(ATT_REP):
            q = _rms(q_ref[rows, g * qw + h * hd:g * qw + (h + 1) * hd].astype(F32), qn_ref[...])
            if latent:
                q = _rope(q, cos_ref[rows, :], sin_ref[rows, :])
            qs_bufs[g][h * tq:(h + 1) * tq, :] = q.astype(BF16)

    def scores(item, g):
        qs = qs_bufs[g][...]
        mx = None
        for kb in range(n_kb):
            s = _dot_nt(qs, kall[g, kb * tk:(kb + 1) * tk, :])
            s_bufs[g][:, kb * tk:(kb + 1) * tk] = s
            for c in range(tk // hd):
                part = s[:, c * hd:(c + 1) * hd]
                mx = part if mx is None else jnp.maximum(mx, part)
        m = jnp.max(mx, axis=-1, keepdims=True) * exp2_scale
        m_bufs[g][...] = jnp.broadcast_to(m, (ATT_REP * tq, hd))

    def mix(item, g):
        m = m_bufs[g][...]
        ps = []
        for c in range(n_keys // hd):
            s = s_bufs[g][:, c * hd:(c + 1) * hd]
            ps.append(jnp.exp2(s * exp2_scale - m).astype(BF16))
        o_bufs[g][...] = _dot(jnp.concatenate(ps, axis=1), vall[g])

    def finish(item, g):
        rows = tile_rows(item)
        o = o_bufs[g][:, 0:hd] / o_bufs[g][:, hd:hd + 1]
        for h in range(ATT_REP):
            gate = g_refs[g][rows, h * hd:(h + 1) * hd].astype(F32)
            y = o[h * tq:(h + 1) * tq, :] * (gate * jax.nn.sigmoid(gate))
            y_ref[rows, g * qw + h * hd:g * qw + (h + 1) * hd] = y.astype(BF16)

    stages = (prep, scores, mix, finish)

    def step(k, parity):
        for a, stage in enumerate(stages):
            item = k - a
            if isinstance(item, int) and not 0 <= item < n_items:
                continue
            stage(item, (parity + a) % 2)

    depth = len(stages) - 1
    for k in range(min(depth, n_items + depth)):
        step(k, k % 2)
    n_steady = max(n_items - depth, 0)

    group = ATT_STEADY_STEPS

    def steady_group(j, carry):
        for u in range(group):
            step(depth + group * j + u, (depth + u) % 2)
        return carry

    lax.fori_loop(0, n_steady // group, steady_group, 0)
    for k in range(depth + group * (n_steady // group), n_items + depth):
        step(k, k % 2)


def _attention(z, batch, seq_len, q_norm, k_norm, j, cache=None, new_kv=None):
    hd = ATT_HEAD_DIM
    latent = cache is not None
    qw = ATT_REP * hd
    kv0 = ATT_WIDTH // KV_WIDTH
    gate0 = (ATT_WIDTH + 2 * KV_WIDTH) // qw
    assert (ATT_WIDTH + 2 * KV_WIDTH) % qw == 0 and ATT_WIDTH % KV_WIDTH == 0
    in_specs = [
        pl.BlockSpec((seq_len, ATT_WIDTH), lambda b: (b, 0)),
        pl.BlockSpec((seq_len, KV_WIDTH), lambda b: (b, kv0)),
        pl.BlockSpec((seq_len, KV_WIDTH), lambda b: (b, kv0 + 1)),
        pl.BlockSpec((1, hd), lambda b: (0, 0)),
        pl.BlockSpec((1, hd), lambda b: (0, 0)),
    ] + [pl.BlockSpec((seq_len, qw), lambda b, g=g: (b, gate0 + g)) for g in range(ATT_KV_HEADS)]
    args = [z, z, z, q_norm, k_norm] + [z] * ATT_KV_HEADS
    out_specs = [pl.BlockSpec((seq_len, ATT_WIDTH), lambda b: (b, 0))]
    out_shape = [jax.ShapeDtypeStruct((batch * seq_len, ATT_WIDTH), BF16)]
    n_keys = seq_len
    aliases = {}
    layer = j
    if latent:
        cache_k, cache_v, cos, sin_signed = cache
        past = cache_k.shape[2]
        n_keys += past
        cspec = pl.BlockSpec((None, None, past * ATT_KV_HEADS, hd), lambda b: (b, j, 0, 0))
        tspec = pl.BlockSpec((seq_len, hd), lambda b: (0, 0))
        in_specs += [cspec, cspec, tspec, tspec]
        args += [cache_k.reshape(batch, N_ATT, past * ATT_KV_HEADS, hd),
                 cache_v.reshape(batch, N_ATT, past * ATT_KV_HEADS, hd), cos, sin_signed]
    else:
        out_shape += [jax.ShapeDtypeStruct((batch, N_ATT, seq_len * ATT_KV_HEADS, hd), F32)] * 2
        if new_kv is None:
            kv_spec = pl.BlockSpec((None, N_ATT, seq_len * ATT_KV_HEADS, hd), lambda b: (b, 0, 0, 0))
        else:
            kv_spec = pl.BlockSpec((None, 1, seq_len * ATT_KV_HEADS, hd), lambda b: (b, j, 0, 0))
            layer = 0
            aliases = {len(args): 1, len(args) + 1: 2}
            in_specs += [pl.BlockSpec(memory_space=pl.ANY)] * 2
            args += list(new_kv)
        out_specs += [kv_spec, kv_spec]
    rows = ATT_REP * ATT_Q_TILE
    return pl.pallas_call(
        functools.partial(_att_kernel, seq_len=seq_len, n_keys=n_keys, latent=latent, n_aliased=len(aliases),
                          layer=layer),
        grid=(batch,),
        in_specs=in_specs,
        out_specs=out_specs,
        out_shape=out_shape,
        input_output_aliases=aliases,
        scratch_shapes=[pltpu.VMEM((ATT_KV_HEADS, n_keys, hd), BF16),
                        pltpu.VMEM((ATT_KV_HEADS, n_keys, 2 * hd), BF16),
                        pltpu.VMEM((rows, hd), BF16), pltpu.VMEM((rows, hd), BF16),
                        pltpu.VMEM((rows, n_keys), F32), pltpu.VMEM((rows, n_keys), F32),
                        pltpu.VMEM((rows, hd), F32), pltpu.VMEM((rows, hd), F32),
                        pltpu.VMEM((rows, 2 * hd), F32), pltpu.VMEM((rows, 2 * hd), F32)],
        compiler_params=_cparams(1),
        name="attention",
    )(*args)


def _rope_tables(n_tokens):
    t = jnp.arange(n_tokens)
    row = (t // GRID_W).astype(F32)
    col = (t % GRID_W).astype(F32)
    inv = ROPE_THETA ** (-jnp.arange(0, AXIS_DIM, 2, dtype=F32) / AXIS_DIM)
    ang = jnp.concatenate([row[:, None] * inv[None, :], col[:, None] * inv[None, :]], axis=-1)
    cos = jnp.repeat(jnp.cos(ang), 2, axis=-1)
    sin = jnp.repeat(jnp.sin(ang), 2, axis=-1)
    sign = jnp.where(jnp.arange(ATT_HEAD_DIM) % 2 == 0, -1.0, 1.0).astype(F32)
    return cos, sin * sign


def kernel(x_prompt, x_sample, c, state_hgrn, cache_k, cache_v, c_ctx, ada_w, ada_b, norm_pre, norm_post, rec_w_in, rec_lb_logits, rec_head_norm, pool_w, pool_scale, rec_w_out, att_w_in, att_q_norm, att_k_norm, att_w_out):
    nb_c, len_c, _ = x_prompt.shape
    nb_l, len_l, _ = x_sample.shape

    lb_p = jax.nn.softmax(rec_lb_logits.astype(F32), axis=0)
    lower_bounds = jnp.clip(jnp.cumsum(lb_p, axis=0) - lb_p[0], 0.0, 1.0)
    lower_bounds = lower_bounds.reshape(N_REC, 2, REC_HEADS, 1, REC_HEAD_DIM)
    cos, sin_signed = _rope_tables(len_l)

    cvec = jnp.zeros((MOD_ROWS, D_MODEL), F32).at[0].set(c_ctx).at[1:1 + nb_l].set(c)

    forget_cols = (2 * POOL_WIDTH + REC_WIDTH, 2 * POOL_WIDTH + 3 * REC_WIDTH)
    layers = []
    for i in range(DEPTH):
        rec = i % 2 == 0
        layers.append(dict(
            f32_cols=forget_cols if rec else (0, 0),
            gain_pre=norm_pre[i].reshape(1, D_MODEL), gain_post=norm_post[i].reshape(1, D_MODEL)))

    weights = {("in", 0): rec_w_in[0].astype(BF16)}

    def weight_source(kind, i):
        stacks = (rec_w_in, rec_w_out) if i % 2 == 0 else (att_w_in, att_w_out)
        return stacks[kind == "out"], i // 2

    def cast_part(keys, steps):
        keys = [k for k in keys if k not in weights and k[1] < DEPTH]
        return keys, _cast_part([weight_source(*k) for k in keys], steps)

    def pre_args(i):
        return (weights["in", i], i, layers[i]["gain_pre"], layers[i]["f32_cols"])

    xs = [x_prompt.reshape(nb_c * len_c, D_MODEL), x_sample.reshape(nb_l * len_l, D_MODEL)]
    streams = ((nb_c, len_c, 0), (nb_l, len_l, 1))
    zs = [None, None]
    ys = [None, None]
    new_states, new_kv = None, None

    def proj_part(s, i, steps=None, mod=None):
        nb, sl, row0 = streams[s]
        post = ys[s] + (weights["out", i - 1], i - 1, layers[i - 1]["gain_post"]) if i > 0 else None
        return _proj_part(xs[s], mods if mod is None else mod, sl, row0, post=post,
                          pre=pre_args(i) if i < DEPTH else None, steps=steps)

    def take_proj(s, i, outs):
        if i > 0:
            xs[s] = outs.pop(0)
        zs[s] = outs

    def rec_part(s, i, heads=None):
        nb, sl, _ = streams[s]
        j = i // 2
        hn = rec_head_norm[j].reshape(1, REC_HEAD_DIM)
        z, zf = zs[s]
        if s == 0:
            return _rec_part(z, zf, nb, sl, lower_bounds[j], hn, j, states=new_states, heads=heads)
        return _rec_part(z, zf, nb, sl, lower_bounds[j], hn, j, s0=state_hgrn, heads=heads)

    def pool_part(s, i):
        nb, sl, _ = streams[s]
        j = i // 2
        return _pool_part(zs[s][0], nb, sl, pool_w[j].astype(BF16), pool_scale[j].reshape(1, POOL_WIDTH))

    mod0 = _mod_rows(_run(_mod_part(cvec, ada_w, ada_b, 0, 1), name="modulation")[0][0])
    first = proj_part(0, 0, mod=mod0)
    assert first.steps == DEPTH * 3 * D_MODEL // MOD_COLS
    keys, casts = cast_part([("out", 0), ("in", 1)], first.steps)
    outs, (mods,), cast = _run(first, _mod_part(cvec, ada_w, ada_b, 0, DEPTH), casts, name="proj_modulation")
    weights.update(zip(keys, cast))
    mods = _mod_rows(mods)
    take_proj(0, 0, outs)
    for i in range(DEPTH):
        j = i // 2
        if i % 2 == 0:
            rec_c = rec_part(0, i)
            keys, casts = cast_part([("out", i), ("in", i + 1)], rec_c.steps)
            (y_rec, new_states), outs, cast, (y_pool,) = _run(
                rec_c, proj_part(1, i, steps=rec_c.steps), casts, pool_part(0, i), name="rec_pool_proj")
            weights.update(zip(keys, cast))
            ys[0] = (y_pool, 0, y_rec, 0)
            take_proj(1, i, outs)
            rec_l = rec_part(1, i, heads=REC_HEADS * streams[1][0] // rec_c.steps)
            keys, casts = cast_part([("out", i + 1), ("in", i + 2)], rec_l.steps)
            (y_rec,), outs, cast = _run(rec_l, proj_part(0, i + 1, steps=rec_l.steps), casts, name="rec_proj")
            weights.update(zip(keys, cast))
            ys[1] = (_run(pool_part(1, i), name="pool_mixer")[0][0], 0, y_rec, 0)
            take_proj(0, i + 1, outs)
        else:
            qn = att_q_norm[j].reshape(1, ATT_HEAD_DIM)
            kn = att_k_norm[j].reshape(1, ATT_HEAD_DIM)
            y, *new_kv = _attention(zs[0][0], nb_c, len_c, qn, kn, j, new_kv=new_kv)
            ys[0] = (y, 0, y, 1)
            take_proj(1, i, _run(proj_part(1, i), name="proj")[0])
            (y,) = _attention(zs[1][0], nb_l, len_l, qn, kn, j, cache=(cache_k, cache_v, cos, sin_signed))
            ys[1] = (y, 0, y, 1)
            take_proj(0, i + 1, _run(proj_part(0, i + 1), name="proj")[0])
    take_proj(1, DEPTH, _run(proj_part(1, DEPTH), name="proj")[0])
    xc, xl = xs

    kv_shape = (nb_c, N_ATT, len_c, ATT_KV_HEADS, ATT_HEAD_DIM)
    return (xc.reshape(nb_c, len_c, D_MODEL), xl.reshape(nb_l, len_l, D_MODEL),
            new_states, new_kv[0].reshape(kv_shape), new_kv[1].reshape(kv_shape))
```

```python
import functools
from typing import Any, NamedTuple

import jax
import jax.numpy as jnp
from jax import lax
from jax.experimental import pallas as pl
from jax.experimental.pallas import tpu as pltpu

D_MODEL = 1024
DEPTH = 4
GRID_W = 64
N_REC = (DEPTH + 1) // 2
N_ATT = DEPTH // 2
POOL_WIDTH = D_MODEL // 2
POOL_WINDOWS = (2, 4, 8, 16)
POOL_GROUP_DIM = POOL_WIDTH // len(POOL_WINDOWS)
REC_WIDTH = D_MODEL // 2
REC_HEAD_DIM = 128
REC_HEADS = REC_WIDTH // REC_HEAD_DIM
REC_IN_WIDTH = 2 * POOL_WIDTH + 5 * REC_WIDTH
ATT_HEAD_DIM = 128
ATT_HEADS = D_MODEL // ATT_HEAD_DIM
ATT_KV_HEADS = 2
ATT_REP = ATT_HEADS // ATT_KV_HEADS
ATT_WIDTH = ATT_HEADS * ATT_HEAD_DIM
KV_WIDTH = ATT_KV_HEADS * ATT_HEAD_DIM
ATT_IN_WIDTH = 2 * ATT_WIDTH + 2 * KV_WIDTH
AXIS_DIM = ATT_HEAD_DIM // 2
ROPE_THETA = 10000.0
EPS = 1e-6
F_MIN = 1e-6

MOD_ROWS = 16
TOKEN_TILE = 1024
MIN_PROJ_STEPS = 4
PROJ_GROUP_ROWS = 512
POOL_ROWS = 256
IN_PROJ_STEP = 512
MOD_COLS = 1536
SCAN_CHUNK = 128
SCAN_UNITS = 8
POOL_HALO = 128
ATT_Q_TILE = 128
ATT_KEY_TILE = 128
ATT_STEADY_STEPS = 2
LOG2_E = 1.4426950408889634
VMEM_LIMIT = 56 * 1024 * 1024

F32 = jnp.float32
BF16 = jnp.bfloat16


def _cparams(n_axes):
    return pltpu.CompilerParams(
        dimension_semantics=("arbitrary",) * n_axes, vmem_limit_bytes=VMEM_LIMIT)


def _rms(x, g):
    return x * lax.rsqrt(jnp.mean(x * x, axis=-1, keepdims=True) + EPS) * g


def _dot(a, b):
    return jnp.dot(a, b, preferred_element_type=F32)


def _dot_nt(a, b):
    return lax.dot_general(a, b, (((1,), (1,)), ((), ())), preferred_element_type=F32)


class _Part(NamedTuple):
    kernel: Any
    steps: int
    in_specs: list
    args: list
    out_specs: list
    out_shape: list
    scratch_shapes: list
    aliases: dict


def _run(*parts, name):
    steps = parts[0].steps
    assert all(p.steps == steps for p in parts)
    n_in = [len(p.args) for p in parts]
    n_out = [len(p.out_shape) for p in parts]
    n_scr = [len(p.scratch_shapes) for p in parts]

    def body(*refs):
        ins, outs, scr = refs[:sum(n_in)], refs[sum(n_in):sum(n_in) + sum(n_out)], refs[sum(n_in) + sum(n_out):]
        pending = {}
        for k, p in enumerate(parts):
            i0, o0, s0 = sum(n_in[:k]), sum(n_out[:k]), sum(n_scr[:k])
            pending[k] = (0.0, p.kernel(*ins[i0:i0 + n_in[k]], *outs[o0:o0 + n_out[k]], *scr[s0:s0 + n_scr[k]]))
        while pending:
            k = min(pending, key=lambda k: pending[k][0])
            try:
                pending[k] = (next(pending[k][1]), pending[k][1])
            except StopIteration:
                del pending[k]

    aliases = {}
    for k, p in enumerate(parts):
        for i, o in p.aliases.items():
            aliases[sum(n_in[:k]) + i] = sum(n_out[:k]) + o
    outs = pl.pallas_call(
        body,
        grid=(steps,),
        in_specs=[s for p in parts for s in p.in_specs],
        out_specs=[s for p in parts for s in p.out_specs],
        out_shape=[s for p in parts for s in p.out_shape],
        scratch_shapes=[s for p in parts for s in p.scratch_shapes],
        input_output_aliases=aliases,
        compiler_params=_cparams(1),
        name=name,
    )(*[a for p in parts for a in p.args])
    return [list(outs[sum(n_out[:k]):sum(n_out[:k]) + n_out[k]]) for k in range(len(parts))]


def _mod_kernel(cv_ref, w_ref, b_ref, o_ref):
    cv = cv_ref[...]
    a = (cv * jax.nn.sigmoid(cv)).astype(BF16)
    o_ref[...] = _dot(a, w_ref[...].astype(BF16)) + b_ref[...]
    yield 1.0


def _mod_part(cvec, ada_w, ada_b, layer0, n_layers):
    tn = MOD_COLS
    blocks = 3 * D_MODEL // tn
    in_specs = [
        pl.BlockSpec((MOD_ROWS, D_MODEL), lambda i: (0, 0)),
        pl.BlockSpec((None, D_MODEL, tn), lambda i: (layer0 + i // blocks, 0, i % blocks)),
        pl.BlockSpec((None, 1, tn), lambda i: (layer0 + i // blocks, 0, i % blocks)),
    ]
    out_specs = [pl.BlockSpec((None, MOD_ROWS, tn), lambda i: (i // blocks, 0, i % blocks))]
    out_shape = [jax.ShapeDtypeStruct((n_layers, MOD_ROWS, 3 * D_MODEL), F32)]
    args = [cvec, ada_w, ada_b.reshape(DEPTH, 1, 3 * D_MODEL)]
    return _Part(_mod_kernel, n_layers * blocks, in_specs, args, out_specs, out_shape, [], {})


def _cast_kernel(*refs):
    n = len(refs) // 2
    for w_ref, o_ref in zip(refs[:n], refs[n:]):
        o_ref[...] = w_ref[...].astype(BF16)
    yield 1.0


def _cast_part(sources, steps):
    in_specs, out_specs, out_shape, args = [], [], [], []
    for stack, j in sources:
        _, k, n = stack.shape
        in_specs.append(pl.BlockSpec((None, k // steps, n), lambda i, j=j: (j, i, 0)))
        out_specs.append(pl.BlockSpec((k // steps, n), lambda i: (i, 0)))
        out_shape.append(jax.ShapeDtypeStruct((k, n), BF16))
        args.append(stack)
    return _Part(_cast_kernel, steps, in_specs, args, out_specs, out_shape, [], {})


def _mod_rows(out):
    return out.reshape(out.shape[0], MOD_ROWS, 3, 1, D_MODEL)


def _mod_spec(layer, seq_len, row0, tile):
    assert row0 == 0 or seq_len % tile == 0
    tiles_per_seq = seq_len // tile
    if row0 == 0:
        index = lambda i: (layer, 0, 0, 0, 0)
    else:
        index = lambda i: (layer, row0 + i // tiles_per_seq, 0, 0, 0)
    return pl.BlockSpec((None, None, 3, 1, D_MODEL), index)


def _proj_kernel(*refs, post, pre, n_out, f32_cols):
    refs = list(refs)
    x_ref = refs.pop(0)
    if post:
        ya_ref, yb_ref, wo_ref, modp_ref, gpost_ref = (refs.pop(0) for _ in range(5))
    if pre:
        modn_ref, gpre_ref, wi_ref = (refs.pop(0) for _ in range(3))
    xo_ref = refs.pop(0) if post else None
    rows_all = x_ref.shape[0]
    n_groups = max(1, rows_all // PROJ_GROUP_ROWS)

    def group(rows):
        x = x_ref[rows, :]
        if post:
            half = D_MODEL // 2
            p = _dot(ya_ref[rows, :], wo_ref[0:half, :]) + _dot(yb_ref[rows, :], wo_ref[half:D_MODEL, :])
            x = x + modp_ref[2] * _rms(p, gpost_ref[...])
            xo_ref[rows, :] = x
            yield
        if pre:
            h = (_rms(x, gpre_ref[...]) * (1.0 + modn_ref[1]) + modn_ref[0]).astype(BF16)
            lo, hi = f32_cols
            for c0 in range(0, n_out, IN_PROJ_STEP):
                y = _dot(h, wi_ref[:, c0:c0 + IN_PROJ_STEP])
                if lo <= c0 < hi:
                    refs[1][rows, c0 - lo:c0 - lo + IN_PROJ_STEP] = y
                else:
                    c1 = c0 if c0 < lo else c0 - (hi - lo)
                    refs[0][rows, c1:c1 + IN_PROJ_STEP] = y.astype(BF16)
                yield

    size = rows_all // n_groups
    waiting = [group(slice(k * size, (k + 1) * size)) for k in range(n_groups)]
    running = []
    pieces = n_groups * (post + n_out // IN_PROJ_STEP)
    done = 0
    while waiting or running:
        if waiting:
            running.append(waiting.pop(0))
        for g in list(running):
            try:
                next(g)
                done += 1
            except StopIteration:
                running.remove(g)
        yield min(done / pieces, 1.0)


def _proj_part(x, mod, seq_len, row0, post=None, pre=None, steps=None):
    t = x.shape[0]
    tile = t // steps if steps else min(TOKEN_TILE, t // MIN_PROJ_STEPS)
    half = D_MODEL // 2
    row = lambda i: (i, 0)
    fixed = lambda i: (0, 0)
    resident = dict(index_map=fixed, pipeline_mode=pl.Buffered(1))
    in_specs = [pl.BlockSpec((tile, D_MODEL), row)]
    args = [x]
    out_specs, out_shape = [], []
    n_out, f32_cols = 0, (0, 0)
    if post:
        ya, ia, yb, ib, w_out, layer, gain = post
        in_specs += [pl.BlockSpec((tile, half), lambda i: (i, ia)),
                     pl.BlockSpec((tile, half), lambda i: (i, ib)),
                     pl.BlockSpec((D_MODEL, D_MODEL), **resident),
                     _mod_spec(layer, seq_len, row0, tile),
                     pl.BlockSpec((1, D_MODEL), fixed)]
        args += [ya, yb, w_out, mod, gain]
        out_specs.append(pl.BlockSpec((tile, D_MODEL), row))
        out_shape.append(jax.ShapeDtypeStruct((t, D_MODEL), F32))
    if pre:
        w_in, layer, gain, f32_cols = pre
        n_out = w_in.shape[1]
        lo, hi = f32_cols
        assert lo % IN_PROJ_STEP == 0 and hi % IN_PROJ_STEP == 0 and n_out % IN_PROJ_STEP == 0
        in_specs += [_mod_spec(layer, seq_len, row0, tile),
                     pl.BlockSpec((1, D_MODEL), fixed),
                     pl.BlockSpec((D_MODEL, n_out), **resident)]
        args += [mod, gain, w_in]
        out_specs.append(pl.BlockSpec((tile, n_out - (hi - lo)), row))
        out_shape.append(jax.ShapeDtypeStruct((t, n_out - (hi - lo)), BF16))
        if hi > lo:
            out_specs.append(pl.BlockSpec((tile, hi - lo), row))
            out_shape.append(jax.ShapeDtypeStruct((t, hi - lo), F32))
    kernel = functools.partial(_proj_kernel, post=bool(post), pre=bool(pre), n_out=n_out, f32_cols=f32_cols)
    return _Part(kernel, t // tile, in_specs, args, out_specs, out_shape, [], {})


def _pool_kernel(u_ref, gp_ref, pw_ref, ps_ref, o_ref, pad_ref, band_ref, sum_ref, dif_ref, *, seq_len):
    gd = POOL_GROUP_DIM
    rows = POOL_ROWS
    halo = POOL_HALO
    span = rows + 2 * halo

    @pl.when(pl.program_id(0) == 0)
    def _():
        r = lax.broadcasted_iota(jnp.int32, (rows, span), 0)
        c = lax.broadcasted_iota(jnp.int32, (rows, span), 1)
        offset = c - halo - r
        for g, win in enumerate(POOL_WINDOWS):
            band_ref[g] = jnp.where((offset >= -(win // 2)) & (offset < win // 2), 1.0, 0.0).astype(BF16)

    zeros = jnp.zeros((halo, POOL_WIDTH), BF16)
    pad_ref[0:halo, :] = zeros
    pad_ref[halo + seq_len:2 * halo + seq_len, :] = zeros
    pad_ref[halo:halo + seq_len, :] = u_ref[...]
    units = [(g, r * rows) for g in range(len(POOL_WINDOWS)) for r in range(seq_len // rows)]

    def cols(g):
        return slice(g * gd, (g + 1) * gd)

    for g, base in units:
        sum_ref[base:base + rows, cols(g)] = _dot(band_ref[g], pad_ref[base:base + span, cols(g)])
    yield 0.3
    for g, base in units:
        win = POOL_WINDOWS[g]
        t = base + lax.broadcasted_iota(jnp.int32, (rows, 1), 0)
        count = jnp.clip(t + win // 2, 0, seq_len) - jnp.clip(t - win // 2, 0, seq_len)
        mean = sum_ref[base:base + rows, cols(g)] / count.astype(F32)
        d = mean - pad_ref[halo + base:halo + base + rows, cols(g)].astype(F32)
        dif_ref[base:base + rows, cols(g)] = d.astype(BF16)
    yield 0.6
    for g, base in units:
        y = _dot(dif_ref[base:base + rows, cols(g)], pw_ref[g]) * ps_ref[:, cols(g)]
        gate = gp_ref[base:base + rows, cols(g)].astype(F32)
        o_ref[base:base + rows, cols(g)] = (y * (gate * jax.nn.sigmoid(gate))).astype(BF16)
    yield 1.0


def _pool_part(z, batch, seq_len, pool_w, pool_scale):
    in_specs = [
        pl.BlockSpec((seq_len, POOL_WIDTH), lambda b: (b, 0)),
        pl.BlockSpec((seq_len, POOL_WIDTH), lambda b: (b, 1)),
        pl.BlockSpec((len(POOL_WINDOWS), POOL_GROUP_DIM, POOL_GROUP_DIM), lambda b: (0, 0, 0)),
        pl.BlockSpec((1, POOL_WIDTH), lambda b: (0, 0)),
    ]
    out_specs = [pl.BlockSpec((seq_len, POOL_WIDTH), lambda b: (b, 0))]
    out_shape = [jax.ShapeDtypeStruct((batch * seq_len, POOL_WIDTH), BF16)]
    scratch_shapes = [pltpu.VMEM((seq_len + 2 * POOL_HALO, POOL_WIDTH), BF16),
                      pltpu.VMEM((len(POOL_WINDOWS), POOL_ROWS, POOL_ROWS + 2 * POOL_HALO), BF16),
                      pltpu.VMEM((seq_len, POOL_WIDTH), F32), pltpu.VMEM((seq_len, POOL_WIDTH), BF16)]
    kernel = functools.partial(_pool_kernel, seq_len=seq_len)
    return _Part(kernel, batch, in_specs, [z, z, pool_w, pool_scale], out_specs, out_shape, scratch_shapes, {})


def _block_row(p, block, row):
    c, w = p.shape
    p3 = p.reshape(c // block, block, w)
    return jnp.broadcast_to(p3[:, row:row + 1, :], p3.shape).reshape(c, w)


def _interleave(lo, hi, block):
    half = block // 2
    parts = []
    for b in range(SCAN_CHUNK // block):
        parts.append(lo[b * block:b * block + half])
        parts.append(hi[b * block + half:(b + 1) * block])
    return jnp.concatenate(parts, axis=0)


def _select_levels(products, rev):
    c = SCAN_CHUNK
    lane = lax.broadcasted_iota(jnp.int32, (8, c), 1)
    sub = lax.broadcasted_iota(jnp.int32, (8, c), 0)
    out = []
    for t0 in range(0, c, 8):
        if rev:
            keep = (lane >= t0 + sub) & (lane < t0 + 8)
        else:
            keep = (lane >= t0) & (lane <= t0 + sub)
        row = jnp.where(keep, products[0][t0:t0 + 8, :], 0.0)
        for lv, block in enumerate((16, 32, 64, 128), 1):
            half = block // 2
            start = t0 // block * block
            upper = t0 - start >= half
            if upper == rev:
                continue
            lo = start + half if rev else start
            row = jnp.where((lane >= lo) & (lane < lo + half), products[lv][t0:t0 + 8, :], row)
        out.append(row)
    return jnp.concatenate(out, axis=0)


def _rec_kernel(*refs, seq_len, heads, has_s0, out_state, n_aliased, layer):
    q_ref, ff_ref, fb_ref, v_ref, gr_ref, lb_ref, hn_ref = refs[:7]
    rest = list(refs[7:])
    s0_ref = rest.pop(0) if has_s0 else None
    del rest[:n_aliased]
    o_ref = rest.pop(0)
    st_ref = rest.pop(0) if out_state else None
    acc_ref, qs_ref, k_ref, p_ref, a_ref, qin_ref, u_ref, dec_ref, tri_ref = rest

    c = SCAN_CHUNK
    hd = REC_HEAD_DIM
    n_chunks = seq_len // c
    zf_refs = (ff_ref, fb_ref)
    units = [(h, n) for h in range(heads) for n in range(n_chunks)]

    def rows_in(n):
        return slice(n * c, (n + 1) * c)

    def rows_sc(h, n):
        return slice((h * n_chunks + n) * c, (h * n_chunks + n + 1) * c)

    def cols(h):
        return slice(h * hd, (h + 1) * hd)

    ti = lax.broadcasted_iota(jnp.int32, (c, c), 0)
    si = lax.broadcasted_iota(jnp.int32, (c, c), 1)
    for d, causal in enumerate((si <= ti, si >= ti)):
        tri_ref[d] = jnp.where(causal, 1.0, 0.0).astype(BF16)

    weights = (0.26, 0.43, 0.09, 0.22)
    done = [0.0]

    def progress(stage, share):
        done[0] += weights[stage] * share
        return done[0]

    for h, n in units:
        qz = q_ref[rows_in(n), cols(h)].astype(F32)
        qs_ref[rows_sc(h, n), :] = qz * jax.nn.sigmoid(qz)
        for d in range(2):
            lower = lb_ref[d, h]
            f = jnp.clip(lower + (1.0 - lower) * jax.nn.sigmoid(zf_refs[d][rows_in(n), cols(h)]), F_MIN, 1.0)
            k_ref[d, rows_sc(h, n), :] = 1.0 - f
            g = jnp.log(f) * LOG2_E
            g_hi = g.astype(BF16)
            rest = g - g_hi.astype(F32)
            g_mid = rest.astype(BF16)
            g_lo = (rest - g_mid.astype(F32)).astype(BF16)
            sums = _dot(tri_ref[d], jnp.concatenate([g_hi, g_mid, g_lo], axis=1))
            p_ref[d, rows_sc(h, n), :] = (sums[:, 0:hd] + sums[:, hd:2 * hd]) + sums[:, 2 * hd:3 * hd]
        yield progress(0, 1 / len(units))

    for h, n in units:
        rows = rows_sc(h, n)
        q = qs_ref[rows, :]
        v_t = v_ref[rows_in(n), cols(h)].astype(F32).T.astype(BF16)
        for d in range(2):
            rev = d == 1
            k = k_ref[d, rows, :]
            p = p_ref[d, rows, :]
            e0 = p - _block_row(p, 8, 4 if rev else 3)
            products = [_dot_nt((q * jnp.exp2(e0)).astype(BF16), (k * jnp.exp2(-e0)).astype(BF16))]
            for block in (16, 32, 64, 128):
                beta = _block_row(p, block, block // 2 if rev else block // 2 - 1)
                if rev:
                    e = _interleave(p, beta, block) - _interleave(beta, p, block)
                    src = _interleave(q, k, block)
                else:
                    e = _interleave(beta, p, block) - _interleave(p, beta, block)
                    src = _interleave(k, q, block)
                m = (src * jnp.exp2(e)).astype(BF16)
                products.append(_dot_nt(m, m))
            a_ref[rows, d * c:(d + 1) * c] = _select_levels(products, rev).astype(BF16)

            edge = p[0:1, :] if rev else p[c - 1:c, :]
            qin_ref[d, rows, :] = (q * jnp.exp2(p)).astype(BF16)
            u_ref[d, h * n_chunks + n] = _dot(v_t, (k * jnp.exp2(edge - p)).astype(BF16))
            dec_ref[d, h * n_chunks + n] = jnp.broadcast_to(jnp.exp2(edge), (8, hd))
        yield progress(1, 1 / len(units))

    for h, n in units:
        v_b = v_ref[rows_in(n), cols(h)]
        acc_ref[rows_sc(h, n), :] = _dot(a_ref[rows_sc(h, n), :], jnp.concatenate([v_b, v_b], axis=0))
        yield progress(2, 1 / len(units))

    for h in range(heads):
        if has_s0:
            states = [s0_ref[0, h].T, s0_ref[1, h].T]
        else:
            states = [jnp.zeros((hd, hd), F32)] * 2
        inter = [[None] * n_chunks, [None] * n_chunks]
        for i in range(n_chunks):
            for d, n in ((0, i), (1, n_chunks - 1 - i)):
                inter[d][n] = _dot_nt(qin_ref[d, rows_sc(h, n), :], states[d].astype(BF16))
                states[d] = dec_ref[d, h * n_chunks + n][0:1, :] * states[d] + u_ref[d, h * n_chunks + n]
        for n in range(n_chunks):
            gate = gr_ref[rows_in(n), cols(h)].astype(F32)
            o = acc_ref[rows_sc(h, n), :] + inter[0][n] + inter[1][n]
            o_ref[rows_in(n), cols(h)] = (_rms(o, hn_ref[...]) * (gate * jax.nn.sigmoid(gate))).astype(BF16)
        if out_state:
            st_ref[layer, 0, h] = states[0].T
            st_ref[layer, 1, h] = states[1].T
        yield progress(3, 1 / heads)
    if out_state:
        for other in range(st_ref.shape[0]):
            if other != layer:
                st_ref[other] = jnp.zeros(st_ref.shape[1:], F32)


def _rec_part(z, zf, batch, seq_len, lower, head_norm, j, s0=None, states=None, heads=None):
    hd = REC_HEAD_DIM
    nh = REC_HEADS
    n_chunks = seq_len // SCAN_CHUNK
    heads = heads or min(nh, max(1, SCAN_UNITS // n_chunks))
    width = heads * hd
    col0 = 2 * POOL_WIDTH // width
    groups = nh // heads

    def zspec(part):
        return pl.BlockSpec((seq_len, width), lambda i: (i // groups, col0 + part * groups + i % groups))

    def fspec(part):
        return pl.BlockSpec((seq_len, width), lambda i: (i // groups, part * groups + i % groups))

    in_specs = [zspec(0), fspec(0), fspec(1), zspec(1), zspec(2),
                pl.BlockSpec((2, heads, 1, hd), lambda i: (0, i % groups, 0, 0)),
                pl.BlockSpec((1, hd), lambda i: (0, 0))]
    args = [z, zf, zf, z, z, lower, head_norm]
    has_s0 = s0 is not None
    if has_s0:
        in_specs.append(pl.BlockSpec((None, None, 2, heads, hd, hd),
                                     lambda i: (i // groups, j, 0, i % groups, 0, 0)))
        args.append(s0)
    out_state = not has_s0
    out_specs = [pl.BlockSpec((seq_len, width), lambda i: (i // groups, i % groups))]
    out_shape = [jax.ShapeDtypeStruct((batch * seq_len, REC_WIDTH), BF16)]
    aliases = {}
    layer = j
    if out_state:
        out_shape.append(jax.ShapeDtypeStruct((batch, N_REC, 2, nh, hd, hd), F32))
        if states is None:
            out_specs.append(pl.BlockSpec((None, N_REC, 2, heads, hd, hd),
                                          lambda i: (i // groups, 0, 0, i % groups, 0, 0)))
        else:
            out_specs.append(pl.BlockSpec((None, 1, 2, heads, hd, hd),
                                          lambda i: (i // groups, j, 0, i % groups, 0, 0)))
            layer = 0
            aliases[len(args)] = 1
            in_specs.append(pl.BlockSpec(memory_space=pl.ANY))
            args.append(states)
    tokens = heads * seq_len
    kernel = functools.partial(_rec_kernel, seq_len=seq_len, heads=heads, has_s0=has_s0, out_state=out_state,
                               n_aliased=len(aliases), layer=layer)
    scratch_shapes = [pltpu.VMEM((tokens, hd), F32),
                      pltpu.VMEM((tokens, hd), F32),
                      pltpu.VMEM((2, tokens, hd), F32),
                      pltpu.VMEM((2, tokens, hd), F32),
                      pltpu.VMEM((tokens, 2 * SCAN_CHUNK), BF16),
                      pltpu.VMEM((2, tokens, hd), BF16),
                      pltpu.VMEM((2, tokens // SCAN_CHUNK, hd, hd), F32),
                      pltpu.VMEM((2, tokens // SCAN_CHUNK, 8, hd), F32),
                      pltpu.VMEM((2, SCAN_CHUNK, SCAN_CHUNK), BF16)]
    return _Part(kernel, batch * groups, in_specs, args, out_specs, out_shape, scratch_shapes, aliases)


def _rope(x, cos, sin_signed):
    lane = lax.broadcasted_iota(jnp.int32, x.shape, 1)
    partner = jnp.where((lane & 1) == 0, pltpu.roll(x, ATT_HEAD_DIM - 1, 1), pltpu.roll(x, 1, 1))
    return x * cos + partner * sin_signed


def _att_kernel(*refs, seq_len, n_keys, latent, n_aliased, layer):
    q_ref, k_ref, v_ref, qn_ref, kn_ref = refs[:5]
    g_refs = refs[5:5 + ATT_KV_HEADS]
    if latent:
        ck_ref, cv_ref, cos_ref, sin_ref, y_ref, kall, vall = refs[5 + ATT_KV_HEADS:-8]
    else:
        y_ref, ko_ref, vo_ref, kall, vall = refs[5 + ATT_KV_HEADS + n_aliased:-8]
    qs_bufs, s_bufs, m_bufs, o_bufs = refs[-8:-6], refs[-6:-4], refs[-4:-2], refs[-2:]
    hd = ATT_HEAD_DIM
    qw = ATT_REP * hd
    tq = ATT_Q_TILE
    tk = min(ATT_KEY_TILE, n_keys)
    n_kb = n_keys // tk
    n_past = n_keys - seq_len
    n_items = ATT_KV_HEADS * (seq_len // tq)
    exp2_scale = hd ** -0.5 * LOG2_E
    assert ATT_KV_HEADS == 2

    if not latent:
        for other in range(ko_ref.shape[0]):
            if other != layer:
                ko_ref[other] = jnp.zeros(ko_ref.shape[1:], F32)
                vo_ref[other] = jnp.zeros(vo_ref.shape[1:], F32)
    lane = lax.broadcasted_iota(jnp.int32, (n_keys, hd), 1)
    for g in range(ATT_KV_HEADS):
        k = _rms(k_ref[:, g * hd:(g + 1) * hd].astype(F32), kn_ref[...])
        v = v_ref[:, g * hd:(g + 1) * hd].astype(F32)
        if latent:
            k = _rope(k, cos_ref[...], sin_ref[...])
            kall[g, 0:n_past, :] = ck_ref[pl.ds(g, n_past, stride=ATT_KV_HEADS), :].astype(BF16)
            vall[g, 0:n_past, 0:hd] = cv_ref[pl.ds(g, n_past, stride=ATT_KV_HEADS), :].astype(BF16)
        else:
            ko_ref[layer, pl.ds(g, seq_len, stride=ATT_KV_HEADS), :] = k
            vo_ref[layer, pl.ds(g, seq_len, stride=ATT_KV_HEADS), :] = v
        kall[g, n_past:n_keys, :] = k.astype(BF16)
        vall[g, n_past:n_keys, 0:hd] = v.astype(BF16)
        vall[g, :, hd:2 * hd] = (lane == 0).astype(BF16)

    def tile_rows(item):
        return pl.ds(pl.multiple_of((item // ATT_KV_HEADS) * tq, tq), tq)

    def prep(item, g):
        rows = tile_rows(item)
        for h in range(ATT_REP):
            q = _rms(q_ref[rows, g * qw + h * hd:g * qw + (h + 1) * hd].astype(F32), qn_ref[...])
            if latent:
                q = _rope(q, cos_ref[rows, :], sin_ref[rows, :])
            qs_bufs[g][h * tq:(h + 1) * tq, :] = q.astype(BF16)

    def scores(item, g):
        qs = qs_bufs[g][...]
        mx = None
        for kb in range(n_kb):
            s = _dot_nt(qs, kall[g, kb * tk:(kb + 1) * tk, :])
            s_bufs[g][:, kb * tk:(kb + 1) * tk] = s
            for c in range(tk // hd):
                part = s[:, c * hd:(c + 1) * hd]
                mx = part if mx is None else jnp.maximum(mx, part)
        m = jnp.max(mx, axis=-1, keepdims=True) * exp2_scale
        m_bufs[g][...] = jnp.broadcast_to(m, (ATT_REP * tq, hd))

    def mix(item, g):
        m = m_bufs[g][...]
        ps = []
        for c in range(n_keys // hd):
            s = s_bufs[g][:, c * hd:(c + 1) * hd]
            ps.append(jnp.exp2(s * exp2_scale - m).astype(BF16))
        o_bufs[g][...] = _dot(jnp.concatenate(ps, axis=1), vall[g])

    def finish(item, g):
        rows = tile_rows(item)
        o = o_bufs[g][:, 0:hd] / o_bufs[g][:, hd:hd + 1]
        for h in range(ATT_REP):
            gate = g_refs[g][rows, h * hd:(h + 1) * hd].astype(F32)
            y = o[h * tq:(h + 1) * tq, :] * (gate * jax.nn.sigmoid(gate))
            y_ref[rows, g * qw + h * hd:g * qw + (h + 1) * hd] = y.astype(BF16)

    stages = (prep, scores, mix, finish)

    def step(k, parity):
        for a, stage in enumerate(stages):
            item = k - a
            if isinstance(item, int) and not 0 <= item < n_items:
                continue
            stage(item, (parity + a) % 2)

    depth = len(stages) - 1
    for k in range(min(depth, n_items + depth)):
        step(k, k % 2)
    n_steady = max(n_items - depth, 0)

    group = ATT_STEADY_STEPS

    def steady_group(j, carry):
        for u in range(group):
            step(depth + group * j + u, (depth + u) % 2)
        return carry

    lax.fori_loop(0, n_steady // group, steady_group, 0)
    for k in range(depth + group * (n_steady // group), n_items + depth):
        step(k, k % 2)


def _attention(z, batch, seq_len, q_norm, k_norm, j, cache=None, new_kv=None):
    hd = ATT_HEAD_DIM
    latent = cache is not None
    qw = ATT_REP * hd
    kv0 = ATT_WIDTH // KV_WIDTH
    gate0 = (ATT_WIDTH + 2 * KV_WIDTH) // qw
    assert (ATT_WIDTH + 2 * KV_WIDTH) % qw == 0 and ATT_WIDTH % KV_WIDTH == 0
    in_specs = [
        pl.BlockSpec((seq_len, ATT_WIDTH), lambda b: (b, 0)),
        pl.BlockSpec((seq_len, KV_WIDTH), lambda b: (b, kv0)),
        pl.BlockSpec((seq_len, KV_WIDTH), lambda b: (b, kv0 + 1)),
        pl.BlockSpec((1, hd), lambda b: (0, 0)),
        pl.BlockSpec((1, hd), lambda b: (0, 0)),
    ] + [pl.BlockSpec((seq_len, qw), lambda b, g=g: (b, gate0 + g)) for g in range(ATT_KV_HEADS)]
    args = [z, z, z, q_norm, k_norm] + [z] * ATT_KV_HEADS
    out_specs = [pl.BlockSpec((seq_len, ATT_WIDTH), lambda b: (b, 0))]
    out_shape = [jax.ShapeDtypeStruct((batch * seq_len, ATT_WIDTH), BF16)]
    n_keys = seq_len
    aliases = {}
    layer = j
    if latent:
        cache_k, cache_v, cos, sin_signed = cache
        past = cache_k.shape[2]
        n_keys += past
        cspec = pl.BlockSpec((None, None, past * ATT_KV_HEADS, hd), lambda b: (b, j, 0, 0))
        tspec = pl.BlockSpec((seq_len, hd), lambda b: (0, 0))
        in_specs += [cspec, cspec, tspec, tspec]
        args += [cache_k.reshape(batch, N_ATT, past * ATT_KV_HEADS, hd),
                 cache_v.reshape(batch, N_ATT, past * ATT_KV_HEADS, hd), cos, sin_signed]
    else:
        out_shape += [jax.ShapeDtypeStruct((batch, N_ATT, seq_len * ATT_KV_HEADS, hd), F32)] * 2
        if new_kv is None:
            kv_spec = pl.BlockSpec((None, N_ATT, seq_len * ATT_KV_HEADS, hd), lambda b: (b, 0, 0, 0))
        else:
            kv_spec = pl.BlockSpec((None, 1, seq_len * ATT_KV_HEADS, hd), lambda b: (b, j, 0, 0))
            layer = 0
            aliases = {len(args): 1, len(args) + 1: 2}
            in_specs += [pl.BlockSpec(memory_space=pl.ANY)] * 2
            args += list(new_kv)
        out_specs += [kv_spec, kv_spec]
    rows = ATT_REP * ATT_Q_TILE
    return pl.pallas_call(
        functools.partial(_att_kernel, seq_len=seq_len, n_keys=n_keys, latent=latent, n_aliased=len(aliases),
                          layer=layer),
        grid=(batch,),
        in_specs=in_specs,
        out_specs=out_specs,
        out_shape=out_shape,
        input_output_aliases=aliases,
        scratch_shapes=[pltpu.VMEM((ATT_KV_HEADS, n_keys, hd), BF16),
                        pltpu.VMEM((ATT_KV_HEADS, n_keys, 2 * hd), BF16),
                        pltpu.VMEM((rows, hd), BF16), pltpu.VMEM((rows, hd), BF16),
                        pltpu.VMEM((rows, n_keys), F32), pltpu.VMEM((rows, n_keys), F32),
                        pltpu.VMEM((rows, hd), F32), pltpu.VMEM((rows, hd), F32),
                        pltpu.VMEM((rows, 2 * hd), F32), pltpu.VMEM((rows, 2 * hd), F32)],
        compiler_params=_cparams(1),
        name="attention",
    )(*args)


def _rope_tables(n_tokens):
    t = jnp.arange(n_tokens)
    row = (t // GRID_W).astype(F32)
    col = (t % GRID_W).astype(F32)
    inv = ROPE_THETA ** (-jnp.arange(0, AXIS_DIM, 2, dtype=F32) / AXIS_DIM)
    ang = jnp.concatenate([row[:, None] * inv[None, :], col[:, None] * inv[None, :]], axis=-1)
    cos = jnp.repeat(jnp.cos(ang), 2, axis=-1)
    sin = jnp.repeat(jnp.sin(ang), 2, axis=-1)
    sign = jnp.where(jnp.arange(ATT_HEAD_DIM) % 2 == 0, -1.0, 1.0).astype(F32)
    return cos, sin * sign


def kernel(x_prompt, x_sample, c, state_hgrn, cache_k, cache_v, c_ctx, ada_w, ada_b, norm_pre, norm_post, rec_w_in, rec_lb_logits, rec_head_norm, pool_w, pool_scale, rec_w_out, att_w_in, att_q_norm, att_k_norm, att_w_out):
    nb_c, len_c, _ = x_prompt.shape
    nb_l, len_l, _ = x_sample.shape

    lb_p = jax.nn.softmax(rec_lb_logits.astype(F32), axis=0)
    lower_bounds = jnp.clip(jnp.cumsum(lb_p, axis=0) - lb_p[0], 0.0, 1.0)
    lower_bounds = lower_bounds.reshape(N_REC, 2, REC_HEADS, 1, REC_HEAD_DIM)
    cos, sin_signed = _rope_tables(len_l)

    cvec = jnp.zeros((MOD_ROWS, D_MODEL), F32).at[0].set(c_ctx).at[1:1 + nb_l].set(c)

    forget_cols = (2 * POOL_WIDTH + REC_WIDTH, 2 * POOL_WIDTH + 3 * REC_WIDTH)
    layers = []
    for i in range(DEPTH):
        rec = i % 2 == 0
        layers.append(dict(
            f32_cols=forget_cols if rec else (0, 0),
            gain_pre=norm_pre[i].reshape(1, D_MODEL), gain_post=norm_post[i].reshape(1, D_MODEL)))

    weights = {("in", 0): rec_w_in[0].astype(BF16)}

    def weight_source(kind, i):
        stacks = (rec_w_in, rec_w_out) if i % 2 == 0 else (att_w_in, att_w_out)
        return stacks[kind == "out"], i // 2

    def cast_part(keys, steps):
        keys = [k for k in keys if k not in weights and k[1] < DEPTH]
        return keys, _cast_part([weight_source(*k) for k in keys], steps)

    def pre_args(i):
        return (weights["in", i], i, layers[i]["gain_pre"], layers[i]["f32_cols"])

    xs = [x_prompt.reshape(nb_c * len_c, D_MODEL), x_sample.reshape(nb_l * len_l, D_MODEL)]
    streams = ((nb_c, len_c, 0), (nb_l, len_l, 1))
    zs = [None, None]
    ys = [None, None]
    new_states, new_kv = None, None

    def proj_part(s, i, steps=None, mod=None):
        nb, sl, row0 = streams[s]
        post = ys[s] + (weights["out", i - 1], i - 1, layers[i - 1]["gain_post"]) if i > 0 else None
        return _proj_part(xs[s], mods if mod is None else mod, sl, row0, post=post,
                          pre=pre_args(i) if i < DEPTH else None, steps=steps)

    def take_proj(s, i, outs):
        if i > 0:
            xs[s] = outs.pop(0)
        zs[s] = outs

    def rec_part(s, i, heads=None):
        nb, sl, _ = streams[s]
        j = i // 2
        hn = rec_head_norm[j].reshape(1, REC_HEAD_DIM)
        z, zf = zs[s]
        if s == 0:
            return _rec_part(z, zf, nb, sl, lower_bounds[j], hn, j, states=new_states, heads=heads)
        return _rec_part(z, zf, nb, sl, lower_bounds[j], hn, j, s0=state_hgrn, heads=heads)

    def pool_part(s, i):
        nb, sl, _ = streams[s]
        j = i // 2
        return _pool_part(zs[s][0], nb, sl, pool_w[j].astype(BF16), pool_scale[j].reshape(1, POOL_WIDTH))

    mod0 = _mod_rows(_run(_mod_part(cvec, ada_w, ada_b, 0, 1), name="modulation")[0][0])
    first = proj_part(0, 0, mod=mod0, steps=DEPTH * 3 * D_MODEL // MOD_COLS)
    keys, casts = cast_part([("out", 0), ("in", 1)], first.steps)
    outs, (mods,), cast = _run(first, _mod_part(cvec, ada_w, ada_b, 0, DEPTH), casts, name="proj_modulation")
    weights.update(zip(keys, cast))
    mods = _mod_rows(mods)
    take_proj(0, 0, outs)
    for i in range(DEPTH):
        j = i // 2
        if i % 2 == 0:
            rec_c = rec_part(0, i)
            keys, casts = cast_part([("out", i), ("in", i + 1)], rec_c.steps)
            (y_rec, new_states), outs, cast, (y_pool,) = _run(
                rec_c, proj_part(1, i, steps=rec_c.steps), casts, pool_part(0, i), name="rec_pool_proj")
            weights.update(zip(keys, cast))
            ys[0] = (y_pool, 0, y_rec, 0)
            take_proj(1, i, outs)
            rec_l = rec_part(1, i, heads=REC_HEADS * streams[1][0] // rec_c.steps)
            keys, casts = cast_part([("out", i + 1), ("in", i + 2)], rec_l.steps)
            (y_rec,), outs, cast = _run(rec_l, proj_part(0, i + 1, steps=rec_l.steps), casts, name="rec_proj")
            weights.update(zip(keys, cast))
            ys[1] = (_run(pool_part(1, i), name="pool_mixer")[0][0], 0, y_rec, 0)
            take_proj(0, i + 1, outs)
        else:
            qn = att_q_norm[j].reshape(1, ATT_HEAD_DIM)
            kn = att_k_norm[j].reshape(1, ATT_HEAD_DIM)
            y, *new_kv = _attention(zs[0][0], nb_c, len_c, qn, kn, j, new_kv=new_kv)
            ys[0] = (y, 0, y, 1)
            take_proj(1, i, _run(proj_part(1, i), name="proj")[0])
            (y,) = _attention(zs[1][0], nb_l, len_l, qn, kn, j, cache=(cache_k, cache_v, cos, sin_signed))
            ys[1] = (y, 0, y, 1)
            take_proj(0, i + 1, _run(proj_part(0, i + 1), name="proj")[0])
    take_proj(1, DEPTH, _run(proj_part(1, DEPTH), name="proj")[0])
    xc, xl = xs

    kv_shape = (nb_c, N_ATT, len_c, ATT_KV_HEADS, ATT_HEAD_DIM)
    return (xc.reshape(nb_c, len_c, D_MODEL), xl.reshape(nb_l, len_l, D_MODEL),
            new_states, new_kv[0].reshape(kv_shape), new_kv[1].reshape(kv_shape))
```

```python
import functools
from typing import Any, NamedTuple

import jax
import jax.numpy as jnp
from jax import lax
from jax.experimental import pallas as pl
from jax.experimental.pallas import tpu as pltpu

D_MODEL = 1024
DEPTH = 4
GRID_W = 64
N_REC = (DEPTH + 1) // 2
N_ATT = DEPTH // 2
POOL_WIDTH = D_MODEL // 2
POOL_WINDOWS = (2, 4, 8, 16)
POOL_GROUP_DIM = POOL_WIDTH // len(POOL_WINDOWS)
REC_WIDTH = D_MODEL // 2
REC_HEAD_DIM = 128
REC_HEADS = REC_WIDTH // REC_HEAD_DIM
REC_IN_WIDTH = 2 * POOL_WIDTH + 5 * REC_WIDTH
ATT_HEAD_DIM = 128
ATT_HEADS = D_MODEL // ATT_HEAD_DIM
ATT_KV_HEADS = 2
ATT_REP = ATT_HEADS // ATT_KV_HEADS
ATT_WIDTH = ATT_HEADS * ATT_HEAD_DIM
KV_WIDTH = ATT_KV_HEADS * ATT_HEAD_DIM
ATT_IN_WIDTH = 2 * ATT_WIDTH + 2 * KV_WIDTH
AXIS_DIM = ATT_HEAD_DIM // 2
ROPE_THETA = 10000.0
EPS = 1e-6
F_MIN = 1e-6

MOD_ROWS = 16
TOKEN_TILE = 1024
MIN_PROJ_STEPS = 4
PROJ_GROUP_ROWS = 512
POOL_ROWS = 256
IN_PROJ_STEP = 512
MOD_COLS = 1536
SCAN_CHUNK = 128
SCAN_UNITS = 8
POOL_HALO = 128
ATT_Q_TILE = 256
ATT_KEY_TILE = 128
ATT_STEADY_STEPS = 2
LOG2_E = 1.4426950408889634
VMEM_LIMIT = 56 * 1024 * 1024

F32 = jnp.float32
BF16 = jnp.bfloat16


def _cparams(n_axes):
    return pltpu.CompilerParams(
        dimension_semantics=("arbitrary",) * n_axes, vmem_limit_bytes=VMEM_LIMIT)


def _rms(x, g):
    return x * lax.rsqrt(jnp.mean(x * x, axis=-1, keepdims=True) + EPS) * g


def _dot(a, b):
    return jnp.dot(a, b, preferred_element_type=F32)


def _dot_nt(a, b):
    return lax.dot_general(a, b, (((1,), (1,)), ((), ())), preferred_element_type=F32)


class _Part(NamedTuple):
    kernel: Any
    steps: int
    in_specs: list
    args: list
    out_specs: list
    out_shape: list
    scratch_shapes: list
    aliases: dict


def _run(*parts, name):
    steps = parts[0].steps
    assert all(p.steps == steps for p in parts)
    n_in = [len(p.args) for p in parts]
    n_out = [len(p.out_shape) for p in parts]
    n_scr = [len(p.scratch_shapes) for p in parts]

    def body(*refs):
        ins, outs, scr = refs[:sum(n_in)], refs[sum(n_in):sum(n_in) + sum(n_out)], refs[sum(n_in) + sum(n_out):]
        pending = {}
        for k, p in enumerate(parts):
            i0, o0, s0 = sum(n_in[:k]), sum(n_out[:k]), sum(n_scr[:k])
            pending[k] = (0.0, p.kernel(*ins[i0:i0 + n_in[k]], *outs[o0:o0 + n_out[k]], *scr[s0:s0 + n_scr[k]]))
        while pending:
            k = min(pending, key=lambda k: pending[k][0])
            try:
                pending[k] = (next(pending[k][1]), pending[k][1])
            except StopIteration:
                del pending[k]

    aliases = {}
    for k, p in enumerate(parts):
        for i, o in p.aliases.items():
            aliases[sum(n_in[:k]) + i] = sum(n_out[:k]) + o
    outs = pl.pallas_call(
        body,
        grid=(steps,),
        in_specs=[s for p in parts for s in p.in_specs],
        out_specs=[s for p in parts for s in p.out_specs],
        out_shape=[s for p in parts for s in p.out_shape],
        scratch_shapes=[s for p in parts for s in p.scratch_shapes],
        input_output_aliases=aliases,
        compiler_params=_cparams(1),
        name=name,
    )(*[a for p in parts for a in p.args])
    return [list(outs[sum(n_out[:k]):sum(n_out[:k]) + n_out[k]]) for k in range(len(parts))]


def _mod_kernel(cv_ref, w_ref, b_ref, o_ref):
    cv = cv_ref[...]
    a = (cv * jax.nn.sigmoid(cv)).astype(BF16)
    o_ref[...] = _dot(a, w_ref[...].astype(BF16)) + b_ref[...]
    yield 1.0


def _mod_part(cvec, ada_w, ada_b, layer0, n_layers):
    tn = MOD_COLS
    blocks = 3 * D_MODEL // tn
    in_specs = [
        pl.BlockSpec((MOD_ROWS, D_MODEL), lambda i: (0, 0)),
        pl.BlockSpec((None, D_MODEL, tn), lambda i: (layer0 + i // blocks, 0, i % blocks)),
        pl.BlockSpec((None, 1, tn), lambda i: (layer0 + i // blocks, 0, i % blocks)),
    ]
    out_specs = [pl.BlockSpec((None, MOD_ROWS, tn), lambda i: (i // blocks, 0, i % blocks))]
    out_shape = [jax.ShapeDtypeStruct((n_layers, MOD_ROWS, 3 * D_MODEL), F32)]
    args = [cvec, ada_w, ada_b.reshape(DEPTH, 1, 3 * D_MODEL)]
    return _Part(_mod_kernel, n_layers * blocks, in_specs, args, out_specs, out_shape, [], {})


def _cast_kernel(*refs):
    n = len(refs) // 2
    for w_ref, o_ref in zip(refs[:n], refs[n:]):
        o_ref[...] = w_ref[...].astype(BF16)
    yield 1.0


def _cast_part(sources, steps):
    in_specs, out_specs, out_shape, args = [], [], [], []
    for stack, j in sources:
        _, k, n = stack.shape
        in_specs.append(pl.BlockSpec((None, k // steps, n), lambda i, j=j: (j, i, 0)))
        out_specs.append(pl.BlockSpec((k // steps, n), lambda i: (i, 0)))
        out_shape.append(jax.ShapeDtypeStruct((k, n), BF16))
        args.append(stack)
    return _Part(_cast_kernel, steps, in_specs, args, out_specs, out_shape, [], {})


def _mod_rows(out):
    return out.reshape(out.shape[0], MOD_ROWS, 3, 1, D_MODEL)


def _mod_spec(layer, seq_len, row0, tile):
    assert row0 == 0 or seq_len % tile == 0
    tiles_per_seq = seq_len // tile
    if row0 == 0:
        index = lambda i: (layer, 0, 0, 0, 0)
    else:
        index = lambda i: (layer, row0 + i // tiles_per_seq, 0, 0, 0)
    return pl.BlockSpec((None, None, 3, 1, D_MODEL), index)


def _proj_kernel(*refs, post, pre, n_out, f32_cols):
    refs = list(refs)
    x_ref = refs.pop(0)
    if post:
        ya_ref, yb_ref, wo_ref, modp_ref, gpost_ref = (refs.pop(0) for _ in range(5))
    if pre:
        modn_ref, gpre_ref, wi_ref = (refs.pop(0) for _ in range(3))
    xo_ref = refs.pop(0) if post else None
    rows_all = x_ref.shape[0]
    n_groups = max(1, rows_all // PROJ_GROUP_ROWS)

    def group(rows):
        x = x_ref[rows, :]
        if post:
            half = D_MODEL // 2
            p = _dot(ya_ref[rows, :], wo_ref[0:half, :]) + _dot(yb_ref[rows, :], wo_ref[half:D_MODEL, :])
            x = x + modp_ref[2] * _rms(p, gpost_ref[...])
            xo_ref[rows, :] = x
            yield
        if pre:
            h = (_rms(x, gpre_ref[...]) * (1.0 + modn_ref[1]) + modn_ref[0]).astype(BF16)
            lo, hi = f32_cols
            for c0 in range(0, n_out, IN_PROJ_STEP):
                y = _dot(h, wi_ref[:, c0:c0 + IN_PROJ_STEP])
                if lo <= c0 < hi:
                    refs[1][rows, c0 - lo:c0 - lo + IN_PROJ_STEP] = y
                else:
                    c1 = c0 if c0 < lo else c0 - (hi - lo)
                    refs[0][rows, c1:c1 + IN_PROJ_STEP] = y.astype(BF16)
                yield

    size = rows_all // n_groups
    waiting = [group(slice(k * size, (k + 1) * size)) for k in range(n_groups)]
    running = []
    pieces = n_groups * (post + n_out // IN_PROJ_STEP)
    done = 0
    while waiting or running:
        if waiting:
            running.append(waiting.pop(0))
        for g in list(running):
            try:
                next(g)
                done += 1
            except StopIteration:
                running.remove(g)
        yield min(done / pieces, 1.0)


def _proj_part(x, mod, seq_len, row0, post=None, pre=None, steps=None):
    t = x.shape[0]
    tile = t // steps if steps else min(TOKEN_TILE, t // MIN_PROJ_STEPS)
    half = D_MODEL // 2
    row = lambda i: (i, 0)
    fixed = lambda i: (0, 0)
    resident = dict(index_map=fixed, pipeline_mode=pl.Buffered(1))
    in_specs = [pl.BlockSpec((tile, D_MODEL), row)]
    args = [x]
    out_specs, out_shape = [], []
    n_out, f32_cols = 0, (0, 0)
    if post:
        ya, ia, yb, ib, w_out, layer, gain = post
        in_specs += [pl.BlockSpec((tile, half), lambda i: (i, ia)),
                     pl.BlockSpec((tile, half), lambda i: (i, ib)),
                     pl.BlockSpec((D_MODEL, D_MODEL), **resident),
                     _mod_spec(layer, seq_len, row0, tile),
                     pl.BlockSpec((1, D_MODEL), fixed)]
        args += [ya, yb, w_out, mod, gain]
        out_specs.append(pl.BlockSpec((tile, D_MODEL), row))
        out_shape.append(jax.ShapeDtypeStruct((t, D_MODEL), F32))
    if pre:
        w_in, layer, gain, f32_cols = pre
        n_out = w_in.shape[1]
        lo, hi = f32_cols
        assert lo % IN_PROJ_STEP == 0 and hi % IN_PROJ_STEP == 0 and n_out % IN_PROJ_STEP == 0
        in_specs += [_mod_spec(layer, seq_len, row0, tile),
                     pl.BlockSpec((1, D_MODEL), fixed),
                     pl.BlockSpec((D_MODEL, n_out), **resident)]
        args += [mod, gain, w_in]
        out_specs.append(pl.BlockSpec((tile, n_out - (hi - lo)), row))
        out_shape.append(jax.ShapeDtypeStruct((t, n_out - (hi - lo)), BF16))
        if hi > lo:
            out_specs.append(pl.BlockSpec((tile, hi - lo), row))
            out_shape.append(jax.ShapeDtypeStruct((t, hi - lo), F32))
    kernel = functools.partial(_proj_kernel, post=bool(post), pre=bool(pre), n_out=n_out, f32_cols=f32_cols)
    return _Part(kernel, t // tile, in_specs, args, out_specs, out_shape, [], {})


def _pool_kernel(u_ref, gp_ref, pw_ref, ps_ref, o_ref, pad_ref, band_ref, sum_ref, dif_ref, *, seq_len):
    gd = POOL_GROUP_DIM
    rows = POOL_ROWS
    halo = POOL_HALO
    span = rows + 2 * halo

    @pl.when(pl.program_id(0) == 0)
    def _():
        r = lax.broadcasted_iota(jnp.int32, (rows, span), 0)
        c = lax.broadcasted_iota(jnp.int32, (rows, span), 1)
        offset = c - halo - r
        for g, win in enumerate(POOL_WINDOWS):
            band_ref[g] = jnp.where((offset >= -(win // 2)) & (offset < win // 2), 1.0, 0.0).astype(BF16)

    zeros = jnp.zeros((halo, POOL_WIDTH), BF16)
    pad_ref[0:halo, :] = zeros
    pad_ref[halo + seq_len:2 * halo + seq_len, :] = zeros
    pad_ref[halo:halo + seq_len, :] = u_ref[...]
    units = [(g, r * rows) for g in range(len(POOL_WINDOWS)) for r in range(seq_len // rows)]

    def cols(g):
        return slice(g * gd, (g + 1) * gd)

    for g, base in units:
        sum_ref[base:base + rows, cols(g)] = _dot(band_ref[g], pad_ref[base:base + span, cols(g)])
    yield 0.3
    for g, base in units:
        win = POOL_WINDOWS[g]
        t = base + lax.broadcasted_iota(jnp.int32, (rows, 1), 0)
        count = jnp.clip(t + win // 2, 0, seq_len) - jnp.clip(t - win // 2, 0, seq_len)
        mean = sum_ref[base:base + rows, cols(g)] / count.astype(F32)
        d = mean - pad_ref[halo + base:halo + base + rows, cols(g)].astype(F32)
        dif_ref[base:base + rows, cols(g)] = d.astype(BF16)
    yield 0.6
    for g, base in units:
        y = _dot(dif_ref[base:base + rows, cols(g)], pw_ref[g]) * ps_ref[:, cols(g)]
        gate = gp_ref[base:base + rows, cols(g)].astype(F32)
        o_ref[base:base + rows, cols(g)] = (y * (gate * jax.nn.sigmoid(gate))).astype(BF16)
    yield 1.0


def _pool_part(z, batch, seq_len, pool_w, pool_scale):
    in_specs = [
        pl.BlockSpec((seq_len, POOL_WIDTH), lambda b: (b, 0)),
        pl.BlockSpec((seq_len, POOL_WIDTH), lambda b: (b, 1)),
        pl.BlockSpec((len(POOL_WINDOWS), POOL_GROUP_DIM, POOL_GROUP_DIM), lambda b: (0, 0, 0)),
        pl.BlockSpec((1, POOL_WIDTH), lambda b: (0, 0)),
    ]
    out_specs = [pl.BlockSpec((seq_len, POOL_WIDTH), lambda b: (b, 0))]
    out_shape = [jax.ShapeDtypeStruct((batch * seq_len, POOL_WIDTH), BF16)]
    scratch_shapes = [pltpu.VMEM((seq_len + 2 * POOL_HALO, POOL_WIDTH), BF16),
                      pltpu.VMEM((len(POOL_WINDOWS), POOL_ROWS, POOL_ROWS + 2 * POOL_HALO), BF16),
                      pltpu.VMEM((seq_len, POOL_WIDTH), F32), pltpu.VMEM((seq_len, POOL_WIDTH), BF16)]
    kernel = functools.partial(_pool_kernel, seq_len=seq_len)
    return _Part(kernel, batch, in_specs, [z, z, pool_w, pool_scale], out_specs, out_shape, scratch_shapes, {})


def _block_row(p, block, row):
    c, w = p.shape
    p3 = p.reshape(c // block, block, w)
    return jnp.broadcast_to(p3[:, row:row + 1, :], p3.shape).reshape(c, w)


def _interleave(lo, hi, block):
    half = block // 2
    parts = []
    for b in range(SCAN_CHUNK // block):
        parts.append(lo[b * block:b * block + half])
        parts.append(hi[b * block + half:(b + 1) * block])
    return jnp.concatenate(parts, axis=0)


def _select_levels(products, rev):
    c = SCAN_CHUNK
    lane = lax.broadcasted_iota(jnp.int32, (8, c), 1)
    sub = lax.broadcasted_iota(jnp.int32, (8, c), 0)
    out = []
    for t0 in range(0, c, 8):
        if rev:
            keep = (lane >= t0 + sub) & (lane < t0 + 8)
        else:
            keep = (lane >= t0) & (lane <= t0 + sub)
        row = jnp.where(keep, products[0][t0:t0 + 8, :], 0.0)
        for lv, block in enumerate((16, 32, 64, 128), 1):
            half = block // 2
            start = t0 // block * block
            upper = t0 - start >= half
            if upper == rev:
                continue
            lo = start + half if rev else start
            row = jnp.where((lane >= lo) & (lane < lo + half), products[lv][t0:t0 + 8, :], row)
        out.append(row)
    return jnp.concatenate(out, axis=0)


def _rec_kernel(*refs, seq_len, heads, has_s0, out_state, n_aliased, layer):
    q_ref, ff_ref, fb_ref, v_ref, gr_ref, lb_ref, hn_ref = refs[:7]
    rest = list(refs[7:])
    s0_ref = rest.pop(0) if has_s0 else None
    del rest[:n_aliased]
    o_ref = rest.pop(0)
    st_ref = rest.pop(0) if out_state else None
    acc_ref, qs_ref, k_ref, p_ref, a_ref, qin_ref, u_ref, dec_ref, tri_ref = rest

    c = SCAN_CHUNK
    hd = REC_HEAD_DIM
    n_chunks = seq_len // c
    zf_refs = (ff_ref, fb_ref)
    units = [(h, n) for h in range(heads) for n in range(n_chunks)]

    def rows_in(n):
        return slice(n * c, (n + 1) * c)

    def rows_sc(h, n):
        return slice((h * n_chunks + n) * c, (h * n_chunks + n + 1) * c)

    def cols(h):
        return slice(h * hd, (h + 1) * hd)

    ti = lax.broadcasted_iota(jnp.int32, (c, c), 0)
    si = lax.broadcasted_iota(jnp.int32, (c, c), 1)
    for d, causal in enumerate((si <= ti, si >= ti)):
        tri_ref[d] = jnp.where(causal, 1.0, 0.0).astype(BF16)

    weights = (0.26, 0.43, 0.09, 0.22)
    done = [0.0]

    def progress(stage, share):
        done[0] += weights[stage] * share
        return done[0]

    for h, n in units:
        qz = q_ref[rows_in(n), cols(h)].astype(F32)
        qs_ref[rows_sc(h, n), :] = qz * jax.nn.sigmoid(qz)
        for d in range(2):
            lower = lb_ref[d, h]
            f = jnp.clip(lower + (1.0 - lower) * jax.nn.sigmoid(zf_refs[d][rows_in(n), cols(h)]), F_MIN, 1.0)
            k_ref[d, rows_sc(h, n), :] = 1.0 - f
            g = jnp.log(f) * LOG2_E
            g_hi = g.astype(BF16)
            rest = g - g_hi.astype(F32)
            g_mid = rest.astype(BF16)
            g_lo = (rest - g_mid.astype(F32)).astype(BF16)
            sums = _dot(tri_ref[d], jnp.concatenate([g_hi, g_mid, g_lo], axis=1))
            p_ref[d, rows_sc(h, n), :] = (sums[:, 0:hd] + sums[:, hd:2 * hd]) + sums[:, 2 * hd:3 * hd]
        yield progress(0, 1 / len(units))

    for h, n in units:
        rows = rows_sc(h, n)
        q = qs_ref[rows, :]
        v_t = v_ref[rows_in(n), cols(h)].astype(F32).T.astype(BF16)
        for d in range(2):
            rev = d == 1
            k = k_ref[d, rows, :]
            p = p_ref[d, rows, :]
            e0 = p - _block_row(p, 8, 4 if rev else 3)
            products = [_dot_nt((q * jnp.exp2(e0)).astype(BF16), (k * jnp.exp2(-e0)).astype(BF16))]
            for block in (16, 32, 64, 128):
                beta = _block_row(p, block, block // 2 if rev else block // 2 - 1)
                if rev:
                    e = _interleave(p, beta, block) - _interleave(beta, p, block)
                    src = _interleave(q, k, block)
                else:
                    e = _interleave(beta, p, block) - _interleave(p, beta, block)
                    src = _interleave(k, q, block)
                m = (src * jnp.exp2(e)).astype(BF16)
                products.append(_dot_nt(m, m))
            a_ref[rows, d * c:(d + 1) * c] = _select_levels(products, rev).astype(BF16)

            edge = p[0:1, :] if rev else p[c - 1:c, :]
            qin_ref[d, rows, :] = (q * jnp.exp2(p)).astype(BF16)
            u_ref[d, h * n_chunks + n] = _dot(v_t, (k * jnp.exp2(edge - p)).astype(BF16))
            dec_ref[d, h * n_chunks + n] = jnp.broadcast_to(jnp.exp2(edge), (8, hd))
        yield progress(1, 1 / len(units))

    for h, n in units:
        v_b = v_ref[rows_in(n), cols(h)]
        acc_ref[rows_sc(h, n), :] = _dot(a_ref[rows_sc(h, n), :], jnp.concatenate([v_b, v_b], axis=0))
        yield progress(2, 1 / len(units))

    for h in range(heads):
        if has_s0:
            states = [s0_ref[0, h].T, s0_ref[1, h].T]
        else:
            states = [jnp.zeros((hd, hd), F32)] * 2
        inter = [[None] * n_chunks, [None] * n_chunks]
        for i in range(n_chunks):
            for d, n in ((0, i), (1, n_chunks - 1 - i)):
                inter[d][n] = _dot_nt(qin_ref[d, rows_sc(h, n), :], states[d].astype(BF16))
                states[d] = dec_ref[d, h * n_chunks + n][0:1, :] * states[d] + u_ref[d, h * n_chunks + n]
        for n in range(n_chunks):
            gate = gr_ref[rows_in(n), cols(h)].astype(F32)
            o = acc_ref[rows_sc(h, n), :] + inter[0][n] + inter[1][n]
            o_ref[rows_in(n), cols(h)] = (_rms(o, hn_ref[...]) * (gate * jax.nn.sigmoid(gate))).astype(BF16)
        if out_state:
            st_ref[layer, 0, h] = states[0].T
            st_ref[layer, 1, h] = states[1].T
        yield progress(3, 1 / heads)
    if out_state:
        for other in range(st_ref.shape[0]):
            if other != layer:
                st_ref[other] = jnp.zeros(st_ref.shape[1:], F32)


def _rec_part(z, zf, batch, seq_len, lower, head_norm, j, s0=None, states=None, heads=None):
    hd = REC_HEAD_DIM
    nh = REC_HEADS
    n_chunks = seq_len // SCAN_CHUNK
    heads = heads or min(nh, max(1, SCAN_UNITS // n_chunks))
    width = heads * hd
    col0 = 2 * POOL_WIDTH // width
    groups = nh // heads

    def zspec(part):
        return pl.BlockSpec((seq_len, width), lambda i: (i // groups, col0 + part * groups + i % groups))

    def fspec(part):
        return pl.BlockSpec((seq_len, width), lambda i: (i // groups, part * groups + i % groups))

    in_specs = [zspec(0), fspec(0), fspec(1), zspec(1), zspec(2),
                pl.BlockSpec((2, heads, 1, hd), lambda i: (0, i % groups, 0, 0)),
                pl.BlockSpec((1, hd), lambda i: (0, 0))]
    args = [z, zf, zf, z, z, lower, head_norm]
    has_s0 = s0 is not None
    if has_s0:
        in_specs.append(pl.BlockSpec((None, None, 2, heads, hd, hd),
                                     lambda i: (i // groups, j, 0, i % groups, 0, 0)))
        args.append(s0)
    out_state = not has_s0
    out_specs = [pl.BlockSpec((seq_len, width), lambda i: (i // groups, i % groups))]
    out_shape = [jax.ShapeDtypeStruct((batch * seq_len, REC_WIDTH), BF16)]
    aliases = {}
    layer = j
    if out_state:
        out_shape.append(jax.ShapeDtypeStruct((batch, N_REC, 2, nh, hd, hd), F32))
        if states is None:
            out_specs.append(pl.BlockSpec((None, N_REC, 2, heads, hd, hd),
                                          lambda i: (i // groups, 0, 0, i % groups, 0, 0)))
        else:
            out_specs.append(pl.BlockSpec((None, 1, 2, heads, hd, hd),
                                          lambda i: (i // groups, j, 0, i % groups, 0, 0)))
            layer = 0
            aliases[len(args)] = 1
            in_specs.append(pl.BlockSpec(memory_space=pl.ANY))
            args.append(states)
    tokens = heads * seq_len
    kernel = functools.partial(_rec_kernel, seq_len=seq_len, heads=heads, has_s0=has_s0, out_state=out_state,
                               n_aliased=len(aliases), layer=layer)
    scratch_shapes = [pltpu.VMEM((tokens, hd), F32),
                      pltpu.VMEM((tokens, hd), F32),
                      pltpu.VMEM((2, tokens, hd), F32),
                      pltpu.VMEM((2, tokens, hd), F32),
                      pltpu.VMEM((tokens, 2 * SCAN_CHUNK), BF16),
                      pltpu.VMEM((2, tokens, hd), BF16),
                      pltpu.VMEM((2, tokens // SCAN_CHUNK, hd, hd), F32),
                      pltpu.VMEM((2, tokens // SCAN_CHUNK, 8, hd), F32),
                      pltpu.VMEM((2, SCAN_CHUNK, SCAN_CHUNK), BF16)]
    return _Part(kernel, batch * groups, in_specs, args, out_specs, out_shape, scratch_shapes, aliases)


def _rope(x, cos, sin_signed):
    lane = lax.broadcasted_iota(jnp.int32, x.shape, 1)
    partner = jnp.where((lane & 1) == 0, pltpu.roll(x, ATT_HEAD_DIM - 1, 1), pltpu.roll(x, 1, 1))
    return x * cos + partner * sin_signed


def _att_kernel(*refs, seq_len, n_keys, latent, n_aliased, layer):
    q_ref, k_ref, v_ref, qn_ref, kn_ref = refs[:5]
    g_refs = refs[5:5 + ATT_KV_HEADS]
    if latent:
        ck_ref, cv_ref, cos_ref, sin_ref, y_ref, kall, vall = refs[5 + ATT_KV_HEADS:-8]
    else:
        y_ref, ko_ref, vo_ref, kall, vall = refs[5 + ATT_KV_HEADS + n_aliased:-8]
    qs_bufs, s_bufs, m_bufs, o_bufs = refs[-8:-6], refs[-6:-4], refs[-4:-2], refs[-2:]
    hd = ATT_HEAD_DIM
    qw = ATT_REP * hd
    tq = ATT_Q_TILE
    tk = min(ATT_KEY_TILE, n_keys)
    n_kb = n_keys // tk
    n_past = n_keys - seq_len
    n_items = ATT_KV_HEADS * (seq_len // tq)
    exp2_scale = hd ** -0.5 * LOG2_E
    assert ATT_KV_HEADS == 2

    if not latent:
        for other in range(ko_ref.shape[0]):
            if other != layer:
                ko_ref[other] = jnp.zeros(ko_ref.shape[1:], F32)
                vo_ref[other] = jnp.zeros(vo_ref.shape[1:], F32)
    lane = lax.broadcasted_iota(jnp.int32, (n_keys, hd), 1)
    for g in range(ATT_KV_HEADS):
        k = _rms(k_ref[:, g * hd:(g + 1) * hd].astype(F32), kn_ref[...])
        v = v_ref[:, g * hd:(g + 1) * hd].astype(F32)
        if latent:
            k = _rope(k, cos_ref[...], sin_ref[...])
            kall[g, 0:n_past, :] = ck_ref[pl.ds(g, n_past, stride=ATT_KV_HEADS), :].astype(BF16)
            vall[g, 0:n_past, 0:hd] = cv_ref[pl.ds(g, n_past, stride=ATT_KV_HEADS), :].astype(BF16)
        else:
            ko_ref[layer, pl.ds(g, seq_len, stride=ATT_KV_HEADS), :] = k
            vo_ref[layer, pl.ds(g, seq_len, stride=ATT_KV_HEADS), :] = v
        kall[g, n_past:n_keys, :] = k.astype(BF16)
        vall[g, n_past:n_keys, 0:hd] = v.astype(BF16)
        vall[g, :, hd:2 * hd] = (lane == 0).astype(BF16)

    def tile_rows(item):
        return pl.ds(pl.multiple_of((item // ATT_KV_HEADS) * tq, tq), tq)

    def prep(item, g):
        rows = tile_rows(item)
        for h in range(ATT_REP):
            q = _rms(q_ref[rows, g * qw + h * hd:g * qw + (h + 1) * hd].astype(F32), qn_ref[...])
            if latent:
                q = _rope(q, cos_ref[rows, :], sin_ref[rows, :])
            qs_bufs[g][h * tq:(h + 1) * tq, :] = q.astype(BF16)

    def scores(item, g):
        qs = qs_bufs[g][...]
        mx = None
        for kb in range(n_kb):
            s = _dot_nt(qs, kall[g, kb * tk:(kb + 1) * tk, :])
            s_bufs[g][:, kb * tk:(kb + 1) * tk] = s
            for c in range(tk // hd):
                part = s[:, c * hd:(c + 1) * hd]
                mx = part if mx is None else jnp.maximum(mx, part)
        m = jnp.max(mx, axis=-1, keepdims=True) * exp2_scale
        m_bufs[g][...] = jnp.broadcast_to(m, (ATT_REP * tq, hd))

    def mix(item, g):
        m = m_bufs[g][...]
        ps = []
        for c in range(n_keys // hd):
            s = s_bufs[g][:, c * hd:(c + 1) * hd]
            ps.append(jnp.exp2(s * exp2_scale - m).astype(BF16))
        o_bufs[g][...] = _dot(jnp.concatenate(ps, axis=1), vall[g])

    def finish(item, g):
        rows = tile_rows(item)
        o = o_bufs[g][:, 0:hd] / o_bufs[g][:, hd:hd + 1]
        for h in range(ATT_REP):
            gate = g_refs[g][rows, h * hd:(h + 1) * hd].astype(F32)
            y = o[h * tq:(h + 1) * tq, :] * (gate * jax.nn.sigmoid(gate))
            y_ref[rows, g * qw + h * hd:g * qw + (h + 1) * hd] = y.astype(BF16)

    stages = (prep, scores, mix, finish)

    def step(k, parity):
        for a, stage in enumerate(stages):
            item = k - a
            if isinstance(item, int) and not 0 <= item < n_items:
                continue
            stage(item, (parity + a) % 2)

    depth = len(stages) - 1
    for k in range(min(depth, n_items + depth)):
        step(k, k % 2)
    n_steady = max(n_items - depth, 0)

    group = ATT_STEADY_STEPS

    def steady_group(j, carry):
        for u in range(group):
            step(depth + group * j + u, (depth + u) % 2)
        return carry

    lax.fori_loop(0, n_steady // group, steady_group, 0)
    for k in range(depth + group * (n_steady // group), n_items + depth):
        step(k, k % 2)


def _attention(z, batch, seq_len, q_norm, k_norm, j, cache=None, new_kv=None):
    hd = ATT_HEAD_DIM
    latent = cache is not None
    qw = ATT_REP * hd
    kv0 = ATT_WIDTH // KV_WIDTH
    gate0 = (ATT_WIDTH + 2 * KV_WIDTH) // qw
    assert (ATT_WIDTH + 2 * KV_WIDTH) % qw == 0 and ATT_WIDTH % KV_WIDTH == 0
    in_specs = [
        pl.BlockSpec((seq_len, ATT_WIDTH), lambda b: (b, 0)),
        pl.BlockSpec((seq_len, KV_WIDTH), lambda b: (b, kv0)),
        pl.BlockSpec((seq_len, KV_WIDTH), lambda b: (b, kv0 + 1)),
        pl.BlockSpec((1, hd), lambda b: (0, 0)),
        pl.BlockSpec((1, hd), lambda b: (0, 0)),
    ] + [pl.BlockSpec((seq_len, qw), lambda b, g=g: (b, gate0 + g)) for g in range(ATT_KV_HEADS)]
    args = [z, z, z, q_norm, k_norm] + [z] * ATT_KV_HEADS
    out_specs = [pl.BlockSpec((seq_len, ATT_WIDTH), lambda b: (b, 0))]
    out_shape = [jax.ShapeDtypeStruct((batch * seq_len, ATT_WIDTH), BF16)]
    n_keys = seq_len
    aliases = {}
    layer = j
    if latent:
        cache_k, cache_v, cos, sin_signed = cache
        past = cache_k.shape[2]
        n_keys += past
        cspec = pl.BlockSpec((None, None, past * ATT_KV_HEADS, hd), lambda b: (b, j, 0, 0))
        tspec = pl.BlockSpec((seq_len, hd), lambda b: (0, 0))
        in_specs += [cspec, cspec, tspec, tspec]
        args += [cache_k.reshape(batch, N_ATT, past * ATT_KV_HEADS, hd),
                 cache_v.reshape(batch, N_ATT, past * ATT_KV_HEADS, hd), cos, sin_signed]
    else:
        out_shape += [jax.ShapeDtypeStruct((batch, N_ATT, seq_len * ATT_KV_HEADS, hd), F32)] * 2
        if new_kv is None:
            kv_spec = pl.BlockSpec((None, N_ATT, seq_len * ATT_KV_HEADS, hd), lambda b: (b, 0, 0, 0))
        else:
            kv_spec = pl.BlockSpec((None, 1, seq_len * ATT_KV_HEADS, hd), lambda b: (b, j, 0, 0))
            layer = 0
            aliases = {len(args): 1, len(args) + 1: 2}
            in_specs += [pl.BlockSpec(memory_space=pl.ANY)] * 2
            args += list(new_kv)
        out_specs += [kv_spec, kv_spec]
    rows = ATT_REP * ATT_Q_TILE
    return pl.pallas_call(
        functools.partial(_att_kernel, seq_len=seq_len, n_keys=n_keys, latent=latent, n_aliased=len(aliases),
                          layer=layer),
        grid=(batch,),
        in_specs=in_specs,
        out_specs=out_specs,
        out_shape=out_shape,
        input_output_aliases=aliases,
        scratch_shapes=[pltpu.VMEM((ATT_KV_HEADS, n_keys, hd), BF16),
                        pltpu.VMEM((ATT_KV_HEADS, n_keys, 2 * hd), BF16),
                        pltpu.VMEM((rows, hd), BF16), pltpu.VMEM((rows, hd), BF16),
                        pltpu.VMEM((rows, n_keys), F32), pltpu.VMEM((rows, n_keys), F32),
                        pltpu.VMEM((rows, hd), F32), pltpu.VMEM((rows, hd), F32),
                        pltpu.VMEM((rows, 2 * hd), F32), pltpu.VMEM((rows, 2 * hd), F32)],
        compiler_params=_cparams(1),
        name="attention",
    )(*args)


def _rope_tables(n_tokens):
    t = jnp.arange(n_tokens)
    row = (t // GRID_W).astype(F32)
    col = (t % GRID_W).astype(F32)
    inv = ROPE_THETA ** (-jnp.arange(0, AXIS_DIM, 2, dtype=F32) / AXIS_DIM)
    ang = jnp.concatenate([row[:, None] * inv[None, :], col[:, None] * inv[None, :]], axis=-1)
    cos = jnp.repeat(jnp.cos(ang), 2, axis=-1)
    sin = jnp.repeat(jnp.sin(ang), 2, axis=-1)
    sign = jnp.where(jnp.arange(ATT_HEAD_DIM) % 2 == 0, -1.0, 1.0).astype(F32)
    return cos, sin * sign


def kernel(x_prompt, x_sample, c, state_hgrn, cache_k, cache_v, c_ctx, ada_w, ada_b, norm_pre, norm_post, rec_w_in, rec_lb_logits, rec_head_norm, pool_w, pool_scale, rec_w_out, att_w_in, att_q_norm, att_k_norm, att_w_out):
    nb_c, len_c, _ = x_prompt.shape
    nb_l, len_l, _ = x_sample.shape

    lb_p = jax.nn.softmax(rec_lb_logits.astype(F32), axis=0)
    lower_bounds = jnp.clip(jnp.cumsum(lb_p, axis=0) - lb_p[0], 0.0, 1.0)
    lower_bounds = lower_bounds.reshape(N_REC, 2, REC_HEADS, 1, REC_HEAD_DIM)
    cos, sin_signed = _rope_tables(len_l)

    cvec = jnp.zeros((MOD_ROWS, D_MODEL), F32).at[0].set(c_ctx).at[1:1 + nb_l].set(c)

    forget_cols = (2 * POOL_WIDTH + REC_WIDTH, 2 * POOL_WIDTH + 3 * REC_WIDTH)
    layers = []
    for i in range(DEPTH):
        rec = i % 2 == 0
        layers.append(dict(
            f32_cols=forget_cols if rec else (0, 0),
            gain_pre=norm_pre[i].reshape(1, D_MODEL), gain_post=norm_post[i].reshape(1, D_MODEL)))

    weights = {("in", 0): rec_w_in[0].astype(BF16)}

    def weight_source(kind, i):
        stacks = (rec_w_in, rec_w_out) if i % 2 == 0 else (att_w_in, att_w_out)
        return stacks[kind == "out"], i // 2

    def cast_part(keys, steps):
        keys = [k for k in keys if k not in weights and k[1] < DEPTH]
        return keys, _cast_part([weight_source(*k) for k in keys], steps)

    def pre_args(i):
        return (weights["in", i], i, layers[i]["gain_pre"], layers[i]["f32_cols"])

    xs = [x_prompt.reshape(nb_c * len_c, D_MODEL), x_sample.reshape(nb_l * len_l, D_MODEL)]
    streams = ((nb_c, len_c, 0), (nb_l, len_l, 1))
    zs = [None, None]
    ys = [None, None]
    new_states, new_kv = None, None

    def proj_part(s, i, steps=None, mod=None):
        nb, sl, row0 = streams[s]
        post = ys[s] + (weights["out", i - 1], i - 1, layers[i - 1]["gain_post"]) if i > 0 else None
        return _proj_part(xs[s], mods if mod is None else mod, sl, row0, post=post,
                          pre=pre_args(i) if i < DEPTH else None, steps=steps)

    def take_proj(s, i, outs):
        if i > 0:
            xs[s] = outs.pop(0)
        zs[s] = outs

    def rec_part(s, i, heads=None):
        nb, sl, _ = streams[s]
        j = i // 2
        hn = rec_head_norm[j].reshape(1, REC_HEAD_DIM)
        z, zf = zs[s]
        if s == 0:
            return _rec_part(z, zf, nb, sl, lower_bounds[j], hn, j, states=new_states, heads=heads)
        return _rec_part(z, zf, nb, sl, lower_bounds[j], hn, j, s0=state_hgrn, heads=heads)

    def pool_part(s, i):
        nb, sl, _ = streams[s]
        j = i // 2
        return _pool_part(zs[s][0], nb, sl, pool_w[j].astype(BF16), pool_scale[j].reshape(1, POOL_WIDTH))

    mod0 = _mod_rows(_run(_mod_part(cvec, ada_w, ada_b, 0, 1), name="modulation")[0][0])
    first = proj_part(0, 0, mod=mod0, steps=DEPTH * 3 * D_MODEL // MOD_COLS)
    keys, casts = cast_part([("out", 0), ("in", 1)], first.steps)
    outs, (mods,), cast = _run(first, _mod_part(cvec, ada_w, ada_b, 0, DEPTH), casts, name="proj_modulation")
    weights.update(zip(keys, cast))
    mods = _mod_rows(mods)
    take_proj(0, 0, outs)
    for i in range(DEPTH):
        j = i // 2
        if i % 2 == 0:
            rec_c = rec_part(0, i)
            keys, casts = cast_part([("out", i), ("in", i + 1)], rec_c.steps)
            (y_rec, new_states), outs, cast, (y_pool,) = _run(
                rec_c, proj_part(1, i, steps=rec_c.steps), casts, pool_part(0, i), name="rec_pool_proj")
            weights.update(zip(keys, cast))
            ys[0] = (y_pool, 0, y_rec, 0)
            take_proj(1, i, outs)
            rec_l = rec_part(1, i, heads=REC_HEADS * streams[1][0] // rec_c.steps)
            keys, casts = cast_part([("out", i + 1), ("in", i + 2)], rec_l.steps)
            (y_rec,), outs, cast = _run(rec_l, proj_part(0, i + 1, steps=rec_l.steps), casts, name="rec_proj")
            weights.update(zip(keys, cast))
            ys[1] = (_run(pool_part(1, i), name="pool_mixer")[0][0], 0, y_rec, 0)
            take_proj(0, i + 1, outs)
        else:
            qn = att_q_norm[j].reshape(1, ATT_HEAD_DIM)
            kn = att_k_norm[j].reshape(1, ATT_HEAD_DIM)
            y, *new_kv = _attention(zs[0][0], nb_c, len_c, qn, kn, j, new_kv=new_kv)
            ys[0] = (y, 0, y, 1)
            take_proj(1, i, _run(proj_part(1, i), name="proj")[0])
            (y,) = _attention(zs[1][0], nb_l, len_l, qn, kn, j, cache=(cache_k, cache_v, cos, sin_signed))
            ys[1] = (y, 0, y, 1)
            take_proj(0, i + 1, _run(proj_part(0, i + 1), name="proj")[0])
    take_proj(1, DEPTH, _run(proj_part(1, DEPTH), name="proj")[0])
    xc, xl = xs

    kv_shape = (nb_c, N_ATT, len_c, ATT_KV_HEADS, ATT_HEAD_DIM)
    return (xc.reshape(nb_c, len_c, D_MODEL), xl.reshape(nb_l, len_l, D_MODEL),
            new_states, new_kv[0].reshape(kv_shape), new_kv[1].reshape(kv_shape))
```

```python
import functools
from typing import Any, NamedTuple

import jax
import jax.numpy as jnp
from jax import lax
from jax.experimental import pallas as pl
from jax.experimental.pallas import tpu as pltpu

D_MODEL = 1024
DEPTH = 4
GRID_W = 64
N_REC = (DEPTH + 1) // 2
N_ATT = DEPTH // 2
POOL_WIDTH = D_MODEL // 2
POOL_WINDOWS = (2, 4, 8, 16)
POOL_GROUP_DIM = POOL_WIDTH // len(POOL_WINDOWS)
REC_WIDTH = D_MODEL // 2
REC_HEAD_DIM = 128
REC_HEADS = REC_WIDTH // REC_HEAD_DIM
REC_IN_WIDTH = 2 * POOL_WIDTH + 5 * REC_WIDTH
ATT_HEAD_DIM = 128
ATT_HEADS = D_MODEL // ATT_HEAD_DIM
ATT_KV_HEADS = 2
ATT_REP = ATT_HEADS // ATT_KV_HEADS
ATT_WIDTH = ATT_HEADS * ATT_HEAD_DIM
KV_WIDTH = ATT_KV_HEADS * ATT_HEAD_DIM
ATT_IN_WIDTH = 2 * ATT_WIDTH + 2 * KV_WIDTH
AXIS_DIM = ATT_HEAD_DIM // 2
ROPE_THETA = 10000.0
EPS = 1e-6
F_MIN = 1e-6

MOD_ROWS = 16
TOKEN_TILE = 1024
MIN_PROJ_STEPS = 4
PROJ_GROUP_ROWS = 512
POOL_ROWS = 256
IN_PROJ_STEP = 512
MOD_COLS = 1536
SCAN_CHUNK = 128
SCAN_UNITS = 8
POOL_HALO = 128
ATT_Q_TILE = 256
ATT_KEY_TILE = 128
ATT_STEADY_STEPS = 2
LOG2_E = 1.4426950408889634
VMEM_LIMIT = 56 * 1024 * 1024

F32 = jnp.float32
BF16 = jnp.bfloat16


def _cparams(n_axes):
    return pltpu.CompilerParams(
        dimension_semantics=("arbitrary",) * n_axes, vmem_limit_bytes=VMEM_LIMIT)


def _rms(x, g):
    return x * lax.rsqrt(jnp.mean(x * x, axis=-1, keepdims=True) + EPS) * g


def _dot(a, b):
    return jnp.dot(a, b, preferred_element_type=F32)


def _dot_nt(a, b):
    return lax.dot_general(a, b, (((1,), (1,)), ((), ())), preferred_element_type=F32)


class _Part(NamedTuple):
    kernel: Any
    steps: int
    in_specs: list
    args: list
    out_specs: list
    out_shape: list
    scratch_shapes: list
    aliases: dict


def _run(*parts, name):
    steps = parts[0].steps
    assert all(p.steps == steps for p in parts)
    n_in = [len(p.args) for p in parts]
    n_out = [len(p.out_shape) for p in parts]
    n_scr = [len(p.scratch_shapes) for p in parts]

    def body(*refs):
        ins, outs, scr = refs[:sum(n_in)], refs[sum(n_in):sum(n_in) + sum(n_out)], refs[sum(n_in) + sum(n_out):]
        pending = {}
        for k, p in enumerate(parts):
            i0, o0, s0 = sum(n_in[:k]), sum(n_out[:k]), sum(n_scr[:k])
            pending[k] = (0.0, p.kernel(*ins[i0:i0 + n_in[k]], *outs[o0:o0 + n_out[k]], *scr[s0:s0 + n_scr[k]]))
        while pending:
            k = min(pending, key=lambda k: pending[k][0])
            try:
                pending[k] = (next(pending[k][1]), pending[k][1])
            except StopIteration:
                del pending[k]

    aliases = {}
    for k, p in enumerate(parts):
        for i, o in p.aliases.items():
            aliases[sum(n_in[:k]) + i] = sum(n_out[:k]) + o
    outs = pl.pallas_call(
        body,
        grid=(steps,),
        in_specs=[s for p in parts for s in p.in_specs],
        out_specs=[s for p in parts for s in p.out_specs],
        out_shape=[s for p in parts for s in p.out_shape],
        scratch_shapes=[s for p in parts for s in p.scratch_shapes],
        input_output_aliases=aliases,
        compiler_params=_cparams(1),
        name=name,
    )(*[a for p in parts for a in p.args])
    return [list(outs[sum(n_out[:k]):sum(n_out[:k]) + n_out[k]]) for k in range(len(parts))]


def _mod_kernel(cv_ref, w_ref, b_ref, o_ref):
    cv = cv_ref[...]
    a = (cv * jax.nn.sigmoid(cv)).astype(BF16)
    o_ref[...] = _dot(a, w_ref[...].astype(BF16)) + b_ref[...]
    yield 1.0


def _mod_part(cvec, ada_w, ada_b, layer0, n_layers):
    tn = MOD_COLS
    blocks = 3 * D_MODEL // tn
    in_specs = [
        pl.BlockSpec((MOD_ROWS, D_MODEL), lambda i: (0, 0)),
        pl.BlockSpec((None, D_MODEL, tn), lambda i: (layer0 + i // blocks, 0, i % blocks)),
        pl.BlockSpec((None, 1, tn), lambda i: (layer0 + i // blocks, 0, i % blocks)),
    ]
    out_specs = [pl.BlockSpec((None, MOD_ROWS, tn), lambda i: (i // blocks, 0, i % blocks))]
    out_shape = [jax.ShapeDtypeStruct((n_layers, MOD_ROWS, 3 * D_MODEL), F32)]
    args = [cvec, ada_w, ada_b.reshape(DEPTH, 1, 3 * D_MODEL)]
    return _Part(_mod_kernel, n_layers * blocks, in_specs, args, out_specs, out_shape, [], {})


def _cast_kernel(*refs):
    n = len(refs) // 2
    for w_ref, o_ref in zip(refs[:n], refs[n:]):
        o_ref[...] = w_ref[...].astype(BF16)
    yield 1.0


def _cast_part(sources, steps):
    in_specs, out_specs, out_shape, args = [], [], [], []
    for stack, j in sources:
        _, k, n = stack.shape
        in_specs.append(pl.BlockSpec((None, k // steps, n), lambda i, j=j: (j, i, 0)))
        out_specs.append(pl.BlockSpec((k // steps, n), lambda i: (i, 0)))
        out_shape.append(jax.ShapeDtypeStruct((k, n), BF16))
        args.append(stack)
    return _Part(_cast_kernel, steps, in_specs, args, out_specs, out_shape, [], {})


def _mod_rows(out):
    return out.reshape(out.shape[0], MOD_ROWS, 3, 1, D_MODEL)


def _mod_spec(layer, seq_len, row0, tile):
    assert row0 == 0 or seq_len % tile == 0
    tiles_per_seq = seq_len // tile
    if row0 == 0:
        index = lambda i: (layer, 0, 0, 0, 0)
    else:
        index = lambda i: (layer, row0 + i // tiles_per_seq, 0, 0, 0)
    return pl.BlockSpec((None, None, 3, 1, D_MODEL), index)


def _proj_kernel(*refs, post, pre, n_out, f32_cols):
    refs = list(refs)
    x_ref = refs.pop(0)
    if post:
        ya_ref, yb_ref, wo_ref, modp_ref, gpost_ref = (refs.pop(0) for _ in range(5))
    if pre:
        modn_ref, gpre_ref, wi_ref = (refs.pop(0) for _ in range(3))
    xo_ref = refs.pop(0) if post else None
    rows_all = x_ref.shape[0]
    n_groups = max(1, rows_all // PROJ_GROUP_ROWS)

    def group(rows):
        x = x_ref[rows, :]
        if post:
            half = D_MODEL // 2
            p = _dot(ya_ref[rows, :], wo_ref[0:half, :]) + _dot(yb_ref[rows, :], wo_ref[half:D_MODEL, :])
            x = x + modp_ref[2] * _rms(p, gpost_ref[...])
            xo_ref[rows, :] = x
            yield
        if pre:
            h = (_rms(x, gpre_ref[...]) * (1.0 + modn_ref[1]) + modn_ref[0]).astype(BF16)
            lo, hi = f32_cols
            for c0 in range(0, n_out, IN_PROJ_STEP):
                y = _dot(h, wi_ref[:, c0:c0 + IN_PROJ_STEP])
                if lo <= c0 < hi:
                    refs[1][rows, c0 - lo:c0 - lo + IN_PROJ_STEP] = y
                else:
                    c1 = c0 if c0 < lo else c0 - (hi - lo)
                    refs[0][rows, c1:c1 + IN_PROJ_STEP] = y.astype(BF16)
                yield

    size = rows_all // n_groups
    waiting = [group(slice(k * size, (k + 1) * size)) for k in range(n_groups)]
    running = []
    pieces = n_groups * (post + n_out // IN_PROJ_STEP)
    done = 0
    while waiting or running:
        if waiting:
            running.append(waiting.pop(0))
        for g in list(running):
            try:
                next(g)
                done += 1
            except StopIteration:
                running.remove(g)
        yield min(done / pieces, 1.0)


def _proj_part(x, mod, seq_len, row0, post=None, pre=None, steps=None):
    t = x.shape[0]
    tile = t // steps if steps else min(TOKEN_TILE, t // MIN_PROJ_STEPS)
    half = D_MODEL // 2
    row = lambda i: (i, 0)
    fixed = lambda i: (0, 0)
    resident = dict(index_map=fixed, pipeline_mode=pl.Buffered(1))
    in_specs = [pl.BlockSpec((tile, D_MODEL), row)]
    args = [x]
    out_specs, out_shape = [], []
    n_out, f32_cols = 0, (0, 0)
    if post:
        ya, ia, yb, ib, w_out, layer, gain = post
        in_specs += [pl.BlockSpec((tile, half), lambda i: (i, ia)),
                     pl.BlockSpec((tile, half), lambda i: (i, ib)),
                     pl.BlockSpec((D_MODEL, D_MODEL), **resident),
                     _mod_spec(layer, seq_len, row0, tile),
                     pl.BlockSpec((1, D_MODEL), fixed)]
        args += [ya, yb, w_out, mod, gain]
        out_specs.append(pl.BlockSpec((tile, D_MODEL), row))
        out_shape.append(jax.ShapeDtypeStruct((t, D_MODEL), F32))
    if pre:
        w_in, layer, gain, f32_cols = pre
        n_out = w_in.shape[1]
        lo, hi = f32_cols
        assert lo % IN_PROJ_STEP == 0 and hi % IN_PROJ_STEP == 0 and n_out % IN_PROJ_STEP == 0
        in_specs += [_mod_spec(layer, seq_len, row0, tile),
                     pl.BlockSpec((1, D_MODEL), fixed),
                     pl.BlockSpec((D_MODEL, n_out), **resident)]
        args += [mod, gain, w_in]
        out_specs.append(pl.BlockSpec((tile, n_out - (hi - lo)), row))
        out_shape.append(jax.ShapeDtypeStruct((t, n_out - (hi - lo)), BF16))
        if hi > lo:
            out_specs.append(pl.BlockSpec((tile, hi - lo), row))
            out_shape.append(jax.ShapeDtypeStruct((t, hi - lo), F32))
    kernel = functools.partial(_proj_kernel, post=bool(post), pre=bool(pre), n_out=n_out, f32_cols=f32_cols)
    return _Part(kernel, t // tile, in_specs, args, out_specs, out_shape, [], {})


def _pool_kernel(u_ref, gp_ref, pw_ref, ps_ref, o_ref, pad_ref, band_ref, sum_ref, dif_ref, *, seq_len):
    gd = POOL_GROUP_DIM
    rows = POOL_ROWS
    halo = POOL_HALO
    span = rows + 2 * halo

    @pl.when(pl.program_id(0) == 0)
    def _():
        r = lax.broadcasted_iota(jnp.int32, (rows, span), 0)
        c = lax.broadcasted_iota(jnp.int32, (rows, span), 1)
        offset = c - halo - r
        for g, win in enumerate(POOL_WINDOWS):
            band_ref[g] = jnp.where((offset >= -(win // 2)) & (offset < win // 2), 1.0, 0.0).astype(BF16)

    zeros = jnp.zeros((halo, POOL_WIDTH), BF16)
    pad_ref[0:halo, :] = zeros
    pad_ref[halo + seq_len:2 * halo + seq_len, :] = zeros
    pad_ref[halo:halo + seq_len, :] = u_ref[...]
    units = [(g, r * rows) for g in range(len(POOL_WINDOWS)) for r in range(seq_len // rows)]

    def cols(g):
        return slice(g * gd, (g + 1) * gd)

    for g, base in units:
        sum_ref[base:base + rows, cols(g)] = _dot(band_ref[g], pad_ref[base:base + span, cols(g)])
    yield 0.3
    for g, base in units:
        win = POOL_WINDOWS[g]
        t = base + lax.broadcasted_iota(jnp.int32, (rows, 1), 0)
        count = jnp.clip(t + win // 2, 0, seq_len) - jnp.clip(t - win // 2, 0, seq_len)
        mean = sum_ref[base:base + rows, cols(g)] / count.astype(F32)
        d = mean - pad_ref[halo + base:halo + base + rows, cols(g)].astype(F32)
        dif_ref[base:base + rows, cols(g)] = d.astype(BF16)
    yield 0.6
    for g, base in units:
        y = _dot(dif_ref[base:base + rows, cols(g)], pw_ref[g]) * ps_ref[:, cols(g)]
        gate = gp_ref[base:base + rows, cols(g)].astype(F32)
        o_ref[base:base + rows, cols(g)] = (y * (gate * jax.nn.sigmoid(gate))).astype(BF16)
    yield 1.0


def _pool_part(z, batch, seq_len, pool_w, pool_scale):
    in_specs = [
        pl.BlockSpec((seq_len, POOL_WIDTH), lambda b: (b, 0)),
        pl.BlockSpec((seq_len, POOL_WIDTH), lambda b: (b, 1)),
        pl.BlockSpec((len(POOL_WINDOWS), POOL_GROUP_DIM, POOL_GROUP_DIM), lambda b: (0, 0, 0)),
        pl.BlockSpec((1, POOL_WIDTH), lambda b: (0, 0)),
    ]
    out_specs = [pl.BlockSpec((seq_len, POOL_WIDTH), lambda b: (b, 0))]
    out_shape = [jax.ShapeDtypeStruct((batch * seq_len, POOL_WIDTH), BF16)]
    scratch_shapes = [pltpu.VMEM((seq_len + 2 * POOL_HALO, POOL_WIDTH), BF16),
                      pltpu.VMEM((len(POOL_WINDOWS), POOL_ROWS, POOL_ROWS + 2 * POOL_HALO), BF16),
                      pltpu.VMEM((seq_len, POOL_WIDTH), F32), pltpu.VMEM((seq_len, POOL_WIDTH), BF16)]
    kernel = functools.partial(_pool_kernel, seq_len=seq_len)
    return _Part(kernel, batch, in_specs, [z, z, pool_w, pool_scale], out_specs, out_shape, scratch_shapes, {})


def _block_row(p, block, row):
    c, w = p.shape
    p3 = p.reshape(c // block, block, w)
    return jnp.broadcast_to(p3[:, row:row + 1, :], p3.shape).reshape(c, w)


def _interleave(lo, hi, block):
    half = block // 2
    parts = []
    for b in range(SCAN_CHUNK // block):
        parts.append(lo[b * block:b * block + half])
        parts.append(hi[b * block + half:(b + 1) * block])
    return jnp.concatenate(parts, axis=0)


def _select_levels(products, rev):
    c = SCAN_CHUNK
    lane = lax.broadcasted_iota(jnp.int32, (8, c), 1)
    sub = lax.broadcasted_iota(jnp.int32, (8, c), 0)
    out = []
    for t0 in range(0, c, 8):
        if rev:
            keep = (lane >= t0 + sub) & (lane < t0 + 8)
        else:
            keep = (lane >= t0) & (lane <= t0 + sub)
        row = jnp.where(keep, products[0][t0:t0 + 8, :], 0.0)
        for lv, block in enumerate((16, 32, 64, 128), 1):
            half = block // 2
            start = t0 // block * block
            upper = t0 - start >= half
            if upper == rev:
                continue
            lo = start + half if rev else start
            row = jnp.where((lane >= lo) & (lane < lo + half), products[lv][t0:t0 + 8, :], row)
        out.append(row)
    return jnp.concatenate(out, axis=0)


def _rec_kernel(*refs, seq_len, heads, has_s0, out_state, n_aliased, layer):
    q_ref, ff_ref, fb_ref, v_ref, gr_ref, lb_ref, hn_ref = refs[:7]
    rest = list(refs[7:])
    s0_ref = rest.pop(0) if has_s0 else None
    del rest[:n_aliased]
    o_ref = rest.pop(0)
    st_ref = rest.pop(0) if out_state else None
    acc_ref, qs_ref, k_ref, p_ref, a_ref, qin_ref, u_ref, dec_ref, tri_ref = rest

    c = SCAN_CHUNK
    hd = REC_HEAD_DIM
    n_chunks = seq_len // c
    zf_refs = (ff_ref, fb_ref)
    units = [(h, n) for h in range(heads) for n in range(n_chunks)]

    def rows_in(n):
        return slice(n * c, (n + 1) * c)

    def rows_sc(h, n):
        return slice((h * n_chunks + n) * c, (h * n_chunks + n + 1) * c)

    def cols(h):
        return slice(h * hd, (h + 1) * hd)

    ti = lax.broadcasted_iota(jnp.int32, (c, c), 0)
    si = lax.broadcasted_iota(jnp.int32, (c, c), 1)
    for d, causal in enumerate((si <= ti, si >= ti)):
        tri_ref[d] = jnp.where(causal, 1.0, 0.0).astype(BF16)

    weights = (0.26, 0.43, 0.09, 0.22)
    done = [0.0]

    def progress(stage, share):
        done[0] += weights[stage] * share
        return done[0]

    for h, n in units:
        qz = q_ref[rows_in(n), cols(h)].astype(F32)
        qs_ref[rows_sc(h, n), :] = qz * jax.nn.sigmoid(qz)
        for d in range(2):
            lower = lb_ref[d, h]
            f = jnp.clip(lower + (1.0 - lower) * jax.nn.sigmoid(zf_refs[d][rows_in(n), cols(h)]), F_MIN, 1.0)
            k_ref[d, rows_sc(h, n), :] = 1.0 - f
            g = jnp.log(f) * LOG2_E
            g_hi = g.astype(BF16)
            rest = g - g_hi.astype(F32)
            g_mid = rest.astype(BF16)
            g_lo = (rest - g_mid.astype(F32)).astype(BF16)
            sums = _dot(tri_ref[d], jnp.concatenate([g_hi, g_mid, g_lo], axis=1))
            p_ref[d, rows_sc(h, n), :] = (sums[:, 0:hd] + sums[:, hd:2 * hd]) + sums[:, 2 * hd:3 * hd]
        yield progress(0, 1 / len(units))

    for h, n in units:
        rows = rows_sc(h, n)
        q = qs_ref[rows, :]
        v_t = v_ref[rows_in(n), cols(h)].astype(F32).T.astype(BF16)
        for d in range(2):
            rev = d == 1
            k = k_ref[d, rows, :]
            p = p_ref[d, rows, :]
            e0 = p - _block_row(p, 8, 4 if rev else 3)
            products = [_dot_nt((q * jnp.exp2(e0)).astype(BF16), (k * jnp.exp2(-e0)).astype(BF16))]
            for block in (16, 32, 64, 128):
                beta = _block_row(p, block, block // 2 if rev else block // 2 - 1)
                if rev:
                    e = _interleave(p, beta, block) - _interleave(beta, p, block)
                    src = _interleave(q, k, block)
                else:
                    e = _interleave(beta, p, block) - _interleave(p, beta, block)
                    src = _interleave(k, q, block)
                m = (src * jnp.exp2(e)).astype(BF16)
                products.append(_dot_nt(m, m))
            a_ref[rows, d * c:(d + 1) * c] = _select_levels(products, rev).astype(BF16)

            edge = p[0:1, :] if rev else p[c - 1:c, :]
            qin_ref[d, rows, :] = (q * jnp.exp2(p)).astype(BF16)
            u_ref[d, h * n_chunks + n] = _dot(v_t, (k * jnp.exp2(edge - p)).astype(BF16))
            dec_ref[d, h * n_chunks + n] = jnp.broadcast_to(jnp.exp2(edge), (8, hd))
        yield progress(1, 1 / len(units))

    for h, n in units:
        v_b = v_ref[rows_in(n), cols(h)]
        acc_ref[rows_sc(h, n), :] = _dot(a_ref[rows_sc(h, n), :], jnp.concatenate([v_b, v_b], axis=0))
        yield progress(2, 1 / len(units))

    for h in range(heads):
        if has_s0:
            states = [s0_ref[0, h].T, s0_ref[1, h].T]
        else:
            states = [jnp.zeros((hd, hd), F32)] * 2
        inter = [[None] * n_chunks, [None] * n_chunks]
        for i in range(n_chunks):
            for d, n in ((0, i), (1, n_chunks - 1 - i)):
                inter[d][n] = _dot_nt(qin_ref[d, rows_sc(h, n), :], states[d].astype(BF16))
                states[d] = dec_ref[d, h * n_chunks + n][0:1, :] * states[d] + u_ref[d, h * n_chunks + n]
        for n in range(n_chunks):
            gate = gr_ref[rows_in(n), cols(h)].astype(F32)
            o = acc_ref[rows_sc(h, n), :] + inter[0][n] + inter[1][n]
            o_ref[rows_in(n), cols(h)] = (_rms(o, hn_ref[...]) * (gate * jax.nn.sigmoid(gate))).astype(BF16)
        if out_state:
            st_ref[layer, 0, h] = states[0].T
            st_ref[layer, 1, h] = states[1].T
        yield progress(3, 1 / heads)
    if out_state:
        for other in range(st_ref.shape[0]):
            if other != layer:
                st_ref[other] = jnp.zeros(st_ref.shape[1:], F32)


def _rec_part(z, zf, batch, seq_len, lower, head_norm, j, s0=None, states=None, heads=None):
    hd = REC_HEAD_DIM
    nh = REC_HEADS
    n_chunks = seq_len // SCAN_CHUNK
    heads = heads or min(nh, max(1, SCAN_UNITS // n_chunks))
    width = heads * hd
    col0 = 2 * POOL_WIDTH // width
    groups = nh // heads

    def zspec(part):
        return pl.BlockSpec((seq_len, width), lambda i: (i // groups, col0 + part * groups + i % groups))

    def fspec(part):
        return pl.BlockSpec((seq_len, width), lambda i: (i // groups, part * groups + i % groups))

    in_specs = [zspec(0), fspec(0), fspec(1), zspec(1), zspec(2),
                pl.BlockSpec((2, heads, 1, hd), lambda i: (0, i % groups, 0, 0)),
                pl.BlockSpec((1, hd), lambda i: (0, 0))]
    args = [z, zf, zf, z, z, lower, head_norm]
    has_s0 = s0 is not None
    if has_s0:
        in_specs.append(pl.BlockSpec((None, None, 2, heads, hd, hd),
                                     lambda i: (i // groups, j, 0, i % groups, 0, 0)))
        args.append(s0)
    out_state = not has_s0
    out_specs = [pl.BlockSpec((seq_len, width), lambda i: (i // groups, i % groups))]
    out_shape = [jax.ShapeDtypeStruct((batch * seq_len, REC_WIDTH), BF16)]
    aliases = {}
    layer = j
    if out_state:
        out_shape.append(jax.ShapeDtypeStruct((batch, N_REC, 2, nh, hd, hd), F32))
        if states is None:
            out_specs.append(pl.BlockSpec((None, N_REC, 2, heads, hd, hd),
                                          lambda i: (i // groups, 0, 0, i % groups, 0, 0)))
        else:
            out_specs.append(pl.BlockSpec((None, 1, 2, heads, hd, hd),
                                          lambda i: (i // groups, j, 0, i % groups, 0, 0)))
            layer = 0
            aliases[len(args)] = 1
            in_specs.append(pl.BlockSpec(memory_space=pl.ANY))
            args.append(states)
    tokens = heads * seq_len
    kernel = functools.partial(_rec_kernel, seq_len=seq_len, heads=heads, has_s0=has_s0, out_state=out_state,
                               n_aliased=len(aliases), layer=layer)
    scratch_shapes = [pltpu.VMEM((tokens, hd), F32),
                      pltpu.VMEM((tokens, hd), F32),
                      pltpu.VMEM((2, tokens, hd), F32),
                      pltpu.VMEM((2, tokens, hd), F32),
                      pltpu.VMEM((tokens, 2 * SCAN_CHUNK), BF16),
                      pltpu.VMEM((2, tokens, hd), BF16),
                      pltpu.VMEM((2, tokens // SCAN_CHUNK, hd, hd), F32),
                      pltpu.VMEM((2, tokens // SCAN_CHUNK, 8, hd), F32),
                      pltpu.VMEM((2, SCAN_CHUNK, SCAN_CHUNK), BF16)]
    return _Part(kernel, batch * groups, in_specs, args, out_specs, out_shape, scratch_shapes, aliases)


def _rope(x, cos, sin_signed):
    lane = lax.broadcasted_iota(jnp.int32, x.shape, 1)
    partner = jnp.where((lane & 1) == 0, pltpu.roll(x, ATT_HEAD_DIM - 1, 1), pltpu.roll(x, 1, 1))
    return x * cos + partner * sin_signed


def _att_kernel(*refs, seq_len, n_keys, latent, n_aliased, layer):
    q_ref, k_ref, v_ref, qn_ref, kn_ref = refs[:5]
    g_refs = refs[5:5 + ATT_KV_HEADS]
    if latent:
        ck_ref, cv_ref, cos_ref, sin_ref, y_ref, kall, vall = refs[5 + ATT_KV_HEADS:-8]
    else:
        y_ref, ko_ref, vo_ref, kall, vall = refs[5 + ATT_KV_HEADS + n_aliased:-8]
    qs_bufs, s_bufs, m_bufs, o_bufs = refs[-8:-6], refs[-6:-4], refs[-4:-2], refs[-2:]
    hd = ATT_HEAD_DIM
    qw = ATT_REP * hd
    tq = ATT_Q_TILE
    tk = min(ATT_KEY_TILE, n_keys)
    n_kb = n_keys // tk
    n_past = n_keys - seq_len
    n_items = ATT_KV_HEADS * (seq_len // tq)
    exp2_scale = hd ** -0.5 * LOG2_E
    assert ATT_KV_HEADS == 2

    if not latent:
        for other in range(ko_ref.shape[0]):
            if other != layer:
                ko_ref[other] = jnp.zeros(ko_ref.shape[1:], F32)
                vo_ref[other] = jnp.zeros(vo_ref.shape[1:], F32)
    lane = lax.broadcasted_iota(jnp.int32, (n_keys, hd), 1)
    for g in range(ATT_KV_HEADS):
        k = _rms(k_ref[:, g * hd:(g + 1) * hd].astype(F32), kn_ref[...])
        v = v_ref[:, g * hd:(g + 1) * hd].astype(F32)
        if latent:
            k = _rope(k, cos_ref[...], sin_ref[...])
            kall[g, 0:n_past, :] = ck_ref[pl.ds(g, n_past, stride=ATT_KV_HEADS), :].astype(BF16)
            vall[g, 0:n_past, 0:hd] = cv_ref[pl.ds(g, n_past, stride=ATT_KV_HEADS), :].astype(BF16)
        else:
            ko_ref[layer, pl.ds(g, seq_len, stride=ATT_KV_HEADS), :] = k
            vo_ref[layer, pl.ds(g, seq_len, stride=ATT_KV_HEADS), :] = v
        kall[g, n_past:n_keys, :] = k.astype(BF16)
        vall[g, n_past:n_keys, 0:hd] = v.astype(BF16)
        vall[g, :, hd:2 * hd] = (lane == 0).astype(BF16)

    def tile_rows(item):
        return pl.ds(pl.multiple_of((item // ATT_KV_HEADS) * tq, tq), tq)

    qn_scaled = qn_ref[...] * exp2_scale

    def prep(item, g):
        rows = tile_rows(item)
        for h in range(ATT_REP):
            q = _rms(q_ref[rows, g * qw + h * hd:g * qw + (h + 1) * hd].astype(F32), qn_scaled)
            if latent:
                q = _rope(q, cos_ref[rows, :], sin_ref[rows, :])
            qs_bufs[g][h * tq:(h + 1) * tq, :] = q.astype(BF16)

    def scores(item, g):
        qs = qs_bufs[g][...]
        mx = None
        for kb in range(n_kb):
            s = _dot_nt(qs, kall[g, kb * tk:(kb + 1) * tk, :])
            s_bufs[g][:, kb * tk:(kb + 1) * tk] = s
            for c in range(tk // hd):
                part = s[:, c * hd:(c + 1) * hd]
                mx = part if mx is None else jnp.maximum(mx, part)
        m = jnp.max(mx, axis=-1, keepdims=True)
        m_bufs[g][...] = jnp.broadcast_to(m, (ATT_REP * tq, hd))

    def mix(item, g):
        m = m_bufs[g][...]
        ps = []
        for c in range(n_keys // hd):
            s = s_bufs[g][:, c * hd:(c + 1) * hd]
            ps.append(jnp.exp2(s - m).astype(BF16))
        o_bufs[g][...] = _dot(jnp.concatenate(ps, axis=1), vall[g])

    def finish(item, g):
        rows = tile_rows(item)
        o = o_bufs[g][:, 0:hd] / o_bufs[g][:, hd:hd + 1]
        for h in range(ATT_REP):
            gate = g_refs[g][rows, h * hd:(h + 1) * hd].astype(F32)
            y = o[h * tq:(h + 1) * tq, :] * (gate * jax.nn.sigmoid(gate))
            y_ref[rows, g * qw + h * hd:g * qw + (h + 1) * hd] = y.astype(BF16)

    stages = (prep, scores, mix, finish)

    def step(k, parity):
        for a, stage in enumerate(stages):
            item = k - a
            if isinstance(item, int) and not 0 <= item < n_items:
                continue
            stage(item, (parity + a) % 2)

    depth = len(stages) - 1
    for k in range(min(depth, n_items + depth)):
        step(k, k % 2)
    n_steady = max(n_items - depth, 0)

    group = ATT_STEADY_STEPS

    def steady_group(j, carry):
        for u in range(group):
            step(depth + group * j + u, (depth + u) % 2)
        return carry

    lax.fori_loop(0, n_steady // group, steady_group, 0)
    for k in range(depth + group * (n_steady // group), n_items + depth):
        step(k, k % 2)


def _attention(z, batch, seq_len, q_norm, k_norm, j, cache=None, new_kv=None):
    hd = ATT_HEAD_DIM
    latent = cache is not None
    qw = ATT_REP * hd
    kv0 = ATT_WIDTH // KV_WIDTH
    gate0 = (ATT_WIDTH + 2 * KV_WIDTH) // qw
    assert (ATT_WIDTH + 2 * KV_WIDTH) % qw == 0 and ATT_WIDTH % KV_WIDTH == 0
    in_specs = [
        pl.BlockSpec((seq_len, ATT_WIDTH), lambda b: (b, 0)),
        pl.BlockSpec((seq_len, KV_WIDTH), lambda b: (b, kv0)),
        pl.BlockSpec((seq_len, KV_WIDTH), lambda b: (b, kv0 + 1)),
        pl.BlockSpec((1, hd), lambda b: (0, 0)),
        pl.BlockSpec((1, hd), lambda b: (0, 0)),
    ] + [pl.BlockSpec((seq_len, qw), lambda b, g=g: (b, gate0 + g)) for g in range(ATT_KV_HEADS)]
    args = [z, z, z, q_norm, k_norm] + [z] * ATT_KV_HEADS
    out_specs = [pl.BlockSpec((seq_len, ATT_WIDTH), lambda b: (b, 0))]
    out_shape = [jax.ShapeDtypeStruct((batch * seq_len, ATT_WIDTH), BF16)]
    n_keys = seq_len
    aliases = {}
    layer = j
    if latent:
        cache_k, cache_v, cos, sin_signed = cache
        past = cache_k.shape[2]
        n_keys += past
        cspec = pl.BlockSpec((None, None, past * ATT_KV_HEADS, hd), lambda b: (b, j, 0, 0))
        tspec = pl.BlockSpec((seq_len, hd), lambda b: (0, 0))
        in_specs += [cspec, cspec, tspec, tspec]
        args += [cache_k.reshape(batch, N_ATT, past * ATT_KV_HEADS, hd),
                 cache_v.reshape(batch, N_ATT, past * ATT_KV_HEADS, hd), cos, sin_signed]
    else:
        out_shape += [jax.ShapeDtypeStruct((batch, N_ATT, seq_len * ATT_KV_HEADS, hd), F32)] * 2
        if new_kv is None:
            kv_spec = pl.BlockSpec((None, N_ATT, seq_len * ATT_KV_HEADS, hd), lambda b: (b, 0, 0, 0))
        else:
            kv_spec = pl.BlockSpec((None, 1, seq_len * ATT_KV_HEADS, hd), lambda b: (b, j, 0, 0))
            layer = 0
            aliases = {len(args): 1, len(args) + 1: 2}
            in_specs += [pl.BlockSpec(memory_space=pl.ANY)] * 2
            args += list(new_kv)
        out_specs += [kv_spec, kv_spec]
    rows = ATT_REP * ATT_Q_TILE
    return pl.pallas_call(
        functools.partial(_att_kernel, seq_len=seq_len, n_keys=n_keys, latent=latent, n_aliased=len(aliases),
                          layer=layer),
        grid=(batch,),
        in_specs=in_specs,
        out_specs=out_specs,
        out_shape=out_shape,
        input_output_aliases=aliases,
        scratch_shapes=[pltpu.VMEM((ATT_KV_HEADS, n_keys, hd), BF16),
                        pltpu.VMEM((ATT_KV_HEADS, n_keys, 2 * hd), BF16),
                        pltpu.VMEM((rows, hd), BF16), pltpu.VMEM((rows, hd), BF16),
                        pltpu.VMEM((rows, n_keys), F32), pltpu.VMEM((rows, n_keys), F32),
                        pltpu.VMEM((rows, hd), F32), pltpu.VMEM((rows, hd), F32),
                        pltpu.VMEM((rows, 2 * hd), F32), pltpu.VMEM((rows, 2 * hd), F32)],
        compiler_params=_cparams(1),
        name="attention",
    )(*args)


def _rope_tables(n_tokens):
    t = jnp.arange(n_tokens)
    row = (t // GRID_W).astype(F32)
    col = (t % GRID_W).astype(F32)
    inv = ROPE_THETA ** (-jnp.arange(0, AXIS_DIM, 2, dtype=F32) / AXIS_DIM)
    ang = jnp.concatenate([row[:, None] * inv[None, :], col[:, None] * inv[None, :]], axis=-1)
    cos = jnp.repeat(jnp.cos(ang), 2, axis=-1)
    sin = jnp.repeat(jnp.sin(ang), 2, axis=-1)
    sign = jnp.where(jnp.arange(ATT_HEAD_DIM) % 2 == 0, -1.0, 1.0).astype(F32)
    return cos, sin * sign


def kernel(x_prompt, x_sample, c, state_hgrn, cache_k, cache_v, c_ctx, ada_w, ada_b, norm_pre, norm_post, rec_w_in, rec_lb_logits, rec_head_norm, pool_w, pool_scale, rec_w_out, att_w_in, att_q_norm, att_k_norm, att_w_out):
    nb_c, len_c, _ = x_prompt.shape
    nb_l, len_l, _ = x_sample.shape

    lb_p = jax.nn.softmax(rec_lb_logits.astype(F32), axis=0)
    lower_bounds = jnp.clip(jnp.cumsum(lb_p, axis=0) - lb_p[0], 0.0, 1.0)
    lower_bounds = lower_bounds.reshape(N_REC, 2, REC_HEADS, 1, REC_HEAD_DIM)
    cos, sin_signed = _rope_tables(len_l)

    cvec = jnp.zeros((MOD_ROWS, D_MODEL), F32).at[0].set(c_ctx).at[1:1 + nb_l].set(c)

    forget_cols = (2 * POOL_WIDTH + REC_WIDTH, 2 * POOL_WIDTH + 3 * REC_WIDTH)
    layers = []
    for i in range(DEPTH):
        rec = i % 2 == 0
        layers.append(dict(
            f32_cols=forget_cols if rec else (0, 0),
            gain_pre=norm_pre[i].reshape(1, D_MODEL), gain_post=norm_post[i].reshape(1, D_MODEL)))

    weights = {("in", 0): rec_w_in[0].astype(BF16)}

    def weight_source(kind, i):
        stacks = (rec_w_in, rec_w_out) if i % 2 == 0 else (att_w_in, att_w_out)
        return stacks[kind == "out"], i // 2

    def cast_part(keys, steps):
        keys = [k for k in keys if k not in weights and k[1] < DEPTH]
        return keys, _cast_part([weight_source(*k) for k in keys], steps)

    def pre_args(i):
        return (weights["in", i], i, layers[i]["gain_pre"], layers[i]["f32_cols"])

    xs = [x_prompt.reshape(nb_c * len_c, D_MODEL), x_sample.reshape(nb_l * len_l, D_MODEL)]
    streams = ((nb_c, len_c, 0), (nb_l, len_l, 1))
    zs = [None, None]
    ys = [None, None]
    new_states, new_kv = None, None

    def proj_part(s, i, steps=None, mod=None):
        nb, sl, row0 = streams[s]
        post = ys[s] + (weights["out", i - 1], i - 1, layers[i - 1]["gain_post"]) if i > 0 else None
        return _proj_part(xs[s], mods if mod is None else mod, sl, row0, post=post,
                          pre=pre_args(i) if i < DEPTH else None, steps=steps)

    def take_proj(s, i, outs):
        if i > 0:
            xs[s] = outs.pop(0)
        zs[s] = outs

    def rec_part(s, i, heads=None):
        nb, sl, _ = streams[s]
        j = i // 2
        hn = rec_head_norm[j].reshape(1, REC_HEAD_DIM)
        z, zf = zs[s]
        if s == 0:
            return _rec_part(z, zf, nb, sl, lower_bounds[j], hn, j, states=new_states, heads=heads)
        return _rec_part(z, zf, nb, sl, lower_bounds[j], hn, j, s0=state_hgrn, heads=heads)

    def pool_part(s, i):
        nb, sl, _ = streams[s]
        j = i // 2
        return _pool_part(zs[s][0], nb, sl, pool_w[j].astype(BF16), pool_scale[j].reshape(1, POOL_WIDTH))

    mod0 = _mod_rows(_run(_mod_part(cvec, ada_w, ada_b, 0, 1), name="modulation")[0][0])
    first = proj_part(0, 0, mod=mod0, steps=DEPTH * 3 * D_MODEL // MOD_COLS)
    keys, casts = cast_part([("out", 0), ("in", 1)], first.steps)
    outs, (mods,), cast = _run(first, _mod_part(cvec, ada_w, ada_b, 0, DEPTH), casts, name="proj_modulation")
    weights.update(zip(keys, cast))
    mods = _mod_rows(mods)
    take_proj(0, 0, outs)
    for i in range(DEPTH):
        j = i // 2
        if i % 2 == 0:
            rec_c = rec_part(0, i)
            keys, casts = cast_part([("out", i), ("in", i + 1)], rec_c.steps)
            (y_rec, new_states), outs, cast, (y_pool,) = _run(
                rec_c, proj_part(1, i, steps=rec_c.steps), casts, pool_part(0, i), name="rec_pool_proj")
            weights.update(zip(keys, cast))
            ys[0] = (y_pool, 0, y_rec, 0)
            take_proj(1, i, outs)
            rec_l = rec_part(1, i, heads=REC_HEADS * streams[1][0] // rec_c.steps)
            keys, casts = cast_part([("out", i + 1), ("in", i + 2)], rec_l.steps)
            (y_rec,), outs, cast = _run(rec_l, proj_part(0, i + 1, steps=rec_l.steps), casts, name="rec_proj")
            weights.update(zip(keys, cast))
            ys[1] = (_run(pool_part(1, i), name="pool_mixer")[0][0], 0, y_rec, 0)
            take_proj(0, i + 1, outs)
        else:
            qn = att_q_norm[j].reshape(1, ATT_HEAD_DIM)
            kn = att_k_norm[j].reshape(1, ATT_HEAD_DIM)
            y, *new_kv = _attention(zs[0][0], nb_c, len_c, qn, kn, j, new_kv=new_kv)
            ys[0] = (y, 0, y, 1)
            take_proj(1, i, _run(proj_part(1, i), name="proj")[0])
            (y,) = _attention(zs[1][0], nb_l, len_l, qn, kn, j, cache=(cache_k, cache_v, cos, sin_signed))
            ys[1] = (y, 0, y, 1)
            take_proj(0, i + 1, _run(proj_part(0, i + 1), name="proj")[0])
    take_proj(1, DEPTH, _run(proj_part(1, DEPTH), name="proj")[0])
    xc, xl = xs

    kv_shape = (nb_c, N_ATT, len_c, ATT_KV_HEADS, ATT_HEAD_DIM)
    return (xc.reshape(nb_c, len_c, D_MODEL), xl.reshape(nb_l, len_l, D_MODEL),
            new_states, new_kv[0].reshape(kv_shape), new_kv[1].reshape(kv_shape))
```

```python
import functools
from typing import Any, NamedTuple

import jax
import jax.numpy as jnp
from jax import lax
from jax.experimental import pallas as pl
from jax.experimental.pallas import tpu as pltpu

D_MODEL = 1024
DEPTH = 4
GRID_W = 64
N_REC = (DEPTH + 1) // 2
N_ATT = DEPTH // 2
POOL_WIDTH = D_MODEL // 2
POOL_WINDOWS = (2, 4, 8, 16)
POOL_GROUP_DIM = POOL_WIDTH // len(POOL_WINDOWS)
REC_WIDTH = D_MODEL // 2
REC_HEAD_DIM = 128
REC_HEADS = REC_WIDTH // REC_HEAD_DIM
REC_IN_WIDTH = 2 * POOL_WIDTH + 5 * REC_WIDTH
ATT_HEAD_DIM = 128
ATT_HEADS = D_MODEL // ATT_HEAD_DIM
ATT_KV_HEADS = 2
ATT_REP = ATT_HEADS // ATT_KV_HEADS
ATT_WIDTH = ATT_HEADS * ATT_HEAD_DIM
KV_WIDTH = ATT_KV_HEADS * ATT_HEAD_DIM
ATT_IN_WIDTH = 2 * ATT_WIDTH + 2 * KV_WIDTH
AXIS_DIM = ATT_HEAD_DIM // 2
ROPE_THETA = 10000.0
EPS = 1e-6
F_MIN = 1e-6

MOD_ROWS = 16
TOKEN_TILE = 1024
MIN_PROJ_STEPS = 4
PROJ_GROUP_ROWS = 512
POOL_ROWS = 256
IN_PROJ_STEP = 512
MOD_COLS = 1536
SCAN_CHUNK = 128
SCAN_UNITS = 8
POOL_HALO = 128
ATT_Q_TILE = 256
ATT_KEY_TILE = 128
ATT_STEADY_STEPS = 2
LOG2_E = 1.4426950408889634
VMEM_LIMIT = 56 * 1024 * 1024

F32 = jnp.float32
BF16 = jnp.bfloat16


def _cparams(n_axes):
    return pltpu.CompilerParams(
        dimension_semantics=("arbitrary",) * n_axes, vmem_limit_bytes=VMEM_LIMIT)


def _rms(x, g):
    return x * lax.rsqrt(jnp.mean(x * x, axis=-1, keepdims=True) + EPS) * g


def _dot(a, b):
    return jnp.dot(a, b, preferred_element_type=F32)


def _dot_nt(a, b):
    return lax.dot_general(a, b, (((1,), (1,)), ((), ())), preferred_element_type=F32)


class _Part(NamedTuple):
    kernel: Any
    steps: int
    in_specs: list
    args: list
    out_specs: list
    out_shape: list
    scratch_shapes: list
    aliases: dict


def _run(*parts, name):
    steps = parts[0].steps
    assert all(p.steps == steps for p in parts)
    n_in = [len(p.args) for p in parts]
    n_out = [len(p.out_shape) for p in parts]
    n_scr = [len(p.scratch_shapes) for p in parts]

    def body(*refs):
        ins, outs, scr = refs[:sum(n_in)], refs[sum(n_in):sum(n_in) + sum(n_out)], refs[sum(n_in) + sum(n_out):]
        pending = {}
        for k, p in enumerate(parts):
            i0, o0, s0 = sum(n_in[:k]), sum(n_out[:k]), sum(n_scr[:k])
            pending[k] = (0.0, p.kernel(*ins[i0:i0 + n_in[k]], *outs[o0:o0 + n_out[k]], *scr[s0:s0 + n_scr[k]]))
        while pending:
            k = min(pending, key=lambda k: pending[k][0])
            try:
                pending[k] = (next(pending[k][1]), pending[k][1])
            except StopIteration:
                del pending[k]

    aliases = {}
    for k, p in enumerate(parts):
        for i, o in p.aliases.items():
            aliases[sum(n_in[:k]) + i] = sum(n_out[:k]) + o
    outs = pl.pallas_call(
        body,
        grid=(steps,),
        in_specs=[s for p in parts for s in p.in_specs],
        out_specs=[s for p in parts for s in p.out_specs],
        out_shape=[s for p in parts for s in p.out_shape],
        scratch_shapes=[s for p in parts for s in p.scratch_shapes],
        input_output_aliases=aliases,
        compiler_params=_cparams(1),
        name=name,
    )(*[a for p in parts for a in p.args])
    return [list(outs[sum(n_out[:k]):sum(n_out[:k]) + n_out[k]]) for k in range(len(parts))]


def _mod_kernel(cv_ref, w_ref, b_ref, o_ref):
    cv = cv_ref[...]
    a = (cv * jax.nn.sigmoid(cv)).astype(BF16)
    o_ref[...] = _dot(a, w_ref[...].astype(BF16)) + b_ref[...]
    yield 1.0


def _mod_part(cvec, ada_w, ada_b, layer0, n_layers):
    tn = MOD_COLS
    blocks = 3 * D_MODEL // tn
    in_specs = [
        pl.BlockSpec((MOD_ROWS, D_MODEL), lambda i: (0, 0)),
        pl.BlockSpec((None, D_MODEL, tn), lambda i: (layer0 + i // blocks, 0, i % blocks)),
        pl.BlockSpec((None, 1, tn), lambda i: (layer0 + i // blocks, 0, i % blocks)),
    ]
    out_specs = [pl.BlockSpec((None, MOD_ROWS, tn), lambda i: (i // blocks, 0, i % blocks))]
    out_shape = [jax.ShapeDtypeStruct((n_layers, MOD_ROWS, 3 * D_MODEL), F32)]
    args = [cvec, ada_w, ada_b.reshape(DEPTH, 1, 3 * D_MODEL)]
    return _Part(_mod_kernel, n_layers * blocks, in_specs, args, out_specs, out_shape, [], {})


def _cast_kernel(*refs):
    n = len(refs) // 2
    for w_ref, o_ref in zip(refs[:n], refs[n:]):
        o_ref[...] = w_ref[...].astype(BF16)
    yield 1.0


def _cast_part(sources, steps):
    in_specs, out_specs, out_shape, args = [], [], [], []
    for stack, j in sources:
        _, k, n = stack.shape
        in_specs.append(pl.BlockSpec((None, k // steps, n), lambda i, j=j: (j, i, 0)))
        out_specs.append(pl.BlockSpec((k // steps, n), lambda i: (i, 0)))
        out_shape.append(jax.ShapeDtypeStruct((k, n), BF16))
        args.append(stack)
    return _Part(_cast_kernel, steps, in_specs, args, out_specs, out_shape, [], {})


def _mod_rows(out):
    return out.reshape(out.shape[0], MOD_ROWS, 3, 1, D_MODEL)


def _mod_spec(layer, seq_len, row0, tile):
    assert row0 == 0 or seq_len % tile == 0
    tiles_per_seq = seq_len // tile
    if row0 == 0:
        index = lambda i: (layer, 0, 0, 0, 0)
    else:
        index = lambda i: (layer, row0 + i // tiles_per_seq, 0, 0, 0)
    return pl.BlockSpec((None, None, 3, 1, D_MODEL), index)


def _proj_kernel(*refs, post, pre, n_out, f32_cols):
    refs = list(refs)
    x_ref = refs.pop(0)
    if post:
        ya_ref, yb_ref, wo_ref, modp_ref, gpost_ref = (refs.pop(0) for _ in range(5))
    if pre:
        modn_ref, gpre_ref, wi_ref = (refs.pop(0) for _ in range(3))
    xo_ref = refs.pop(0) if post else None
    rows_all = x_ref.shape[0]
    n_groups = max(1, rows_all // PROJ_GROUP_ROWS)

    def group(rows):
        x = x_ref[rows, :]
        if post:
            half = D_MODEL // 2
            p = _dot(ya_ref[rows, :], wo_ref[0:half, :]) + _dot(yb_ref[rows, :], wo_ref[half:D_MODEL, :])
            x = x + modp_ref[2] * _rms(p, gpost_ref[...])
            xo_ref[rows, :] = x
            yield
        if pre:
            h = (_rms(x, gpre_ref[...]) * (1.0 + modn_ref[1]) + modn_ref[0]).astype(BF16)
            lo, hi = f32_cols
            for c0 in range(0, n_out, IN_PROJ_STEP):
                y = _dot(h, wi_ref[:, c0:c0 + IN_PROJ_STEP])
                if lo <= c0 < hi:
                    refs[1][rows, c0 - lo:c0 - lo + IN_PROJ_STEP] = y
                else:
                    c1 = c0 if c0 < lo else c0 - (hi - lo)
                    refs[0][rows, c1:c1 + IN_PROJ_STEP] = y.astype(BF16)
                yield

    size = rows_all // n_groups
    waiting = [group(slice(k * size, (k + 1) * size)) for k in range(n_groups)]
    running = []
    pieces = n_groups * (post + n_out // IN_PROJ_STEP)
    done = 0
    while waiting or running:
        if waiting:
            running.append(waiting.pop(0))
        for g in list(running):
            try:
                next(g)
                done += 1
            except StopIteration:
                running.remove(g)
        yield min(done / pieces, 1.0)


def _proj_part(x, mod, seq_len, row0, post=None, pre=None, steps=None):
    t = x.shape[0]
    tile = t // steps if steps else min(TOKEN_TILE, t // MIN_PROJ_STEPS)
    half = D_MODEL // 2
    row = lambda i: (i, 0)
    fixed = lambda i: (0, 0)
    resident = dict(index_map=fixed, pipeline_mode=pl.Buffered(1))
    in_specs = [pl.BlockSpec((tile, D_MODEL), row)]
    args = [x]
    out_specs, out_shape = [], []
    n_out, f32_cols = 0, (0, 0)
    if post:
        ya, ia, yb, ib, w_out, layer, gain = post
        in_specs += [pl.BlockSpec((tile, half), lambda i: (i, ia)),
                     pl.BlockSpec((tile, half), lambda i: (i, ib)),
                     pl.BlockSpec((D_MODEL, D_MODEL), **resident),
                     _mod_spec(layer, seq_len, row0, tile),
                     pl.BlockSpec((1, D_MODEL), fixed)]
        args += [ya, yb, w_out, mod, gain]
        out_specs.append(pl.BlockSpec((tile, D_MODEL), row))
        out_shape.append(jax.ShapeDtypeStruct((t, D_MODEL), F32))
    if pre:
        w_in, layer, gain, f32_cols = pre
        n_out = w_in.shape[1]
        lo, hi = f32_cols
        assert lo % IN_PROJ_STEP == 0 and hi % IN_PROJ_STEP == 0 and n_out % IN_PROJ_STEP == 0
        in_specs += [_mod_spec(layer, seq_len, row0, tile),
                     pl.BlockSpec((1, D_MODEL), fixed),
                     pl.BlockSpec((D_MODEL, n_out), **resident)]
        args += [mod, gain, w_in]
        out_specs.append(pl.BlockSpec((tile, n_out - (hi - lo)), row))
        out_shape.append(jax.ShapeDtypeStruct((t, n_out - (hi - lo)), BF16))
        if hi > lo:
            out_specs.append(pl.BlockSpec((tile, hi - lo), row))
            out_shape.append(jax.ShapeDtypeStruct((t, hi - lo), F32))
    kernel = functools.partial(_proj_kernel, post=bool(post), pre=bool(pre), n_out=n_out, f32_cols=f32_cols)
    return _Part(kernel, t // tile, in_specs, args, out_specs, out_shape, [], {})


def _pool_kernel(u_ref, gp_ref, pw_ref, ps_ref, o_ref, pad_ref, band_ref, sum_ref, dif_ref, *, seq_len):
    gd = POOL_GROUP_DIM
    rows = POOL_ROWS
    halo = POOL_HALO
    span = rows + 2 * halo

    @pl.when(pl.program_id(0) == 0)
    def _():
        r = lax.broadcasted_iota(jnp.int32, (rows, span), 0)
        c = lax.broadcasted_iota(jnp.int32, (rows, span), 1)
        offset = c - halo - r
        for g, win in enumerate(POOL_WINDOWS):
            band_ref[g] = jnp.where((offset >= -(win // 2)) & (offset < win // 2), 1.0, 0.0).astype(BF16)

    zeros = jnp.zeros((halo, POOL_WIDTH), BF16)
    pad_ref[0:halo, :] = zeros
    pad_ref[halo + seq_len:2 * halo + seq_len, :] = zeros
    pad_ref[halo:halo + seq_len, :] = u_ref[...]
    units = [(g, r * rows) for g in range(len(POOL_WINDOWS)) for r in range(seq_len // rows)]

    def cols(g):
        return slice(g * gd, (g + 1) * gd)

    for g, base in units:
        sum_ref[base:base + rows, cols(g)] = _dot(band_ref[g], pad_ref[base:base + span, cols(g)])
    yield 0.3
    for g, base in units:
        win = POOL_WINDOWS[g]
        t = base + lax.broadcasted_iota(jnp.int32, (rows, 1), 0)
        count = jnp.clip(t + win // 2, 0, seq_len) - jnp.clip(t - win // 2, 0, seq_len)
        mean = sum_ref[base:base + rows, cols(g)] / count.astype(F32)
        d = mean - pad_ref[halo + base:halo + base + rows, cols(g)].astype(F32)
        dif_ref[base:base + rows, cols(g)] = d.astype(BF16)
    yield 0.6
    for g, base in units:
        y = _dot(dif_ref[base:base + rows, cols(g)], pw_ref[g]) * ps_ref[:, cols(g)]
        gate = gp_ref[base:base + rows, cols(g)].astype(F32)
        o_ref[base:base + rows, cols(g)] = (y * (gate * jax.nn.sigmoid(gate))).astype(BF16)
    yield 1.0


def _pool_part(z, batch, seq_len, pool_w, pool_scale):
    in_specs = [
        pl.BlockSpec((seq_len, POOL_WIDTH), lambda b: (b, 0)),
        pl.BlockSpec((seq_len, POOL_WIDTH), lambda b: (b, 1)),
        pl.BlockSpec((len(POOL_WINDOWS), POOL_GROUP_DIM, POOL_GROUP_DIM), lambda b: (0, 0, 0)),
        pl.BlockSpec((1, POOL_WIDTH), lambda b: (0, 0)),
    ]
    out_specs = [pl.BlockSpec((seq_len, POOL_WIDTH), lambda b: (b, 0))]
    out_shape = [jax.ShapeDtypeStruct((batch * seq_len, POOL_WIDTH), BF16)]
    scratch_shapes = [pltpu.VMEM((seq_len + 2 * POOL_HALO, POOL_WIDTH), BF16),
                      pltpu.VMEM((len(POOL_WINDOWS), POOL_ROWS, POOL_ROWS + 2 * POOL_HALO), BF16),
                      pltpu.VMEM((seq_len, POOL_WIDTH), F32), pltpu.VMEM((seq_len, POOL_WIDTH), BF16)]
    kernel = functools.partial(_pool_kernel, seq_len=seq_len)
    return _Part(kernel, batch, in_specs, [z, z, pool_w, pool_scale], out_specs, out_shape, scratch_shapes, {})


def _block_row(p, block, row):
    c, w = p.shape
    p3 = p.reshape(c // block, block, w)
    return jnp.broadcast_to(p3[:, row:row + 1, :], p3.shape).reshape(c, w)


def _interleave(lo, hi, block):
    half = block // 2
    parts = []
    for b in range(SCAN_CHUNK // block):
        parts.append(lo[b * block:b * block + half])
        parts.append(hi[b * block + half:(b + 1) * block])
    return jnp.concatenate(parts, axis=0)


def _select_levels(products, rev):
    c = SCAN_CHUNK
    lane = lax.broadcasted_iota(jnp.int32, (8, c), 1)
    sub = lax.broadcasted_iota(jnp.int32, (8, c), 0)
    out = []
    for t0 in range(0, c, 8):
        if rev:
            keep = (lane >= t0 + sub) & (lane < t0 + 8)
        else:
            keep = (lane >= t0) & (lane <= t0 + sub)
        row = jnp.where(keep, products[0][t0:t0 + 8, :], 0.0)
        for lv, block in enumerate((16, 32, 64, 128), 1):
            half = block // 2
            start = t0 // block * block
            upper = t0 - start >= half
            if upper == rev:
                continue
            lo = start + half if rev else start
            row = jnp.where((lane >= lo) & (lane < lo + half), products[lv][t0:t0 + 8, :], row)
        out.append(row)
    return jnp.concatenate(out, axis=0)


def _rec_kernel(*refs, seq_len, heads, has_s0, out_state, n_aliased, layer):
    q_ref, ff_ref, fb_ref, v_ref, gr_ref, lb_ref, hn_ref = refs[:7]
    rest = list(refs[7:])
    s0_ref = rest.pop(0) if has_s0 else None
    del rest[:n_aliased]
    o_ref = rest.pop(0)
    st_ref = rest.pop(0) if out_state else None
    acc_ref, qs_ref, k_ref, p_ref, a_ref, qin_ref, u_ref, dec_ref, tri_ref = rest

    c = SCAN_CHUNK
    hd = REC_HEAD_DIM
    n_chunks = seq_len // c
    zf_refs = (ff_ref, fb_ref)
    units = [(h, n) for h in range(heads) for n in range(n_chunks)]

    def rows_in(n):
        return slice(n * c, (n + 1) * c)

    def rows_sc(h, n):
        return slice((h * n_chunks + n) * c, (h * n_chunks + n + 1) * c)

    def cols(h):
        return slice(h * hd, (h + 1) * hd)

    ti = lax.broadcasted_iota(jnp.int32, (c, c), 0)
    si = lax.broadcasted_iota(jnp.int32, (c, c), 1)
    for d, causal in enumerate((si <= ti, si >= ti)):
        tri_ref[d] = jnp.where(causal, 1.0, 0.0).astype(BF16)

    weights = (0.26, 0.43, 0.09, 0.22)
    done = [0.0]

    def progress(stage, share):
        done[0] += weights[stage] * share
        return done[0]

    for h, n in units:
        qz = q_ref[rows_in(n), cols(h)].astype(F32)
        qs_ref[rows_sc(h, n), :] = qz * jax.nn.sigmoid(qz)
        for d in range(2):
            lower = lb_ref[d, h]
            f = jnp.clip(lower + (1.0 - lower) * jax.nn.sigmoid(zf_refs[d][rows_in(n), cols(h)]), F_MIN, 1.0)
            k_ref[d, rows_sc(h, n), :] = 1.0 - f
            g = jnp.log(f) * LOG2_E
            g_hi = g.astype(BF16)
            rest = g - g_hi.astype(F32)
            g_mid = rest.astype(BF16)
            g_lo = (rest - g_mid.astype(F32)).astype(BF16)
            sums = _dot(tri_ref[d], jnp.concatenate([g_hi, g_mid, g_lo], axis=1))
            p_ref[d, rows_sc(h, n), :] = (sums[:, 0:hd] + sums[:, hd:2 * hd]) + sums[:, 2 * hd:3 * hd]
        yield progress(0, 1 / len(units))

    for h, n in units:
        rows = rows_sc(h, n)
        q = qs_ref[rows, :]
        v_t = v_ref[rows_in(n), cols(h)].astype(F32).T.astype(BF16)
        for d in range(2):
            rev = d == 1
            k = k_ref[d, rows, :]
            p = p_ref[d, rows, :]
            e0 = p - _block_row(p, 8, 4 if rev else 3)
            products = [_dot_nt((q * jnp.exp2(e0)).astype(BF16), (k * jnp.exp2(-e0)).astype(BF16))]
            for block in (16, 32, 64, 128):
                beta = _block_row(p, block, block // 2 if rev else block // 2 - 1)
                if rev:
                    e = _interleave(p, beta, block) - _interleave(beta, p, block)
                    src = _interleave(q, k, block)
                else:
                    e = _interleave(beta, p, block) - _interleave(p, beta, block)
                    src = _interleave(k, q, block)
                m = (src * jnp.exp2(e)).astype(BF16)
                products.append(_dot_nt(m, m))
            a_ref[rows, d * c:(d + 1) * c] = _select_levels(products, rev).astype(BF16)

            edge = p[0:1, :] if rev else p[c - 1:c, :]
            qin_ref[d, rows, :] = (q * jnp.exp2(p)).astype(BF16)
            u_ref[d, h * n_chunks + n] = _dot(v_t, (k * jnp.exp2(edge - p)).astype(BF16))
            dec_ref[d, h * n_chunks + n] = jnp.broadcast_to(jnp.exp2(edge), (8, hd))
        yield progress(1, 1 / len(units))

    for h, n in units:
        v_b = v_ref[rows_in(n), cols(h)]
        acc_ref[rows_sc(h, n), :] = _dot(a_ref[rows_sc(h, n), :], jnp.concatenate([v_b, v_b], axis=0))
        yield progress(2, 1 / len(units))

    for h in range(heads):
        if has_s0:
            states = [s0_ref[0, h].T, s0_ref[1, h].T]
        else:
            states = [jnp.zeros((hd, hd), F32)] * 2
        inter = [[None] * n_chunks, [None] * n_chunks]
        for i in range(n_chunks):
            for d, n in ((0, i), (1, n_chunks - 1 - i)):
                inter[d][n] = _dot_nt(qin_ref[d, rows_sc(h, n), :], states[d].astype(BF16))
                states[d] = dec_ref[d, h * n_chunks + n][0:1, :] * states[d] + u_ref[d, h * n_chunks + n]
        for n in range(n_chunks):
            gate = gr_ref[rows_in(n), cols(h)].astype(F32)
            o = acc_ref[rows_sc(h, n), :] + inter[0][n] + inter[1][n]
            o_ref[rows_in(n), cols(h)] = (_rms(o, hn_ref[...]) * (gate * jax.nn.sigmoid(gate))).astype(BF16)
        if out_state:
            st_ref[layer, 0, h] = states[0].T
            st_ref[layer, 1, h] = states[1].T
        yield progress(3, 1 / heads)
    if out_state:
        for other in range(st_ref.shape[0]):
            if other != layer:
                st_ref[other] = jnp.zeros(st_ref.shape[1:], F32)


def _rec_part(z, zf, batch, seq_len, lower, head_norm, j, s0=None, states=None, heads=None):
    hd = REC_HEAD_DIM
    nh = REC_HEADS
    n_chunks = seq_len // SCAN_CHUNK
    heads = heads or min(nh, max(1, SCAN_UNITS // n_chunks))
    width = heads * hd
    col0 = 2 * POOL_WIDTH // width
    groups = nh // heads

    def zspec(part):
        return pl.BlockSpec((seq_len, width), lambda i: (i // groups, col0 + part * groups + i % groups))

    def fspec(part):
        return pl.BlockSpec((seq_len, width), lambda i: (i // groups, part * groups + i % groups))

    in_specs = [zspec(0), fspec(0), fspec(1), zspec(1), zspec(2),
                pl.BlockSpec((2, heads, 1, hd), lambda i: (0, i % groups, 0, 0)),
                pl.BlockSpec((1, hd), lambda i: (0, 0))]
    args = [z, zf, zf, z, z, lower, head_norm]
    has_s0 = s0 is not None
    if has_s0:
        in_specs.append(pl.BlockSpec((None, None, 2, heads, hd, hd),
                                     lambda i: (i // groups, j, 0, i % groups, 0, 0)))
        args.append(s0)
    out_state = not has_s0
    out_specs = [pl.BlockSpec((seq_len, width), lambda i: (i // groups, i % groups))]
    out_shape = [jax.ShapeDtypeStruct((batch * seq_len, REC_WIDTH), BF16)]
    aliases = {}
    layer = j
    if out_state:
        out_shape.append(jax.ShapeDtypeStruct((batch, N_REC, 2, nh, hd, hd), F32))
        if states is None:
            out_specs.append(pl.BlockSpec((None, N_REC, 2, heads, hd, hd),
                                          lambda i: (i // groups, 0, 0, i % groups, 0, 0)))
        else:
            out_specs.append(pl.BlockSpec((None, 1, 2, heads, hd, hd),
                                          lambda i: (i // groups, j, 0, i % groups, 0, 0)))
            layer = 0
            aliases[len(args)] = 1
            in_specs.append(pl.BlockSpec(memory_space=pl.ANY))
            args.append(states)
    tokens = heads * seq_len
    kernel = functools.partial(_rec_kernel, seq_len=seq_len, heads=heads, has_s0=has_s0, out_state=out_state,
                               n_aliased=len(aliases), layer=layer)
    scratch_shapes = [pltpu.VMEM((tokens, hd), F32),
                      pltpu.VMEM((tokens, hd), F32),
                      pltpu.VMEM((2, tokens, hd), F32),
                      pltpu.VMEM((2, tokens, hd), F32),
                      pltpu.VMEM((tokens, 2 * SCAN_CHUNK), BF16),
                      pltpu.VMEM((2, tokens, hd), BF16),
                      pltpu.VMEM((2, tokens // SCAN_CHUNK, hd, hd), F32),
                      pltpu.VMEM((2, tokens // SCAN_CHUNK, 8, hd), F32),
                      pltpu.VMEM((2, SCAN_CHUNK, SCAN_CHUNK), BF16)]
    return _Part(kernel, batch * groups, in_specs, args, out_specs, out_shape, scratch_shapes, aliases)


def _rope(x, cos, sin_signed):
    lane = lax.broadcasted_iota(jnp.int32, x.shape, 1)
    partner = jnp.where((lane & 1) == 0, pltpu.roll(x, ATT_HEAD_DIM - 1, 1), pltpu.roll(x, 1, 1))
    return x * cos + partner * sin_signed


def _att_kernel(*refs, seq_len, n_keys, latent, n_aliased, layer):
    q_ref, k_ref, v_ref, qn_ref, kn_ref = refs[:5]
    g_refs = refs[5:5 + ATT_KV_HEADS]
    if latent:
        ck_ref, cv_ref, cos_ref, sin_ref, y_ref, kall, vall = refs[5 + ATT_KV_HEADS:-8]
    else:
        y_ref, ko_ref, vo_ref, kall, vall = refs[5 + ATT_KV_HEADS + n_aliased:-8]
    qs_bufs, s_bufs, m_bufs, o_bufs = refs[-8:-6], refs[-6:-4], refs[-4:-2], refs[-2:]
    hd = ATT_HEAD_DIM
    qw = ATT_REP * hd
    tq = ATT_Q_TILE
    tk = min(ATT_KEY_TILE, n_keys)
    n_kb = n_keys // tk
    n_past = n_keys - seq_len
    n_items = ATT_KV_HEADS * (seq_len // tq)
    exp2_scale = hd ** -0.5 * LOG2_E
    assert ATT_KV_HEADS == 2

    if not latent:
        for other in range(ko_ref.shape[0]):
            if other != layer:
                ko_ref[other] = jnp.zeros(ko_ref.shape[1:], F32)
                vo_ref[other] = jnp.zeros(vo_ref.shape[1:], F32)
    lane = lax.broadcasted_iota(jnp.int32, (n_keys, hd), 1)
    for g in range(ATT_KV_HEADS):
        k = _rms(k_ref[:, g * hd:(g + 1) * hd].astype(F32), kn_ref[...])
        v = v_ref[:, g * hd:(g + 1) * hd].astype(F32)
        if latent:
            k = _rope(k, cos_ref[...], sin_ref[...])
            kall[g, 0:n_past, :] = ck_ref[pl.ds(g, n_past, stride=ATT_KV_HEADS), :].astype(BF16)
            vall[g, 0:n_past, 0:hd] = cv_ref[pl.ds(g, n_past, stride=ATT_KV_HEADS), :].astype(BF16)
        else:
            ko_ref[layer, pl.ds(g, seq_len, stride=ATT_KV_HEADS), :] = k
            vo_ref[layer, pl.ds(g, seq_len, stride=ATT_KV_HEADS), :] = v
        kall[g, n_past:n_keys, :] = k.astype(BF16)
        vall[g, n_past:n_keys, 0:hd] = v.astype(BF16)
        vall[g, :, hd:2 * hd] = (lane == 0).astype(BF16)

    def tile_rows(item):
        return pl.ds(pl.multiple_of((item // ATT_KV_HEADS) * tq, tq), tq)

    qn_scaled = qn_ref[...] * exp2_scale

    def prep(item, g):
        rows = tile_rows(item)
        for h in range(ATT_REP):
            q = _rms(q_ref[rows, g * qw + h * hd:g * qw + (h + 1) * hd].astype(F32), qn_scaled)
            if latent:
                q = _rope(q, cos_ref[rows, :], sin_ref[rows, :])
            qs_bufs[g][h * tq:(h + 1) * tq, :] = q.astype(BF16)

    def scores(item, g):
        qs = qs_bufs[g][...]
        mx = None
        for kb in range(n_kb):
            s = _dot_nt(qs, kall[g, kb * tk:(kb + 1) * tk, :])
            s_bufs[g][:, kb * tk:(kb + 1) * tk] = s
            for c in range(tk // hd):
                part = s[:, c * hd:(c + 1) * hd]
                mx = part if mx is None else jnp.maximum(mx, part)
        m = jnp.max(mx, axis=-1, keepdims=True)
        m_bufs[g][...] = jnp.broadcast_to(m, (ATT_REP * tq, hd))

    def mix(item, g):
        m = m_bufs[g][...]
        ps = []
        for c in range(n_keys // hd):
            s = s_bufs[g][:, c * hd:(c + 1) * hd]
            ps.append(jnp.exp2(s - m).astype(BF16))
        o_bufs[g][...] = _dot(jnp.concatenate(ps, axis=1), vall[g])

    def finish(item, g):
        rows = tile_rows(item)
        o = o_bufs[g][:, 0:hd] / o_bufs[g][:, hd:hd + 1]
        for h in range(ATT_REP):
            gate = g_refs[g][rows, h * hd:(h + 1) * hd].astype(F32)
            y = o[h * tq:(h + 1) * tq, :] * (gate * jax.nn.sigmoid(gate))
            y_ref[rows, g * qw + h * hd:g * qw + (h + 1) * hd] = y.astype(BF16)

    def mix_finish(item, g):
        mix(item, g)
        finish(item, g)

    stages = (prep, scores, mix_finish)

    def step(k, parity):
        for a, stage in enumerate(stages):
            item = k - a
            if isinstance(item, int) and not 0 <= item < n_items:
                continue
            stage(item, (parity + a) % 2)

    depth = len(stages) - 1
    for k in range(min(depth, n_items + depth)):
        step(k, k % 2)
    n_steady = max(n_items - depth, 0)

    group = ATT_STEADY_STEPS

    def steady_group(j, carry):
        for u in range(group):
            step(depth + group * j + u, (depth + u) % 2)
        return carry

    lax.fori_loop(0, n_steady // group, steady_group, 0)
    for k in range(depth + group * (n_steady // group), n_items + depth):
        step(k, k % 2)


def _attention(z, batch, seq_len, q_norm, k_norm, j, cache=None, new_kv=None):
    hd = ATT_HEAD_DIM
    latent = cache is not None
    qw = ATT_REP * hd
    kv0 = ATT_WIDTH // KV_WIDTH
    gate0 = (ATT_WIDTH + 2 * KV_WIDTH) // qw
    assert (ATT_WIDTH + 2 * KV_WIDTH) % qw == 0 and ATT_WIDTH % KV_WIDTH == 0
    in_specs = [
        pl.BlockSpec((seq_len, ATT_WIDTH), lambda b: (b, 0)),
        pl.BlockSpec((seq_len, KV_WIDTH), lambda b: (b, kv0)),
        pl.BlockSpec((seq_len, KV_WIDTH), lambda b: (b, kv0 + 1)),
        pl.BlockSpec((1, hd), lambda b: (0, 0)),
        pl.BlockSpec((1, hd), lambda b: (0, 0)),
    ] + [pl.BlockSpec((seq_len, qw), lambda b, g=g: (b, gate0 + g)) for g in range(ATT_KV_HEADS)]
    args = [z, z, z, q_norm, k_norm] + [z] * ATT_KV_HEADS
    out_specs = [pl.BlockSpec((seq_len, ATT_WIDTH), lambda b: (b, 0))]
    out_shape = [jax.ShapeDtypeStruct((batch * seq_len, ATT_WIDTH), BF16)]
    n_keys = seq_len
    aliases = {}
    layer = j
    if latent:
        cache_k, cache_v, cos, sin_signed = cache
        past = cache_k.shape[2]
        n_keys += past
        cspec = pl.BlockSpec((None, None, past * ATT_KV_HEADS, hd), lambda b: (b, j, 0, 0))
        tspec = pl.BlockSpec((seq_len, hd), lambda b: (0, 0))
        in_specs += [cspec, cspec, tspec, tspec]
        args += [cache_k.reshape(batch, N_ATT, past * ATT_KV_HEADS, hd),
                 cache_v.reshape(batch, N_ATT, past * ATT_KV_HEADS, hd), cos, sin_signed]
    else:
        out_shape += [jax.ShapeDtypeStruct((batch, N_ATT, seq_len * ATT_KV_HEADS, hd), F32)] * 2
        if new_kv is None:
            kv_spec = pl.BlockSpec((None, N_ATT, seq_len * ATT_KV_HEADS, hd), lambda b: (b, 0, 0, 0))
        else:
            kv_spec = pl.BlockSpec((None, 1, seq_len * ATT_KV_HEADS, hd), lambda b: (b, j, 0, 0))
            layer = 0
            aliases = {len(args): 1, len(args) + 1: 2}
            in_specs += [pl.BlockSpec(memory_space=pl.ANY)] * 2
            args += list(new_kv)
        out_specs += [kv_spec, kv_spec]
    rows = ATT_REP * ATT_Q_TILE
    return pl.pallas_call(
        functools.partial(_att_kernel, seq_len=seq_len, n_keys=n_keys, latent=latent, n_aliased=len(aliases),
                          layer=layer),
        grid=(batch,),
        in_specs=in_specs,
        out_specs=out_specs,
        out_shape=out_shape,
        input_output_aliases=aliases,
        scratch_shapes=[pltpu.VMEM((ATT_KV_HEADS, n_keys, hd), BF16),
                        pltpu.VMEM((ATT_KV_HEADS, n_keys, 2 * hd), BF16),
                        pltpu.VMEM((rows, hd), BF16), pltpu.VMEM((rows, hd), BF16),
                        pltpu.VMEM((rows, n_keys), F32), pltpu.VMEM((rows, n_keys), F32),
                        pltpu.VMEM((rows, hd), F32), pltpu.VMEM((rows, hd), F32),
                        pltpu.VMEM((rows, 2 * hd), F32), pltpu.VMEM((rows, 2 * hd), F32)],
        compiler_params=_cparams(1),
        name="attention",
    )(*args)


def _rope_tables(n_tokens):
    t = jnp.arange(n_tokens)
    row = (t // GRID_W).astype(F32)
    col = (t % GRID_W).astype(F32)
    inv = ROPE_THETA ** (-jnp.arange(0, AXIS_DIM, 2, dtype=F32) / AXIS_DIM)
    ang = jnp.concatenate([row[:, None] * inv[None, :], col[:, None] * inv[None, :]], axis=-1)
    cos = jnp.repeat(jnp.cos(ang), 2, axis=-1)
    sin = jnp.repeat(jnp.sin(ang), 2, axis=-1)
    sign = jnp.where(jnp.arange(ATT_HEAD_DIM) % 2 == 0, -1.0, 1.0).astype(F32)
    return cos, sin * sign


def kernel(x_prompt, x_sample, c, state_hgrn, cache_k, cache_v, c_ctx, ada_w, ada_b, norm_pre, norm_post, rec_w_in, rec_lb_logits, rec_head_norm, pool_w, pool_scale, rec_w_out, att_w_in, att_q_norm, att_k_norm, att_w_out):
    nb_c, len_c, _ = x_prompt.shape
    nb_l, len_l, _ = x_sample.shape

    lb_p = jax.nn.softmax(rec_lb_logits.astype(F32), axis=0)
    lower_bounds = jnp.clip(jnp.cumsum(lb_p, axis=0) - lb_p[0], 0.0, 1.0)
    lower_bounds = lower_bounds.reshape(N_REC, 2, REC_HEADS, 1, REC_HEAD_DIM)
    cos, sin_signed = _rope_tables(len_l)

    cvec = jnp.zeros((MOD_ROWS, D_MODEL), F32).at[0].set(c_ctx).at[1:1 + nb_l].set(c)

    forget_cols = (2 * POOL_WIDTH + REC_WIDTH, 2 * POOL_WIDTH + 3 * REC_WIDTH)
    layers = []
    for i in range(DEPTH):
        rec = i % 2 == 0
        layers.append(dict(
            f32_cols=forget_cols if rec else (0, 0),
            gain_pre=norm_pre[i].reshape(1, D_MODEL), gain_post=norm_post[i].reshape(1, D_MODEL)))

    weights = {("in", 0): rec_w_in[0].astype(BF16)}

    def weight_source(kind, i):
        stacks = (rec_w_in, rec_w_out) if i % 2 == 0 else (att_w_in, att_w_out)
        return stacks[kind == "out"], i // 2

    def cast_part(keys, steps):
        keys = [k for k in keys if k not in weights and k[1] < DEPTH]
        return keys, _cast_part([weight_source(*k) for k in keys], steps)

    def pre_args(i):
        return (weights["in", i], i, layers[i]["gain_pre"], layers[i]["f32_cols"])

    xs = [x_prompt.reshape(nb_c * len_c, D_MODEL), x_sample.reshape(nb_l * len_l, D_MODEL)]
    streams = ((nb_c, len_c, 0), (nb_l, len_l, 1))
    zs = [None, None]
    ys = [None, None]
    new_states, new_kv = None, None

    def proj_part(s, i, steps=None, mod=None):
        nb, sl, row0 = streams[s]
        post = ys[s] + (weights["out", i - 1], i - 1, layers[i - 1]["gain_post"]) if i > 0 else None
        return _proj_part(xs[s], mods if mod is None else mod, sl, row0, post=post,
                          pre=pre_args(i) if i < DEPTH else None, steps=steps)

    def take_proj(s, i, outs):
        if i > 0:
            xs[s] = outs.pop(0)
        zs[s] = outs

    def rec_part(s, i, heads=None):
        nb, sl, _ = streams[s]
        j = i // 2
        hn = rec_head_norm[j].reshape(1, REC_HEAD_DIM)
        z, zf = zs[s]
        if s == 0:
            return _rec_part(z, zf, nb, sl, lower_bounds[j], hn, j, states=new_states, heads=heads)
        return _rec_part(z, zf, nb, sl, lower_bounds[j], hn, j, s0=state_hgrn, heads=heads)

    def pool_part(s, i):
        nb, sl, _ = streams[s]
        j = i // 2
        return _pool_part(zs[s][0], nb, sl, pool_w[j].astype(BF16), pool_scale[j].reshape(1, POOL_WIDTH))

    mod0 = _mod_rows(_run(_mod_part(cvec, ada_w, ada_b, 0, 1), name="modulation")[0][0])
    first = proj_part(0, 0, mod=mod0, steps=DEPTH * 3 * D_MODEL // MOD_COLS)
    keys, casts = cast_part([("out", 0), ("in", 1)], first.steps)
    outs, (mods,), cast = _run(first, _mod_part(cvec, ada_w, ada_b, 0, DEPTH), casts, name="proj_modulation")
    weights.update(zip(keys, cast))
    mods = _mod_rows(mods)
    take_proj(0, 0, outs)
    for i in range(DEPTH):
        j = i // 2
        if i % 2 == 0:
            rec_c = rec_part(0, i)
            keys, casts = cast_part([("out", i), ("in", i + 1)], rec_c.steps)
            (y_rec, new_states), outs, cast, (y_pool,) = _run(
                rec_c, proj_part(1, i, steps=rec_c.steps), casts, pool_part(0, i), name="rec_pool_proj")
            weights.update(zip(keys, cast))
            ys[0] = (y_pool, 0, y_rec, 0)
            take_proj(1, i, outs)
            rec_l = rec_part(1, i, heads=REC_HEADS * streams[1][0] // rec_c.steps)
            keys, casts = cast_part([("out", i + 1), ("in", i + 2)], rec_l.steps)
            (y_rec,), outs, cast = _run(rec_l, proj_part(0, i + 1, steps=rec_l.steps), casts, name="rec_proj")
            weights.update(zip(keys, cast))
            ys[1] = (_run(pool_part(1, i), name="pool_mixer")[0][0], 0, y_rec, 0)
            take_proj(0, i + 1, outs)
        else:
            qn = att_q_norm[j].reshape(1, ATT_HEAD_DIM)
            kn = att_k_norm[j].reshape(1, ATT_HEAD_DIM)
            y, *new_kv = _attention(zs[0][0], nb_c, len_c, qn, kn, j, new_kv=new_kv)
            ys[0] = (y, 0, y, 1)
            take_proj(1, i, _run(proj_part(1, i), name="proj")[0])
            (y,) = _attention(zs[1][0], nb_l, len_l, qn, kn, j, cache=(cache_k, cache_v, cos, sin_signed))
            ys[1] = (y, 0, y, 1)
            take_proj(0, i + 1, _run(proj_part(0, i + 1), name="proj")[0])
    take_proj(1, DEPTH, _run(proj_part(1, DEPTH), name="proj")[0])
    xc, xl = xs

    kv_shape = (nb_c, N_ATT, len_c, ATT_KV_HEADS, ATT_HEAD_DIM)
    return (xc.reshape(nb_c, len_c, D_MODEL), xl.reshape(nb_l, len_l, D_MODEL),
            new_states, new_kv[0].reshape(kv_shape), new_kv[1].reshape(kv_shape))
```
